```python
import math
import jax
import jax.numpy as jnp
from jax import lax
import numpy as np

D_MODEL = 1024
BATCH = 8
SEQ = 8192
DEPTH = 2

CHUNK = 64
EPS = 1e-6
PLE_DIM = 256
FFN_DIM = 2816
N_BRANCH = 4
BRANCH_WIDTH = 512
S5_GROUP_CH = 16
S5_GROUPS = BRANCH_WIDTH // S5_GROUP_CH
S5_STATE = 64
LRU_HEADS = 8
LRU_HEAD_DIM = BRANCH_WIDTH // LRU_HEADS
LRU_CONV = 4
LRU_C = 8.0
M2_HEAD_DIM = 64
M2_HEADS = BRANCH_WIDTH // M2_HEAD_DIM
M2_GROUPS = 2
M2_STATE = 128
M2_CONV = 4
M2_CONV_DIM = BRANCH_WIDTH + 2 * M2_GROUPS * M2_STATE
GDN_HEAD_DIM = 128
GDN_HEADS = BRANCH_WIDTH // GDN_HEAD_DIM
GDN_CONV = 4
IN_SPLITS = (BRANCH_WIDTH, BRANCH_WIDTH, BRANCH_WIDTH, BRANCH_WIDTH, M2_CONV_DIM, M2_HEADS, 3 * BRANCH_WIDTH, BRANCH_WIDTH, GDN_HEADS, GDN_HEADS)
IN_WIDTH = 4 * BRANCH_WIDTH + M2_CONV_DIM + M2_HEADS + 3 * BRANCH_WIDTH + BRANCH_WIDTH + 2 * GDN_HEADS

kernel_name = "hybrid_parallel_ssm_lru_ssd_gdn_trunk"


def _split(t, sizes):
    out, start = [], 0
    for s in sizes:
        out.append(t[..., start:start + s])
        start += s
    return out


def rmsnorm(x, g):
    xf = x.astype(jnp.float32)
    y = xf * lax.rsqrt(jnp.mean(xf * xf, axis=-1, keepdims=True) + EPS)
    return (y * g.astype(jnp.float32)).astype(x.dtype)


def _l2norm(x):
    return x * lax.rsqrt(jnp.sum(x * x, axis=-1, keepdims=True) + EPS)


def swiglu(h, w_in, w_out):
    gate, up = jnp.split(h @ w_in, 2, axis=-1)
    return (jax.nn.silu(gate) * up) @ w_out


def causal_depthwise_conv(x, w, b=None):
    k, c = w.shape
    y = lax.conv_general_dilated(x, w[:, None, :].astype(x.dtype), window_strides=(1,), padding=[(k - 1, 0)], dimension_numbers=('NWC', 'WIO', 'NWC'), feature_group_count=c)
    return y if b is None else y + b.astype(y.dtype)


def _linear_combine(l, r):
    return (l[0] * r[0], r[0] * l[1] + r[1])


def _complex_combine(l, r):
    a1r, a1i, b1r, b1i = l
    a2r, a2i, b2r, b2i = r
    return (a2r * a1r - a2i * a1i, a2r * a1i + a2i * a1r, a2r * b1r - a2i * b1i + b2r, a2r * b1i + a2i * b1r + b2i)


def _decay_matrix(cs):
    n = cs.shape[-1]
    causal = jnp.tril(jnp.ones((n, n), dtype=bool))
    diff = cs[..., :, None] - cs[..., None, :]
    return jnp.where(causal, jnp.exp(jnp.where(causal, diff, 0.0)), 0.0)


def s5_mixer(u, log_step, a_re, a_im, b_re, b_im, c_re, c_im, d_skip, w_glu, b_glu):
    f32 = jnp.float32
    dtype = u.dtype
    bsz, s, _ = u.shape
    ug = u.astype(f32).reshape(bsz, s, S5_GROUPS, S5_GROUP_CH)
    a_re, a_im = a_re.astype(f32), a_im.astype(f32)
    b_re, b_im = b_re.astype(f32), b_im.astype(f32)
    step = jnp.exp(log_step.astype(f32))[:, None]
    mag = jnp.exp(a_re * step)
    ab_re, ab_im = mag * jnp.cos(a_im * step), mag * jnp.sin(a_im * step)
    den = a_re * a_re + a_im * a_im
    num_re = ab_re - 1.0
    f_re = (num_re * a_re + ab_im * a_im) / den
    f_im = (ab_im * a_re - num_re * a_im) / den
    bb_re = f_re[..., None] * b_re - f_im[..., None] * b_im
    bb_im = f_re[..., None] * b_im + f_im[..., None] * b_re
    bu_re = jnp.einsum('bsgc,gpc->bsgp', ug, bb_re)
    bu_im = jnp.einsum('bsgc,gpc->bsgp', ug, bb_im)
    ar = jnp.broadcast_to(ab_re, bu_re.shape)
    ai = jnp.broadcast_to(ab_im, bu_im.shape)
    _, _, h_re, h_im = lax.associative_scan(_complex_combine, (ar, ai, bu_re, bu_im), axis=1)
    y = jnp.einsum('bsgp,gcp->bsgc', h_re, c_re.astype(f32)) - jnp.einsum('bsgp,gcp->bsgc', h_im, c_im.astype(f32)) + d_skip.astype(f32) * ug
    z = jax.nn.gelu(y.reshape(bsz, s, BRANCH_WIDTH))
    return (z * jax.nn.sigmoid(z @ w_glu.astype(f32) + b_glu.astype(f32))).astype(dtype)


def rglru_mixer(xb, gate, conv_w, conv_b, w_r, b_r, w_i, b_i, lam):
    f32 = jnp.float32
    dtype = xb.dtype
    bsz, s, _ = xb.shape
    xc = causal_depthwise_conv(xb, conv_w, conv_b).astype(f32)
    xh = xc.reshape(bsz, s, LRU_HEADS, LRU_HEAD_DIM)
    r = jax.nn.sigmoid(jnp.einsum('bshi,hij->bshj', xh, w_r.astype(f32)).reshape(bsz, s, BRANCH_WIDTH) + b_r.astype(f32))
    i_g = jax.nn.sigmoid(jnp.einsum('bshi,hij->bshj', xh, w_i.astype(f32)).reshape(bsz, s, BRANCH_WIDTH) + b_i.astype(f32))
    log_a = -LRU_C * r * jax.nn.softplus(-lam.astype(f32))
    a = jnp.exp(log_a)
    inp = jnp.sqrt(-jnp.expm1(2.0 * log_a)) * (i_g * xc)
    _, h = lax.associative_scan(_linear_combine, (a, inp), axis=1)
    return (h * jax.nn.gelu(gate.astype(f32))).astype(dtype)


def ssd_chunked(x, a, bm, cm):
    bsz, s, h, pdim = x.shape
    n = bm.shape[-1]
    c = s // CHUNK
    xc = x.reshape(bsz, c, CHUNK, h, pdim)
    bc = bm.reshape(bsz, c, CHUNK, h, n)
    cc = cm.reshape(bsz, c, CHUNK, h, n)
    a_cs = jnp.cumsum(a.reshape(bsz, c, CHUNK, h).transpose(0, 3, 1, 2), axis=-1)
    scores = jnp.einsum('bclhn,bcshn->bhcls', cc, bc) * _decay_matrix(a_cs)
    y_diag = jnp.einsum('bhcls,bcshp->bclhp', scores, xc)
    decay_to_end = jnp.exp(a_cs[..., -1:] - a_cs)
    states = jnp.einsum('bclhn,bhcl,bclhp->bchpn', bc, decay_to_end, xc)
    chunk_decay = jnp.exp(a_cs[..., -1])

    def step(carry, inp):
        st, dec = inp
        return carry * dec[..., None, None] + st, carry

    init = jnp.zeros((bsz, h, pdim, n), x.dtype)
    _, prev = lax.scan(step, init, (jnp.moveaxis(states, 1, 0), jnp.moveaxis(chunk_decay, 2, 0)))
    prev = jnp.moveaxis(prev, 0, 1)
    y_off = jnp.einsum('bclhn,bchpn,bhcl->bclhp', cc, prev, jnp.exp(a_cs))
    return (y_diag + y_off).reshape(bsz, s, h, pdim)


def mamba2_mixer(z, xbc, dt_raw, conv_w, conv_b, dt_bias, a_log, d_skip, norm_g):
    f32 = jnp.float32
    dtype = z.dtype
    bsz, s, _ = z.shape
    xbc = jax.nn.silu(causal_depthwise_conv(xbc, conv_w, conv_b)).astype(f32)
    xs, bm, cm = _split(xbc, (BRANCH_WIDTH, M2_GROUPS * M2_STATE, M2_GROUPS * M2_STATE))
    xh = xs.reshape(bsz, s, M2_HEADS, M2_HEAD_DIM)
    rep = M2_HEADS // M2_GROUPS
    bm = jnp.repeat(bm.reshape(bsz, s, M2_GROUPS, M2_STATE), rep, axis=2)
    cm = jnp.repeat(cm.reshape(bsz, s, M2_GROUPS, M2_STATE), rep, axis=2)
    dt = jax.nn.softplus(dt_raw.astype(f32) + dt_bias.astype(f32))
    a = -jnp.exp(a_log.astype(f32))
    y = ssd_chunked(xh * dt[..., None], dt * a, bm, cm)
    y = y + d_skip.astype(f32)[:, None] * xh
    y = y.reshape(bsz, s, BRANCH_WIDTH) * jax.nn.silu(z.astype(f32))
    return rmsnorm(y, norm_g).astype(dtype)


def chunk_gated_delta_rule(q, k, v, g, beta):
    bsz, s, h, dk = q.shape
    dv = v.shape[-1]
    c = s // CHUNK

    def to_chunks(t):
        return t.reshape(bsz, c, CHUNK, h, -1).transpose(0, 3, 1, 2, 4)

    q, k, v = to_chunks(q), to_chunks(k), to_chunks(v)
    g_cs = jnp.cumsum(g.reshape(bsz, c, CHUNK, h).transpose(0, 3, 1, 2), axis=-1)
    beta = beta.reshape(bsz, c, CHUNK, h).transpose(0, 3, 1, 2)[..., None]
    decay = _decay_matrix(g_cs)
    kb = k * beta
    strict = jnp.tril(jnp.ones((CHUNK, CHUNK), dtype=bool), -1)
    m = jnp.where(strict, jnp.einsum('bhcld,bhcsd->bhcls', kb, k) * decay, 0.0)
    eye = jnp.eye(CHUNK, dtype=m.dtype)
    rhs = jnp.concatenate([v * beta, kb * jnp.exp(g_cs)[..., None]], axis=-1)
    sol = lax.linalg.triangular_solve(m + eye, rhs, left_side=True, lower=True, unit_diagonal=True)
    u, w = sol[..., :dv], sol[..., dv:]
    qk = jnp.einsum('bhcld,bhcsd->bhcls', q, k) * decay
    q_dec = q * jnp.exp(g_cs)[..., None]
    k_dec = k * jnp.exp(g_cs[..., -1:] - g_cs)[..., None]
    chunk_decay = jnp.exp(g_cs[..., -1])

    def step(state, inp):
        q_i, k_i, w_i, u_i, qk_i, dec_i = inp
        v_new = u_i - jnp.einsum('bhld,bhde->bhle', w_i, state)
        o_i = jnp.einsum('bhld,bhde->bhle', q_i, state) + jnp.einsum('bhls,bhse->bhle', qk_i, v_new)
        state = state * dec_i[..., None, None] + jnp.einsum('bhld,bhle->bhde', k_i, v_new)
        return state, o_i

    xs = (jnp.moveaxis(q_dec, 2, 0), jnp.moveaxis(k_dec, 2, 0), jnp.moveaxis(w, 2, 0), jnp.moveaxis(u, 2, 0), jnp.moveaxis(qk, 2, 0), jnp.moveaxis(chunk_decay, 2, 0))
    _, o = lax.scan(step, jnp.zeros((bsz, h, dk, dv), q.dtype), xs)
    return o.transpose(1, 0, 3, 2, 4).reshape(bsz, s, h, dv)


def gated_deltanet_mixer(qkv, gate, beta_raw, a_raw, conv_w, dt_bias, a_log, norm_g):
    f32 = jnp.float32
    dtype = qkv.dtype
    bsz, s, _ = qkv.shape
    qkv = jax.nn.silu(causal_depthwise_conv(qkv, conv_w)).astype(f32)
    q, k, v = _split(qkv, (BRANCH_WIDTH, BRANCH_WIDTH, BRANCH_WIDTH))
    shape = (bsz, s, GDN_HEADS, GDN_HEAD_DIM)
    q = _l2norm(q.reshape(shape)) * (GDN_HEAD_DIM ** -0.5)
    k = _l2norm(k.reshape(shape))
    v = v.reshape(shape)
    beta = jax.nn.sigmoid(beta_raw.astype(f32))
    g = -jnp.exp(a_log.astype(f32)) * jax.nn.softplus(a_raw.astype(f32) + dt_bias.astype(f32))
    o = chunk_gated_delta_rule(q, k, v, g, beta)
    o = rmsnorm(o, norm_g) * jax.nn.silu(gate.astype(f32).reshape(shape))
    return o.reshape(bsz, s, BRANCH_WIDTH).astype(dtype)


def _fwd_setup_inputs(seed: int = 0) -> dict:
    key = jax.random.key(seed)
    keys = iter(jax.random.split(key, 64))
    f32 = jnp.float32

    def normal(shape, scale):
        return scale * jax.random.normal(next(keys), shape, f32)

    def gain(shape):
        return 1.0 + 0.01 * jax.random.normal(next(keys), shape, f32)

    def uniform(shape, lo, hi):
        return jax.random.uniform(next(keys), shape, f32, lo, hi)

    def dt_bias(shape):
        dt = jnp.exp(uniform(shape, math.log(1e-3), math.log(1e-1)))
        return dt + jnp.log(-jnp.expm1(-dt))

    L, D, W = DEPTH, D_MODEL, BRANCH_WIDTH
    n_idx = jnp.arange(S5_STATE, dtype=f32)
    a_pow = uniform((L, W), 0.9, 0.999) ** (1.0 / LRU_C)
    return {
        'x': normal((BATCH, SEQ, D), 1.0),
        'p': normal((DEPTH, BATCH, SEQ, PLE_DIM), 1.0),
        'ffn1_norm': gain((L, D)),
        'ffn1_w_in': normal((L, D, 2 * FFN_DIM), D ** -0.5),
        'ffn1_w_out': normal((L, FFN_DIM, D), FFN_DIM ** -0.5),
        'mix_norm': gain((L, D)),
        'w_in': normal((L, D, IN_WIDTH), D ** -0.5),
        'w_gate': normal((L, D, N_BRANCH * D), D ** -0.5),
        'b_gate': normal((L, N_BRANCH * D), 0.01),
        's5_log_step': uniform((L, S5_GROUPS), math.log(1e-3), math.log(1e-1)),
        's5_a_re': -0.5 + normal((L, S5_GROUPS, S5_STATE), 0.01),
        's5_a_im': math.pi * n_idx + normal((L, S5_GROUPS, S5_STATE), 0.01),
        's5_b_re': normal((L, S5_GROUPS, S5_STATE, S5_GROUP_CH), (2 * S5_GROUP_CH) ** -0.5),
        's5_b_im': normal((L, S5_GROUPS, S5_STATE, S5_GROUP_CH), (2 * S5_GROUP_CH) ** -0.5),
        's5_c_re': normal((L, S5_GROUPS, S5_GROUP_CH, S5_STATE), (2 * S5_STATE) ** -0.5),
        's5_c_im': normal((L, S5_GROUPS, S5_GROUP_CH, S5_STATE), (2 * S5_STATE) ** -0.5),
        's5_d': normal((L, S5_GROUPS, S5_GROUP_CH), 1.0),
        's5_w_glu': normal((L, W, W), W ** -0.5),
        's5_b_glu': normal((L, W), 0.01),
        'lru_conv_w': normal((L, LRU_CONV, W), LRU_CONV ** -0.5),
        'lru_conv_b': normal((L, W), 0.01),
        'lru_w_r': normal((L, LRU_HEADS, LRU_HEAD_DIM, LRU_HEAD_DIM), LRU_HEAD_DIM ** -0.5),
        'lru_b_r': normal((L, W), 0.01),
        'lru_w_i': normal((L, LRU_HEADS, LRU_HEAD_DIM, LRU_HEAD_DIM), LRU_HEAD_DIM ** -0.5),
        'lru_b_i': normal((L, W), 0.01),
        'lru_lambda': jnp.log(a_pow) - jnp.log1p(-a_pow),
        'm2_conv_w': normal((L, M2_CONV, M2_CONV_DIM), M2_CONV ** -0.5),
        'm2_conv_b': normal((L, M2_CONV_DIM), 0.01),
        'm2_dt_bias': dt_bias((L, M2_HEADS)),
        'm2_a_log': jnp.log(uniform((L, M2_HEADS), 1.0, 16.0)),
        'm2_d': gain((L, M2_HEADS)),
        'm2_norm': gain((L, W)),
        'gdn_conv_w': normal((L, GDN_CONV, 3 * W), GDN_CONV ** -0.5),
        'gdn_dt_bias': dt_bias((L, GDN_HEADS)),
        'gdn_a_log': jnp.log(uniform((L, GDN_HEADS), 1.0, 16.0)),
        'gdn_norm': gain((L, GDN_HEAD_DIM)),
        'w_branch': normal((L, N_BRANCH, W, D), W ** -0.5),
        'w_out': normal((L, D, D), D ** -0.5),
        'ffn2_norm': gain((L, D)),
        'ffn2_w_in': normal((L, D, 2 * FFN_DIM), D ** -0.5),
        'ffn2_w_out': normal((L, FFN_DIM, D), FFN_DIM ** -0.5),
        'ple_norm': gain((L, D)),
        'ple_w_gate': normal((L, D, D), D ** -0.5),
        'ple_w_proj': normal((L, PLE_DIM, D), PLE_DIM ** -0.5),
        'final_norm': gain((D,)),
    }


def _fwd_reference(x, p, ffn1_norm, ffn1_w_in, ffn1_w_out, mix_norm, w_in, w_gate, b_gate, s5_log_step, s5_a_re, s5_a_im, s5_b_re, s5_b_im, s5_c_re, s5_c_im, s5_d, s5_w_glu, s5_b_glu, lru_conv_w, lru_conv_b, lru_w_r, lru_b_r, lru_w_i, lru_b_i, lru_lambda, m2_conv_w, m2_conv_b, m2_dt_bias, m2_a_log, m2_d, m2_norm, gdn_conv_w, gdn_dt_bias, gdn_a_log, gdn_norm, w_branch, w_out, ffn2_norm, ffn2_w_in, ffn2_w_out, ple_norm, ple_w_gate, ple_w_proj, final_norm):
    bsz, s, d = x.shape
    h = x
    for i in range(DEPTH):
        h = h + 0.5 * swiglu(rmsnorm(h, ffn1_norm[i]), ffn1_w_in[i], ffn1_w_out[i])
        u = rmsnorm(h, mix_norm[i])
        s5_u, lru_x, lru_g, m2_z, m2_xbc, m2_dt, gdn_qkv, gdn_g, gdn_b, gdn_a = _split(u @ w_in[i], IN_SPLITS)
        y_a = s5_mixer(s5_u, s5_log_step[i], s5_a_re[i], s5_a_im[i], s5_b_re[i], s5_b_im[i], s5_c_re[i], s5_c_im[i], s5_d[i], s5_w_glu[i], s5_b_glu[i])
        y_b = rglru_mixer(lru_x, lru_g, lru_conv_w[i], lru_conv_b[i], lru_w_r[i], lru_b_r[i], lru_w_i[i], lru_b_i[i], lru_lambda[i])
        y_c = mamba2_mixer(m2_z, m2_xbc, m2_dt, m2_conv_w[i], m2_conv_b[i], m2_dt_bias[i], m2_a_log[i], m2_d[i], m2_norm[i])
        y_d = gated_deltanet_mixer(gdn_qkv, gdn_g, gdn_b, gdn_a, gdn_conv_w[i], gdn_dt_bias[i], gdn_a_log[i], gdn_norm[i])
        ys = jnp.stack([y_a, y_b, y_c, y_d], axis=2)
        yb = jnp.einsum('bsnc,ncd->bsnd', ys, w_branch[i])
        gates = jax.nn.sigmoid(u @ w_gate[i] + b_gate[i]).reshape(bsz, s, N_BRANCH, d)
        h = h + jnp.einsum('bsnd,bsnd->bsd', gates, yb) @ w_out[i]
        h = h + 0.5 * swiglu(rmsnorm(h, ffn2_norm[i]), ffn2_w_in[i], ffn2_w_out[i])
        h = h + jax.nn.sigmoid(rmsnorm(h, ple_norm[i]) @ ple_w_gate[i]) * (p[i] @ ple_w_proj[i])
    return rmsnorm(h, final_norm)


import jax as _jax
import jax.numpy as _jnp

TWIN_FORMAT = 'train_step'
FWD_PARAMS = ['x', 'p', 'ffn1_norm', 'ffn1_w_in', 'ffn1_w_out', 'mix_norm', 'w_in', 'w_gate', 'b_gate', 's5_log_step', 's5_a_re', 's5_a_im', 's5_b_re', 's5_b_im', 's5_c_re', 's5_c_im', 's5_d', 's5_w_glu', 's5_b_glu', 'lru_conv_w', 'lru_conv_b', 'lru_w_r', 'lru_b_r', 'lru_w_i', 'lru_b_i', 'lru_lambda', 'm2_conv_w', 'm2_conv_b', 'm2_dt_bias', 'm2_a_log', 'm2_d', 'm2_norm', 'gdn_conv_w', 'gdn_dt_bias', 'gdn_a_log', 'gdn_norm', 'w_branch', 'w_out', 'ffn2_norm', 'ffn2_w_in', 'ffn2_w_out', 'ple_norm', 'ple_w_gate', 'ple_w_proj', 'final_norm']
TWIN_WEIGHTS = ['ffn1_norm', 'ffn1_w_in', 'ffn1_w_out', 'mix_norm', 'w_in', 'w_gate', 'b_gate', 's5_log_step', 's5_a_re', 's5_a_im', 's5_b_re', 's5_b_im', 's5_c_re', 's5_c_im', 's5_d', 's5_w_glu', 's5_b_glu', 'lru_conv_w', 'lru_conv_b', 'lru_w_r', 'lru_b_r', 'lru_w_i', 'lru_b_i', 'lru_lambda', 'm2_conv_w', 'm2_conv_b', 'm2_dt_bias', 'm2_a_log', 'm2_d', 'm2_norm', 'gdn_conv_w', 'gdn_dt_bias', 'gdn_a_log', 'gdn_norm', 'w_branch', 'w_out', 'ffn2_norm', 'ffn2_w_in', 'ffn2_w_out', 'ple_norm', 'ple_w_gate', 'ple_w_proj', 'final_norm']
TWIN_DIFF_INPUT = 'x'
TWIN_INPUTS = ['x', 'p', 'ffn1_norm', 'ffn1_w_in', 'ffn1_w_out', 'mix_norm', 'w_in', 'w_gate', 'b_gate', 's5_log_step', 's5_a_re', 's5_a_im', 's5_b_re', 's5_b_im', 's5_c_re', 's5_c_im', 's5_d', 's5_w_glu', 's5_b_glu', 'lru_conv_w', 'lru_conv_b', 'lru_w_r', 'lru_b_r', 'lru_w_i', 'lru_b_i', 'lru_lambda', 'm2_conv_w', 'm2_conv_b', 'm2_dt_bias', 'm2_a_log', 'm2_d', 'm2_norm', 'gdn_conv_w', 'gdn_dt_bias', 'gdn_a_log', 'gdn_norm', 'w_branch', 'w_out', 'ffn2_norm', 'ffn2_w_in', 'ffn2_w_out', 'ple_norm', 'ple_w_gate', 'ple_w_proj', 'final_norm', 'loss_target', 'm_ffn1_norm', 'm_ffn1_w_in', 'm_ffn1_w_out', 'm_mix_norm', 'm_w_in', 'm_w_gate', 'm_b_gate', 'm_s5_log_step', 'm_s5_a_re', 'm_s5_a_im', 'm_s5_b_re', 'm_s5_b_im', 'm_s5_c_re', 'm_s5_c_im', 'm_s5_d', 'm_s5_w_glu', 'm_s5_b_glu', 'm_lru_conv_w', 'm_lru_conv_b', 'm_lru_w_r', 'm_lru_b_r', 'm_lru_w_i', 'm_lru_b_i', 'm_lru_lambda', 'm_m2_conv_w', 'm_m2_conv_b', 'm_m2_dt_bias', 'm_m2_a_log', 'm_m2_d', 'm_m2_norm', 'm_gdn_conv_w', 'm_gdn_dt_bias', 'm_gdn_a_log', 'm_gdn_norm', 'm_w_branch', 'm_w_out', 'm_ffn2_norm', 'm_ffn2_w_in', 'm_ffn2_w_out', 'm_ple_norm', 'm_ple_w_gate', 'm_ple_w_proj', 'm_final_norm', 'v_ffn1_norm', 'v_ffn1_w_in', 'v_ffn1_w_out', 'v_mix_norm', 'v_w_in', 'v_w_gate', 'v_b_gate', 'v_s5_log_step', 'v_s5_a_re', 'v_s5_a_im', 'v_s5_b_re', 'v_s5_b_im', 'v_s5_c_re', 'v_s5_c_im', 'v_s5_d', 'v_s5_w_glu', 'v_s5_b_glu', 'v_lru_conv_w', 'v_lru_conv_b', 'v_lru_w_r', 'v_lru_b_r', 'v_lru_w_i', 'v_lru_b_i', 'v_lru_lambda', 'v_m2_conv_w', 'v_m2_conv_b', 'v_m2_dt_bias', 'v_m2_a_log', 'v_m2_d', 'v_m2_norm', 'v_gdn_conv_w', 'v_gdn_dt_bias', 'v_gdn_a_log', 'v_gdn_norm', 'v_w_branch', 'v_w_out', 'v_ffn2_norm', 'v_ffn2_w_in', 'v_ffn2_w_out', 'v_ple_norm', 'v_ple_w_gate', 'v_ple_w_proj', 'v_final_norm']
TWIN_OUTPUTS = ['loss', 'grad_x', 'grad_ffn1_norm', 'grad_ffn1_w_in', 'grad_ffn1_w_out', 'grad_mix_norm', 'grad_w_in', 'grad_w_gate', 'grad_b_gate', 'grad_s5_log_step', 'grad_s5_a_re', 'grad_s5_a_im', 'grad_s5_b_re', 'grad_s5_b_im', 'grad_s5_c_re', 'grad_s5_c_im', 'grad_s5_d', 'grad_s5_w_glu', 'grad_s5_b_glu', 'grad_lru_conv_w', 'grad_lru_conv_b', 'grad_lru_w_r', 'grad_lru_b_r', 'grad_lru_w_i', 'grad_lru_b_i', 'grad_lru_lambda', 'grad_m2_conv_w', 'grad_m2_conv_b', 'grad_m2_dt_bias', 'grad_m2_a_log', 'grad_m2_d', 'grad_m2_norm', 'grad_gdn_conv_w', 'grad_gdn_dt_bias', 'grad_gdn_a_log', 'grad_gdn_norm', 'grad_w_branch', 'grad_w_out', 'grad_ffn2_norm', 'grad_ffn2_w_in', 'grad_ffn2_w_out', 'grad_ple_norm', 'grad_ple_w_gate', 'grad_ple_w_proj', 'grad_final_norm', 'delta_ffn1_norm', 'delta_ffn1_w_in', 'delta_ffn1_w_out', 'delta_mix_norm', 'delta_w_in', 'delta_w_gate', 'delta_b_gate', 'delta_s5_log_step', 'delta_s5_a_re', 'delta_s5_a_im', 'delta_s5_b_re', 'delta_s5_b_im', 'delta_s5_c_re', 'delta_s5_c_im', 'delta_s5_d', 'delta_s5_w_glu', 'delta_s5_b_glu', 'delta_lru_conv_w', 'delta_lru_conv_b', 'delta_lru_w_r', 'delta_lru_b_r', 'delta_lru_w_i', 'delta_lru_b_i', 'delta_lru_lambda', 'delta_m2_conv_w', 'delta_m2_conv_b', 'delta_m2_dt_bias', 'delta_m2_a_log', 'delta_m2_d', 'delta_m2_norm', 'delta_gdn_conv_w', 'delta_gdn_dt_bias', 'delta_gdn_a_log', 'delta_gdn_norm', 'delta_w_branch', 'delta_w_out', 'delta_ffn2_norm', 'delta_ffn2_w_in', 'delta_ffn2_w_out', 'delta_ple_norm', 'delta_ple_w_gate', 'delta_ple_w_proj', 'delta_final_norm', 'new_m_ffn1_norm', 'new_m_ffn1_w_in', 'new_m_ffn1_w_out', 'new_m_mix_norm', 'new_m_w_in', 'new_m_w_gate', 'new_m_b_gate', 'new_m_s5_log_step', 'new_m_s5_a_re', 'new_m_s5_a_im', 'new_m_s5_b_re', 'new_m_s5_b_im', 'new_m_s5_c_re', 'new_m_s5_c_im', 'new_m_s5_d', 'new_m_s5_w_glu', 'new_m_s5_b_glu', 'new_m_lru_conv_w', 'new_m_lru_conv_b', 'new_m_lru_w_r', 'new_m_lru_b_r', 'new_m_lru_w_i', 'new_m_lru_b_i', 'new_m_lru_lambda', 'new_m_m2_conv_w', 'new_m_m2_conv_b', 'new_m_m2_dt_bias', 'new_m_m2_a_log', 'new_m_m2_d', 'new_m_m2_norm', 'new_m_gdn_conv_w', 'new_m_gdn_dt_bias', 'new_m_gdn_a_log', 'new_m_gdn_norm', 'new_m_w_branch', 'new_m_w_out', 'new_m_ffn2_norm', 'new_m_ffn2_w_in', 'new_m_ffn2_w_out', 'new_m_ple_norm', 'new_m_ple_w_gate', 'new_m_ple_w_proj', 'new_m_final_norm', 'new_v_ffn1_norm', 'new_v_ffn1_w_in', 'new_v_ffn1_w_out', 'new_v_mix_norm', 'new_v_w_in', 'new_v_w_gate', 'new_v_b_gate', 'new_v_s5_log_step', 'new_v_s5_a_re', 'new_v_s5_a_im', 'new_v_s5_b_re', 'new_v_s5_b_im', 'new_v_s5_c_re', 'new_v_s5_c_im', 'new_v_s5_d', 'new_v_s5_w_glu', 'new_v_s5_b_glu', 'new_v_lru_conv_w', 'new_v_lru_conv_b', 'new_v_lru_w_r', 'new_v_lru_b_r', 'new_v_lru_w_i', 'new_v_lru_b_i', 'new_v_lru_lambda', 'new_v_m2_conv_w', 'new_v_m2_conv_b', 'new_v_m2_dt_bias', 'new_v_m2_a_log', 'new_v_m2_d', 'new_v_m2_norm', 'new_v_gdn_conv_w', 'new_v_gdn_dt_bias', 'new_v_gdn_a_log', 'new_v_gdn_norm', 'new_v_w_branch', 'new_v_w_out', 'new_v_ffn2_norm', 'new_v_ffn2_w_in', 'new_v_ffn2_w_out', 'new_v_ple_norm', 'new_v_ple_w_gate', 'new_v_ple_w_proj', 'new_v_final_norm']
TWIN_LEAF_KINDS = {'loss': 'loss', 'grad_x': 'grad_x', 'grad_ffn1_norm': 'grad_w', 'grad_ffn1_w_in': 'grad_w', 'grad_ffn1_w_out': 'grad_w', 'grad_mix_norm': 'grad_w', 'grad_w_in': 'grad_w', 'grad_w_gate': 'grad_w', 'grad_b_gate': 'grad_w', 'grad_s5_log_step': 'grad_w', 'grad_s5_a_re': 'grad_w', 'grad_s5_a_im': 'grad_w', 'grad_s5_b_re': 'grad_w', 'grad_s5_b_im': 'grad_w', 'grad_s5_c_re': 'grad_w', 'grad_s5_c_im': 'grad_w', 'grad_s5_d': 'grad_w', 'grad_s5_w_glu': 'grad_w', 'grad_s5_b_glu': 'grad_w', 'grad_lru_conv_w': 'grad_w', 'grad_lru_conv_b': 'grad_w', 'grad_lru_w_r': 'grad_w', 'grad_lru_b_r': 'grad_w', 'grad_lru_w_i': 'grad_w', 'grad_lru_b_i': 'grad_w', 'grad_lru_lambda': 'grad_w', 'grad_m2_conv_w': 'grad_w', 'grad_m2_conv_b': 'grad_w', 'grad_m2_dt_bias': 'grad_w', 'grad_m2_a_log': 'grad_w', 'grad_m2_d': 'grad_w', 'grad_m2_norm': 'grad_w', 'grad_gdn_conv_w': 'grad_w', 'grad_gdn_dt_bias': 'grad_w', 'grad_gdn_a_log': 'grad_w', 'grad_gdn_norm': 'grad_w', 'grad_w_branch': 'grad_w', 'grad_w_out': 'grad_w', 'grad_ffn2_norm': 'grad_w', 'grad_ffn2_w_in': 'grad_w', 'grad_ffn2_w_out': 'grad_w', 'grad_ple_norm': 'grad_w', 'grad_ple_w_gate': 'grad_w', 'grad_ple_w_proj': 'grad_w', 'grad_final_norm': 'grad_w', 'delta_ffn1_norm': 'delta_w', 'delta_ffn1_w_in': 'delta_w', 'delta_ffn1_w_out': 'delta_w', 'delta_mix_norm': 'delta_w', 'delta_w_in': 'delta_w', 'delta_w_gate': 'delta_w', 'delta_b_gate': 'delta_w', 'delta_s5_log_step': 'delta_w', 'delta_s5_a_re': 'delta_w', 'delta_s5_a_im': 'delta_w', 'delta_s5_b_re': 'delta_w', 'delta_s5_b_im': 'delta_w', 'delta_s5_c_re': 'delta_w', 'delta_s5_c_im': 'delta_w', 'delta_s5_d': 'delta_w', 'delta_s5_w_glu': 'delta_w', 'delta_s5_b_glu': 'delta_w', 'delta_lru_conv_w': 'delta_w', 'delta_lru_conv_b': 'delta_w', 'delta_lru_w_r': 'delta_w', 'delta_lru_b_r': 'delta_w', 'delta_lru_w_i': 'delta_w', 'delta_lru_b_i': 'delta_w', 'delta_lru_lambda': 'delta_w', 'delta_m2_conv_w': 'delta_w', 'delta_m2_conv_b': 'delta_w', 'delta_m2_dt_bias': 'delta_w', 'delta_m2_a_log': 'delta_w', 'delta_m2_d': 'delta_w', 'delta_m2_norm': 'delta_w', 'delta_gdn_conv_w': 'delta_w', 'delta_gdn_dt_bias': 'delta_w', 'delta_gdn_a_log': 'delta_w', 'delta_gdn_norm': 'delta_w', 'delta_w_branch': 'delta_w', 'delta_w_out': 'delta_w', 'delta_ffn2_norm': 'delta_w', 'delta_ffn2_w_in': 'delta_w', 'delta_ffn2_w_out': 'delta_w', 'delta_ple_norm': 'delta_w', 'delta_ple_w_gate': 'delta_w', 'delta_ple_w_proj': 'delta_w', 'delta_final_norm': 'delta_w', 'new_m_ffn1_norm': 'new_m', 'new_m_ffn1_w_in': 'new_m', 'new_m_ffn1_w_out': 'new_m', 'new_m_mix_norm': 'new_m', 'new_m_w_in': 'new_m', 'new_m_w_gate': 'new_m', 'new_m_b_gate': 'new_m', 'new_m_s5_log_step': 'new_m', 'new_m_s5_a_re': 'new_m', 'new_m_s5_a_im': 'new_m', 'new_m_s5_b_re': 'new_m', 'new_m_s5_b_im': 'new_m', 'new_m_s5_c_re': 'new_m', 'new_m_s5_c_im': 'new_m', 'new_m_s5_d': 'new_m', 'new_m_s5_w_glu': 'new_m', 'new_m_s5_b_glu': 'new_m', 'new_m_lru_conv_w': 'new_m', 'new_m_lru_conv_b': 'new_m', 'new_m_lru_w_r': 'new_m', 'new_m_lru_b_r': 'new_m', 'new_m_lru_w_i': 'new_m', 'new_m_lru_b_i': 'new_m', 'new_m_lru_lambda': 'new_m', 'new_m_m2_conv_w': 'new_m', 'new_m_m2_conv_b': 'new_m', 'new_m_m2_dt_bias': 'new_m', 'new_m_m2_a_log': 'new_m', 'new_m_m2_d': 'new_m', 'new_m_m2_norm': 'new_m', 'new_m_gdn_conv_w': 'new_m', 'new_m_gdn_dt_bias': 'new_m', 'new_m_gdn_a_log': 'new_m', 'new_m_gdn_norm': 'new_m', 'new_m_w_branch': 'new_m', 'new_m_w_out': 'new_m', 'new_m_ffn2_norm': 'new_m', 'new_m_ffn2_w_in': 'new_m', 'new_m_ffn2_w_out': 'new_m', 'new_m_ple_norm': 'new_m', 'new_m_ple_w_gate': 'new_m', 'new_m_ple_w_proj': 'new_m', 'new_m_final_norm': 'new_m', 'new_v_ffn1_norm': 'new_v', 'new_v_ffn1_w_in': 'new_v', 'new_v_ffn1_w_out': 'new_v', 'new_v_mix_norm': 'new_v', 'new_v_w_in': 'new_v', 'new_v_w_gate': 'new_v', 'new_v_b_gate': 'new_v', 'new_v_s5_log_step': 'new_v', 'new_v_s5_a_re': 'new_v', 'new_v_s5_a_im': 'new_v', 'new_v_s5_b_re': 'new_v', 'new_v_s5_b_im': 'new_v', 'new_v_s5_c_re': 'new_v', 'new_v_s5_c_im': 'new_v', 'new_v_s5_d': 'new_v', 'new_v_s5_w_glu': 'new_v', 'new_v_s5_b_glu': 'new_v', 'new_v_lru_conv_w': 'new_v', 'new_v_lru_conv_b': 'new_v', 'new_v_lru_w_r': 'new_v', 'new_v_lru_b_r': 'new_v', 'new_v_lru_w_i': 'new_v', 'new_v_lru_b_i': 'new_v', 'new_v_lru_lambda': 'new_v', 'new_v_m2_conv_w': 'new_v', 'new_v_m2_conv_b': 'new_v', 'new_v_m2_dt_bias': 'new_v', 'new_v_m2_a_log': 'new_v', 'new_v_m2_d': 'new_v', 'new_v_m2_norm': 'new_v', 'new_v_gdn_conv_w': 'new_v', 'new_v_gdn_dt_bias': 'new_v', 'new_v_gdn_a_log': 'new_v', 'new_v_gdn_norm': 'new_v', 'new_v_w_branch': 'new_v', 'new_v_w_out': 'new_v', 'new_v_ffn2_norm': 'new_v', 'new_v_ffn2_w_in': 'new_v', 'new_v_ffn2_w_out': 'new_v', 'new_v_ple_norm': 'new_v', 'new_v_ple_w_gate': 'new_v', 'new_v_ple_w_proj': 'new_v', 'new_v_final_norm': 'new_v'}


def _forward(args):
    return _fwd_reference(*[args[k] for k in FWD_PARAMS])


def _output_shape():
    def fwd():
        inp = _fwd_setup_inputs(0)
        return _fwd_reference(*[inp[k] for k in FWD_PARAMS])
    out = _jax.eval_shape(fwd)
    return out.shape, out.dtype

N_MICROBATCH = 1
ADAM_LR = 0.001
ADAM_B1 = 0.9
ADAM_B2 = 0.999
ADAM_EPS = 1e-08
ADAM_WD = 0.01
ADAM_STEP = 10
PER_EXAMPLE_BATCH_AXIS = {'x': 0, 'p': 1, 'loss_target': 0}
SHARED_INPUTS = []
_WEIGHT_DTYPES = {'ffn1_norm': _jnp.float32, 'ffn1_w_in': _jnp.float32, 'ffn1_w_out': _jnp.float32, 'mix_norm': _jnp.float32, 'w_in': _jnp.float32, 'w_gate': _jnp.float32, 'b_gate': _jnp.float32, 's5_log_step': _jnp.float32, 's5_a_re': _jnp.float32, 's5_a_im': _jnp.float32, 's5_b_re': _jnp.float32, 's5_b_im': _jnp.float32, 's5_c_re': _jnp.float32, 's5_c_im': _jnp.float32, 's5_d': _jnp.float32, 's5_w_glu': _jnp.float32, 's5_b_glu': _jnp.float32, 'lru_conv_w': _jnp.float32, 'lru_conv_b': _jnp.float32, 'lru_w_r': _jnp.float32, 'lru_b_r': _jnp.float32, 'lru_w_i': _jnp.float32, 'lru_b_i': _jnp.float32, 'lru_lambda': _jnp.float32, 'm2_conv_w': _jnp.float32, 'm2_conv_b': _jnp.float32, 'm2_dt_bias': _jnp.float32, 'm2_a_log': _jnp.float32, 'm2_d': _jnp.float32, 'm2_norm': _jnp.float32, 'gdn_conv_w': _jnp.float32, 'gdn_dt_bias': _jnp.float32, 'gdn_a_log': _jnp.float32, 'gdn_norm': _jnp.float32, 'w_branch': _jnp.float32, 'w_out': _jnp.float32, 'ffn2_norm': _jnp.float32, 'ffn2_w_in': _jnp.float32, 'ffn2_w_out': _jnp.float32, 'ple_norm': _jnp.float32, 'ple_w_gate': _jnp.float32, 'ple_w_proj': _jnp.float32, 'final_norm': _jnp.float32}
MOMENT_SCALE = {'ffn1_norm': 1.140365e-01, 'ffn1_w_in': 4.718487e-02, 'ffn1_w_out': 7.683314e-02, 'mix_norm': 2.060706e-01, 'w_in': 8.827516e-02, 'w_gate': 2.590764e-02, 'b_gate': 2.802060e-02, 's5_log_step': 2.590008e+00, 's5_a_re': 2.637742e-03, 's5_a_im': 2.705820e-03, 's5_b_re': 1.733281e-03, 's5_b_im': 1.640425e-03, 's5_c_re': 3.481672e-03, 's5_c_im': 3.539105e-03, 's5_d': 6.506666e-02, 's5_w_glu': 1.486655e-02, 's5_b_glu': 2.821551e-02, 'lru_conv_w': 8.886724e-02, 'lru_conv_b': 8.410557e-01, 'lru_w_r': 2.739227e-02, 'lru_b_r': 2.302981e-02, 'lru_w_i': 5.058163e-02, 'lru_b_i': 2.753453e-02, 'lru_lambda': 4.405147e-02, 'm2_conv_w': 1.110523e-01, 'm2_conv_b': 1.536448e-01, 'm2_dt_bias': 2.749376e-01, 'm2_a_log': 3.214060e-01, 'm2_d': 8.615848e-01, 'm2_norm': 1.443283e-01, 'gdn_conv_w': 6.359870e-02, 'gdn_dt_bias': 4.596542e-01, 'gdn_a_log': 5.799312e-01, 'gdn_norm': 1.676520e-01, 'w_branch': 6.915290e-02, 'w_out': 1.380539e-01, 'ffn2_norm': 7.809312e-02, 'ffn2_w_in': 3.265442e-02, 'ffn2_w_out': 5.326994e-02, 'ple_norm': 3.829652e-02, 'ple_w_gate': 3.740255e-02, 'ple_w_proj': 9.544484e-02, 'final_norm': 6.398211e+01}


def _to_microbatches(a, axis):
    t = _jnp.moveaxis(a, axis, 0)
    t = t.reshape((N_MICROBATCH, t.shape[0] // N_MICROBATCH) + t.shape[1:])
    return _jnp.moveaxis(t, 1, axis + 1)


def setup_inputs(seed: int = 0) -> dict:
    inp = _fwd_setup_inputs(seed)
    key = _jax.random.fold_in(_jax.random.key(seed), 7919)
    shape, _ = _output_shape()
    out = dict(inp)
    out["loss_target"] = _jax.random.normal(_jax.random.fold_in(key, 0), shape, _jnp.float32)
    for i, name in enumerate(TWIN_WEIGHTS):
        w = inp[name].astype(_jnp.float32)
        if MOMENT_SCALE is None:
            s = _jnp.sqrt(_jnp.mean(_jnp.square(w)) + 1e-30)
        else:
            s = MOMENT_SCALE[name]
        km, kv = _jax.random.split(_jax.random.fold_in(key, i + 1))
        out[name] = w
        out["m_" + name] = s * _jax.random.normal(km, w.shape, _jnp.float32)
        out["v_" + name] = (s * s) * _jax.random.uniform(kv, w.shape, _jnp.float32, 0.5, 1.5)
    if N_MICROBATCH > 1:
        for name, axis in PER_EXAMPLE_BATCH_AXIS.items():
            out[name] = _to_microbatches(out[name], axis)
    return {'x': out['x'], 'p': out['p'], 'ffn1_norm': out['ffn1_norm'], 'ffn1_w_in': out['ffn1_w_in'], 'ffn1_w_out': out['ffn1_w_out'], 'mix_norm': out['mix_norm'], 'w_in': out['w_in'], 'w_gate': out['w_gate'], 'b_gate': out['b_gate'], 's5_log_step': out['s5_log_step'], 's5_a_re': out['s5_a_re'], 's5_a_im': out['s5_a_im'], 's5_b_re': out['s5_b_re'], 's5_b_im': out['s5_b_im'], 's5_c_re': out['s5_c_re'], 's5_c_im': out['s5_c_im'], 's5_d': out['s5_d'], 's5_w_glu': out['s5_w_glu'], 's5_b_glu': out['s5_b_glu'], 'lru_conv_w': out['lru_conv_w'], 'lru_conv_b': out['lru_conv_b'], 'lru_w_r': out['lru_w_r'], 'lru_b_r': out['lru_b_r'], 'lru_w_i': out['lru_w_i'], 'lru_b_i': out['lru_b_i'], 'lru_lambda': out['lru_lambda'], 'm2_conv_w': out['m2_conv_w'], 'm2_conv_b': out['m2_conv_b'], 'm2_dt_bias': out['m2_dt_bias'], 'm2_a_log': out['m2_a_log'], 'm2_d': out['m2_d'], 'm2_norm': out['m2_norm'], 'gdn_conv_w': out['gdn_conv_w'], 'gdn_dt_bias': out['gdn_dt_bias'], 'gdn_a_log': out['gdn_a_log'], 'gdn_norm': out['gdn_norm'], 'w_branch': out['w_branch'], 'w_out': out['w_out'], 'ffn2_norm': out['ffn2_norm'], 'ffn2_w_in': out['ffn2_w_in'], 'ffn2_w_out': out['ffn2_w_out'], 'ple_norm': out['ple_norm'], 'ple_w_gate': out['ple_w_gate'], 'ple_w_proj': out['ple_w_proj'], 'final_norm': out['final_norm'], 'loss_target': out['loss_target'], 'm_ffn1_norm': out['m_ffn1_norm'], 'm_ffn1_w_in': out['m_ffn1_w_in'], 'm_ffn1_w_out': out['m_ffn1_w_out'], 'm_mix_norm': out['m_mix_norm'], 'm_w_in': out['m_w_in'], 'm_w_gate': out['m_w_gate'], 'm_b_gate': out['m_b_gate'], 'm_s5_log_step': out['m_s5_log_step'], 'm_s5_a_re': out['m_s5_a_re'], 'm_s5_a_im': out['m_s5_a_im'], 'm_s5_b_re': out['m_s5_b_re'], 'm_s5_b_im': out['m_s5_b_im'], 'm_s5_c_re': out['m_s5_c_re'], 'm_s5_c_im': out['m_s5_c_im'], 'm_s5_d': out['m_s5_d'], 'm_s5_w_glu': out['m_s5_w_glu'], 'm_s5_b_glu': out['m_s5_b_glu'], 'm_lru_conv_w': out['m_lru_conv_w'], 'm_lru_conv_b': out['m_lru_conv_b'], 'm_lru_w_r': out['m_lru_w_r'], 'm_lru_b_r': out['m_lru_b_r'], 'm_lru_w_i': out['m_lru_w_i'], 'm_lru_b_i': out['m_lru_b_i'], 'm_lru_lambda': out['m_lru_lambda'], 'm_m2_conv_w': out['m_m2_conv_w'], 'm_m2_conv_b': out['m_m2_conv_b'], 'm_m2_dt_bias': out['m_m2_dt_bias'], 'm_m2_a_log': out['m_m2_a_log'], 'm_m2_d': out['m_m2_d'], 'm_m2_norm': out['m_m2_norm'], 'm_gdn_conv_w': out['m_gdn_conv_w'], 'm_gdn_dt_bias': out['m_gdn_dt_bias'], 'm_gdn_a_log': out['m_gdn_a_log'], 'm_gdn_norm': out['m_gdn_norm'], 'm_w_branch': out['m_w_branch'], 'm_w_out': out['m_w_out'], 'm_ffn2_norm': out['m_ffn2_norm'], 'm_ffn2_w_in': out['m_ffn2_w_in'], 'm_ffn2_w_out': out['m_ffn2_w_out'], 'm_ple_norm': out['m_ple_norm'], 'm_ple_w_gate': out['m_ple_w_gate'], 'm_ple_w_proj': out['m_ple_w_proj'], 'm_final_norm': out['m_final_norm'], 'v_ffn1_norm': out['v_ffn1_norm'], 'v_ffn1_w_in': out['v_ffn1_w_in'], 'v_ffn1_w_out': out['v_ffn1_w_out'], 'v_mix_norm': out['v_mix_norm'], 'v_w_in': out['v_w_in'], 'v_w_gate': out['v_w_gate'], 'v_b_gate': out['v_b_gate'], 'v_s5_log_step': out['v_s5_log_step'], 'v_s5_a_re': out['v_s5_a_re'], 'v_s5_a_im': out['v_s5_a_im'], 'v_s5_b_re': out['v_s5_b_re'], 'v_s5_b_im': out['v_s5_b_im'], 'v_s5_c_re': out['v_s5_c_re'], 'v_s5_c_im': out['v_s5_c_im'], 'v_s5_d': out['v_s5_d'], 'v_s5_w_glu': out['v_s5_w_glu'], 'v_s5_b_glu': out['v_s5_b_glu'], 'v_lru_conv_w': out['v_lru_conv_w'], 'v_lru_conv_b': out['v_lru_conv_b'], 'v_lru_w_r': out['v_lru_w_r'], 'v_lru_b_r': out['v_lru_b_r'], 'v_lru_w_i': out['v_lru_w_i'], 'v_lru_b_i': out['v_lru_b_i'], 'v_lru_lambda': out['v_lru_lambda'], 'v_m2_conv_w': out['v_m2_conv_w'], 'v_m2_conv_b': out['v_m2_conv_b'], 'v_m2_dt_bias': out['v_m2_dt_bias'], 'v_m2_a_log': out['v_m2_a_log'], 'v_m2_d': out['v_m2_d'], 'v_m2_norm': out['v_m2_norm'], 'v_gdn_conv_w': out['v_gdn_conv_w'], 'v_gdn_dt_bias': out['v_gdn_dt_bias'], 'v_gdn_a_log': out['v_gdn_a_log'], 'v_gdn_norm': out['v_gdn_norm'], 'v_w_branch': out['v_w_branch'], 'v_w_out': out['v_w_out'], 'v_ffn2_norm': out['v_ffn2_norm'], 'v_ffn2_w_in': out['v_ffn2_w_in'], 'v_ffn2_w_out': out['v_ffn2_w_out'], 'v_ple_norm': out['v_ple_norm'], 'v_ple_w_gate': out['v_ple_w_gate'], 'v_ple_w_proj': out['v_ple_w_proj'], 'v_final_norm': out['v_final_norm']}


def _loss(weights, diff, rest, loss_target):
    with _jax.named_scope("forward"):
        args = {**rest, TWIN_DIFF_INPUT: diff, **{k: w.astype(_WEIGHT_DTYPES[k]) for k, w in weights.items()}}
        y = _forward(args)
    with _jax.named_scope("loss_head"):
        err = _jnp.square(y.astype(_jnp.float32) - loss_target)
        return 0.5 * _jnp.sum(_jnp.mean(err, axis=-1)) if err.ndim else 0.5 * err


def _adamw(w, g, m, v):
    m = ADAM_B1 * m + (1.0 - ADAM_B1) * g
    v = ADAM_B2 * v + (1.0 - ADAM_B2) * _jnp.square(g)
    m_hat = m / (1.0 - ADAM_B1 ** ADAM_STEP)
    v_hat = v / (1.0 - ADAM_B2 ** ADAM_STEP)
    delta = -ADAM_LR * (m_hat / (_jnp.sqrt(v_hat) + ADAM_EPS) + ADAM_WD * w)
    return delta, m, v


def reference(x, p, ffn1_norm, ffn1_w_in, ffn1_w_out, mix_norm, w_in, w_gate, b_gate, s5_log_step, s5_a_re, s5_a_im, s5_b_re, s5_b_im, s5_c_re, s5_c_im, s5_d, s5_w_glu, s5_b_glu, lru_conv_w, lru_conv_b, lru_w_r, lru_b_r, lru_w_i, lru_b_i, lru_lambda, m2_conv_w, m2_conv_b, m2_dt_bias, m2_a_log, m2_d, m2_norm, gdn_conv_w, gdn_dt_bias, gdn_a_log, gdn_norm, w_branch, w_out, ffn2_norm, ffn2_w_in, ffn2_w_out, ple_norm, ple_w_gate, ple_w_proj, final_norm, loss_target, m_ffn1_norm, m_ffn1_w_in, m_ffn1_w_out, m_mix_norm, m_w_in, m_w_gate, m_b_gate, m_s5_log_step, m_s5_a_re, m_s5_a_im, m_s5_b_re, m_s5_b_im, m_s5_c_re, m_s5_c_im, m_s5_d, m_s5_w_glu, m_s5_b_glu, m_lru_conv_w, m_lru_conv_b, m_lru_w_r, m_lru_b_r, m_lru_w_i, m_lru_b_i, m_lru_lambda, m_m2_conv_w, m_m2_conv_b, m_m2_dt_bias, m_m2_a_log, m_m2_d, m_m2_norm, m_gdn_conv_w, m_gdn_dt_bias, m_gdn_a_log, m_gdn_norm, m_w_branch, m_w_out, m_ffn2_norm, m_ffn2_w_in, m_ffn2_w_out, m_ple_norm, m_ple_w_gate, m_ple_w_proj, m_final_norm, v_ffn1_norm, v_ffn1_w_in, v_ffn1_w_out, v_mix_norm, v_w_in, v_w_gate, v_b_gate, v_s5_log_step, v_s5_a_re, v_s5_a_im, v_s5_b_re, v_s5_b_im, v_s5_c_re, v_s5_c_im, v_s5_d, v_s5_w_glu, v_s5_b_glu, v_lru_conv_w, v_lru_conv_b, v_lru_w_r, v_lru_b_r, v_lru_w_i, v_lru_b_i, v_lru_lambda, v_m2_conv_w, v_m2_conv_b, v_m2_dt_bias, v_m2_a_log, v_m2_d, v_m2_norm, v_gdn_conv_w, v_gdn_dt_bias, v_gdn_a_log, v_gdn_norm, v_w_branch, v_w_out, v_ffn2_norm, v_ffn2_w_in, v_ffn2_w_out, v_ple_norm, v_ple_w_gate, v_ple_w_proj, v_final_norm):
    given = dict(x=x, p=p, ffn1_norm=ffn1_norm, ffn1_w_in=ffn1_w_in, ffn1_w_out=ffn1_w_out, mix_norm=mix_norm, w_in=w_in, w_gate=w_gate, b_gate=b_gate, s5_log_step=s5_log_step, s5_a_re=s5_a_re, s5_a_im=s5_a_im, s5_b_re=s5_b_re, s5_b_im=s5_b_im, s5_c_re=s5_c_re, s5_c_im=s5_c_im, s5_d=s5_d, s5_w_glu=s5_w_glu, s5_b_glu=s5_b_glu, lru_conv_w=lru_conv_w, lru_conv_b=lru_conv_b, lru_w_r=lru_w_r, lru_b_r=lru_b_r, lru_w_i=lru_w_i, lru_b_i=lru_b_i, lru_lambda=lru_lambda, m2_conv_w=m2_conv_w, m2_conv_b=m2_conv_b, m2_dt_bias=m2_dt_bias, m2_a_log=m2_a_log, m2_d=m2_d, m2_norm=m2_norm, gdn_conv_w=gdn_conv_w, gdn_dt_bias=gdn_dt_bias, gdn_a_log=gdn_a_log, gdn_norm=gdn_norm, w_branch=w_branch, w_out=w_out, ffn2_norm=ffn2_norm, ffn2_w_in=ffn2_w_in, ffn2_w_out=ffn2_w_out, ple_norm=ple_norm, ple_w_gate=ple_w_gate, ple_w_proj=ple_w_proj, final_norm=final_norm, loss_target=loss_target, m_ffn1_norm=m_ffn1_norm, m_ffn1_w_in=m_ffn1_w_in, m_ffn1_w_out=m_ffn1_w_out, m_mix_norm=m_mix_norm, m_w_in=m_w_in, m_w_gate=m_w_gate, m_b_gate=m_b_gate, m_s5_log_step=m_s5_log_step, m_s5_a_re=m_s5_a_re, m_s5_a_im=m_s5_a_im, m_s5_b_re=m_s5_b_re, m_s5_b_im=m_s5_b_im, m_s5_c_re=m_s5_c_re, m_s5_c_im=m_s5_c_im, m_s5_d=m_s5_d, m_s5_w_glu=m_s5_w_glu, m_s5_b_glu=m_s5_b_glu, m_lru_conv_w=m_lru_conv_w, m_lru_conv_b=m_lru_conv_b, m_lru_w_r=m_lru_w_r, m_lru_b_r=m_lru_b_r, m_lru_w_i=m_lru_w_i, m_lru_b_i=m_lru_b_i, m_lru_lambda=m_lru_lambda, m_m2_conv_w=m_m2_conv_w, m_m2_conv_b=m_m2_conv_b, m_m2_dt_bias=m_m2_dt_bias, m_m2_a_log=m_m2_a_log, m_m2_d=m_m2_d, m_m2_norm=m_m2_norm, m_gdn_conv_w=m_gdn_conv_w, m_gdn_dt_bias=m_gdn_dt_bias, m_gdn_a_log=m_gdn_a_log, m_gdn_norm=m_gdn_norm, m_w_branch=m_w_branch, m_w_out=m_w_out, m_ffn2_norm=m_ffn2_norm, m_ffn2_w_in=m_ffn2_w_in, m_ffn2_w_out=m_ffn2_w_out, m_ple_norm=m_ple_norm, m_ple_w_gate=m_ple_w_gate, m_ple_w_proj=m_ple_w_proj, m_final_norm=m_final_norm, v_ffn1_norm=v_ffn1_norm, v_ffn1_w_in=v_ffn1_w_in, v_ffn1_w_out=v_ffn1_w_out, v_mix_norm=v_mix_norm, v_w_in=v_w_in, v_w_gate=v_w_gate, v_b_gate=v_b_gate, v_s5_log_step=v_s5_log_step, v_s5_a_re=v_s5_a_re, v_s5_a_im=v_s5_a_im, v_s5_b_re=v_s5_b_re, v_s5_b_im=v_s5_b_im, v_s5_c_re=v_s5_c_re, v_s5_c_im=v_s5_c_im, v_s5_d=v_s5_d, v_s5_w_glu=v_s5_w_glu, v_s5_b_glu=v_s5_b_glu, v_lru_conv_w=v_lru_conv_w, v_lru_conv_b=v_lru_conv_b, v_lru_w_r=v_lru_w_r, v_lru_b_r=v_lru_b_r, v_lru_w_i=v_lru_w_i, v_lru_b_i=v_lru_b_i, v_lru_lambda=v_lru_lambda, v_m2_conv_w=v_m2_conv_w, v_m2_conv_b=v_m2_conv_b, v_m2_dt_bias=v_m2_dt_bias, v_m2_a_log=v_m2_a_log, v_m2_d=v_m2_d, v_m2_norm=v_m2_norm, v_gdn_conv_w=v_gdn_conv_w, v_gdn_dt_bias=v_gdn_dt_bias, v_gdn_a_log=v_gdn_a_log, v_gdn_norm=v_gdn_norm, v_w_branch=v_w_branch, v_w_out=v_w_out, v_ffn2_norm=v_ffn2_norm, v_ffn2_w_in=v_ffn2_w_in, v_ffn2_w_out=v_ffn2_w_out, v_ple_norm=v_ple_norm, v_ple_w_gate=v_ple_w_gate, v_ple_w_proj=v_ple_w_proj, v_final_norm=v_final_norm)
    weights = {n: given[n] for n in TWIN_WEIGHTS}
    shared = {n: given[n] for n in SHARED_INPUTS}
    per_example = {n: given[n] for n in ['x', 'p']}
    grad_fn = _jax.value_and_grad(_loss, argnums=(0, 1))

    def one_microbatch(ex, loss_target):
        ex = dict(ex)
        diff = ex.pop(TWIN_DIFF_INPUT)
        return grad_fn(weights, diff, {**shared, **ex}, loss_target)

    if N_MICROBATCH == 1:
        loss, (grad_w, grad_x) = one_microbatch(per_example, given["loss_target"])
    else:
        def body(carry, xs):
            loss_sum, grad_sum = carry
            l_k, (gw_k, gx_k) = one_microbatch(xs[0], xs[1])
            with _jax.named_scope("update"):
                return (loss_sum + l_k, _jax.tree.map(_jnp.add, grad_sum, gw_k)), gx_k

        init = (_jnp.zeros((), _jnp.float32), _jax.tree.map(_jnp.zeros_like, weights))
        (loss, grad_w), grad_x = _jax.lax.scan(body, init, (per_example, given["loss_target"]))
    with _jax.named_scope("update"):
        delta_w, new_m, new_v = {}, {}, {}
        for n in TWIN_WEIGHTS:
            delta_w[n], new_m[n], new_v[n] = _adamw(weights[n], grad_w[n], given["m_" + n], given["v_" + n])
    return (loss, grad_x, *[grad_w[n] for n in TWIN_WEIGHTS], *[delta_w[n] for n in TWIN_WEIGHTS],
            *[new_m[n] for n in TWIN_WEIGHTS], *[new_v[n] for n in TWIN_WEIGHTS])
```

```python
import functools
import math

import jax
import jax.numpy as jnp
from jax import lax
from jax.experimental import pallas as pl
from jax.experimental.pallas import tpu as pltpu

f32 = jnp.float32
bf16 = jnp.bfloat16
HI = lax.Precision.HIGHEST

EPS = 1e-6
DEPTH = 2
D_MODEL = 1024
FFN_DIM = 2816
BW = 512
IN_WIDTH = 5136
PW = 5376
SMALL_OFF = 5120
CHUNK = 64
LS = 256
LRU_C = 8.0
N_DEV = 8
VMEM_LIMIT_BYTES = 56 * 1024 * 1024

ADAM_LR, ADAM_B1, ADAM_B2, ADAM_EPS, ADAM_WD, ADAM_STEP = 0.001, 0.9, 0.999, 1e-08, 0.01, 10


def _dg(a, b, dims, hi):
    if hi:
        return lax.dot_general(a, b, (dims, ((), ())), precision=HI, preferred_element_type=f32)
    return lax.dot_general(a.astype(bf16), b.astype(bf16), (dims, ((), ())), preferred_element_type=f32)


def _make_mm(hi):
    @jax.custom_vjp
    def nn(a, b):
        return _dg(a, b, ((1,), (0,)), hi)

    @jax.custom_vjp
    def nt(a, b):
        return _dg(a, b, ((1,), (1,)), hi)

    @jax.custom_vjp
    def tn(a, b):
        return _dg(a, b, ((0,), (0,)), hi)

    nn.defvjp(lambda a, b: (nn(a, b), (a, b)), lambda r, g: (nt(g, r[1]), tn(r[0], g)))
    nt.defvjp(lambda a, b: (nt(a, b), (a, b)), lambda r, g: (nn(g, r[1]), tn(g, r[0])))
    tn.defvjp(lambda a, b: (tn(a, b), (a, b)), lambda r, g: (nt(r[1], g), nn(r[0], g)))
    return nn, nt, tn


mm, mm_nt, mm_tn = _make_mm(False)
hmm, hmm_nt, hmm_tn = _make_mm(True)


def _rows(shape):
    return lax.broadcasted_iota(jnp.int32, shape, 0)


def _lanes(shape):
    return lax.broadcasted_iota(jnp.int32, shape, 1)


def _rms(x, g):
    return x * lax.rsqrt(jnp.mean(x * x, axis=-1, keepdims=True) + EPS) * g


def _silu(x):
    return x * jax.nn.sigmoid(x)


def _gelu(x):
    return 0.5 * x * (1.0 + jnp.tanh(0.7978845608028654 * (x + 0.044715 * x * x * x)))


def _softplus(x):
    return jnp.maximum(x, 0.0) + jnp.log1p(jnp.exp(-jnp.abs(x)))


def _expm1(x):
    p = x * (1.0 + x * (0.5 + x * (1.0 / 6 + x * (1.0 / 24 + x * (1.0 / 120 + x * (1.0 / 720 + x * (1.0 / 5040)))))))
    return jnp.where(x > -0.3, p, jnp.exp(x) - 1.0)


def _pick_row(x, r):
    return jnp.sum(jnp.where(_rows(x.shape) == r, x, 0.0), axis=0, keepdims=True)


def _pick_lane(x, c):
    return jnp.sum(jnp.where(_lanes(x.shape) == c, x, 0.0), axis=1, keepdims=True)


def _shift_up(g, j):
    n = g.shape[0]
    return jnp.where(_rows(g.shape) < n - j, pltpu.roll(g, n - j, 0), 0.0)


@functools.partial(jax.custom_vjp, nondiff_argnums=(1, 2))
def _shift(x, j, fill):
    return jnp.where(_rows(x.shape) >= j, pltpu.roll(x, j, 0), fill)


_shift.defvjp(lambda x, j, fill: (_shift(x, j, fill), None), lambda j, fill, _, g: (_shift_up(g, j),))


@functools.partial(jax.custom_vjp, nondiff_argnums=(2,))
def _shift_halo(x, prev8, j):
    xr = pltpu.roll(x, j, 0)
    pr = pltpu.roll(prev8, j, 0)
    top = jnp.where(_rows(pr.shape) < j, pr, xr[:8])
    return jnp.concatenate([top, xr[8:]], axis=0)


def _shift_halo_bwd(j, _, g):
    g8 = g[:8]
    dprev = jnp.where(_rows(g8.shape) >= 8 - j, pltpu.roll(g8, 8 - j, 0), 0.0)
    return _shift_up(g, j), dprev


_shift_halo.defvjp(lambda x, p, j: (_shift_halo(x, p, j), None), _shift_halo_bwd)


def _conv4(x, prev8, w, b):
    y = _pick_row(w, 3) * x
    for k in range(3):
        y = y + _pick_row(w, k) * _shift_halo(x, prev8, 3 - k)
    return y if b is None else y + b


def _cmul(ar, ai, br, bi):
    return ar * br - ai * bi, ar * bi + ai * br


def _params(grid):
    return pltpu.CompilerParams(dimension_semantics=("arbitrary",) * len(grid), vmem_limit_bytes=VMEM_LIMIT_BYTES)


def _first(axes):
    ok = pl.program_id(axes[0]) == 0
    for a in axes[1:]:
        ok = jnp.logical_and(ok, pl.program_id(a) == 0)
    return ok


def _store(ref, val, acc):
    val = val.astype(ref.dtype)
    if acc is None:
        ref[...] = val
        return
    first = _first(acc)

    @pl.when(first)
    def _():
        ref[...] = val

    @pl.when(jnp.logical_not(first))
    def _():
        ref[...] += val


def _full(a):
    nd = a.ndim
    return pl.BlockSpec(a.shape, lambda *g: (0,) * nd)


def _out(shape, spec, acc=None, dtype=f32):
    return dict(shape=tuple(shape), spec=spec, acc=acc, dtype=dtype)


def run_fwd(name, fn, grid, ins, in_specs, outs, carry=None):
    n_in, n_out = len(ins), len(outs)
    cshapes = carry["shapes"] if carry else []
    nc = len(cshapes)
    ng = len(grid)

    def body(*refs):
        in_refs = refs[:n_in]
        out_refs = refs[n_in:n_in + n_out]
        save_refs = refs[n_in + n_out:n_in + n_out + nc]
        c_refs = refs[n_in + n_out + nc:]
        vals = [r[...] for r in in_refs]
        if carry:
            @pl.when(pl.program_id(carry["axis"]) == 0)
            def _():
                for c in c_refs:
                    c[...] = jnp.zeros(c.shape, f32)
            cin = tuple(c[...] for c in c_refs)
            for s, v in zip(save_refs, cin):
                s[...] = v
            cout, res = fn(cin, *vals)
            for c, v in zip(c_refs, cout):
                c[...] = v
        else:
            res = fn(*vals)
        for o, r, d in zip(out_refs, res, outs):
            _store(o, r, d["acc"])

    out_shape = [jax.ShapeDtypeStruct(d["shape"], d["dtype"]) for d in outs]
    out_specs = [d["spec"] for d in outs]
    for cs in cshapes:
        out_shape.append(jax.ShapeDtypeStruct(tuple(grid) + tuple(cs), f32))
        out_specs.append(pl.BlockSpec((None,) * ng + tuple(cs), lambda *g, _n=len(cs): tuple(g) + (0,) * _n))
    res = pl.pallas_call(
        body, name=name, grid=grid, in_specs=list(in_specs), out_specs=out_specs, out_shape=out_shape,
        scratch_shapes=[pltpu.VMEM(tuple(cs), f32) for cs in cshapes], compiler_params=_params(grid),
    )(*ins)
    return list(res[:n_out]), list(res[n_out:])


def run_bwd(name, fn, grid, ins, in_specs, gouts, gout_specs, wants, carry=None):
    n_in, n_g, n_w = len(ins), len(gouts), len(wants)
    saved = carry["saved"] if carry else []
    nc = len(saved)
    ng = len(grid)

    def body(*refs):
        in_refs = refs[:n_in]
        g_refs = refs[n_in:n_in + n_g]
        s_refs = refs[n_in + n_g:n_in + n_g + nc]
        w_refs = refs[n_in + n_g + nc:n_in + n_g + nc + n_w]
        dc_refs = refs[n_in + n_g + nc + n_w:]
        vals = [r[...].astype(f32) for r in in_refs]
        gs = tuple(r[...].astype(f32) for r in g_refs)
        if carry:
            @pl.when(pl.program_id(carry["axis"]) == 0)
            def _():
                for c in dc_refs:
                    c[...] = jnp.zeros(c.shape, f32)
            cin = tuple(s[...] for s in s_refs)
            _, vjp = jax.vjp(fn, cin, *vals)
            grads = vjp((tuple(c[...] for c in dc_refs), gs))
            for c, v in zip(dc_refs, grads[0]):
                c[...] = v
            dvals = grads[1:]
        else:
            _, vjp = jax.vjp(fn, *vals)
            dvals = vjp(gs)
        for o, d in zip(w_refs, wants):
            _store(o, dvals[d["idx"]], d["acc"])

    rev = carry["rev"] if carry else None
    s_specs = []
    for a in saved:
        n = a.ndim - ng
        s_specs.append(pl.BlockSpec((None,) * ng + tuple(a.shape[ng:]), lambda *g, _n=n: tuple(rev(g)) + (0,) * _n))
    res = pl.pallas_call(
        body, name=name, grid=grid, in_specs=list(in_specs) + list(gout_specs) + s_specs,
        out_specs=[d["spec"] for d in wants],
        out_shape=[jax.ShapeDtypeStruct(d["shape"], d["dtype"]) for d in wants],
        scratch_shapes=[pltpu.VMEM(tuple(a.shape[ng:]), f32) for a in saved], compiler_params=_params(grid),
    )(*ins, *gouts, *saved)
    return list(res)


def _want(idx, shape, spec, acc=None):
    d = _out(shape, spec, acc)
    d["idx"] = idx
    return d


def addn(name, items, t):
    s, w = items[0][0].shape[-2:]
    specs = []
    for a, j in items:
        if j is None:
            specs.append(pl.BlockSpec((t, w), lambda i: (i, 0)))
        else:
            specs.append(pl.BlockSpec((None, t, w), lambda i, _j=j: (_j, i, 0)))

    def fn(*xs):
        y = xs[0]
        for x in xs[1:]:
            y = y + x
        return (y,)

    return run_fwd(name, fn, (s // t,), [a for a, _ in items], specs,
                   [_out((s, w), pl.BlockSpec((t, w), lambda i: (i, 0)))])[0][0]


def _parts(a):
    return [(a, j) for j in range(a.shape[0])]


def _ffn_tile(j_axis, residual):
    def fn(h, g, wg, wu, wo):
        n = _rms(h, g)
        act = _silu(mm(n, wg)) * mm(n, wu)
        y = 0.5 * mm(act, wo)
        if residual:
            y = y + (pl.program_id(j_axis) == 0).astype(f32) * h
        return (y,)
    return fn


def ffn_fwd(name, h, g, wg, wu, wo, t=512, tf=1408):
    s, d = h.shape
    f = wg.shape[1]
    specs = [pl.BlockSpec((t, d), lambda i, j: (i, 0)), _full(g), pl.BlockSpec((d, tf), lambda i, j: (0, j)),
             pl.BlockSpec((d, tf), lambda i, j: (0, j)), pl.BlockSpec((tf, d), lambda i, j: (j, 0))]
    outs = [_out((s, d), pl.BlockSpec((t, d), lambda i, j: (i, 0)), acc=(1,))]
    return run_fwd(name, _ffn_tile(1, True), (s // t, f // tf), [h, g, wg, wu, wo], specs, outs)[0][0]


def ffn_bwd(name, h, g, wg, wu, wo, gout, t=512, tf=256):
    s, d = h.shape
    f = wg.shape[1]
    nj = f // tf
    specs = [pl.BlockSpec((t, d), lambda j, i: (i, 0)), _full(g), pl.BlockSpec((d, tf), lambda j, i: (0, j)),
             pl.BlockSpec((d, tf), lambda j, i: (0, j)), pl.BlockSpec((tf, d), lambda j, i: (j, 0))]
    wants = [_want(0, (nj, s, d), pl.BlockSpec((None, t, d), lambda j, i: (j, i, 0))),
             _want(1, g.shape, _full(g), acc=(0, 1)),
             _want(2, wg.shape, pl.BlockSpec((d, tf), lambda j, i: (0, j)), acc=(1,)),
             _want(3, wu.shape, pl.BlockSpec((d, tf), lambda j, i: (0, j)), acc=(1,)),
             _want(4, wo.shape, pl.BlockSpec((tf, d), lambda j, i: (j, 0)), acc=(1,))]
    return run_bwd(name, _ffn_tile(0, False), (nj, s // t), [h, g, wg, wu, wo], specs,
                   [gout], [pl.BlockSpec((t, d), lambda j, i: (i, 0))], wants)


def _normlin_tile(h, g, w, b):
    return (mm(_rms(h, g), w) + b,)


def normlin_fwd(name, h, g, w, b, t=512, tn=1024):
    s, d = h.shape
    n = w.shape[1]
    specs = [pl.BlockSpec((t, d), lambda i, j: (i, 0)), _full(g), pl.BlockSpec((d, tn), lambda i, j: (0, j)),
             pl.BlockSpec((1, tn), lambda i, j: (0, j))]
    outs = [_out((s, n), pl.BlockSpec((t, tn), lambda i, j: (i, j)))]
    return run_fwd(name, _normlin_tile, (s // t, n // tn), [h, g, w, b], specs, outs)[0][0]


def normlin_bwd(name, h, g, w, b, gout, t=256, tn=1024):
    s, d = h.shape
    n = w.shape[1]
    nj = n // tn
    specs = [pl.BlockSpec((t, d), lambda j, i: (i, 0)), _full(g), pl.BlockSpec((d, tn), lambda j, i: (0, j)),
             pl.BlockSpec((1, tn), lambda j, i: (0, j))]
    wants = [_want(0, (nj, s, d), pl.BlockSpec((None, t, d), lambda j, i: (j, i, 0))),
             _want(1, g.shape, _full(g), acc=(0, 1)),
             _want(2, w.shape, pl.BlockSpec((d, tn), lambda j, i: (0, j)), acc=(1,)),
             _want(3, b.shape, pl.BlockSpec((1, tn), lambda j, i: (0, j)), acc=(1,))]
    return run_bwd(name, _normlin_tile, (nj, s // t), [h, g, w, b], specs,
                   [gout], [pl.BlockSpec((t, tn), lambda j, i: (i, j))], wants)


def _merge_tile(n_axis, residual):
    def fn(h, y, gp, wb, wo):
        part = mm(jax.nn.sigmoid(gp) * mm(y, wb), wo)
        if residual:
            part = part + (pl.program_id(n_axis) == 0).astype(f32) * h
        return (part,)
    return fn


def merge_fwd(name, h, ys, gpre, wb, wo, t=512):
    s, d = h.shape
    specs = [pl.BlockSpec((t, d), lambda i, n: (i, 0)), pl.BlockSpec((t, BW), lambda i, n: (i, n)),
             pl.BlockSpec((t, d), lambda i, n: (i, n)), pl.BlockSpec((None, BW, d), lambda i, n: (n, 0, 0)), _full(wo)]
    outs = [_out((s, d), pl.BlockSpec((t, d), lambda i, n: (i, 0)), acc=(1,))]
    return run_fwd(name, _merge_tile(1, True), (s // t, 4), [h, ys, gpre, wb, wo], specs, outs)[0][0]


def merge_bwd(name, h, ys, gpre, wb, wo, gout, t=256):
    s, d = h.shape
    specs = [pl.BlockSpec((t, d), lambda n, i: (i, 0)), pl.BlockSpec((t, BW), lambda n, i: (i, n)),
             pl.BlockSpec((t, d), lambda n, i: (i, n)), pl.BlockSpec((None, BW, d), lambda n, i: (n, 0, 0)), _full(wo)]
    wants = [_want(1, ys.shape, pl.BlockSpec((t, BW), lambda n, i: (i, n))),
             _want(2, gpre.shape, pl.BlockSpec((t, d), lambda n, i: (i, n))),
             _want(3, wb.shape, pl.BlockSpec((None, BW, d), lambda n, i: (n, 0, 0)), acc=(1,)),
             _want(4, wo.shape, _full(wo), acc=(0, 1))]
    return run_bwd(name, _merge_tile(0, False), (4, s // t), [h, ys, gpre, wb, wo], specs,
                   [gout], [pl.BlockSpec((t, d), lambda n, i: (i, 0))], wants)


def _ple_tile(residual):
    def fn(h, pe, g, wgate, wproj):
        y = jax.nn.sigmoid(mm(_rms(h, g), wgate)) * mm(pe, wproj)
        return (y + h,) if residual else (y,)
    return fn


def ple_fwd(name, h, pe, g, wgate, wproj, t=512):
    s, d = h.shape
    specs = [pl.BlockSpec((t, d), lambda i: (i, 0)), pl.BlockSpec((t, pe.shape[1]), lambda i: (i, 0)),
             _full(g), _full(wgate), _full(wproj)]
    return run_fwd(name, _ple_tile(True), (s // t,), [h, pe, g, wgate, wproj], specs,
                   [_out((s, d), pl.BlockSpec((t, d), lambda i: (i, 0)))])[0][0]


def ple_bwd(name, h, pe, g, wgate, wproj, gout, t=256):
    s, d = h.shape
    specs = [pl.BlockSpec((t, d), lambda i: (i, 0)), pl.BlockSpec((t, pe.shape[1]), lambda i: (i, 0)),
             _full(g), _full(wgate), _full(wproj)]
    wants = [_want(0, h.shape, pl.BlockSpec((t, d), lambda i: (i, 0))), _want(2, g.shape, _full(g), acc=(0,)),
             _want(3, wgate.shape, _full(wgate), acc=(0,)), _want(4, wproj.shape, _full(wproj), acc=(0,))]
    return run_bwd(name, _ple_tile(False), (s // t,), [h, pe, g, wgate, wproj], specs,
                   [gout], [pl.BlockSpec((t, d), lambda i: (i, 0))], wants)


def final_loss(name, h, g, target, t=512):
    s, d = h.shape

    def fn(hh, gg, tt):
        def loss_fn(a, b):
            err = _rms(a, b) - tt
            return 0.5 * jnp.sum(jnp.mean(err * err, axis=-1, keepdims=True), axis=0, keepdims=True)
        loss, vjp = jax.vjp(loss_fn, hh, gg)
        dh, dgain = vjp(jnp.ones((1, 1), f32))
        return loss, dh, dgain

    specs = [pl.BlockSpec((t, d), lambda i: (i, 0)), _full(g), pl.BlockSpec((t, d), lambda i: (i, 0))]
    outs = [_out((1, 1), pl.BlockSpec((1, 1), lambda i: (0, 0)), acc=(0,)),
            _out((s, d), pl.BlockSpec((t, d), lambda i: (i, 0))), _out(g.shape, _full(g), acc=(0,))]
    return run_fwd(name, fn, (s // t,), [h, g, target], specs, outs)[0]


def _rev(nc, rev):
    return (lambda c: nc - 1 - c) if rev else (lambda c: c)


def _s5_ops_tile(are, aim, lstep, bre, bim):
    step = jnp.exp(lstep)
    mag = jnp.exp(are * step)
    ab_re, ab_im = mag * jnp.cos(aim * step), mag * jnp.sin(aim * step)
    den = are * are + aim * aim
    num_re = ab_re - 1.0
    f_re = (num_re * are + ab_im * aim) / den
    f_im = (ab_im * are - num_re * aim) / den
    bb_re = f_re * bre - f_im * bim
    bb_im = f_re * bim + f_im * bre
    pr = jnp.broadcast_to(ab_re, (LS, ab_re.shape[1]))
    pi = jnp.broadcast_to(ab_im, (LS, ab_im.shape[1]))
    k = 1
    while k < LS:
        pr, pi = _cmul(pr, pi, _shift(pr, k, 1.0), _shift(pi, k, 0.0))
        k *= 2
    return ab_re, ab_im, bb_re, bb_im, pr, pi


def _s5_ops_specs(arrs):
    return [pl.BlockSpec((None,) + a.shape[1:], lambda gb: (gb, 0, 0)) for a in arrs]


def s5_ops_fwd(name, raw):
    shapes = [(4, 1, 512), (4, 1, 512), (4, 128, 512), (4, 128, 512), (4, LS, 512), (4, LS, 512)]
    outs = [_out(sh, pl.BlockSpec((None,) + sh[1:], lambda gb: (gb, 0, 0))) for sh in shapes]
    return run_fwd(name, _s5_ops_tile, (4,), raw, _s5_ops_specs(raw), outs)[0]


def s5_ops_bwd(name, raw, gops):
    wants = [_want(i, a.shape, pl.BlockSpec((None,) + a.shape[1:], lambda gb: (gb, 0, 0))) for i, a in enumerate(raw)]
    return run_bwd(name, _s5_ops_tile, (4,), raw, _s5_ops_specs(raw), gops, _s5_ops_specs(gops), wants)


def _s5_tile(carry, u, ab_re, ab_im, bb_re, bb_im, pw_re, pw_im, c_re, c_im, dskip):
    h_re, h_im = carry
    xr, xi = mm(u, bb_re), mm(u, bb_im)
    pr, pi = ab_re, ab_im
    k = 1
    while k < LS:
        sr, si = _cmul(pr, pi, _shift(xr, k, 0.0), _shift(xi, k, 0.0))
        xr, xi = xr + sr, xi + si
        pr, pi = _cmul(pr, pi, pr, pi)
        k *= 2
    cr, ci = _cmul(pw_re, pw_im, h_re, h_im)
    xr, xi = xr + cr, xi + ci
    y = mm(xr, c_re) - mm(xi, c_im) + dskip * u
    return (_pick_row(xr, LS - 1), _pick_row(xi, LS - 1)), (y,)


def _s5_io(u, ops, c_re, c_im, dskip, nc, rev):
    cm = _rev(nc, rev)
    ins = [u] + list(ops) + [c_re, c_im, dskip]
    specs = [pl.BlockSpec((LS, 128), lambda gb, c: (cm(c), gb))]
    specs += [pl.BlockSpec((None,) + a.shape[1:], lambda gb, c: (gb, 0, 0)) for a in list(ops) + [c_re, c_im]]
    specs += [pl.BlockSpec((1, 128), lambda gb, c: (0, gb))]
    return ins, specs, cm


def s5_fwd(name, proj, ops, c_re, c_im, dskip):
    s = proj.shape[0]
    nc = s // LS
    ins, specs, cm = _s5_io(proj, ops, c_re, c_im, dskip, nc, False)
    outs = [_out((s, BW), pl.BlockSpec((LS, 128), lambda gb, c: (c, gb)))]
    (y,), saved = run_fwd(name, _s5_tile, (4, nc), ins, specs, outs, carry=dict(shapes=[(1, 512), (1, 512)], axis=1))
    return y, saved


def s5_bwd(name, proj, ops, c_re, c_im, dskip, saved, gy):
    s = proj.shape[0]
    nc = s // LS
    ins, specs, cm = _s5_io(proj, ops, c_re, c_im, dskip, nc, True)
    wants = [_want(0, (s, BW), pl.BlockSpec((LS, 128), lambda gb, c: (cm(c), gb)))]
    for i, a in enumerate(list(ops) + [c_re, c_im]):
        wants.append(_want(1 + i, a.shape, pl.BlockSpec((None,) + a.shape[1:], lambda gb, c: (gb, 0, 0)), acc=(1,)))
    wants.append(_want(9, dskip.shape, pl.BlockSpec((1, 128), lambda gb, c: (0, gb)), acc=(1,)))
    return run_bwd(name, _s5_tile, (4, nc), ins, specs, [gy], [pl.BlockSpec((LS, 128), lambda gb, c: (cm(c), gb))],
                   wants, carry=dict(axis=1, saved=saved, rev=lambda g: (g[0], cm(g[1]))))


def _s5_glu_tile(y, w, b):
    z = _gelu(y)
    return (z * jax.nn.sigmoid(mm(z, w) + b),)


def s5_glu_fwd(name, y, w, b, t=512):
    s = y.shape[0]
    spec = pl.BlockSpec((t, BW), lambda i: (i, 0))
    return run_fwd(name, _s5_glu_tile, (s // t,), [y, w, b], [spec, _full(w), _full(b)], [_out((s, BW), spec)])[0][0]


def s5_glu_bwd(name, y, w, b, gout, t=512):
    s = y.shape[0]
    spec = pl.BlockSpec((t, BW), lambda i: (i, 0))
    wants = [_want(0, y.shape, spec), _want(1, w.shape, _full(w), acc=(0,)), _want(2, b.shape, _full(b), acc=(0,))]
    return run_bwd(name, _s5_glu_tile, (s // t,), [y, w, b], [spec, _full(w), _full(b)], [gout], [spec], wants)


def _lru_tile(carry, xb, gate, cw, cb, wr, br, wi, bi, lam):
    h_in, prev8 = carry
    xc = _conv4(xb, prev8, cw, cb)
    r = jax.nn.sigmoid(mm(xc, wr) + br)
    ig = jax.nn.sigmoid(mm(xc, wi) + bi)
    log_a = -LRU_C * r * _softplus(-lam)
    a = jnp.exp(log_a)
    b = jnp.sqrt(-_expm1(2.0 * log_a)) * (ig * xc)
    k = 1
    while k < LS:
        b = b + a * _shift(b, k, 0.0)
        a = a * _shift(a, k, 1.0)
        k *= 2
    h = b + a * h_in
    return (_pick_row(h, LS - 1), xb[LS - 8:, :]), (h * _gelu(gate),)


def _lru_io(proj, ws, nc, rev):
    cm = _rev(nc, rev)
    ins = [proj, proj] + list(ws)
    specs = [pl.BlockSpec((LS, BW), lambda c: (cm(c), 1)), pl.BlockSpec((LS, BW), lambda c: (cm(c), 2))]
    specs += [_full(a) for a in ws]
    return ins, specs, cm


def lru_fwd(name, proj, ws):
    s = proj.shape[0]
    nc = s // LS
    ins, specs, cm = _lru_io(proj, ws, nc, False)
    outs = [_out((s, BW), pl.BlockSpec((LS, BW), lambda c: (c, 0)))]
    (y,), saved = run_fwd(name, _lru_tile, (nc,), ins, specs, outs, carry=dict(shapes=[(1, BW), (8, BW)], axis=0))
    return y, saved


def lru_bwd(name, proj, ws, saved, gy):
    s = proj.shape[0]
    nc = s // LS
    ins, specs, cm = _lru_io(proj, ws, nc, True)
    row = pl.BlockSpec((LS, BW), lambda c: (cm(c), 0))
    wants = [_want(0, (s, BW), row), _want(1, (s, BW), row)]
    wants += [_want(2 + i, a.shape, _full(a), acc=(0,)) for i, a in enumerate(ws)]
    return run_bwd(name, _lru_tile, (nc,), ins, specs, [gy], [row], wants,
                   carry=dict(axis=0, saved=saved, rev=lambda g: (cm(g[0]),)))


def _causal(n):
    return _rows((n, n)) >= _lanes((n, n))


def _decay(col, rowv):
    causal = _causal(col.shape[0])
    return jnp.where(causal, jnp.exp(jnp.where(causal, col - rowv, 0.0)), 0.0)


def _m2_tile(carry, z, xs_raw, b_raw, c_raw, small, cwx, cwb, cwc, cbx, cbb, cbc, dtb, alog, dsk, ng):
    state, px, pb, pc = carry
    n = CHUNK
    xs = _silu(_conv4(xs_raw, px, cwx, cbx))
    bm = _silu(_conv4(b_raw, pb, cwb, cbb))
    cmx = _silu(_conv4(c_raw, pc, cwc, cbc))
    expand = (_lanes((16, BW)) // 64 == _rows((16, BW))).astype(f32)
    tri = _causal(n).astype(f32)
    triu = (_rows((n, n)) <= _lanes((n, n))).astype(f32)
    dt = _softplus(small + dtb)
    da = dt * (-jnp.exp(alog))
    cs = hmm(tri, da)
    cs_t = hmm_tn(da, triu)
    cs_w = hmm(cs, expand)
    last_w = _pick_row(cs_w, n - 1)
    xdt = xs * hmm(dt, expand)
    g0 = _lanes((n, BW)) < 256
    bm0, bm1, cm0, cm1 = bm[:, :128], bm[:, 128:], cmx[:, :128], cmx[:, 128:]
    cb0, cb1 = mm_nt(cm0, bm0), mm_nt(cm1, bm1)
    y = jnp.where(g0, mm(cm0, state), mm(cm1, state)) * jnp.exp(cs_w)
    for h in range(8):
        sc = (cb0 if h < 4 else cb1) * _decay(_pick_lane(cs, h), _pick_row(cs_t, h))
        y = y + jnp.where(_lanes((n, BW)) // 64 == h, mm(sc, xdt), 0.0)
    xd = xdt * jnp.exp(last_w - cs_w)
    g0s = _lanes((128, BW)) < 256
    state_out = state * jnp.exp(last_w) + jnp.where(g0s, mm_tn(bm0, xd), mm_tn(bm1, xd))
    y = (y + dsk * xs) * _silu(z)
    return (state_out, xs_raw[n - 8:, :], b_raw[n - 8:, :], c_raw[n - 8:, :]), (_rms(y, ng),)


def _m2_io(proj, small, ws, nc, rev):
    cm = _rev(nc, rev)
    n = CHUNK
    ins = [proj, proj, proj, proj, small] + list(ws)
    specs = [pl.BlockSpec((n, BW), lambda c: (cm(c), 3)), pl.BlockSpec((n, BW), lambda c: (cm(c), 4)),
             pl.BlockSpec((n, 256), lambda c: (cm(c), 10)), pl.BlockSpec((n, 256), lambda c: (cm(c), 11)),
             pl.BlockSpec((n, 16), lambda c: (cm(c), 0))]
    specs += [_full(a) for a in ws]
    return ins, specs, cm


_M2_CARRY = [(128, BW), (8, BW), (8, 256), (8, 256)]


def m2_fwd(name, proj, small, ws):
    s = proj.shape[0]
    nc = s // CHUNK
    ins, specs, cm = _m2_io(proj, small, ws, nc, False)
    outs = [_out((s, BW), pl.BlockSpec((CHUNK, BW), lambda c: (c, 0)))]
    (y,), saved = run_fwd(name, _m2_tile, (nc,), ins, specs, outs, carry=dict(shapes=_M2_CARRY, axis=0))
    return y, saved


def m2_bwd(name, proj, small, ws, saved, gy):
    s = proj.shape[0]
    nc = s // CHUNK
    ins, specs, cm = _m2_io(proj, small, ws, nc, True)
    n = CHUNK
    wants = [_want(0, (s, BW), pl.BlockSpec((n, BW), lambda c: (cm(c), 0))),
             _want(1, (s, BW), pl.BlockSpec((n, BW), lambda c: (cm(c), 0))),
             _want(2, (s, 256), pl.BlockSpec((n, 256), lambda c: (cm(c), 0))),
             _want(3, (s, 256), pl.BlockSpec((n, 256), lambda c: (cm(c), 0))),
             _want(4, (s, 16), pl.BlockSpec((n, 16), lambda c: (cm(c), 0)))]
    wants += [_want(5 + i, a.shape, _full(a), acc=(0,)) for i, a in enumerate(ws)]
    return run_bwd(name, _m2_tile, (nc,), ins, specs, [gy], [pl.BlockSpec((n, BW), lambda c: (cm(c), 0))], wants,
                   carry=dict(axis=0, saved=saved, rev=lambda g: (cm(g[0]),)))


def _l2n(x):
    return x * lax.rsqrt(jnp.sum(x * x, axis=-1, keepdims=True) + EPS)


def _gdn_tile(carry, q_raw, k_raw, v_raw, gate, small, cwq, cwk, cwv, dtb, alog, ng):
    state, pq, pk, pv = carry
    n = CHUNK
    head = pl.program_id(0)
    q = _l2n(_silu(_conv4(q_raw, pq, cwq, None))) * (128 ** -0.5)
    k = _l2n(_silu(_conv4(k_raw, pk, cwk, None)))
    v = _silu(_conv4(v_raw, pv, cwv, None))
    beta = _pick_lane(jax.nn.sigmoid(small), 8 + head)
    g16 = -jnp.exp(alog) * _softplus(small + dtb)
    tri = _causal(n).astype(f32)
    triu = (_rows((n, n)) <= _lanes((n, n))).astype(f32)
    col = _pick_lane(hmm(tri, g16), 12 + head)
    rowv = _pick_row(hmm_tn(g16, triu), 12 + head)
    decay = _decay(col, rowv)
    kb = k * beta
    strict = _rows((n, n)) > _lanes((n, n))
    m = jnp.where(strict, mm_nt(kb, k) * decay, 0.0)
    eye = (_rows((n, n)) == _lanes((n, n))).astype(f32)
    p = -m
    t = eye + p
    for _ in range(5):
        p = hmm(p, p)
        t = t + hmm(t, p)
    e_col = jnp.exp(col)
    u = hmm(t, v * beta)
    w = hmm(t, kb * e_col)
    qk = mm_nt(q, k) * decay
    last = _pick_row(col, n - 1)
    v_new = u - mm(w, state)
    o = mm(q * e_col, state) + mm(qk, v_new)
    state_out = state * jnp.exp(last) + mm_tn(k * jnp.exp(last - col), v_new)
    out = _rms(o, ng) * _silu(gate)
    return (state_out, q_raw[n - 8:, :], k_raw[n - 8:, :], v_raw[n - 8:, :]), (out,)


def _gdn_io(proj, small, cw, dtb, alog, ng, nc, rev):
    cm = _rev(nc, rev)
    n = CHUNK
    ins = [proj, proj, proj, proj, small, cw, cw, cw, dtb, alog, ng]
    specs = [pl.BlockSpec((n, 128), lambda h, c: (cm(c), 24 + h)), pl.BlockSpec((n, 128), lambda h, c: (cm(c), 28 + h)),
             pl.BlockSpec((n, 128), lambda h, c: (cm(c), 32 + h)), pl.BlockSpec((n, 128), lambda h, c: (cm(c), 36 + h)),
             pl.BlockSpec((n, 16), lambda h, c: (cm(c), 0)),
             pl.BlockSpec((4, 128), lambda h, c: (0, h)), pl.BlockSpec((4, 128), lambda h, c: (0, 4 + h)),
             pl.BlockSpec((4, 128), lambda h, c: (0, 8 + h)), _full(dtb), _full(alog), _full(ng)]
    return ins, specs, cm


_GDN_CARRY = [(128, 128), (8, 128), (8, 128), (8, 128)]


def gdn_fwd(name, proj, small, cw, dtb, alog, ng):
    s = proj.shape[0]
    nc = s // CHUNK
    ins, specs, cm = _gdn_io(proj, small, cw, dtb, alog, ng, nc, False)
    outs = [_out((s, BW), pl.BlockSpec((CHUNK, 128), lambda h, c: (c, h)))]
    (y,), saved = run_fwd(name, _gdn_tile, (4, nc), ins, specs, outs, carry=dict(shapes=_GDN_CARRY, axis=1))
    return y, saved


def gdn_bwd(name, proj, small, cw, dtb, alog, ng, saved, gy):
    s = proj.shape[0]
    nc = s // CHUNK
    n = CHUNK
    ins, specs, cm = _gdn_io(proj, small, cw, dtb, alog, ng, nc, True)
    row = pl.BlockSpec((n, 128), lambda h, c: (cm(c), h))
    wants = [_want(i, (s, BW), row) for i in range(4)]
    wants.append(_want(4, (4, s, 16), pl.BlockSpec((None, n, 16), lambda h, c: (h, cm(c), 0))))
    for i in range(3):
        wants.append(_want(5 + i, (4, BW), pl.BlockSpec((4, 128), lambda h, c: (0, h)), acc=(1,)))
    wants += [_want(8, dtb.shape, _full(dtb), acc=(0, 1)), _want(9, alog.shape, _full(alog), acc=(0, 1)),
              _want(10, ng.shape, _full(ng), acc=(0, 1))]
    return run_bwd(name, _gdn_tile, (4, nc), ins, specs, [gy], [row], wants,
                   carry=dict(axis=1, saved=saved, rev=lambda g: (g[0], cm(g[1]))))


def _perm_cols(w):
    pad = jnp.zeros(w.shape[:-1] + (PW - IN_WIDTH,), w.dtype)
    return jnp.concatenate([w[..., :3072], w[..., 3080:5128], w[..., 3072:3080], w[..., 5128:5136], pad], axis=-1)


def _unperm_cols(g):
    return jnp.concatenate([g[..., :3072], g[..., 5120:5128], g[..., 3072:5120], g[..., 5128:5136]], axis=-1)


def _bd(blocks):
    n, a, b = blocks.shape
    eye = jnp.eye(n, dtype=blocks.dtype)
    return jnp.einsum("nab,nm->namb", blocks, eye).reshape(n * a, n * b)


def _layer_layout(lw):
    o = {}
    row = lambda a: a.reshape(1, -1)
    o["b_gate"] = row(lw["b_gate"])
    o["s5_are"] = lw["s5_a_re"].reshape(4, 1, 512)
    o["s5_aim"] = lw["s5_a_im"].reshape(4, 1, 512)
    o["s5_lstep"] = jnp.repeat(lw["s5_log_step"], 64).reshape(4, 1, 512)
    bt = lambda b: jax.vmap(_bd)(jnp.swapaxes(b, 1, 2).reshape(4, 8, 16, 64))
    o["s5_bre"], o["s5_bim"] = bt(lw["s5_b_re"]), bt(lw["s5_b_im"])
    ct = lambda c: jax.vmap(_bd)(jnp.swapaxes(c, 1, 2).reshape(4, 8, 64, 16))
    o["s5_cre"], o["s5_cim"] = ct(lw["s5_c_re"]), ct(lw["s5_c_im"])
    o["s5_d"] = row(lw["s5_d"])
    o["s5_b_glu"] = row(lw["s5_b_glu"])
    o["lru_conv_b"], o["lru_b_r"], o["lru_b_i"], o["lru_lambda"] = (row(lw[k]) for k in ("lru_conv_b", "lru_b_r", "lru_b_i", "lru_lambda"))
    o["lru_wr"], o["lru_wi"] = _bd(lw["lru_w_r"]), _bd(lw["lru_w_i"])
    cw, cb = lw["m2_conv_w"], lw["m2_conv_b"]
    o["m2_cwx"], o["m2_cwb"], o["m2_cwc"] = cw[:, :512], cw[:, 512:768], cw[:, 768:]
    o["m2_cbx"], o["m2_cbb"], o["m2_cbc"] = row(cb[:512]), row(cb[512:768]), row(cb[768:])
    o["m2_dtb"] = jnp.pad(lw["m2_dt_bias"], (0, 8)).reshape(1, 16)
    o["m2_alog"] = jnp.pad(lw["m2_a_log"], (0, 8)).reshape(1, 16)
    o["m2_dsk"] = jnp.repeat(lw["m2_d"], 64).reshape(1, 512)
    o["m2_norm"] = row(lw["m2_norm"])
    o["gdn_dtb"] = jnp.pad(lw["gdn_dt_bias"], (12, 0)).reshape(1, 16)
    o["gdn_alog"] = jnp.pad(lw["gdn_a_log"], (12, 0)).reshape(1, 16)
    o["gdn_norm"] = row(lw["gdn_norm"])
    for k in ("ffn1_norm", "mix_norm", "ffn2_norm", "ple_norm"):
        o[k] = row(lw[k])
    return o


_BIG_PLAIN = ("w_gate", "s5_w_glu", "w_branch", "w_out", "ple_w_gate", "ple_w_proj")


def big_layout(w):
    o = {k: w[k] for k in _BIG_PLAIN}
    for f in ("ffn1", "ffn2"):
        o[f + "_wg"], o[f + "_wu"] = w[f + "_w_in"][:, :FFN_DIM], w[f + "_w_in"][:, FFN_DIM:]
        o[f + "_wo"] = w[f + "_w_out"]
    o["w_in"] = _perm_cols(w["w_in"])
    return {k: v.astype(bf16) for k, v in o.items()}


def big_unlayout(g):
    o = {k: g[k] for k in _BIG_PLAIN}
    for f in ("ffn1", "ffn2"):
        o[f + "_w_in"] = jnp.concatenate([g[f + "_wg"], g[f + "_wu"]], axis=1)
        o[f + "_w_out"] = g[f + "_wo"]
    o["w_in"] = _unperm_cols(g["w_in"])
    return o


_SMALL_KEYS = ("b_gate", "s5_log_step", "s5_a_re", "s5_a_im", "s5_b_re", "s5_b_im", "s5_c_re", "s5_c_im", "s5_d",
               "s5_b_glu", "lru_conv_b", "lru_w_r", "lru_b_r", "lru_w_i", "lru_b_i", "lru_lambda", "m2_conv_w",
               "m2_conv_b", "m2_dt_bias", "m2_a_log", "m2_d", "m2_norm", "gdn_dt_bias", "gdn_a_log", "gdn_norm",
               "ffn1_norm", "mix_norm", "ffn2_norm", "ple_norm")


def _layer_fwd(i, h0, pe, lw, big):
    n = f"l{i}_"
    lay = _layer_layout(lw)
    a = {"h0": h0, "lay": lay}
    h1 = ffn_fwd(n + "ffn1_fwd", h0, lay["ffn1_norm"], big["ffn1_wg"], big["ffn1_wu"], big["ffn1_wo"])
    zero_b = jnp.zeros((1, PW), f32)
    proj = normlin_fwd(n + "inproj_fwd", h1, lay["mix_norm"], big["w_in"], zero_b, tn=896)
    gpre = normlin_fwd(n + "gate_fwd", h1, lay["mix_norm"], big["w_gate"], lay["b_gate"], tn=1024)
    small = proj[:, SMALL_OFF:SMALL_OFF + 16]
    raw = [lay["s5_are"], lay["s5_aim"], lay["s5_lstep"], lay["s5_bre"], lay["s5_bim"]]
    ops = s5_ops_fwd(n + "s5ops_fwd", raw)
    y5, sv5 = s5_fwd(n + "s5_fwd", proj, ops, lay["s5_cre"], lay["s5_cim"], lay["s5_d"])
    ya = s5_glu_fwd(n + "s5glu_fwd", y5, big["s5_w_glu"], lay["s5_b_glu"])
    lru_ws = [lw["lru_conv_w"], lay["lru_conv_b"], lay["lru_wr"], lay["lru_b_r"], lay["lru_wi"], lay["lru_b_i"], lay["lru_lambda"]]
    yb, svb = lru_fwd(n + "lru_fwd", proj, lru_ws)
    m2_ws = [lay[k] for k in ("m2_cwx", "m2_cwb", "m2_cwc", "m2_cbx", "m2_cbb", "m2_cbc", "m2_dtb", "m2_alog", "m2_dsk", "m2_norm")]
    yc, svc = m2_fwd(n + "m2_fwd", proj, small, m2_ws)
    yd, svd = gdn_fwd(n + "gdn_fwd", proj, small, lw["gdn_conv_w"], lay["gdn_dtb"], lay["gdn_alog"], lay["gdn_norm"])
    ys = jnp.concatenate([ya, yb, yc, yd], axis=1)
    h2 = merge_fwd(n + "merge_fwd", h1, ys, gpre, big["w_branch"], big["w_out"])
    h3 = ffn_fwd(n + "ffn2_fwd", h2, lay["ffn2_norm"], big["ffn2_wg"], big["ffn2_wu"], big["ffn2_wo"])
    h4 = ple_fwd(n + "ple_fwd", h3, pe, lay["ple_norm"], big["ple_w_gate"], big["ple_w_proj"])
    a.update(h1=h1, proj=proj, gpre=gpre, small=small, raw=raw, ops=ops, y5=y5, sv5=sv5, lru_ws=lru_ws, svb=svb,
             m2_ws=m2_ws, svc=svc, svd=svd, ys=ys, h2=h2, h3=h3, pe=pe)
    return h4, a


def _layer_bwd(i, a, lw, big, gh4):
    n = f"l{i}_"
    lay = a["lay"]
    gb, gl = {}, {}
    s = gh4.shape[0]
    t = 512
    dh3, gl["ple_norm"], gb["ple_w_gate"], gb["ple_w_proj"] = ple_bwd(
        n + "ple_bwd", a["h3"], a["pe"], lay["ple_norm"], big["ple_w_gate"], big["ple_w_proj"], gh4)
    gh3 = addn(n + "gh3", [(gh4, None), (dh3, None)], t)
    dh2, gl["ffn2_norm"], gb["ffn2_wg"], gb["ffn2_wu"], gb["ffn2_wo"] = ffn_bwd(
        n + "ffn2_bwd", a["h2"], lay["ffn2_norm"], big["ffn2_wg"], big["ffn2_wu"], big["ffn2_wo"], gh3)
    gh2 = addn(n + "gh2", [(gh3, None)] + _parts(dh2), t)
    gys, ggpre, gb["w_branch"], gb["w_out"] = merge_bwd(
        n + "merge_bwd", a["h1"], a["ys"], a["gpre"], big["w_branch"], big["w_out"], gh2)
    gya, gyb, gyc, gyd = (gys[:, k * BW:(k + 1) * BW] for k in range(4))
    gy5, gb["s5_w_glu"], gl["s5_b_glu"] = s5_glu_bwd(n + "s5glu_bwd", a["y5"], big["s5_w_glu"], lay["s5_b_glu"], gya)
    r5 = s5_bwd(n + "s5_bwd", a["proj"], a["ops"], lay["s5_cre"], lay["s5_cim"], lay["s5_d"], a["sv5"], gy5)
    du5, gops, gl["s5_cre"], gl["s5_cim"], gl["s5_d"] = r5[0], r5[1:7], r5[7], r5[8], r5[9]
    gl["s5_are"], gl["s5_aim"], gl["s5_lstep"], gl["s5_bre"], gl["s5_bim"] = s5_ops_bwd(n + "s5ops_bwd", a["raw"], gops)
    rb = lru_bwd(n + "lru_bwd", a["proj"], a["lru_ws"], a["svb"], gyb)
    dxb, dgl = rb[0], rb[1]
    gl["lru_conv_w"], gl["lru_conv_b"], gl["lru_wr"], gl["lru_b_r"], gl["lru_wi"], gl["lru_b_i"], gl["lru_lambda"] = rb[2:]
    rc = m2_bwd(n + "m2_bwd", a["proj"], a["small"], a["m2_ws"], a["svc"], gyc)
    dz, dxs, dbr, dcr, dsm_c = rc[:5]
    for k, v in zip(("m2_cwx", "m2_cwb", "m2_cwc", "m2_cbx", "m2_cbb", "m2_cbc", "m2_dtb", "m2_alog", "m2_dsk", "m2_norm"), rc[5:]):
        gl[k] = v
    rd = gdn_bwd(n + "gdn_bwd", a["proj"], a["small"], lw["gdn_conv_w"], lay["gdn_dtb"], lay["gdn_alog"], lay["gdn_norm"], a["svd"], gyd)
    dq, dk, dv, dgg, dsm_d = rd[:5]
    gl["gdn_conv_w"] = jnp.concatenate(rd[5:8], axis=1)
    gl["gdn_dtb"], gl["gdn_alog"], gl["gdn_norm"] = rd[8:]
    dsm = addn(n + "gsmall", [(dsm_c, None)] + _parts(dsm_d), 512)
    gproj = jnp.concatenate([du5, dxb, dgl, dz, dxs, dbr, dcr, dq, dk, dv, dgg, dsm, jnp.zeros((s, PW - SMALL_OFF - 16), f32)], axis=1)
    zero_b = jnp.zeros((1, PW), f32)
    dh1_p, gmn_p, gb["w_in"], _ = normlin_bwd(n + "inproj_bwd", a["h1"], lay["mix_norm"], big["w_in"], zero_b, gproj, tn=896)
    dh1_g, gmn_g, gb["w_gate"], gl["b_gate"] = normlin_bwd(n + "gate_bwd", a["h1"], lay["mix_norm"], big["w_gate"], lay["b_gate"], ggpre, tn=1024)
    gh1 = addn(n + "gh1", [(gh2, None)] + _parts(dh1_p) + _parts(dh1_g), t)
    dh0, gl["ffn1_norm"], gb["ffn1_wg"], gb["ffn1_wu"], gb["ffn1_wo"] = ffn_bwd(
        n + "ffn1_bwd", a["h0"], lay["ffn1_norm"], big["ffn1_wg"], big["ffn1_wu"], big["ffn1_wo"], gh1)
    gh0 = addn(n + "gh0", [(gh1, None)] + _parts(dh0), t)
    gl["mix_norm"] = gmn_p + gmn_g
    return gh0, gb, gl


def _local_step(x, p, target, bigs, smalls, final_norm):
    h = x
    acts = []
    for i in range(DEPTH):
        h, a = _layer_fwd(i, h, p[i], smalls[i], bigs[i])
        acts.append(a)
    fg = final_norm.reshape(1, -1)
    loss, gh, gfn = final_loss("final_loss", h, fg, target)
    gbs, gss = [None] * DEPTH, [None] * DEPTH
    for i in reversed(range(DEPTH)):
        gh, gb, gl = _layer_bwd(i, acts[i], smalls[i], bigs[i], gh)
        _, pull = jax.vjp(_layer_layout, smalls[i])
        lay_g = {k: gl[k] for k in acts[i]["lay"]}
        gs = pull(lay_g)[0]
        gs = dict(gs)
        gs["lru_conv_w"] = gl["lru_conv_w"]
        gs["gdn_conv_w"] = gl["gdn_conv_w"]
        gbs[i], gss[i] = gb, gs
    return loss, gh, gbs, gss, gfn.reshape(-1)


MESH_AXES = ("x", "y", "c")
PACK_W = 1024
PACK_ROWS = 256

W_NAMES = ("ffn1_norm", "ffn1_w_in", "ffn1_w_out", "mix_norm", "w_in", "w_gate", "b_gate", "s5_log_step", "s5_a_re",
           "s5_a_im", "s5_b_re", "s5_b_im", "s5_c_re", "s5_c_im", "s5_d", "s5_w_glu", "s5_b_glu", "lru_conv_w",
           "lru_conv_b", "lru_w_r", "lru_b_r", "lru_w_i", "lru_b_i", "lru_lambda", "m2_conv_w", "m2_conv_b",
           "m2_dt_bias", "m2_a_log", "m2_d", "m2_norm", "gdn_conv_w", "gdn_dt_bias", "gdn_a_log", "gdn_norm",
           "w_branch", "w_out", "ffn2_norm", "ffn2_w_in", "ffn2_w_out", "ple_norm", "ple_w_gate", "ple_w_proj",
           "final_norm")
COL_SHARDED = ("ffn1_w_in", "w_in", "w_gate", "lru_conv_w", "m2_conv_w", "gdn_conv_w", "w_branch", "ffn2_w_in", "ple_w_proj")
ROW_SHARDED = ("ffn1_w_out", "s5_w_glu", "w_out", "ffn2_w_out", "ple_w_gate")
BIG_NAMES = ("ffn1_w_in", "ffn1_w_out", "w_in", "w_gate", "s5_w_glu", "w_branch", "w_out", "ffn2_w_in", "ffn2_w_out",
             "ple_w_gate", "ple_w_proj")
CONV_NAMES = ("lru_conv_w", "m2_conv_w", "gdn_conv_w")
SHARDED = BIG_NAMES + CONV_NAMES
REPLICATED = tuple(k for k in W_NAMES if k not in SHARDED)


def _gathered_to_full(name, g):
    if name in COL_SHARDED:
        g = jnp.moveaxis(g, 0, -2)
        return g.reshape(g.shape[:-2] + (g.shape[-2] * g.shape[-1],))
    g = jnp.moveaxis(g, 0, 1)
    return g.reshape((g.shape[0], g.shape[1] * g.shape[2]) + g.shape[3:])


def _full_to_scattered(name, w):
    if name in COL_SHARDED:
        w = w.reshape(w.shape[:-1] + (N_DEV, w.shape[-1] // N_DEV))
        return jnp.moveaxis(w, -2, 0)
    w = w.reshape((w.shape[0], N_DEV, w.shape[1] // N_DEV) + w.shape[2:])
    return jnp.moveaxis(w, 1, 0)


def _pack(arrs, lead=0):
    lshape = arrs[0].shape[:lead]
    flat = jnp.concatenate([a.reshape(lshape + (-1,)) for a in arrs], axis=-1)
    n = flat.shape[-1]
    rows = -(-n // (PACK_W * PACK_ROWS)) * PACK_ROWS
    flat = jnp.pad(flat, [(0, 0)] * lead + [(0, rows * PACK_W - n)])
    return flat.reshape(lshape + (rows, PACK_W))


def _unpack(buf, shapes, lead=0):
    lshape = buf.shape[:lead]
    flat = buf.reshape(lshape + (-1,))
    out, off = [], 0
    for sh in shapes:
        n = math.prod(sh)
        out.append(flat[..., off:off + n].reshape(lshape + tuple(sh)))
        off += n
    return out


def _peer(k):
    mx, my, mc = (lax.axis_index(a) for a in MESH_AXES)
    px = 1 - mx if k & 4 else mx
    py = 1 - my if k & 2 else my
    pc = 1 - mc if k & 1 else mc
    return (px, py, pc), 4 * px + 2 * py + pc


_ANY = pl.BlockSpec(memory_space=pl.ANY)


def all_gather(name, x):
    r, c = x.shape

    def body(x_ref, out_ref, send_sems, recv_sems, local_sem):
        mx, my, mc = (lax.axis_index(a) for a in MESH_AXES)
        me, sibling = (mx, my, mc), (mx, my, 1 - mc)
        chips = [(1 - mx, my), (mx, 1 - my), (1 - mx, 1 - my)]

        def slot(px, py, pc):
            return out_ref.at[4 * px + 2 * py + pc]

        def copy(k, block, to, src=None):
            return pltpu.make_async_remote_copy(
                src_ref=slot(*block) if src is None else src, dst_ref=slot(*block),
                send_sem=send_sems.at[k], recv_sem=recv_sems.at[k], device_id=to, device_id_type=pl.DeviceIdType.MESH)

        mine = pltpu.make_async_copy(x_ref, slot(*me), local_sem)
        mine.start()
        first = [copy(0, me, sibling, src=x_ref)]
        first += [copy(1 + j, me, (*chip, mc), src=x_ref) for j, chip in enumerate(chips)]
        for cp in first:
            cp.start()
        passed = [copy(4 + j, (*chip, mc), sibling) for j, chip in enumerate(chips)]
        for j, chip in enumerate(chips):
            copy(1 + j, (*chip, mc), me).wait_recv()
            passed[j].start()
        copy(0, sibling, me).wait_recv()
        for j, chip in enumerate(chips):
            copy(4 + j, (*chip, 1 - mc), me).wait_recv()
        for cp in first + passed:
            cp.wait_send()
        mine.wait()

    return pl.pallas_call(
        body, name=name, out_shape=jax.ShapeDtypeStruct((N_DEV, r, c), x.dtype), in_specs=[_ANY], out_specs=_ANY,
        scratch_shapes=[pltpu.SemaphoreType.DMA((7,)), pltpu.SemaphoreType.DMA((7,)), pltpu.SemaphoreType.DMA(())],
    )(x)


def exchange(name, g):
    _, r, c = g.shape

    def body(g_ref, out_ref, send_sems, recv_sems, local_sem):
        mx, my, mc = (lax.axis_index(a) for a in MESH_AXES)
        me = 4 * mx + 2 * my + mc
        mine = pltpu.make_async_copy(g_ref.at[me], out_ref.at[me], local_sem)
        mine.start()
        copies = []
        for k in range(1, N_DEV):
            peer, pidx = _peer(k)
            cp = pltpu.make_async_remote_copy(
                src_ref=g_ref.at[pidx], dst_ref=out_ref.at[me], send_sem=send_sems.at[k - 1], recv_sem=recv_sems.at[k - 1],
                device_id=peer, device_id_type=pl.DeviceIdType.MESH)
            cp.start()
            copies.append(cp)
        for k in range(1, N_DEV):
            peer, pidx = _peer(k)
            pltpu.make_async_remote_copy(
                src_ref=g_ref.at[pidx], dst_ref=out_ref.at[pidx], send_sem=send_sems.at[k - 1], recv_sem=recv_sems.at[k - 1],
                device_id=peer, device_id_type=pl.DeviceIdType.MESH).wait_recv()
        for cp in copies:
            cp.wait_send()
        mine.wait()

    return pl.pallas_call(
        body, name=name, out_shape=jax.ShapeDtypeStruct(g.shape, g.dtype), in_specs=[_ANY], out_specs=_ANY,
        scratch_shapes=[pltpu.SemaphoreType.DMA((7,)), pltpu.SemaphoreType.DMA((7,)), pltpu.SemaphoreType.DMA(())],
    )(g)


def sum_slots(name, buf):
    return addn(name, _parts(buf), PACK_ROWS)


def adamw(name, w, g, m, v):
    def fn(ww, gg, mm_, vv):
        m2 = ADAM_B1 * mm_ + (1.0 - ADAM_B1) * gg
        v2 = ADAM_B2 * vv + (1.0 - ADAM_B2) * (gg * gg)
        m_hat = m2 / (1.0 - ADAM_B1 ** ADAM_STEP)
        v_hat = v2 / (1.0 - ADAM_B2 ** ADAM_STEP)
        delta = -ADAM_LR * (m_hat / (jnp.sqrt(v_hat) + ADAM_EPS) + ADAM_WD * ww)
        return delta, m2, v2

    spec = pl.BlockSpec((PACK_ROWS, PACK_W), lambda i: (i, 0))
    return run_fwd(name, fn, (w.shape[0] // PACK_ROWS,), [w, g, m, v], [spec] * 4, [_out(w.shape, spec)] * 3)[0]


def kernel(*args):
    nw = len(W_NAMES)
    x, p = args[0], args[1]
    w = dict(zip(W_NAMES, args[2:2 + nw]))
    target = args[2 + nw]
    m = dict(zip(W_NAMES, args[3 + nw:3 + 2 * nw]))
    v = dict(zip(W_NAMES, args[3 + 2 * nw:3 + 3 * nw]))

    big_sh = all_gather("ag_big", _pack([w[k].astype(bf16) for k in BIG_NAMES]))
    conv_sh = all_gather("ag_conv", _pack([w[k] for k in CONV_NAMES]))
    full = {}
    for k, g in zip(BIG_NAMES, _unpack(big_sh, [w[k].shape for k in BIG_NAMES], lead=1)):
        full[k] = _gathered_to_full(k, g)
    for k, g in zip(CONV_NAMES, _unpack(conv_sh, [w[k].shape for k in CONV_NAMES], lead=1)):
        full[k] = _gathered_to_full(k, g)

    bigs = [big_layout({k: full[k][i] for k in BIG_NAMES}) for i in range(DEPTH)]
    smalls = []
    for i in range(DEPTH):
        sm = {k: w[k][i] for k in _SMALL_KEYS if k not in CONV_NAMES}
        sm.update({k: full[k][i] for k in CONV_NAMES})
        smalls.append(sm)
    loss, gx, gbs, gss, gfn = _local_step(x[0], p[:, 0], target[0], bigs, smalls, w["final_norm"])

    gfull = [dict(big_unlayout(gbs[i]), **gss[i]) for i in range(DEPTH)]
    stack = lambda k: jnp.stack([gfull[i][k] for i in range(DEPTH)])
    g_sh = sum_slots("sum_sharded", exchange("rs_sharded", _pack([_full_to_scattered(k, stack(k)) for k in SHARDED], lead=1)))
    g_rep = sum_slots("sum_replicated", all_gather("ag_replicated", _pack([gfn if k == "final_norm" else stack(k) for k in REPLICATED])))
    loss = lax.psum(loss[0, 0], MESH_AXES)

    outs = {}
    for names, g in ((SHARDED, g_sh), (REPLICATED, g_rep)):
        shapes = [w[k].shape for k in names]
        res = adamw("adamw_" + ("sharded" if names is SHARDED else "replicated"),
                    _pack([w[k] for k in names]), g, _pack([m[k] for k in names]), _pack([v[k] for k in names]))
        for kind, buf in zip(("grad", "delta", "new_m", "new_v"), [g] + list(res)):
            for k, a in zip(names, _unpack(buf, shapes)):
                outs[kind + "_" + k] = a
    return (loss, gx[None]) + tuple(outs[kind + "_" + k] for kind in ("grad", "delta", "new_m", "new_v") for k in W_NAMES)
```

```python
import functools
import math

import jax
import jax.numpy as jnp
from jax import lax
from jax.experimental import pallas as pl
from jax.experimental.pallas import tpu as pltpu

f32 = jnp.float32
bf16 = jnp.bfloat16
HI = lax.Precision.HIGHEST

EPS = 1e-6
DEPTH = 2
D_MODEL = 1024
FFN_DIM = 2816
BW = 512
IN_WIDTH = 5136
PW = 5376
SMALL_OFF = 5120
CHUNK = 64
LS = 256
LRU_C = 8.0
N_DEV = 8
VMEM_LIMIT_BYTES = 56 * 1024 * 1024

ADAM_LR, ADAM_B1, ADAM_B2, ADAM_EPS, ADAM_WD, ADAM_STEP = 0.001, 0.9, 0.999, 1e-08, 0.01, 10


def _dg(a, b, dims, hi):
    if hi:
        return lax.dot_general(a, b, (dims, ((), ())), precision=HI, preferred_element_type=f32)
    return lax.dot_general(a.astype(bf16), b.astype(bf16), (dims, ((), ())), preferred_element_type=f32)


def _make_mm(hi):
    @jax.custom_vjp
    def nn(a, b):
        return _dg(a, b, ((1,), (0,)), hi)

    @jax.custom_vjp
    def nt(a, b):
        return _dg(a, b, ((1,), (1,)), hi)

    @jax.custom_vjp
    def tn(a, b):
        return _dg(a, b, ((0,), (0,)), hi)

    nn.defvjp(lambda a, b: (nn(a, b), (a, b)), lambda r, g: (nt(g, r[1]), tn(r[0], g)))
    nt.defvjp(lambda a, b: (nt(a, b), (a, b)), lambda r, g: (nn(g, r[1]), tn(g, r[0])))
    tn.defvjp(lambda a, b: (tn(a, b), (a, b)), lambda r, g: (nt(r[1], g), nn(r[0], g)))
    return nn, nt, tn


mm, mm_nt, mm_tn = _make_mm(False)
hmm, hmm_nt, hmm_tn = _make_mm(True)


def _rows(shape):
    return lax.broadcasted_iota(jnp.int32, shape, 0)


def _lanes(shape):
    return lax.broadcasted_iota(jnp.int32, shape, 1)


def _rms(x, g):
    return x * lax.rsqrt(jnp.mean(x * x, axis=-1, keepdims=True) + EPS) * g


def _silu(x):
    return x * jax.nn.sigmoid(x)


def _gelu(x):
    return 0.5 * x * (1.0 + jnp.tanh(0.7978845608028654 * (x + 0.044715 * x * x * x)))


def _softplus(x):
    return jnp.maximum(x, 0.0) + jnp.log1p(jnp.exp(-jnp.abs(x)))


def _expm1(x):
    p = x * (1.0 + x * (0.5 + x * (1.0 / 6 + x * (1.0 / 24 + x * (1.0 / 120 + x * (1.0 / 720 + x * (1.0 / 5040)))))))
    return jnp.where(x > -0.3, p, jnp.exp(x) - 1.0)


def _pick_row(x, r):
    return jnp.sum(jnp.where(_rows(x.shape) == r, x, 0.0), axis=0, keepdims=True)


def _pick_lane(x, c):
    return jnp.sum(jnp.where(_lanes(x.shape) == c, x, 0.0), axis=1, keepdims=True)


def _shift_up(g, j):
    n = g.shape[0]
    return jnp.where(_rows(g.shape) < n - j, pltpu.roll(g, n - j, 0), 0.0)


@functools.partial(jax.custom_vjp, nondiff_argnums=(1, 2))
def _shift(x, j, fill):
    return jnp.where(_rows(x.shape) >= j, pltpu.roll(x, j, 0), fill)


_shift.defvjp(lambda x, j, fill: (_shift(x, j, fill), None), lambda j, fill, _, g: (_shift_up(g, j),))


@functools.partial(jax.custom_vjp, nondiff_argnums=(2,))
def _shift_halo(x, prev8, j):
    xr = pltpu.roll(x, j, 0)
    pr = pltpu.roll(prev8, j, 0)
    top = jnp.where(_rows(pr.shape) < j, pr, xr[:8])
    return jnp.concatenate([top, xr[8:]], axis=0)


def _shift_halo_bwd(j, _, g):
    g8 = g[:8]
    dprev = jnp.where(_rows(g8.shape) >= 8 - j, pltpu.roll(g8, 8 - j, 0), 0.0)
    return _shift_up(g, j), dprev


_shift_halo.defvjp(lambda x, p, j: (_shift_halo(x, p, j), None), _shift_halo_bwd)


def _conv4(x, prev8, w, b):
    y = _pick_row(w, 3) * x
    for k in range(3):
        y = y + _pick_row(w, k) * _shift_halo(x, prev8, 3 - k)
    return y if b is None else y + b


def _cmul(ar, ai, br, bi):
    return ar * br - ai * bi, ar * bi + ai * br


def _params(grid):
    return pltpu.CompilerParams(dimension_semantics=("arbitrary",) * len(grid), vmem_limit_bytes=VMEM_LIMIT_BYTES)


def _first(axes):
    ok = pl.program_id(axes[0]) == 0
    for a in axes[1:]:
        ok = jnp.logical_and(ok, pl.program_id(a) == 0)
    return ok


def _store(ref, val, acc):
    val = val.astype(ref.dtype)
    if acc is None:
        ref[...] = val
        return
    first = _first(acc)

    @pl.when(first)
    def _():
        ref[...] = val

    @pl.when(jnp.logical_not(first))
    def _():
        ref[...] += val


def _full(a):
    nd = a.ndim
    return pl.BlockSpec(a.shape, lambda *g: (0,) * nd)


def _out(shape, spec, acc=None, dtype=f32):
    return dict(shape=tuple(shape), spec=spec, acc=acc, dtype=dtype)


def run_fwd(name, fn, grid, ins, in_specs, outs, carry=None):
    n_in, n_out = len(ins), len(outs)
    cshapes = carry["shapes"] if carry else []
    nc = len(cshapes)
    ng = len(grid)

    def body(*refs):
        in_refs = refs[:n_in]
        out_refs = refs[n_in:n_in + n_out]
        save_refs = refs[n_in + n_out:n_in + n_out + nc]
        c_refs = refs[n_in + n_out + nc:]
        vals = [r[...] for r in in_refs]
        if carry:
            @pl.when(pl.program_id(carry["axis"]) == 0)
            def _():
                for c in c_refs:
                    c[...] = jnp.zeros(c.shape, f32)
            cin = tuple(c[...] for c in c_refs)
            for s, v in zip(save_refs, cin):
                s[...] = v
            cout, res = fn(cin, *vals)
            for c, v in zip(c_refs, cout):
                c[...] = v
        else:
            res = fn(*vals)
        for o, r, d in zip(out_refs, res, outs):
            _store(o, r, d["acc"])

    out_shape = [jax.ShapeDtypeStruct(d["shape"], d["dtype"]) for d in outs]
    out_specs = [d["spec"] for d in outs]
    for cs in cshapes:
        out_shape.append(jax.ShapeDtypeStruct(tuple(grid) + tuple(cs), f32))
        out_specs.append(pl.BlockSpec((None,) * ng + tuple(cs), lambda *g, _n=len(cs): tuple(g) + (0,) * _n))
    res = pl.pallas_call(
        body, name=name, grid=grid, in_specs=list(in_specs), out_specs=out_specs, out_shape=out_shape,
        scratch_shapes=[pltpu.VMEM(tuple(cs), f32) for cs in cshapes], compiler_params=_params(grid),
    )(*ins)
    return list(res[:n_out]), list(res[n_out:])


def run_bwd(name, fn, grid, ins, in_specs, gouts, gout_specs, wants, carry=None):
    n_in, n_g, n_w = len(ins), len(gouts), len(wants)
    saved = carry["saved"] if carry else []
    nc = len(saved)
    ng = len(grid)

    def body(*refs):
        in_refs = refs[:n_in]
        g_refs = refs[n_in:n_in + n_g]
        s_refs = refs[n_in + n_g:n_in + n_g + nc]
        w_refs = refs[n_in + n_g + nc:n_in + n_g + nc + n_w]
        dc_refs = refs[n_in + n_g + nc + n_w:]
        vals = [r[...].astype(f32) for r in in_refs]
        gs = tuple(r[...].astype(f32) for r in g_refs)
        if carry:
            @pl.when(pl.program_id(carry["axis"]) == 0)
            def _():
                for c in dc_refs:
                    c[...] = jnp.zeros(c.shape, f32)
            cin = tuple(s[...] for s in s_refs)
            _, vjp = jax.vjp(fn, cin, *vals)
            grads = vjp((tuple(c[...] for c in dc_refs), gs))
            for c, v in zip(dc_refs, grads[0]):
                c[...] = v
            dvals = grads[1:]
        else:
            _, vjp = jax.vjp(fn, *vals)
            dvals = vjp(gs)
        for o, d in zip(w_refs, wants):
            _store(o, dvals[d["idx"]], d["acc"])

    rev = carry["rev"] if carry else None
    s_specs = []
    for a in saved:
        n = a.ndim - ng
        s_specs.append(pl.BlockSpec((None,) * ng + tuple(a.shape[ng:]), lambda *g, _n=n: tuple(rev(g)) + (0,) * _n))
    res = pl.pallas_call(
        body, name=name, grid=grid, in_specs=list(in_specs) + list(gout_specs) + s_specs,
        out_specs=[d["spec"] for d in wants],
        out_shape=[jax.ShapeDtypeStruct(d["shape"], d["dtype"]) for d in wants],
        scratch_shapes=[pltpu.VMEM(tuple(a.shape[ng:]), f32) for a in saved], compiler_params=_params(grid),
    )(*ins, *gouts, *saved)
    return list(res)


def _want(idx, shape, spec, acc=None):
    d = _out(shape, spec, acc)
    d["idx"] = idx
    return d


def addn(name, items, t):
    s, w = items[0][0].shape[-2:]
    specs = []
    for a, j in items:
        if j is None:
            specs.append(pl.BlockSpec((t, w), lambda i: (i, 0)))
        else:
            specs.append(pl.BlockSpec((None, t, w), lambda i, _j=j: (_j, i, 0)))

    def fn(*xs):
        y = xs[0]
        for x in xs[1:]:
            y = y + x
        return (y,)

    return run_fwd(name, fn, (s // t,), [a for a, _ in items], specs,
                   [_out((s, w), pl.BlockSpec((t, w), lambda i: (i, 0)))])[0][0]


def _parts(a):
    return [(a, j) for j in range(a.shape[0])]


def _ffn_tile(j_axis, residual):
    def fn(h, g, wg, wu, wo):
        n = _rms(h, g)
        act = _silu(mm(n, wg)) * mm(n, wu)
        y = 0.5 * mm(act, wo)
        if residual:
            y = y + (pl.program_id(j_axis) == 0).astype(f32) * h
        return (y,)
    return fn


def ffn_fwd(name, h, g, wg, wu, wo, t=512, tf=1408):
    s, d = h.shape
    f = wg.shape[1]
    specs = [pl.BlockSpec((t, d), lambda i, j: (i, 0)), _full(g), pl.BlockSpec((d, tf), lambda i, j: (0, j)),
             pl.BlockSpec((d, tf), lambda i, j: (0, j)), pl.BlockSpec((tf, d), lambda i, j: (j, 0))]
    outs = [_out((s, d), pl.BlockSpec((t, d), lambda i, j: (i, 0)), acc=(1,))]
    return run_fwd(name, _ffn_tile(1, True), (s // t, f // tf), [h, g, wg, wu, wo], specs, outs)[0][0]


def ffn_bwd(name, h, g, wg, wu, wo, gout, t=512, tf=256):
    s, d = h.shape
    f = wg.shape[1]
    nj = f // tf
    specs = [pl.BlockSpec((t, d), lambda j, i: (i, 0)), _full(g), pl.BlockSpec((d, tf), lambda j, i: (0, j)),
             pl.BlockSpec((d, tf), lambda j, i: (0, j)), pl.BlockSpec((tf, d), lambda j, i: (j, 0))]
    wants = [_want(0, (nj, s, d), pl.BlockSpec((None, t, d), lambda j, i: (j, i, 0))),
             _want(1, g.shape, _full(g), acc=(0, 1)),
             _want(2, wg.shape, pl.BlockSpec((d, tf), lambda j, i: (0, j)), acc=(1,)),
             _want(3, wu.shape, pl.BlockSpec((d, tf), lambda j, i: (0, j)), acc=(1,)),
             _want(4, wo.shape, pl.BlockSpec((tf, d), lambda j, i: (j, 0)), acc=(1,))]
    return run_bwd(name, _ffn_tile(0, False), (nj, s // t), [h, g, wg, wu, wo], specs,
                   [gout], [pl.BlockSpec((t, d), lambda j, i: (i, 0))], wants)


def _normlin_tile(h, g, w, b):
    return (mm(_rms(h, g), w) + b,)


def normlin_fwd(name, h, g, w, b, t=512, tn=1024):
    s, d = h.shape
    n = w.shape[1]
    specs = [pl.BlockSpec((t, d), lambda i, j: (i, 0)), _full(g), pl.BlockSpec((d, tn), lambda i, j: (0, j)),
             pl.BlockSpec((1, tn), lambda i, j: (0, j))]
    outs = [_out((s, n), pl.BlockSpec((t, tn), lambda i, j: (i, j)))]
    return run_fwd(name, _normlin_tile, (s // t, n // tn), [h, g, w, b], specs, outs)[0][0]


def normlin_bwd(name, h, g, w, b, gout, t=256, tn=1024):
    s, d = h.shape
    n = w.shape[1]
    nj = n // tn
    specs = [pl.BlockSpec((t, d), lambda j, i: (i, 0)), _full(g), pl.BlockSpec((d, tn), lambda j, i: (0, j)),
             pl.BlockSpec((1, tn), lambda j, i: (0, j))]
    wants = [_want(0, (nj, s, d), pl.BlockSpec((None, t, d), lambda j, i: (j, i, 0))),
             _want(1, g.shape, _full(g), acc=(0, 1)),
             _want(2, w.shape, pl.BlockSpec((d, tn), lambda j, i: (0, j)), acc=(1,)),
             _want(3, b.shape, pl.BlockSpec((1, tn), lambda j, i: (0, j)), acc=(1,))]
    return run_bwd(name, _normlin_tile, (nj, s // t), [h, g, w, b], specs,
                   [gout], [pl.BlockSpec((t, tn), lambda j, i: (i, j))], wants)


def _merge_tile(n_axis, residual):
    def fn(h, y, gp, wb, wo):
        part = mm(jax.nn.sigmoid(gp) * mm(y, wb), wo)
        if residual:
            part = part + (pl.program_id(n_axis) == 0).astype(f32) * h
        return (part,)
    return fn


def merge_fwd(name, h, ys, gpre, wb, wo, t=512):
    s, d = h.shape
    specs = [pl.BlockSpec((t, d), lambda i, n: (i, 0)), pl.BlockSpec((t, BW), lambda i, n: (i, n)),
             pl.BlockSpec((t, d), lambda i, n: (i, n)), pl.BlockSpec((None, BW, d), lambda i, n: (n, 0, 0)), _full(wo)]
    outs = [_out((s, d), pl.BlockSpec((t, d), lambda i, n: (i, 0)), acc=(1,))]
    return run_fwd(name, _merge_tile(1, True), (s // t, 4), [h, ys, gpre, wb, wo], specs, outs)[0][0]


def merge_bwd(name, h, ys, gpre, wb, wo, gout, t=256):
    s, d = h.shape
    specs = [pl.BlockSpec((t, d), lambda n, i: (i, 0)), pl.BlockSpec((t, BW), lambda n, i: (i, n)),
             pl.BlockSpec((t, d), lambda n, i: (i, n)), pl.BlockSpec((None, BW, d), lambda n, i: (n, 0, 0)), _full(wo)]
    wants = [_want(1, ys.shape, pl.BlockSpec((t, BW), lambda n, i: (i, n))),
             _want(2, gpre.shape, pl.BlockSpec((t, d), lambda n, i: (i, n))),
             _want(3, wb.shape, pl.BlockSpec((None, BW, d), lambda n, i: (n, 0, 0)), acc=(1,)),
             _want(4, wo.shape, _full(wo), acc=(0, 1))]
    return run_bwd(name, _merge_tile(0, False), (4, s // t), [h, ys, gpre, wb, wo], specs,
                   [gout], [pl.BlockSpec((t, d), lambda n, i: (i, 0))], wants)


def _ple_tile(residual):
    def fn(h, pe, g, wgate, wproj):
        y = jax.nn.sigmoid(mm(_rms(h, g), wgate)) * mm(pe, wproj)
        return (y + h,) if residual else (y,)
    return fn


def ple_fwd(name, h, pe, g, wgate, wproj, t=512):
    s, d = h.shape
    specs = [pl.BlockSpec((t, d), lambda i: (i, 0)), pl.BlockSpec((t, pe.shape[1]), lambda i: (i, 0)),
             _full(g), _full(wgate), _full(wproj)]
    return run_fwd(name, _ple_tile(True), (s // t,), [h, pe, g, wgate, wproj], specs,
                   [_out((s, d), pl.BlockSpec((t, d), lambda i: (i, 0)))])[0][0]


def ple_bwd(name, h, pe, g, wgate, wproj, gout, t=256):
    s, d = h.shape
    specs = [pl.BlockSpec((t, d), lambda i: (i, 0)), pl.BlockSpec((t, pe.shape[1]), lambda i: (i, 0)),
             _full(g), _full(wgate), _full(wproj)]
    wants = [_want(0, h.shape, pl.BlockSpec((t, d), lambda i: (i, 0))), _want(2, g.shape, _full(g), acc=(0,)),
             _want(3, wgate.shape, _full(wgate), acc=(0,)), _want(4, wproj.shape, _full(wproj), acc=(0,))]
    return run_bwd(name, _ple_tile(False), (s // t,), [h, pe, g, wgate, wproj], specs,
                   [gout], [pl.BlockSpec((t, d), lambda i: (i, 0))], wants)


def final_loss(name, h, g, target, t=512):
    s, d = h.shape

    def fn(hh, gg, tt):
        def loss_fn(a, b):
            err = _rms(a, b) - tt
            return 0.5 * jnp.sum(jnp.mean(err * err, axis=-1, keepdims=True), axis=0, keepdims=True)
        loss, vjp = jax.vjp(loss_fn, hh, gg)
        dh, dgain = vjp(jnp.ones((1, 1), f32))
        return loss, dh, dgain

    specs = [pl.BlockSpec((t, d), lambda i: (i, 0)), _full(g), pl.BlockSpec((t, d), lambda i: (i, 0))]
    outs = [_out((1, 1), pl.BlockSpec((1, 1), lambda i: (0, 0)), acc=(0,)),
            _out((s, d), pl.BlockSpec((t, d), lambda i: (i, 0))), _out(g.shape, _full(g), acc=(0,))]
    return run_fwd(name, fn, (s // t,), [h, g, target], specs, outs)[0]


def _rev(nc, rev):
    return (lambda c: nc - 1 - c) if rev else (lambda c: c)


def _s5_ops_tile(are, aim, lstep, bre, bim):
    step = jnp.exp(lstep)
    mag = jnp.exp(are * step)
    ab_re, ab_im = mag * jnp.cos(aim * step), mag * jnp.sin(aim * step)
    den = are * are + aim * aim
    num_re = ab_re - 1.0
    f_re = (num_re * are + ab_im * aim) / den
    f_im = (ab_im * are - num_re * aim) / den
    bb_re = f_re * bre - f_im * bim
    bb_im = f_re * bim + f_im * bre
    pr = jnp.broadcast_to(ab_re, (LS, ab_re.shape[1]))
    pi = jnp.broadcast_to(ab_im, (LS, ab_im.shape[1]))
    k = 1
    while k < LS:
        pr, pi = _cmul(pr, pi, _shift(pr, k, 1.0), _shift(pi, k, 0.0))
        k *= 2
    return ab_re, ab_im, bb_re, bb_im, pr, pi


def _s5_ops_specs(arrs):
    return [pl.BlockSpec((None,) + a.shape[1:], lambda gb: (gb, 0, 0)) for a in arrs]


def s5_ops_fwd(name, raw):
    shapes = [(4, 1, 512), (4, 1, 512), (4, 128, 512), (4, 128, 512), (4, LS, 512), (4, LS, 512)]
    outs = [_out(sh, pl.BlockSpec((None,) + sh[1:], lambda gb: (gb, 0, 0))) for sh in shapes]
    return run_fwd(name, _s5_ops_tile, (4,), raw, _s5_ops_specs(raw), outs)[0]


def s5_ops_bwd(name, raw, gops):
    wants = [_want(i, a.shape, pl.BlockSpec((None,) + a.shape[1:], lambda gb: (gb, 0, 0))) for i, a in enumerate(raw)]
    return run_bwd(name, _s5_ops_tile, (4,), raw, _s5_ops_specs(raw), gops, _s5_ops_specs(gops), wants)


def _s5_tile(carry, u, ab_re, ab_im, bb_re, bb_im, pw_re, pw_im, c_re, c_im, dskip):
    h_re, h_im = carry
    xr, xi = mm(u, bb_re), mm(u, bb_im)
    pr, pi = ab_re, ab_im
    k = 1
    while k < LS:
        sr, si = _cmul(pr, pi, _shift(xr, k, 0.0), _shift(xi, k, 0.0))
        xr, xi = xr + sr, xi + si
        pr, pi = _cmul(pr, pi, pr, pi)
        k *= 2
    cr, ci = _cmul(pw_re, pw_im, h_re, h_im)
    xr, xi = xr + cr, xi + ci
    y = mm(xr, c_re) - mm(xi, c_im) + dskip * u
    return (_pick_row(xr, LS - 1), _pick_row(xi, LS - 1)), (y,)


def _s5_io(u, ops, c_re, c_im, dskip, nc, rev):
    cm = _rev(nc, rev)
    ins = [u] + list(ops) + [c_re, c_im, dskip]
    specs = [pl.BlockSpec((LS, 128), lambda gb, c: (cm(c), gb))]
    specs += [pl.BlockSpec((None,) + a.shape[1:], lambda gb, c: (gb, 0, 0)) for a in list(ops) + [c_re, c_im]]
    specs += [pl.BlockSpec((1, 128), lambda gb, c: (0, gb))]
    return ins, specs, cm


def s5_fwd(name, proj, ops, c_re, c_im, dskip):
    s = proj.shape[0]
    nc = s // LS
    ins, specs, cm = _s5_io(proj, ops, c_re, c_im, dskip, nc, False)
    outs = [_out((s, BW), pl.BlockSpec((LS, 128), lambda gb, c: (c, gb)))]
    (y,), saved = run_fwd(name, _s5_tile, (4, nc), ins, specs, outs, carry=dict(shapes=[(1, 512), (1, 512)], axis=1))
    return y, saved


def s5_bwd(name, proj, ops, c_re, c_im, dskip, saved, gy):
    s = proj.shape[0]
    nc = s // LS
    ins, specs, cm = _s5_io(proj, ops, c_re, c_im, dskip, nc, True)
    wants = [_want(0, (s, BW), pl.BlockSpec((LS, 128), lambda gb, c: (cm(c), gb)))]
    for i, a in enumerate(list(ops) + [c_re, c_im]):
        wants.append(_want(1 + i, a.shape, pl.BlockSpec((None,) + a.shape[1:], lambda gb, c: (gb, 0, 0)), acc=(1,)))
    wants.append(_want(9, dskip.shape, pl.BlockSpec((1, 128), lambda gb, c: (0, gb)), acc=(1,)))
    return run_bwd(name, _s5_tile, (4, nc), ins, specs, [gy], [pl.BlockSpec((LS, 128), lambda gb, c: (cm(c), gb))],
                   wants, carry=dict(axis=1, saved=saved, rev=lambda g: (g[0], cm(g[1]))))


def _s5_glu_tile(y, w, b):
    z = _gelu(y)
    return (z * jax.nn.sigmoid(mm(z, w) + b),)


def s5_glu_fwd(name, y, w, b, t=512):
    s = y.shape[0]
    spec = pl.BlockSpec((t, BW), lambda i: (i, 0))
    return run_fwd(name, _s5_glu_tile, (s // t,), [y, w, b], [spec, _full(w), _full(b)], [_out((s, BW), spec)])[0][0]


def s5_glu_bwd(name, y, w, b, gout, gout_col, t=512):
    s = y.shape[0]
    spec = pl.BlockSpec((t, BW), lambda i: (i, 0))
    wants = [_want(0, y.shape, spec), _want(1, w.shape, _full(w), acc=(0,)), _want(2, b.shape, _full(b), acc=(0,))]
    return run_bwd(name, _s5_glu_tile, (s // t,), [y, w, b], [spec, _full(w), _full(b)], [gout],
                   [pl.BlockSpec((t, BW), lambda i: (i, gout_col))], wants)


def _lru_tile(carry, xb, gate, cw, cb, wr, br, wi, bi, lam):
    h_in, prev8 = carry
    xc = _conv4(xb, prev8, cw, cb)
    r = jax.nn.sigmoid(mm(xc, wr) + br)
    ig = jax.nn.sigmoid(mm(xc, wi) + bi)
    log_a = -LRU_C * r * _softplus(-lam)
    a = jnp.exp(log_a)
    b = jnp.sqrt(-_expm1(2.0 * log_a)) * (ig * xc)
    k = 1
    while k < LS:
        b = b + a * _shift(b, k, 0.0)
        a = a * _shift(a, k, 1.0)
        k *= 2
    h = b + a * h_in
    return (_pick_row(h, LS - 1), xb[LS - 8:, :]), (h * _gelu(gate),)


def _lru_io(proj, ws, nc, rev):
    cm = _rev(nc, rev)
    ins = [proj, proj] + list(ws)
    specs = [pl.BlockSpec((LS, BW), lambda c: (cm(c), 1)), pl.BlockSpec((LS, BW), lambda c: (cm(c), 2))]
    specs += [_full(a) for a in ws]
    return ins, specs, cm


def lru_fwd(name, proj, ws):
    s = proj.shape[0]
    nc = s // LS
    ins, specs, cm = _lru_io(proj, ws, nc, False)
    outs = [_out((s, BW), pl.BlockSpec((LS, BW), lambda c: (c, 0)))]
    (y,), saved = run_fwd(name, _lru_tile, (nc,), ins, specs, outs, carry=dict(shapes=[(1, BW), (8, BW)], axis=0))
    return y, saved


def lru_bwd(name, proj, ws, saved, gy, gy_col):
    s = proj.shape[0]
    nc = s // LS
    ins, specs, cm = _lru_io(proj, ws, nc, True)
    row = pl.BlockSpec((LS, BW), lambda c: (cm(c), 0))
    wants = [_want(0, (s, BW), row), _want(1, (s, BW), row)]
    wants += [_want(2 + i, a.shape, _full(a), acc=(0,)) for i, a in enumerate(ws)]
    return run_bwd(name, _lru_tile, (nc,), ins, specs, [gy], [pl.BlockSpec((LS, BW), lambda c: (cm(c), gy_col))], wants,
                   carry=dict(axis=0, saved=saved, rev=lambda g: (cm(g[0]),)))


def _causal(n):
    return _rows((n, n)) >= _lanes((n, n))


def _decay(col, rowv):
    causal = _causal(col.shape[0])
    return jnp.where(causal, jnp.exp(jnp.where(causal, col - rowv, 0.0)), 0.0)


def _m2_tile(carry, z, xs_raw, b_raw, c_raw, small, cwx, cwb, cwc, cbx, cbb, cbc, dtb, alog, dsk, ng):
    state, px, pb, pc = carry
    n = CHUNK
    xs = _silu(_conv4(xs_raw, px, cwx, cbx))
    bm = _silu(_conv4(b_raw, pb, cwb, cbb))
    cmx = _silu(_conv4(c_raw, pc, cwc, cbc))
    expand = (_lanes((16, BW)) // 64 == _rows((16, BW))).astype(f32)
    tri = _causal(n).astype(f32)
    triu = (_rows((n, n)) <= _lanes((n, n))).astype(f32)
    dt = _softplus(small + dtb)
    da = dt * (-jnp.exp(alog))
    cs = hmm(tri, da)
    cs_t = hmm_tn(da, triu)
    cs_w = hmm(cs, expand)
    last_w = _pick_row(cs_w, n - 1)
    xdt = xs * hmm(dt, expand)
    g0 = _lanes((n, BW)) < 256
    bm0, bm1, cm0, cm1 = bm[:, :128], bm[:, 128:], cmx[:, :128], cmx[:, 128:]
    cb0, cb1 = mm_nt(cm0, bm0), mm_nt(cm1, bm1)
    y = jnp.where(g0, mm(cm0, state), mm(cm1, state)) * jnp.exp(cs_w)
    for h in range(8):
        sc = (cb0 if h < 4 else cb1) * _decay(_pick_lane(cs, h), _pick_row(cs_t, h))
        y = y + jnp.where(_lanes((n, BW)) // 64 == h, mm(sc, xdt), 0.0)
    xd = xdt * jnp.exp(last_w - cs_w)
    g0s = _lanes((128, BW)) < 256
    state_out = state * jnp.exp(last_w) + jnp.where(g0s, mm_tn(bm0, xd), mm_tn(bm1, xd))
    y = (y + dsk * xs) * _silu(z)
    return (state_out, xs_raw[n - 8:, :], b_raw[n - 8:, :], c_raw[n - 8:, :]), (_rms(y, ng),)


def _m2_io(proj, small, ws, nc, rev):
    cm = _rev(nc, rev)
    n = CHUNK
    ins = [proj, proj, proj, proj, small] + list(ws)
    specs = [pl.BlockSpec((n, BW), lambda c: (cm(c), 3)), pl.BlockSpec((n, BW), lambda c: (cm(c), 4)),
             pl.BlockSpec((n, 256), lambda c: (cm(c), 10)), pl.BlockSpec((n, 256), lambda c: (cm(c), 11)),
             pl.BlockSpec((n, 16), lambda c: (cm(c), 0))]
    specs += [_full(a) for a in ws]
    return ins, specs, cm


_M2_CARRY = [(128, BW), (8, BW), (8, 256), (8, 256)]


def m2_fwd(name, proj, small, ws):
    s = proj.shape[0]
    nc = s // CHUNK
    ins, specs, cm = _m2_io(proj, small, ws, nc, False)
    outs = [_out((s, BW), pl.BlockSpec((CHUNK, BW), lambda c: (c, 0)))]
    (y,), saved = run_fwd(name, _m2_tile, (nc,), ins, specs, outs, carry=dict(shapes=_M2_CARRY, axis=0))
    return y, saved


def m2_bwd(name, proj, small, ws, saved, gy, gy_col):
    s = proj.shape[0]
    nc = s // CHUNK
    ins, specs, cm = _m2_io(proj, small, ws, nc, True)
    n = CHUNK
    wants = [_want(0, (s, BW), pl.BlockSpec((n, BW), lambda c: (cm(c), 0))),
             _want(1, (s, BW), pl.BlockSpec((n, BW), lambda c: (cm(c), 0))),
             _want(2, (s, 256), pl.BlockSpec((n, 256), lambda c: (cm(c), 0))),
             _want(3, (s, 256), pl.BlockSpec((n, 256), lambda c: (cm(c), 0))),
             _want(4, (s, 16), pl.BlockSpec((n, 16), lambda c: (cm(c), 0)))]
    wants += [_want(5 + i, a.shape, _full(a), acc=(0,)) for i, a in enumerate(ws)]
    return run_bwd(name, _m2_tile, (nc,), ins, specs, [gy], [pl.BlockSpec((n, BW), lambda c: (cm(c), gy_col))], wants,
                   carry=dict(axis=0, saved=saved, rev=lambda g: (cm(g[0]),)))


def _l2n(x):
    return x * lax.rsqrt(jnp.sum(x * x, axis=-1, keepdims=True) + EPS)


def _gdn_tile(carry, q_raw, k_raw, v_raw, gate, small, cwq, cwk, cwv, dtb, alog, ng):
    states, (pq, pk, pv) = carry[:4], carry[4:]
    n = CHUNK
    qc = _silu(_conv4(q_raw, pq, cwq, None))
    kc = _silu(_conv4(k_raw, pk, cwk, None))
    vc = _silu(_conv4(v_raw, pv, cwv, None))
    beta16 = jax.nn.sigmoid(small)
    g16 = -jnp.exp(alog) * _softplus(small + dtb)
    tri = _causal(n).astype(f32)
    triu = (_rows((n, n)) <= _lanes((n, n))).astype(f32)
    cs16 = hmm(tri, g16)
    cs16_t = hmm_tn(g16, triu)
    strict = _rows((n, n)) > _lanes((n, n))
    eye = (_rows((n, n)) == _lanes((n, n))).astype(f32)
    outs, new_states = [], []
    for h in range(4):
        sl = slice(128 * h, 128 * (h + 1))
        state = states[h]
        q = _l2n(qc[:, sl]) * (128 ** -0.5)
        k = _l2n(kc[:, sl])
        v = vc[:, sl]
        beta = _pick_lane(beta16, 8 + h)
        col = _pick_lane(cs16, 12 + h)
        decay = _decay(col, _pick_row(cs16_t, 12 + h))
        kb = k * beta
        m = jnp.where(strict, mm_nt(kb, k) * decay, 0.0)
        p = -m
        t = eye + p
        for _ in range(5):
            p = hmm(p, p)
            t = t + hmm(t, p)
        e_col = jnp.exp(col)
        u = hmm(t, v * beta)
        w = hmm(t, kb * e_col)
        qk = mm_nt(q, k) * decay
        last = _pick_row(col, n - 1)
        v_new = u - mm(w, state)
        o = mm(q * e_col, state) + mm(qk, v_new)
        new_states.append(state * jnp.exp(last) + mm_tn(k * jnp.exp(last - col), v_new))
        outs.append(_rms(o, ng) * _silu(gate[:, sl]))
    carry_out = tuple(new_states) + (q_raw[n - 8:, :], k_raw[n - 8:, :], v_raw[n - 8:, :])
    return carry_out, (jnp.concatenate(outs, axis=1),)


def _gdn_io(proj, small, cw, dtb, alog, ng, nc, rev):
    cm = _rev(nc, rev)
    n = CHUNK
    ins = [proj, proj, proj, proj, small, cw, cw, cw, dtb, alog, ng]
    specs = [pl.BlockSpec((n, BW), lambda c, _j=j: (cm(c), _j)) for j in (6, 7, 8, 9)]
    specs += [pl.BlockSpec((n, 16), lambda c: (cm(c), 0))]
    specs += [pl.BlockSpec((4, BW), lambda c, _j=j: (0, _j)) for j in (0, 1, 2)]
    specs += [_full(dtb), _full(alog), _full(ng)]
    return ins, specs, cm


_GDN_CARRY = [(128, 128)] * 4 + [(8, BW)] * 3


def gdn_fwd(name, proj, small, cw, dtb, alog, ng):
    s = proj.shape[0]
    nc = s // CHUNK
    ins, specs, cm = _gdn_io(proj, small, cw, dtb, alog, ng, nc, False)
    outs = [_out((s, BW), pl.BlockSpec((CHUNK, BW), lambda c: (c, 0)))]
    (y,), saved = run_fwd(name, _gdn_tile, (nc,), ins, specs, outs, carry=dict(shapes=_GDN_CARRY, axis=0))
    return y, saved


def gdn_bwd(name, proj, small, cw, dtb, alog, ng, saved, gy, gy_col):
    s = proj.shape[0]
    nc = s // CHUNK
    n = CHUNK
    ins, specs, cm = _gdn_io(proj, small, cw, dtb, alog, ng, nc, True)
    row = pl.BlockSpec((n, BW), lambda c: (cm(c), 0))
    wants = [_want(i, (s, BW), row) for i in range(4)]
    wants.append(_want(4, (s, 16), pl.BlockSpec((n, 16), lambda c: (cm(c), 0))))
    wants += [_want(5 + i, (4, BW), pl.BlockSpec((4, BW), lambda c: (0, 0)), acc=(0,)) for i in range(3)]
    wants += [_want(8, dtb.shape, _full(dtb), acc=(0,)), _want(9, alog.shape, _full(alog), acc=(0,)),
              _want(10, ng.shape, _full(ng), acc=(0,))]
    return run_bwd(name, _gdn_tile, (nc,), ins, specs, [gy], [pl.BlockSpec((n, BW), lambda c: (cm(c), gy_col))], wants,
                   carry=dict(axis=0, saved=saved, rev=lambda g: (cm(g[0]),)))


def _perm_cols(w):
    pad = jnp.zeros(w.shape[:-1] + (PW - IN_WIDTH,), w.dtype)
    return jnp.concatenate([w[..., :3072], w[..., 3080:5128], w[..., 3072:3080], w[..., 5128:5136], pad], axis=-1)


def _unperm_cols(g):
    return jnp.concatenate([g[..., :3072], g[..., 5120:5128], g[..., 3072:5120], g[..., 5128:5136]], axis=-1)


def _bd(blocks):
    n, a, b = blocks.shape
    eye = jnp.eye(n, dtype=blocks.dtype)
    return jnp.einsum("nab,nm->namb", blocks, eye).reshape(n * a, n * b)


def _layer_layout(lw):
    o = {}
    row = lambda a: a.reshape(1, -1)
    o["b_gate"] = row(lw["b_gate"])
    o["s5_are"] = lw["s5_a_re"].reshape(4, 1, 512)
    o["s5_aim"] = lw["s5_a_im"].reshape(4, 1, 512)
    o["s5_lstep"] = jnp.repeat(lw["s5_log_step"], 64).reshape(4, 1, 512)
    bt = lambda b: jax.vmap(_bd)(jnp.swapaxes(b, 1, 2).reshape(4, 8, 16, 64))
    o["s5_bre"], o["s5_bim"] = bt(lw["s5_b_re"]), bt(lw["s5_b_im"])
    ct = lambda c: jax.vmap(_bd)(jnp.swapaxes(c, 1, 2).reshape(4, 8, 64, 16))
    o["s5_cre"], o["s5_cim"] = ct(lw["s5_c_re"]), ct(lw["s5_c_im"])
    o["s5_d"] = row(lw["s5_d"])
    o["s5_b_glu"] = row(lw["s5_b_glu"])
    o["lru_conv_b"], o["lru_b_r"], o["lru_b_i"], o["lru_lambda"] = (row(lw[k]) for k in ("lru_conv_b", "lru_b_r", "lru_b_i", "lru_lambda"))
    o["lru_wr"], o["lru_wi"] = _bd(lw["lru_w_r"]), _bd(lw["lru_w_i"])
    cw, cb = lw["m2_conv_w"], lw["m2_conv_b"]
    o["m2_cwx"], o["m2_cwb"], o["m2_cwc"] = cw[:, :512], cw[:, 512:768], cw[:, 768:]
    o["m2_cbx"], o["m2_cbb"], o["m2_cbc"] = row(cb[:512]), row(cb[512:768]), row(cb[768:])
    o["m2_dtb"] = jnp.pad(lw["m2_dt_bias"], (0, 8)).reshape(1, 16)
    o["m2_alog"] = jnp.pad(lw["m2_a_log"], (0, 8)).reshape(1, 16)
    o["m2_dsk"] = jnp.repeat(lw["m2_d"], 64).reshape(1, 512)
    o["m2_norm"] = row(lw["m2_norm"])
    o["gdn_dtb"] = jnp.pad(lw["gdn_dt_bias"], (12, 0)).reshape(1, 16)
    o["gdn_alog"] = jnp.pad(lw["gdn_a_log"], (12, 0)).reshape(1, 16)
    o["gdn_norm"] = row(lw["gdn_norm"])
    for k in ("ffn1_norm", "mix_norm", "ffn2_norm", "ple_norm"):
        o[k] = row(lw[k])
    return o


_BIG_PLAIN = ("w_gate", "s5_w_glu", "w_branch", "w_out", "ple_w_gate", "ple_w_proj")


def big_layout(w):
    o = {k: w[k] for k in _BIG_PLAIN}
    for f in ("ffn1", "ffn2"):
        o[f + "_wg"], o[f + "_wu"] = w[f + "_w_in"][:, :FFN_DIM], w[f + "_w_in"][:, FFN_DIM:]
        o[f + "_wo"] = w[f + "_w_out"]
    o["w_in"] = _perm_cols(w["w_in"])
    return {k: v.astype(bf16) for k, v in o.items()}


def big_unlayout(g):
    o = {k: g[k] for k in _BIG_PLAIN}
    for f in ("ffn1", "ffn2"):
        o[f + "_w_in"] = jnp.concatenate([g[f + "_wg"], g[f + "_wu"]], axis=1)
        o[f + "_w_out"] = g[f + "_wo"]
    o["w_in"] = _unperm_cols(g["w_in"])
    return o


_SMALL_KEYS = ("b_gate", "s5_log_step", "s5_a_re", "s5_a_im", "s5_b_re", "s5_b_im", "s5_c_re", "s5_c_im", "s5_d",
               "s5_b_glu", "lru_conv_b", "lru_w_r", "lru_b_r", "lru_w_i", "lru_b_i", "lru_lambda", "m2_conv_w",
               "m2_conv_b", "m2_dt_bias", "m2_a_log", "m2_d", "m2_norm", "gdn_dt_bias", "gdn_a_log", "gdn_norm",
               "ffn1_norm", "mix_norm", "ffn2_norm", "ple_norm")


def _layer_fwd(i, h0, pe, lw, big):
    n = f"l{i}_"
    lay = _layer_layout(lw)
    a = {"h0": h0, "lay": lay}
    h1 = ffn_fwd(n + "ffn1_fwd", h0, lay["ffn1_norm"], big["ffn1_wg"], big["ffn1_wu"], big["ffn1_wo"])
    zero_b = jnp.zeros((1, PW), f32)
    proj = normlin_fwd(n + "inproj_fwd", h1, lay["mix_norm"], big["w_in"], zero_b, tn=896)
    gpre = normlin_fwd(n + "gate_fwd", h1, lay["mix_norm"], big["w_gate"], lay["b_gate"], tn=1024)
    small = proj[:, SMALL_OFF:SMALL_OFF + 16]
    raw = [lay["s5_are"], lay["s5_aim"], lay["s5_lstep"], lay["s5_bre"], lay["s5_bim"]]
    ops = s5_ops_fwd(n + "s5ops_fwd", raw)
    y5, sv5 = s5_fwd(n + "s5_fwd", proj, ops, lay["s5_cre"], lay["s5_cim"], lay["s5_d"])
    ya = s5_glu_fwd(n + "s5glu_fwd", y5, big["s5_w_glu"], lay["s5_b_glu"])
    lru_ws = [lw["lru_conv_w"], lay["lru_conv_b"], lay["lru_wr"], lay["lru_b_r"], lay["lru_wi"], lay["lru_b_i"], lay["lru_lambda"]]
    yb, svb = lru_fwd(n + "lru_fwd", proj, lru_ws)
    m2_ws = [lay[k] for k in ("m2_cwx", "m2_cwb", "m2_cwc", "m2_cbx", "m2_cbb", "m2_cbc", "m2_dtb", "m2_alog", "m2_dsk", "m2_norm")]
    yc, svc = m2_fwd(n + "m2_fwd", proj, small, m2_ws)
    yd, svd = gdn_fwd(n + "gdn_fwd", proj, small, lw["gdn_conv_w"], lay["gdn_dtb"], lay["gdn_alog"], lay["gdn_norm"])
    ys = jnp.concatenate([ya, yb, yc, yd], axis=1)
    h2 = merge_fwd(n + "merge_fwd", h1, ys, gpre, big["w_branch"], big["w_out"])
    h3 = ffn_fwd(n + "ffn2_fwd", h2, lay["ffn2_norm"], big["ffn2_wg"], big["ffn2_wu"], big["ffn2_wo"])
    h4 = ple_fwd(n + "ple_fwd", h3, pe, lay["ple_norm"], big["ple_w_gate"], big["ple_w_proj"])
    a.update(h1=h1, proj=proj, gpre=gpre, small=small, raw=raw, ops=ops, y5=y5, sv5=sv5, lru_ws=lru_ws, svb=svb,
             m2_ws=m2_ws, svc=svc, svd=svd, ys=ys, h2=h2, h3=h3, pe=pe)
    return h4, a


def _layer_bwd(i, a, lw, big, gh4):
    n = f"l{i}_"
    lay = a["lay"]
    gb, gl = {}, {}
    s = gh4.shape[0]
    t = 512
    dh3, gl["ple_norm"], gb["ple_w_gate"], gb["ple_w_proj"] = ple_bwd(
        n + "ple_bwd", a["h3"], a["pe"], lay["ple_norm"], big["ple_w_gate"], big["ple_w_proj"], gh4)
    gh3 = addn(n + "gh3", [(gh4, None), (dh3, None)], t)
    dh2, gl["ffn2_norm"], gb["ffn2_wg"], gb["ffn2_wu"], gb["ffn2_wo"] = ffn_bwd(
        n + "ffn2_bwd", a["h2"], lay["ffn2_norm"], big["ffn2_wg"], big["ffn2_wu"], big["ffn2_wo"], gh3)
    gh2 = addn(n + "gh2", [(gh3, None)] + _parts(dh2), t)
    gys, ggpre, gb["w_branch"], gb["w_out"] = merge_bwd(
        n + "merge_bwd", a["h1"], a["ys"], a["gpre"], big["w_branch"], big["w_out"], gh2)
    gy5, gb["s5_w_glu"], gl["s5_b_glu"] = s5_glu_bwd(n + "s5glu_bwd", a["y5"], big["s5_w_glu"], lay["s5_b_glu"], gys, 0)
    r5 = s5_bwd(n + "s5_bwd", a["proj"], a["ops"], lay["s5_cre"], lay["s5_cim"], lay["s5_d"], a["sv5"], gy5)
    du5, gops, gl["s5_cre"], gl["s5_cim"], gl["s5_d"] = r5[0], r5[1:7], r5[7], r5[8], r5[9]
    gl["s5_are"], gl["s5_aim"], gl["s5_lstep"], gl["s5_bre"], gl["s5_bim"] = s5_ops_bwd(n + "s5ops_bwd", a["raw"], gops)
    rb = lru_bwd(n + "lru_bwd", a["proj"], a["lru_ws"], a["svb"], gys, 1)
    dxb, dgl = rb[0], rb[1]
    gl["lru_conv_w"], gl["lru_conv_b"], gl["lru_wr"], gl["lru_b_r"], gl["lru_wi"], gl["lru_b_i"], gl["lru_lambda"] = rb[2:]
    rc = m2_bwd(n + "m2_bwd", a["proj"], a["small"], a["m2_ws"], a["svc"], gys, 2)
    dz, dxs, dbr, dcr, dsm_c = rc[:5]
    for k, v in zip(("m2_cwx", "m2_cwb", "m2_cwc", "m2_cbx", "m2_cbb", "m2_cbc", "m2_dtb", "m2_alog", "m2_dsk", "m2_norm"), rc[5:]):
        gl[k] = v
    rd = gdn_bwd(n + "gdn_bwd", a["proj"], a["small"], lw["gdn_conv_w"], lay["gdn_dtb"], lay["gdn_alog"], lay["gdn_norm"], a["svd"], gys, 3)
    dq, dk, dv, dgg, dsm_d = rd[:5]
    gl["gdn_conv_w"] = jnp.concatenate(rd[5:8], axis=1)
    gl["gdn_dtb"], gl["gdn_alog"], gl["gdn_norm"] = rd[8:]
    dsm = addn(n + "gsmall", [(dsm_c, None), (dsm_d, None)], 512)
    gproj = jnp.concatenate([du5, dxb, dgl, dz, dxs, dbr, dcr, dq, dk, dv, dgg, dsm, jnp.zeros((s, PW - SMALL_OFF - 16), f32)], axis=1)
    zero_b = jnp.zeros((1, PW), f32)
    dh1_p, gmn_p, gb["w_in"], _ = normlin_bwd(n + "inproj_bwd", a["h1"], lay["mix_norm"], big["w_in"], zero_b, gproj, tn=896)
    dh1_g, gmn_g, gb["w_gate"], gl["b_gate"] = normlin_bwd(n + "gate_bwd", a["h1"], lay["mix_norm"], big["w_gate"], lay["b_gate"], ggpre, tn=1024)
    gh1 = addn(n + "gh1", [(gh2, None)] + _parts(dh1_p) + _parts(dh1_g), t)
    dh0, gl["ffn1_norm"], gb["ffn1_wg"], gb["ffn1_wu"], gb["ffn1_wo"] = ffn_bwd(
        n + "ffn1_bwd", a["h0"], lay["ffn1_norm"], big["ffn1_wg"], big["ffn1_wu"], big["ffn1_wo"], gh1)
    gh0 = addn(n + "gh0", [(gh1, None)] + _parts(dh0), t)
    gl["mix_norm"] = gmn_p + gmn_g
    return gh0, gb, gl


def _local_step(x, p, target, bigs, smalls, final_norm):
    h = x
    acts = []
    for i in range(DEPTH):
        h, a = _layer_fwd(i, h, p[i], smalls[i], bigs[i])
        acts.append(a)
    fg = final_norm.reshape(1, -1)
    loss, gh, gfn = final_loss("final_loss", h, fg, target)
    gbs, gss = [None] * DEPTH, [None] * DEPTH
    for i in reversed(range(DEPTH)):
        gh, gb, gl = _layer_bwd(i, acts[i], smalls[i], bigs[i], gh)
        _, pull = jax.vjp(_layer_layout, smalls[i])
        lay_g = {k: gl[k] for k in acts[i]["lay"]}
        gs = pull(lay_g)[0]
        gs = dict(gs)
        gs["lru_conv_w"] = gl["lru_conv_w"]
        gs["gdn_conv_w"] = gl["gdn_conv_w"]
        gbs[i], gss[i] = gb, gs
    return loss, gh, gbs, gss, gfn.reshape(-1)


MESH_AXES = ("x", "y", "c")
PACK_W = 1024
PACK_ROWS = 256

W_NAMES = ("ffn1_norm", "ffn1_w_in", "ffn1_w_out", "mix_norm", "w_in", "w_gate", "b_gate", "s5_log_step", "s5_a_re",
           "s5_a_im", "s5_b_re", "s5_b_im", "s5_c_re", "s5_c_im", "s5_d", "s5_w_glu", "s5_b_glu", "lru_conv_w",
           "lru_conv_b", "lru_w_r", "lru_b_r", "lru_w_i", "lru_b_i", "lru_lambda", "m2_conv_w", "m2_conv_b",
           "m2_dt_bias", "m2_a_log", "m2_d", "m2_norm", "gdn_conv_w", "gdn_dt_bias", "gdn_a_log", "gdn_norm",
           "w_branch", "w_out", "ffn2_norm", "ffn2_w_in", "ffn2_w_out", "ple_norm", "ple_w_gate", "ple_w_proj",
           "final_norm")
COL_SHARDED = ("ffn1_w_in", "w_in", "w_gate", "lru_conv_w", "m2_conv_w", "gdn_conv_w", "w_branch", "ffn2_w_in", "ple_w_proj")
ROW_SHARDED = ("ffn1_w_out", "s5_w_glu", "w_out", "ffn2_w_out", "ple_w_gate")
BIG_NAMES = ("ffn1_w_in", "ffn1_w_out", "w_in", "w_gate", "s5_w_glu", "w_branch", "w_out", "ffn2_w_in", "ffn2_w_out",
             "ple_w_gate", "ple_w_proj")
CONV_NAMES = ("lru_conv_w", "m2_conv_w", "gdn_conv_w")
SHARDED = BIG_NAMES + CONV_NAMES
REPLICATED = tuple(k for k in W_NAMES if k not in SHARDED)


def _gathered_to_full(name, g):
    if name in COL_SHARDED:
        g = jnp.moveaxis(g, 0, -2)
        return g.reshape(g.shape[:-2] + (g.shape[-2] * g.shape[-1],))
    g = jnp.moveaxis(g, 0, 1)
    return g.reshape((g.shape[0], g.shape[1] * g.shape[2]) + g.shape[3:])


def _full_to_scattered(name, w):
    if name in COL_SHARDED:
        w = w.reshape(w.shape[:-1] + (N_DEV, w.shape[-1] // N_DEV))
        return jnp.moveaxis(w, -2, 0)
    w = w.reshape((w.shape[0], N_DEV, w.shape[1] // N_DEV) + w.shape[2:])
    return jnp.moveaxis(w, 1, 0)


def _pack(arrs, lead=0):
    lshape = arrs[0].shape[:lead]
    flat = jnp.concatenate([a.reshape(lshape + (-1,)) for a in arrs], axis=-1)
    n = flat.shape[-1]
    rows = -(-n // (PACK_W * PACK_ROWS)) * PACK_ROWS
    flat = jnp.pad(flat, [(0, 0)] * lead + [(0, rows * PACK_W - n)])
    return flat.reshape(lshape + (rows, PACK_W))


def _unpack(buf, shapes, lead=0):
    lshape = buf.shape[:lead]
    flat = buf.reshape(lshape + (-1,))
    out, off = [], 0
    for sh in shapes:
        n = math.prod(sh)
        out.append(flat[..., off:off + n].reshape(lshape + tuple(sh)))
        off += n
    return out


def _peer(k):
    mx, my, mc = (lax.axis_index(a) for a in MESH_AXES)
    px = 1 - mx if k & 4 else mx
    py = 1 - my if k & 2 else my
    pc = 1 - mc if k & 1 else mc
    return (px, py, pc), 4 * px + 2 * py + pc


_ANY = pl.BlockSpec(memory_space=pl.ANY)


def all_gather(name, xs):
    n = len(xs)

    def body(*refs):
        x_refs, out_refs = refs[:n], refs[n:2 * n]
        send_sems, recv_sems, local_sems = refs[2 * n:]
        mx, my, mc = (lax.axis_index(a) for a in MESH_AXES)
        me, sibling = (mx, my, mc), (mx, my, 1 - mc)
        chips = [(1 - mx, my), (mx, 1 - my), (1 - mx, 1 - my)]

        def slot(i, px, py, pc):
            return out_refs[i].at[4 * px + 2 * py + pc]

        def copy(k, i, block, to, src=None):
            return pltpu.make_async_remote_copy(
                src_ref=slot(i, *block) if src is None else src, dst_ref=slot(i, *block),
                send_sem=send_sems.at[k, i], recv_sem=recv_sems.at[k, i], device_id=to, device_id_type=pl.DeviceIdType.MESH)

        mine = [pltpu.make_async_copy(x_refs[i], slot(i, *me), local_sems.at[i]) for i in range(n)]
        first = []
        for i in range(n):
            mine[i].start()
            first.append(copy(0, i, me, sibling, src=x_refs[i]))
            first += [copy(1 + j, i, me, (*chip, mc), src=x_refs[i]) for j, chip in enumerate(chips)]
        for cp in first:
            cp.start()
        passed = []
        for i in range(n):
            for j, chip in enumerate(chips):
                copy(1 + j, i, (*chip, mc), me).wait_recv()
                cp = copy(4 + j, i, (*chip, mc), sibling)
                cp.start()
                passed.append(cp)
        for i in range(n):
            copy(0, i, sibling, me).wait_recv()
            for j, chip in enumerate(chips):
                copy(4 + j, i, (*chip, 1 - mc), me).wait_recv()
        for cp in first + passed:
            cp.wait_send()
        for cp in mine:
            cp.wait()

    res = pl.pallas_call(
        body, name=name, out_shape=[jax.ShapeDtypeStruct((N_DEV,) + x.shape, x.dtype) for x in xs],
        in_specs=[_ANY] * n, out_specs=[_ANY] * n,
        scratch_shapes=[pltpu.SemaphoreType.DMA((7, n)), pltpu.SemaphoreType.DMA((7, n)), pltpu.SemaphoreType.DMA((n,))],
    )(*xs)
    return list(res)


def exchange(name, gs):
    n = len(gs)

    def body(*refs):
        g_refs, out_refs = refs[:n], refs[n:2 * n]
        send_sems, recv_sems, local_sems = refs[2 * n:]
        mx, my, mc = (lax.axis_index(a) for a in MESH_AXES)
        me = 4 * mx + 2 * my + mc
        mine, copies = [], []
        for i in range(n):
            cp = pltpu.make_async_copy(g_refs[i].at[me], out_refs[i].at[me], local_sems.at[i])
            cp.start()
            mine.append(cp)
        for k in range(1, N_DEV):
            peer, pidx = _peer(k)
            for i in range(n):
                cp = pltpu.make_async_remote_copy(
                    src_ref=g_refs[i].at[pidx], dst_ref=out_refs[i].at[me], send_sem=send_sems.at[k - 1, i],
                    recv_sem=recv_sems.at[k - 1, i], device_id=peer, device_id_type=pl.DeviceIdType.MESH)
                cp.start()
                copies.append(cp)
        for k in range(1, N_DEV):
            peer, pidx = _peer(k)
            for i in range(n):
                pltpu.make_async_remote_copy(
                    src_ref=g_refs[i].at[pidx], dst_ref=out_refs[i].at[pidx], send_sem=send_sems.at[k - 1, i],
                    recv_sem=recv_sems.at[k - 1, i], device_id=peer, device_id_type=pl.DeviceIdType.MESH).wait_recv()
        for cp in copies:
            cp.wait_send()
        for cp in mine:
            cp.wait()

    res = pl.pallas_call(
        body, name=name, out_shape=[jax.ShapeDtypeStruct(g.shape, g.dtype) for g in gs], in_specs=[_ANY] * n, out_specs=[_ANY] * n,
        scratch_shapes=[pltpu.SemaphoreType.DMA((7, n)), pltpu.SemaphoreType.DMA((7, n)), pltpu.SemaphoreType.DMA((n,))],
    )(*gs)
    return list(res)


def sum_slots(name, buf):
    return addn(name, _parts(buf), PACK_ROWS)


def _adamw_math(ww, gg, mm_, vv):
    m2 = ADAM_B1 * mm_ + (1.0 - ADAM_B1) * gg
    v2 = ADAM_B2 * vv + (1.0 - ADAM_B2) * (gg * gg)
    m_hat = m2 / (1.0 - ADAM_B1 ** ADAM_STEP)
    v_hat = v2 / (1.0 - ADAM_B2 ** ADAM_STEP)
    delta = -ADAM_LR * (m_hat / (jnp.sqrt(v_hat) + ADAM_EPS) + ADAM_WD * ww)
    return delta, m2, v2


def adamw(name, w, g, m, v):
    spec = pl.BlockSpec((PACK_ROWS, PACK_W), lambda i: (i, 0))
    return run_fwd(name, _adamw_math, (w.shape[0] // PACK_ROWS,), [w, g, m, v], [spec] * 4, [_out(w.shape, spec)] * 3)[0]


def reduce_adamw(name, slots, w, m, v):
    shape = w.shape
    r, c = shape[-2:]
    a = math.prod(shape[:-2])
    tr = r
    while tr * c * 4 > (1 << 20) and tr % 16 == 0:
        tr //= 2
    s3 = slots.reshape((N_DEV, a, r, c))
    w3, m3, v3 = (t.reshape((a, r, c)) for t in (w, m, v))

    def fn(*t):
        g = t[0].astype(f32)
        for d in range(1, N_DEV):
            g = g + t[d].astype(f32)
        return (g,) + _adamw_math(t[N_DEV], g, t[N_DEV + 1], t[N_DEV + 2])

    specs = [pl.BlockSpec((None, None, tr, c), lambda i, j, _d=d: (_d, i, j, 0)) for d in range(N_DEV)]
    spec = pl.BlockSpec((None, tr, c), lambda i, j: (i, j, 0))
    res = run_fwd(name, fn, (a, r // tr), [s3] * N_DEV + [w3, m3, v3], specs + [spec] * 3, [_out((a, r, c), spec)] * 4)[0]
    return [t.reshape(shape) for t in res]


def kernel(*args):
    nw = len(W_NAMES)
    x, p = args[0], args[1]
    w = dict(zip(W_NAMES, args[2:2 + nw]))
    target = args[2 + nw]
    m = dict(zip(W_NAMES, args[3 + nw:3 + 2 * nw]))
    v = dict(zip(W_NAMES, args[3 + 2 * nw:3 + 3 * nw]))

    gathered = all_gather("ag_weights", [w[k].astype(bf16) for k in BIG_NAMES] + [w[k] for k in CONV_NAMES])
    full = {k: _gathered_to_full(k, g) for k, g in zip(SHARDED, gathered)}

    bigs = [big_layout({k: full[k][i] for k in BIG_NAMES}) for i in range(DEPTH)]
    smalls = []
    for i in range(DEPTH):
        sm = {k: w[k][i] for k in _SMALL_KEYS if k not in CONV_NAMES}
        sm.update({k: full[k][i] for k in CONV_NAMES})
        smalls.append(sm)
    loss, gx, gbs, gss, gfn = _local_step(x[0], p[:, 0], target[0], bigs, smalls, w["final_norm"])
    loss = lax.psum(loss[0, 0], MESH_AXES)

    gfull = [dict(big_unlayout(gbs[i]), **gss[i]) for i in range(DEPTH)]
    stack = lambda k: jnp.stack([gfull[i][k] for i in range(DEPTH)])
    slots = exchange("rs_sharded", [_full_to_scattered(k, stack(k)).astype(bf16) for k in SHARDED])
    outs = {}
    kinds = ("grad", "delta", "new_m", "new_v")
    for k, sl in zip(SHARDED, slots):
        for kind, a in zip(kinds, reduce_adamw("adamw_" + k, sl, w[k], m[k], v[k])):
            outs[kind + "_" + k] = a

    shapes = [w[k].shape for k in REPLICATED]
    g_rep = sum_slots("sum_replicated", all_gather("ag_replicated", [_pack([gfn if k == "final_norm" else stack(k) for k in REPLICATED])])[0])
    res = adamw("adamw_replicated", _pack([w[k] for k in REPLICATED]), g_rep, _pack([m[k] for k in REPLICATED]), _pack([v[k] for k in REPLICATED]))
    for kind, buf in zip(kinds, [g_rep] + list(res)):
        for k, a in zip(REPLICATED, _unpack(buf, shapes)):
            outs[kind + "_" + k] = a
    return (loss, gx[None]) + tuple(outs[kind + "_" + k] for kind in kinds for k in W_NAMES)
```

```python
import functools
import math

import jax
import jax.numpy as jnp
from jax import lax
from jax.experimental import pallas as pl
from jax.experimental.pallas import tpu as pltpu

f32 = jnp.float32
bf16 = jnp.bfloat16

EPS = 1e-6
DEPTH = 2
D_MODEL = 1024
FFN_DIM = 2816
BW = 512
IN_WIDTH = 5136
PW = 5376
SMALL_OFF = 5120
CHUNK = 64
LS = 256
LRU_C = 8.0
N_DEV = 8
VMEM_LIMIT_BYTES = 56 * 1024 * 1024

ADAM_LR, ADAM_B1, ADAM_B2, ADAM_EPS, ADAM_WD, ADAM_STEP = 0.001, 0.9, 0.999, 1e-08, 0.01, 10


_NN, _NT, _TN = ((1,), (0,)), ((1,), (1,)), ((0,), (0,))


def _pieces(x, n):
    parts, r = [], x
    for i in range(n):
        p = r.astype(bf16)
        parts.append(p)
        if i + 1 < n:
            r = r - p.astype(f32)
    return parts


def _dg(a, b, dims, mode):
    sa, sb = mode
    pa, pb = _pieces(a, sa), _pieces(b, sb)
    out = None
    for i in reversed(range(sa)):
        for j in reversed(range(sb)):
            if i + j < max(sa, sb):
                d = lax.dot_general(pa[i], pb[j], (dims, ((), ())), preferred_element_type=f32)
                out = d if out is None else out + d
    return out


def _make_mm(mode):
    sa, sb = mode
    cot = lambda s_other: 1 if mode == (1, 1) else (3 if s_other == 1 else 2)
    m_g_b, m_a_g, m_g_a, m_b_g = (cot(sb), sb), (sa, cot(sa)), (cot(sa), sa), (sb, cot(sb))

    @jax.custom_vjp
    def nn(a, b):
        return _dg(a, b, _NN, mode)

    @jax.custom_vjp
    def nt(a, b):
        return _dg(a, b, _NT, mode)

    @jax.custom_vjp
    def tn(a, b):
        return _dg(a, b, _TN, mode)

    nn.defvjp(lambda a, b: (nn(a, b), (a, b)), lambda r, g: (_dg(g, r[1], _NT, m_g_b), _dg(r[0], g, _TN, m_a_g)))
    nt.defvjp(lambda a, b: (nt(a, b), (a, b)), lambda r, g: (_dg(g, r[1], _NN, m_g_b), _dg(g, r[0], _TN, m_g_a)))
    tn.defvjp(lambda a, b: (tn(a, b), (a, b)), lambda r, g: (_dg(r[1], g, _NT, m_b_g), _dg(r[0], g, _NN, m_a_g)))
    return nn, nt, tn


mm, mm_nt, mm_tn = _make_mm((1, 1))
xmm, xmm_nt, xmm_tn = _make_mm((2, 2))
lmm, lmm_nt, lmm_tn = _make_mm((1, 3))
rmm, rmm_nt, rmm_tn = _make_mm((3, 1))


@jax.custom_vjp
def _tri_inv(m):
    n = m.shape[0]
    eye = (_rows((n, n)) == _lanes((n, n))).astype(f32)
    blk = (_rows((n, n)) // 16) == (_lanes((n, n)) // 16)
    x = lambda a, b: _dg(a, b, _NN, (2, 2))
    nb = jnp.where(blk, m, 0.0)
    p = -nb
    t = eye + p
    for _ in range(3):
        p = x(p, p)
        t = t + x(t, p)
    q = x(t, m - nb)
    imq = eye - q
    return x(imq + x(imq, x(q, q)), t)


def _tri_inv_fwd(m):
    t = _tri_inv(m)
    return t, t


def _tri_inv_bwd(t, g):
    return (-_dg(_dg(t, g, _TN, (2, 2)), t, _NT, (2, 2)),)


_tri_inv.defvjp(_tri_inv_fwd, _tri_inv_bwd)


def _rows(shape):
    return lax.broadcasted_iota(jnp.int32, shape, 0)


def _lanes(shape):
    return lax.broadcasted_iota(jnp.int32, shape, 1)


def _rms(x, g):
    return x * lax.rsqrt(jnp.mean(x * x, axis=-1, keepdims=True) + EPS) * g


def _silu(x):
    return x * jax.nn.sigmoid(x)


def _gelu(x):
    return 0.5 * x * (1.0 + jnp.tanh(0.7978845608028654 * (x + 0.044715 * x * x * x)))


def _softplus(x):
    return jnp.maximum(x, 0.0) + jnp.log1p(jnp.exp(-jnp.abs(x)))


def _expm1(x):
    p = x * (1.0 + x * (0.5 + x * (1.0 / 6 + x * (1.0 / 24 + x * (1.0 / 120 + x * (1.0 / 720 + x * (1.0 / 5040)))))))
    return jnp.where(x > -0.3, p, jnp.exp(x) - 1.0)


def _pick_row(x, r):
    return jnp.sum(jnp.where(_rows(x.shape) == r, x, 0.0), axis=0, keepdims=True)


def _pick_lane(x, c):
    return jnp.sum(jnp.where(_lanes(x.shape) == c, x, 0.0), axis=1, keepdims=True)


def _shift_up(g, j):
    n = g.shape[0]
    return jnp.where(_rows(g.shape) < n - j, pltpu.roll(g, n - j, 0), 0.0)


@functools.partial(jax.custom_vjp, nondiff_argnums=(1, 2))
def _shift(x, j, fill):
    return jnp.where(_rows(x.shape) >= j, pltpu.roll(x, j, 0), fill)


_shift.defvjp(lambda x, j, fill: (_shift(x, j, fill), None), lambda j, fill, _, g: (_shift_up(g, j),))


@functools.partial(jax.custom_vjp, nondiff_argnums=(2,))
def _shift_halo(x, prev8, j):
    xr = pltpu.roll(x, j, 0)
    pr = pltpu.roll(prev8, j, 0)
    top = jnp.where(_rows(pr.shape) < j, pr, xr[:8])
    return jnp.concatenate([top, xr[8:]], axis=0)


def _shift_halo_bwd(j, _, g):
    g8 = g[:8]
    dprev = jnp.where(_rows(g8.shape) >= 8 - j, pltpu.roll(g8, 8 - j, 0), 0.0)
    return _shift_up(g, j), dprev


_shift_halo.defvjp(lambda x, p, j: (_shift_halo(x, p, j), None), _shift_halo_bwd)


def _conv4(x, prev8, w, b):
    y = _pick_row(w, 3) * x
    for k in range(3):
        y = y + _pick_row(w, k) * _shift_halo(x, prev8, 3 - k)
    return y if b is None else y + b


def _cmul(ar, ai, br, bi):
    return ar * br - ai * bi, ar * bi + ai * br


_ANY = pl.BlockSpec(memory_space=pl.ANY)


def _params(grid):
    return pltpu.CompilerParams(dimension_semantics=("arbitrary",) * len(grid), vmem_limit_bytes=VMEM_LIMIT_BYTES)


def _first(axes):
    ok = pl.program_id(axes[0]) == 0
    for a in axes[1:]:
        ok = jnp.logical_and(ok, pl.program_id(a) == 0)
    return ok


def _store(ref, val, acc):
    val = val.astype(ref.dtype)
    if acc is None:
        ref[...] = val
        return
    first = _first(acc)

    @pl.when(first)
    def _():
        ref[...] = val

    @pl.when(jnp.logical_not(first))
    def _():
        ref[...] += val


def _full(a):
    nd = a.ndim
    return pl.BlockSpec(a.shape, lambda *g: (0,) * nd)


def _out(shape, spec, acc=None, dtype=f32):
    return dict(shape=tuple(shape), spec=spec, acc=acc, dtype=dtype)


def run_fwd(name, fn, grid, ins, in_specs, outs, carry=None):
    n_in, n_out = len(ins), len(outs)
    cshapes = carry["shapes"] if carry else []
    nc = len(cshapes)
    ng = len(grid)

    def body(*refs):
        in_refs = refs[:n_in]
        out_refs = refs[n_in:n_in + n_out]
        save_refs = refs[n_in + n_out:n_in + n_out + nc]
        c_refs = refs[n_in + n_out + nc:]
        vals = [r[...] for r in in_refs]
        if carry:
            @pl.when(pl.program_id(carry["axis"]) == 0)
            def _():
                for c in c_refs:
                    c[...] = jnp.zeros(c.shape, f32)
            cin = tuple(c[...] for c in c_refs)
            for s, v in zip(save_refs, cin):
                s[...] = v
            cout, res = fn(cin, *vals)
            for c, v in zip(c_refs, cout):
                c[...] = v
        else:
            res = fn(*vals)
        for o, r, d in zip(out_refs, res, outs):
            _store(o, r, d["acc"])

    out_shape = [jax.ShapeDtypeStruct(d["shape"], d["dtype"]) for d in outs]
    out_specs = [d["spec"] for d in outs]
    for cs in cshapes:
        out_shape.append(jax.ShapeDtypeStruct(tuple(grid) + tuple(cs), f32))
        out_specs.append(pl.BlockSpec((None,) * ng + tuple(cs), lambda *g, _n=len(cs): tuple(g) + (0,) * _n))
    res = pl.pallas_call(
        body, name=name, grid=grid, in_specs=list(in_specs), out_specs=out_specs, out_shape=out_shape,
        scratch_shapes=[pltpu.VMEM(tuple(cs), f32) for cs in cshapes], compiler_params=_params(grid),
    )(*ins)
    return list(res[:n_out]), list(res[n_out:])


def run_bwd(name, fn, grid, ins, in_specs, gouts, gout_specs, wants, carry=None, into=None):
    n_in, n_g, n_w = len(ins), len(gouts), len(wants)
    n_a = 0 if into is None else 1
    saved = carry["saved"] if carry else []
    nc = len(saved)
    ng = len(grid)

    def body(*refs):
        in_refs = refs[:n_in]
        g_refs = refs[n_in:n_in + n_g]
        s_refs = refs[n_in + n_g:n_in + n_g + nc]
        w_refs = refs[n_in + n_g + nc + n_a:n_in + n_g + nc + n_a + n_w]
        dc_refs = refs[n_in + n_g + nc + n_a + n_w:]
        vals = [r[...].astype(f32) for r in in_refs]
        gs = tuple(r[...].astype(f32) for r in g_refs)
        if carry:
            @pl.when(pl.program_id(carry["axis"]) == 0)
            def _():
                for c in dc_refs:
                    c[...] = jnp.zeros(c.shape, f32)
            cin = tuple(s[...] for s in s_refs)
            _, vjp = jax.vjp(fn, cin, *vals)
            grads = vjp((tuple(c[...] for c in dc_refs), gs))
            for c, v in zip(dc_refs, grads[0]):
                c[...] = v
            dvals = grads[1:]
        else:
            _, vjp = jax.vjp(fn, *vals)
            dvals = vjp(gs)
        for o, d in zip(w_refs, wants):
            idx = d["idx"]
            val = dvals[idx] if isinstance(idx, int) else jnp.concatenate([dvals[j] for j in idx], axis=1)
            _store(o, val, d["acc"])

    rev = carry["rev"] if carry else None
    s_specs = []
    for a in saved:
        n = a.ndim - ng
        s_specs.append(pl.BlockSpec((None,) * ng + tuple(a.shape[ng:]), lambda *g, _n=n: tuple(rev(g)) + (0,) * _n))
    res = pl.pallas_call(
        body, name=name, grid=grid, in_specs=list(in_specs) + list(gout_specs) + s_specs + [_ANY] * n_a,
        out_specs=[d["spec"] for d in wants],
        out_shape=[jax.ShapeDtypeStruct(d["shape"], d["dtype"]) for d in wants],
        input_output_aliases={n_in + n_g + nc: 0} if n_a else {},
        scratch_shapes=[pltpu.VMEM(tuple(a.shape[ng:]), f32) for a in saved], compiler_params=_params(grid),
    )(*ins, *gouts, *saved, *([into] if n_a else []))
    return list(res)


def _want(idx, shape, spec, acc=None):
    d = _out(shape, spec, acc)
    d["idx"] = idx
    return d


def addn(name, items, t):
    s, w = items[0][0].shape[-2:]
    specs = []
    for a, j in items:
        if j is None:
            specs.append(pl.BlockSpec((t, w), lambda i: (i, 0)))
        else:
            specs.append(pl.BlockSpec((None, t, w), lambda i, _j=j: (_j, i, 0)))

    def fn(*xs):
        y = xs[0]
        for x in xs[1:]:
            y = y + x
        return (y,)

    return run_fwd(name, fn, (s // t,), [a for a, _ in items], specs,
                   [_out((s, w), pl.BlockSpec((t, w), lambda i: (i, 0)))])[0][0]


def _parts(a):
    return [(a, j) for j in range(a.shape[0])]


def _ffn_tile(j_axis, residual):
    def fn(h, g, wg, wu, wo):
        n = _rms(h, g)
        act = _silu(mm(n, wg)) * mm(n, wu)
        y = 0.5 * mm(act, wo)
        if residual:
            y = y + (pl.program_id(j_axis) == 0).astype(f32) * h
        return (y,)
    return fn


def ffn_fwd(name, h, g, wg, wu, wo, t=512, tf=1408):
    s, d = h.shape
    f = wg.shape[1]
    specs = [pl.BlockSpec((t, d), lambda i, j: (i, 0)), _full(g), pl.BlockSpec((d, tf), lambda i, j: (0, j)),
             pl.BlockSpec((d, tf), lambda i, j: (0, j)), pl.BlockSpec((tf, d), lambda i, j: (j, 0))]
    outs = [_out((s, d), pl.BlockSpec((t, d), lambda i, j: (i, 0)), acc=(1,))]
    return run_fwd(name, _ffn_tile(1, True), (s // t, f // tf), [h, g, wg, wu, wo], specs, outs)[0][0]


def ffn_bwd(name, h, g, wg, wu, wo, gout, t=512, tf=256):
    s, d = h.shape
    f = wg.shape[1]
    nj = f // tf
    specs = [pl.BlockSpec((t, d), lambda j, i: (i, 0)), _full(g), pl.BlockSpec((d, tf), lambda j, i: (0, j)),
             pl.BlockSpec((d, tf), lambda j, i: (0, j)), pl.BlockSpec((tf, d), lambda j, i: (j, 0))]
    wants = [_want(0, (nj, s, d), pl.BlockSpec((None, t, d), lambda j, i: (j, i, 0))),
             _want(1, g.shape, _full(g), acc=(0, 1)),
             _want(2, wg.shape, pl.BlockSpec((d, tf), lambda j, i: (0, j)), acc=(1,)),
             _want(3, wu.shape, pl.BlockSpec((d, tf), lambda j, i: (0, j)), acc=(1,)),
             _want(4, wo.shape, pl.BlockSpec((tf, d), lambda j, i: (j, 0)), acc=(1,))]
    return run_bwd(name, _ffn_tile(0, False), (nj, s // t), [h, g, wg, wu, wo], specs,
                   [gout], [pl.BlockSpec((t, d), lambda j, i: (i, 0))], wants)


def _normlin_tile(h, g, w, b):
    return (mm(_rms(h, g), w) + b,)


def normlin_fwd(name, h, g, w, b, t=512, tn=1024):
    s, d = h.shape
    n = w.shape[1]
    specs = [pl.BlockSpec((t, d), lambda i, j: (i, 0)), _full(g), pl.BlockSpec((d, tn), lambda i, j: (0, j)),
             pl.BlockSpec((1, tn), lambda i, j: (0, j))]
    outs = [_out((s, n), pl.BlockSpec((t, tn), lambda i, j: (i, j)))]
    return run_fwd(name, _normlin_tile, (s // t, n // tn), [h, g, w, b], specs, outs)[0][0]


def normlin_bwd(name, h, g, w, b, gout, t=256, tn=1024):
    s, d = h.shape
    n = w.shape[1]
    nj = n // tn
    specs = [pl.BlockSpec((t, d), lambda j, i: (i, 0)), _full(g), pl.BlockSpec((d, tn), lambda j, i: (0, j)),
             pl.BlockSpec((1, tn), lambda j, i: (0, j))]
    wants = [_want(0, (nj, s, d), pl.BlockSpec((None, t, d), lambda j, i: (j, i, 0))),
             _want(1, g.shape, _full(g), acc=(0, 1)),
             _want(2, w.shape, pl.BlockSpec((d, tn), lambda j, i: (0, j)), acc=(1,)),
             _want(3, b.shape, pl.BlockSpec((1, tn), lambda j, i: (0, j)), acc=(1,))]
    return run_bwd(name, _normlin_tile, (nj, s // t), [h, g, w, b], specs,
                   [gout], [pl.BlockSpec((t, tn), lambda j, i: (i, j))], wants)


def _merge_tile(n_axis, residual):
    def fn(h, y, gp, wb, wo):
        part = mm(jax.nn.sigmoid(gp) * mm(y, wb), wo)
        if residual:
            part = part + (pl.program_id(n_axis) == 0).astype(f32) * h
        return (part,)
    return fn


def merge_fwd(name, h, ys, gpre, wb, wo, t=512):
    s, d = h.shape
    specs = [pl.BlockSpec((t, d), lambda i, n: (i, 0)), pl.BlockSpec((t, BW), lambda i, n: (i, n)),
             pl.BlockSpec((t, d), lambda i, n: (i, n)), pl.BlockSpec((None, BW, d), lambda i, n: (n, 0, 0)), _full(wo)]
    outs = [_out((s, d), pl.BlockSpec((t, d), lambda i, n: (i, 0)), acc=(1,))]
    return run_fwd(name, _merge_tile(1, True), (s // t, 4), [h, ys, gpre, wb, wo], specs, outs)[0][0]


def merge_bwd(name, h, ys, gpre, wb, wo, gout, t=256):
    s, d = h.shape
    specs = [pl.BlockSpec((t, d), lambda n, i: (i, 0)), pl.BlockSpec((t, BW), lambda n, i: (i, n)),
             pl.BlockSpec((t, d), lambda n, i: (i, n)), pl.BlockSpec((None, BW, d), lambda n, i: (n, 0, 0)), _full(wo)]
    wants = [_want(1, ys.shape, pl.BlockSpec((t, BW), lambda n, i: (i, n))),
             _want(2, gpre.shape, pl.BlockSpec((t, d), lambda n, i: (i, n))),
             _want(3, wb.shape, pl.BlockSpec((None, BW, d), lambda n, i: (n, 0, 0)), acc=(1,)),
             _want(4, wo.shape, _full(wo), acc=(0, 1))]
    return run_bwd(name, _merge_tile(0, False), (4, s // t), [h, ys, gpre, wb, wo], specs,
                   [gout], [pl.BlockSpec((t, d), lambda n, i: (i, 0))], wants)


def _ple_tile(residual):
    def fn(h, pe, g, wgate, wproj):
        y = jax.nn.sigmoid(mm(_rms(h, g), wgate)) * mm(pe, wproj)
        return (y + h,) if residual else (y,)
    return fn


def ple_fwd(name, h, pe, g, wgate, wproj, t=512):
    s, d = h.shape
    specs = [pl.BlockSpec((t, d), lambda i: (i, 0)), pl.BlockSpec((t, pe.shape[1]), lambda i: (i, 0)),
             _full(g), _full(wgate), _full(wproj)]
    return run_fwd(name, _ple_tile(True), (s // t,), [h, pe, g, wgate, wproj], specs,
                   [_out((s, d), pl.BlockSpec((t, d), lambda i: (i, 0)))])[0][0]


def ple_bwd(name, h, pe, g, wgate, wproj, gout, t=256):
    s, d = h.shape
    specs = [pl.BlockSpec((t, d), lambda i: (i, 0)), pl.BlockSpec((t, pe.shape[1]), lambda i: (i, 0)),
             _full(g), _full(wgate), _full(wproj)]
    wants = [_want(0, h.shape, pl.BlockSpec((t, d), lambda i: (i, 0))), _want(2, g.shape, _full(g), acc=(0,)),
             _want(3, wgate.shape, _full(wgate), acc=(0,)), _want(4, wproj.shape, _full(wproj), acc=(0,))]
    return run_bwd(name, _ple_tile(True), (s // t,), [h, pe, g, wgate, wproj], specs,
                   [gout], [pl.BlockSpec((t, d), lambda i: (i, 0))], wants)


def final_loss(name, h, g, target, t=512):
    s, d = h.shape

    def fn(hh, gg, tt):
        def loss_fn(a, b):
            err = _rms(a, b) - tt
            return 0.5 * jnp.sum(jnp.mean(err * err, axis=-1, keepdims=True), axis=0, keepdims=True)
        loss, vjp = jax.vjp(loss_fn, hh, gg)
        dh, dgain = vjp(jnp.ones((1, 1), f32))
        return loss, dh, dgain

    specs = [pl.BlockSpec((t, d), lambda i: (i, 0)), _full(g), pl.BlockSpec((t, d), lambda i: (i, 0))]
    outs = [_out((1, 1), pl.BlockSpec((1, 1), lambda i: (0, 0)), acc=(0,)),
            _out((s, d), pl.BlockSpec((t, d), lambda i: (i, 0))), _out(g.shape, _full(g), acc=(0,))]
    return run_fwd(name, fn, (s // t,), [h, g, target], specs, outs)[0]


def _rev(nc, rev):
    return (lambda c: nc - 1 - c) if rev else (lambda c: c)


def _s5_ops_tile(are, aim, lstep, bre, bim):
    step = jnp.exp(lstep)
    mag = jnp.exp(are * step)
    ab_re, ab_im = mag * jnp.cos(aim * step), mag * jnp.sin(aim * step)
    den = are * are + aim * aim
    num_re = ab_re - 1.0
    f_re = (num_re * are + ab_im * aim) / den
    f_im = (ab_im * are - num_re * aim) / den
    bb_re = f_re * bre - f_im * bim
    bb_im = f_re * bim + f_im * bre
    pr = jnp.broadcast_to(ab_re, (LS, ab_re.shape[1]))
    pi = jnp.broadcast_to(ab_im, (LS, ab_im.shape[1]))
    k = 1
    while k < LS:
        pr, pi = _cmul(pr, pi, _shift(pr, k, 1.0), _shift(pi, k, 0.0))
        k *= 2
    return ab_re, ab_im, bb_re, bb_im, pr, pi


def _s5_ops_specs(arrs):
    return [pl.BlockSpec((None,) + a.shape[1:], lambda gb: (gb, 0, 0)) for a in arrs]


def s5_ops_fwd(name, raw):
    shapes = [(4, 1, 512), (4, 1, 512), (4, 128, 512), (4, 128, 512), (4, LS, 512), (4, LS, 512)]
    outs = [_out(sh, pl.BlockSpec((None,) + sh[1:], lambda gb: (gb, 0, 0))) for sh in shapes]
    return run_fwd(name, _s5_ops_tile, (4,), raw, _s5_ops_specs(raw), outs)[0]


def s5_ops_bwd(name, raw, gops):
    wants = [_want(i, a.shape, pl.BlockSpec((None,) + a.shape[1:], lambda gb: (gb, 0, 0))) for i, a in enumerate(raw)]
    return run_bwd(name, _s5_ops_tile, (4,), raw, _s5_ops_specs(raw), gops, _s5_ops_specs(gops), wants)


def _s5_tile(carry, u, ab_re, ab_im, bb_re, bb_im, pw_re, pw_im, c_re, c_im, dskip):
    h_re, h_im = carry
    xr, xi = mm(u, bb_re), mm(u, bb_im)
    pr, pi = ab_re, ab_im
    k = 1
    while k < LS:
        sr, si = _cmul(pr, pi, _shift(xr, k, 0.0), _shift(xi, k, 0.0))
        xr, xi = xr + sr, xi + si
        pr, pi = _cmul(pr, pi, pr, pi)
        k *= 2
    cr, ci = _cmul(pw_re, pw_im, h_re, h_im)
    xr, xi = xr + cr, xi + ci
    y = mm(xr, c_re) - mm(xi, c_im) + dskip * u
    return (_pick_row(xr, LS - 1), _pick_row(xi, LS - 1)), (y,)


def _s5_io(u, ops, c_re, c_im, dskip, nc, rev):
    cm = _rev(nc, rev)
    ins = [u] + list(ops) + [c_re, c_im, dskip]
    specs = [pl.BlockSpec((LS, 128), lambda gb, c: (cm(c), 36 + gb))]
    specs += [pl.BlockSpec((None,) + a.shape[1:], lambda gb, c: (gb, 0, 0)) for a in list(ops) + [c_re, c_im]]
    specs += [pl.BlockSpec((1, 128), lambda gb, c: (0, gb))]
    return ins, specs, cm


def s5_fwd(name, proj, ops, c_re, c_im, dskip):
    s = proj.shape[0]
    nc = s // LS
    ins, specs, cm = _s5_io(proj, ops, c_re, c_im, dskip, nc, False)
    outs = [_out((s, BW), pl.BlockSpec((LS, 128), lambda gb, c: (c, gb)))]
    (y,), saved = run_fwd(name, _s5_tile, (4, nc), ins, specs, outs, carry=dict(shapes=[(1, 512), (1, 512)], axis=1))
    return y, saved


def s5_bwd(name, proj, ops, c_re, c_im, dskip, saved, gy, gproj):
    s = proj.shape[0]
    nc = s // LS
    ins, specs, cm = _s5_io(proj, ops, c_re, c_im, dskip, nc, True)
    wants = [_want(0, (s, PW), pl.BlockSpec((LS, 128), lambda gb, c: (cm(c), 36 + gb)))]
    for i, a in enumerate(list(ops) + [c_re, c_im]):
        wants.append(_want(1 + i, a.shape, pl.BlockSpec((None,) + a.shape[1:], lambda gb, c: (gb, 0, 0)), acc=(1,)))
    wants.append(_want(9, dskip.shape, pl.BlockSpec((1, 128), lambda gb, c: (0, gb)), acc=(1,)))
    return run_bwd(name, _s5_tile, (4, nc), ins, specs, [gy], [pl.BlockSpec((LS, 128), lambda gb, c: (cm(c), gb))],
                   wants, carry=dict(axis=1, saved=saved, rev=lambda g: (g[0], cm(g[1]))), into=gproj)


def _s5_glu_tile(y, w, b):
    z = _gelu(y)
    return (z * jax.nn.sigmoid(mm(z, w) + b),)


def s5_glu_fwd(name, y, w, b, t=512):
    s = y.shape[0]
    spec = pl.BlockSpec((t, BW), lambda i: (i, 0))
    return run_fwd(name, _s5_glu_tile, (s // t,), [y, w, b], [spec, _full(w), _full(b)], [_out((s, BW), spec)])[0][0]


def s5_glu_bwd(name, y, w, b, gout, gout_col, t=512):
    s = y.shape[0]
    spec = pl.BlockSpec((t, BW), lambda i: (i, 0))
    wants = [_want(0, y.shape, spec), _want(1, w.shape, _full(w), acc=(0,)), _want(2, b.shape, _full(b), acc=(0,))]
    return run_bwd(name, _s5_glu_tile, (s // t,), [y, w, b], [spec, _full(w), _full(b)], [gout],
                   [pl.BlockSpec((t, BW), lambda i: (i, gout_col))], wants)


def _lru_tile(carry, xb, gate, cw, cb, wr, br, wi, bi, lam):
    h_in, prev8 = carry
    xc = _conv4(xb, prev8, cw, cb)
    r = jax.nn.sigmoid(mm(xc, wr) + br)
    ig = jax.nn.sigmoid(mm(xc, wi) + bi)
    log_a = -LRU_C * r * _softplus(-lam)
    a = jnp.exp(log_a)
    b = jnp.sqrt(-_expm1(2.0 * log_a)) * (ig * xc)
    k = 1
    while k < LS:
        b = b + a * _shift(b, k, 0.0)
        a = a * _shift(a, k, 1.0)
        k *= 2
    h = b + a * h_in
    return (_pick_row(h, LS - 1), xb[LS - 8:, :]), (h * _gelu(gate),)


def _lru_io(proj, ws, nc, rev):
    cm = _rev(nc, rev)
    ins = [proj, proj] + list(ws)
    specs = [pl.BlockSpec((LS, BW), lambda c: (cm(c), 4)), pl.BlockSpec((LS, BW), lambda c: (cm(c), 5))]
    specs += [_full(a) for a in ws]
    return ins, specs, cm


def lru_fwd(name, proj, ws):
    s = proj.shape[0]
    nc = s // LS
    ins, specs, cm = _lru_io(proj, ws, nc, False)
    outs = [_out((s, BW), pl.BlockSpec((LS, BW), lambda c: (c, 0)))]
    (y,), saved = run_fwd(name, _lru_tile, (nc,), ins, specs, outs, carry=dict(shapes=[(1, BW), (8, BW)], axis=0))
    return y, saved


def lru_bwd(name, proj, ws, saved, gy, gy_col, gproj):
    s = proj.shape[0]
    nc = s // LS
    ins, specs, cm = _lru_io(proj, ws, nc, True)
    wants = [_want((0, 1), (s, PW), pl.BlockSpec((LS, 2 * BW), lambda c: (cm(c), 2)))]
    wants += [_want(2 + i, a.shape, _full(a), acc=(0,)) for i, a in enumerate(ws)]
    return run_bwd(name, _lru_tile, (nc,), ins, specs, [gy], [pl.BlockSpec((LS, BW), lambda c: (cm(c), gy_col))], wants,
                   carry=dict(axis=0, saved=saved, rev=lambda g: (cm(g[0]),)), into=gproj)


def _causal(n):
    return _rows((n, n)) >= _lanes((n, n))


def _decay(col, rowv):
    causal = _causal(col.shape[0])
    return jnp.where(causal, jnp.exp(jnp.where(causal, col - rowv, 0.0)), 0.0)


def _m2_tile(carry, z, xs_raw, b_raw, c_raw, small, cwx, cwb, cwc, cbx, cbb, cbc, dtb, alog, dsk, ng):
    state, px, pb, pc = carry
    n = CHUNK
    xs = _silu(_conv4(xs_raw, px, cwx, cbx))
    bm = _silu(_conv4(b_raw, pb, cwb, cbb))
    cmx = _silu(_conv4(c_raw, pc, cwc, cbc))
    expand = (_lanes((16, BW)) // 64 == _rows((16, BW))).astype(f32)
    tri = _causal(n).astype(f32)
    triu = (_rows((n, n)) <= _lanes((n, n))).astype(f32)
    dt = _softplus(small + dtb)
    da = dt * (-jnp.exp(alog))
    cs = lmm(tri, da)
    cs_t = rmm_tn(da, triu)
    cs_w = rmm(cs, expand)
    last_w = _pick_row(cs_w, n - 1)
    xdt = xs * rmm(dt, expand)
    g0 = _lanes((n, BW)) < 256
    bm0, bm1, cm0, cm1 = bm[:, :128], bm[:, 128:], cmx[:, :128], cmx[:, 128:]
    cb0, cb1 = mm_nt(cm0, bm0), mm_nt(cm1, bm1)
    y = jnp.where(g0, mm(cm0, state), mm(cm1, state)) * jnp.exp(cs_w)
    for h in range(8):
        sc = (cb0 if h < 4 else cb1) * _decay(_pick_lane(cs, h), _pick_row(cs_t, h))
        y = y + jnp.where(_lanes((n, BW)) // 64 == h, mm(sc, xdt), 0.0)
    xd = xdt * jnp.exp(last_w - cs_w)
    g0s = _lanes((128, BW)) < 256
    state_out = state * jnp.exp(last_w) + jnp.where(g0s, mm_tn(bm0, xd), mm_tn(bm1, xd))
    y = (y + dsk * xs) * _silu(z)
    return (state_out, xs_raw[n - 8:, :], b_raw[n - 8:, :], c_raw[n - 8:, :]), (_rms(y, ng),)


def _m2_io(proj, small, ws, nc, rev):
    cm = _rev(nc, rev)
    n = CHUNK
    ins = [proj, proj, proj, proj, small] + list(ws)
    specs = [pl.BlockSpec((n, BW), lambda c: (cm(c), 6)), pl.BlockSpec((n, BW), lambda c: (cm(c), 7)),
             pl.BlockSpec((n, 256), lambda c: (cm(c), 16)), pl.BlockSpec((n, 256), lambda c: (cm(c), 17)),
             pl.BlockSpec((n, 16), lambda c: (cm(c), 0))]
    specs += [_full(a) for a in ws]
    return ins, specs, cm


_M2_CARRY = [(128, BW), (8, BW), (8, 256), (8, 256)]


def m2_fwd(name, proj, small, ws):
    s = proj.shape[0]
    nc = s // CHUNK
    ins, specs, cm = _m2_io(proj, small, ws, nc, False)
    outs = [_out((s, BW), pl.BlockSpec((CHUNK, BW), lambda c: (c, 0)))]
    (y,), saved = run_fwd(name, _m2_tile, (nc,), ins, specs, outs, carry=dict(shapes=_M2_CARRY, axis=0))
    return y, saved


def m2_bwd(name, proj, small, ws, saved, gy, gy_col, gproj):
    s = proj.shape[0]
    nc = s // CHUNK
    ins, specs, cm = _m2_io(proj, small, ws, nc, True)
    n = CHUNK
    wants = [_want((0, 1, 2, 3), (s, PW), pl.BlockSpec((n, 3 * BW), lambda c: (cm(c), 2))),
             _want(4, (s, 16), pl.BlockSpec((n, 16), lambda c: (cm(c), 0)))]
    wants += [_want(5 + i, a.shape, _full(a), acc=(0,)) for i, a in enumerate(ws)]
    return run_bwd(name, _m2_tile, (nc,), ins, specs, [gy], [pl.BlockSpec((n, BW), lambda c: (cm(c), gy_col))], wants,
                   carry=dict(axis=0, saved=saved, rev=lambda g: (cm(g[0]),)), into=gproj)


def _l2n(x):
    return x * lax.rsqrt(jnp.sum(x * x, axis=-1, keepdims=True) + EPS)


def _gdn_tile(carry, q_raw, k_raw, v_raw, gate, small, cwq, cwk, cwv, dtb, alog, ng):
    states, (pq, pk, pv) = carry[:4], carry[4:]
    n = CHUNK
    qc = _silu(_conv4(q_raw, pq, cwq, None))
    kc = _silu(_conv4(k_raw, pk, cwk, None))
    vc = _silu(_conv4(v_raw, pv, cwv, None))
    beta16 = jax.nn.sigmoid(small)
    g16 = -jnp.exp(alog) * _softplus(small + dtb)
    tri = _causal(n).astype(f32)
    triu = (_rows((n, n)) <= _lanes((n, n))).astype(f32)
    cs16 = lmm(tri, g16)
    cs16_t = rmm_tn(g16, triu)
    strict = _rows((n, n)) > _lanes((n, n))
    outs, new_states = [], []
    for h in range(4):
        sl = slice(128 * h, 128 * (h + 1))
        state = states[h]
        q = _l2n(qc[:, sl]) * (128 ** -0.5)
        k = _l2n(kc[:, sl])
        v = vc[:, sl]
        beta = _pick_lane(beta16, 8 + h)
        col = _pick_lane(cs16, 12 + h)
        decay = _decay(col, _pick_row(cs16_t, 12 + h))
        kb = k * beta
        t = _tri_inv(jnp.where(strict, mm_nt(kb, k) * decay, 0.0))
        e_col = jnp.exp(col)
        u = xmm(t, v * beta)
        w = xmm(t, kb * e_col)
        qk = mm_nt(q, k) * decay
        last = _pick_row(col, n - 1)
        v_new = u - mm(w, state)
        o = mm(q * e_col, state) + mm(qk, v_new)
        new_states.append(state * jnp.exp(last) + mm_tn(k * jnp.exp(last - col), v_new))
        outs.append(_rms(o, ng) * _silu(gate[:, sl]))
    carry_out = tuple(new_states) + (q_raw[n - 8:, :], k_raw[n - 8:, :], v_raw[n - 8:, :])
    return carry_out, (jnp.concatenate(outs, axis=1),)


def _gdn_io(proj, small, cw, dtb, alog, ng, nc, rev):
    cm = _rev(nc, rev)
    n = CHUNK
    ins = [proj, proj, proj, proj, small, cw, cw, cw, dtb, alog, ng]
    specs = [pl.BlockSpec((n, BW), lambda c, _j=j: (cm(c), _j)) for j in (0, 1, 2, 3)]
    specs += [pl.BlockSpec((n, 16), lambda c: (cm(c), 0))]
    specs += [pl.BlockSpec((4, BW), lambda c, _j=j: (0, _j)) for j in (0, 1, 2)]
    specs += [_full(dtb), _full(alog), _full(ng)]
    return ins, specs, cm


_GDN_CARRY = [(128, 128)] * 4 + [(8, BW)] * 3


def gdn_fwd(name, proj, small, cw, dtb, alog, ng):
    s = proj.shape[0]
    nc = s // CHUNK
    ins, specs, cm = _gdn_io(proj, small, cw, dtb, alog, ng, nc, False)
    outs = [_out((s, BW), pl.BlockSpec((CHUNK, BW), lambda c: (c, 0)))]
    (y,), saved = run_fwd(name, _gdn_tile, (nc,), ins, specs, outs, carry=dict(shapes=_GDN_CARRY, axis=0))
    return y, saved


def gdn_bwd(name, proj, small, cw, dtb, alog, ng, saved, gy, gy_col):
    s = proj.shape[0]
    nc = s // CHUNK
    n = CHUNK
    ins, specs, cm = _gdn_io(proj, small, cw, dtb, alog, ng, nc, True)
    wants = [_want((0, 1, 2, 3), (s, PW), pl.BlockSpec((n, 4 * BW), lambda c: (cm(c), 0))),
             _want(4, (s, 16), pl.BlockSpec((n, 16), lambda c: (cm(c), 0)))]
    wants += [_want(5 + i, (4, BW), pl.BlockSpec((4, BW), lambda c: (0, 0)), acc=(0,)) for i in range(3)]
    wants += [_want(8, dtb.shape, _full(dtb), acc=(0,)), _want(9, alog.shape, _full(alog), acc=(0,)),
              _want(10, ng.shape, _full(ng), acc=(0,))]
    return run_bwd(name, _gdn_tile, (nc,), ins, specs, [gy], [pl.BlockSpec((n, BW), lambda c: (cm(c), gy_col))], wants,
                   carry=dict(axis=0, saved=saved, rev=lambda g: (cm(g[0]),)))


def _perm_cols(w):
    pad = jnp.zeros(w.shape[:-1] + (PW - IN_WIDTH,), w.dtype)
    return jnp.concatenate([w[..., 3080:5128], w[..., 512:3072], w[..., :512], w[..., 3072:3080], w[..., 5128:5136], pad], axis=-1)


def _unperm_cols(g):
    return jnp.concatenate([g[..., 4608:5120], g[..., 2048:4608], g[..., 5120:5128], g[..., :2048], g[..., 5128:5136]], axis=-1)


def _bd(blocks):
    n, a, b = blocks.shape
    eye = jnp.eye(n, dtype=blocks.dtype)
    return jnp.einsum("nab,nm->namb", blocks, eye).reshape(n * a, n * b)


def _layer_layout(lw):
    o = {}
    row = lambda a: a.reshape(1, -1)
    o["b_gate"] = row(lw["b_gate"])
    o["s5_are"] = lw["s5_a_re"].reshape(4, 1, 512)
    o["s5_aim"] = lw["s5_a_im"].reshape(4, 1, 512)
    o["s5_lstep"] = jnp.repeat(lw["s5_log_step"], 64).reshape(4, 1, 512)
    bt = lambda b: jax.vmap(_bd)(jnp.swapaxes(b, 1, 2).reshape(4, 8, 16, 64))
    o["s5_bre"], o["s5_bim"] = bt(lw["s5_b_re"]), bt(lw["s5_b_im"])
    ct = lambda c: jax.vmap(_bd)(jnp.swapaxes(c, 1, 2).reshape(4, 8, 64, 16))
    o["s5_cre"], o["s5_cim"] = ct(lw["s5_c_re"]), ct(lw["s5_c_im"])
    o["s5_d"] = row(lw["s5_d"])
    o["s5_b_glu"] = row(lw["s5_b_glu"])
    o["lru_conv_b"], o["lru_b_r"], o["lru_b_i"], o["lru_lambda"] = (row(lw[k]) for k in ("lru_conv_b", "lru_b_r", "lru_b_i", "lru_lambda"))
    o["lru_wr"], o["lru_wi"] = _bd(lw["lru_w_r"]), _bd(lw["lru_w_i"])
    cw, cb = lw["m2_conv_w"], lw["m2_conv_b"]
    o["m2_cwx"], o["m2_cwb"], o["m2_cwc"] = cw[:, :512], cw[:, 512:768], cw[:, 768:]
    o["m2_cbx"], o["m2_cbb"], o["m2_cbc"] = row(cb[:512]), row(cb[512:768]), row(cb[768:])
    o["m2_dtb"] = jnp.pad(lw["m2_dt_bias"], (0, 8)).reshape(1, 16)
    o["m2_alog"] = jnp.pad(lw["m2_a_log"], (0, 8)).reshape(1, 16)
    o["m2_dsk"] = jnp.repeat(lw["m2_d"], 64).reshape(1, 512)
    o["m2_norm"] = row(lw["m2_norm"])
    o["gdn_dtb"] = jnp.pad(lw["gdn_dt_bias"], (12, 0)).reshape(1, 16)
    o["gdn_alog"] = jnp.pad(lw["gdn_a_log"], (12, 0)).reshape(1, 16)
    o["gdn_norm"] = row(lw["gdn_norm"])
    for k in ("ffn1_norm", "mix_norm", "ffn2_norm", "ple_norm"):
        o[k] = row(lw[k])
    return o


_BIG_PLAIN = ("w_gate", "s5_w_glu", "w_branch", "w_out", "ple_w_gate", "ple_w_proj")


def big_layout(w):
    o = {k: w[k] for k in _BIG_PLAIN}
    for f in ("ffn1", "ffn2"):
        o[f + "_wg"], o[f + "_wu"] = w[f + "_w_in"][:, :FFN_DIM], w[f + "_w_in"][:, FFN_DIM:]
        o[f + "_wo"] = w[f + "_w_out"]
    o["w_in"] = _perm_cols(w["w_in"])
    return {k: v.astype(bf16) for k, v in o.items()}


def big_unlayout(g):
    o = {k: g[k] for k in _BIG_PLAIN}
    for f in ("ffn1", "ffn2"):
        o[f + "_w_in"] = jnp.concatenate([g[f + "_wg"], g[f + "_wu"]], axis=1)
        o[f + "_w_out"] = g[f + "_wo"]
    o["w_in"] = _unperm_cols(g["w_in"])
    return o


_SMALL_KEYS = ("b_gate", "s5_log_step", "s5_a_re", "s5_a_im", "s5_b_re", "s5_b_im", "s5_c_re", "s5_c_im", "s5_d",
               "s5_b_glu", "lru_conv_b", "lru_w_r", "lru_b_r", "lru_w_i", "lru_b_i", "lru_lambda", "m2_conv_w",
               "m2_conv_b", "m2_dt_bias", "m2_a_log", "m2_d", "m2_norm", "gdn_dt_bias", "gdn_a_log", "gdn_norm",
               "ffn1_norm", "mix_norm", "ffn2_norm", "ple_norm")


def _layer_fwd(i, h0, pe, lw, big):
    n = f"l{i}_"
    lay = _layer_layout(lw)
    a = {"h0": h0, "lay": lay}
    h1 = ffn_fwd(n + "ffn1_fwd", h0, lay["ffn1_norm"], big["ffn1_wg"], big["ffn1_wu"], big["ffn1_wo"])
    zero_b = jnp.zeros((1, PW), f32)
    proj = normlin_fwd(n + "inproj_fwd", h1, lay["mix_norm"], big["w_in"], zero_b, tn=896)
    gpre = normlin_fwd(n + "gate_fwd", h1, lay["mix_norm"], big["w_gate"], lay["b_gate"], tn=1024)
    small = proj[:, SMALL_OFF:SMALL_OFF + 16]
    raw = [lay["s5_are"], lay["s5_aim"], lay["s5_lstep"], lay["s5_bre"], lay["s5_bim"]]
    ops = s5_ops_fwd(n + "s5ops_fwd", raw)
    y5, sv5 = s5_fwd(n + "s5_fwd", proj, ops, lay["s5_cre"], lay["s5_cim"], lay["s5_d"])
    ya = s5_glu_fwd(n + "s5glu_fwd", y5, big["s5_w_glu"], lay["s5_b_glu"])
    lru_ws = [lw["lru_conv_w"], lay["lru_conv_b"], lay["lru_wr"], lay["lru_b_r"], lay["lru_wi"], lay["lru_b_i"], lay["lru_lambda"]]
    yb, svb = lru_fwd(n + "lru_fwd", proj, lru_ws)
    m2_ws = [lay[k] for k in ("m2_cwx", "m2_cwb", "m2_cwc", "m2_cbx", "m2_cbb", "m2_cbc", "m2_dtb", "m2_alog", "m2_dsk", "m2_norm")]
    yc, svc = m2_fwd(n + "m2_fwd", proj, small, m2_ws)
    yd, svd = gdn_fwd(n + "gdn_fwd", proj, small, lw["gdn_conv_w"], lay["gdn_dtb"], lay["gdn_alog"], lay["gdn_norm"])
    ys = jnp.concatenate([ya, yb, yc, yd], axis=1)
    h2 = merge_fwd(n + "merge_fwd", h1, ys, gpre, big["w_branch"], big["w_out"])
    h3 = ffn_fwd(n + "ffn2_fwd", h2, lay["ffn2_norm"], big["ffn2_wg"], big["ffn2_wu"], big["ffn2_wo"])
    h4 = ple_fwd(n + "ple_fwd", h3, pe, lay["ple_norm"], big["ple_w_gate"], big["ple_w_proj"])
    a.update(h1=h1, proj=proj, gpre=gpre, small=small, raw=raw, ops=ops, y5=y5, sv5=sv5, lru_ws=lru_ws, svb=svb,
             m2_ws=m2_ws, svc=svc, svd=svd, ys=ys, h2=h2, h3=h3, pe=pe)
    return h4, a


def small_into(name, a, b, gproj, t=512):
    s, w = a.shape
    pad_w = PW - SMALL_OFF

    def body(a_ref, b_ref, _, o_ref):
        place = (_rows((w, pad_w)) == _lanes((w, pad_w))).astype(f32)
        o_ref[...] = _dg(a_ref[...] + b_ref[...], place, _NN, (3, 1))

    row = pl.BlockSpec((t, w), lambda i: (i, 0))
    return pl.pallas_call(
        body, name=name, grid=(s // t,), in_specs=[row, row, _ANY],
        out_specs=pl.BlockSpec((t, pad_w), lambda i: (i, SMALL_OFF // pad_w)),
        out_shape=jax.ShapeDtypeStruct(gproj.shape, gproj.dtype), input_output_aliases={2: 0}, compiler_params=_params((1,)),
    )(a, b, gproj)


def _layer_bwd(i, a, lw, big, gh4):
    n = f"l{i}_"
    lay = a["lay"]
    gb, gl = {}, {}
    s = gh4.shape[0]
    t = 512
    gh3, gl["ple_norm"], gb["ple_w_gate"], gb["ple_w_proj"] = ple_bwd(
        n + "ple_bwd", a["h3"], a["pe"], lay["ple_norm"], big["ple_w_gate"], big["ple_w_proj"], gh4)
    dh2, gl["ffn2_norm"], gb["ffn2_wg"], gb["ffn2_wu"], gb["ffn2_wo"] = ffn_bwd(
        n + "ffn2_bwd", a["h2"], lay["ffn2_norm"], big["ffn2_wg"], big["ffn2_wu"], big["ffn2_wo"], gh3)
    gh2 = addn(n + "gh2", [(gh3, None)] + _parts(dh2), t)
    gys, ggpre, gb["w_branch"], gb["w_out"] = merge_bwd(
        n + "merge_bwd", a["h1"], a["ys"], a["gpre"], big["w_branch"], big["w_out"], gh2)
    rd = gdn_bwd(n + "gdn_bwd", a["proj"], a["small"], lw["gdn_conv_w"], lay["gdn_dtb"], lay["gdn_alog"], lay["gdn_norm"], a["svd"], gys, 3)
    gproj, dsm_d = rd[:2]
    gl["gdn_conv_w"] = jnp.concatenate(rd[2:5], axis=1)
    gl["gdn_dtb"], gl["gdn_alog"], gl["gdn_norm"] = rd[5:]
    rb = lru_bwd(n + "lru_bwd", a["proj"], a["lru_ws"], a["svb"], gys, 1, gproj)
    gproj = rb[0]
    gl["lru_conv_w"], gl["lru_conv_b"], gl["lru_wr"], gl["lru_b_r"], gl["lru_wi"], gl["lru_b_i"], gl["lru_lambda"] = rb[1:]
    rc = m2_bwd(n + "m2_bwd", a["proj"], a["small"], a["m2_ws"], a["svc"], gys, 2, gproj)
    gproj, dsm_c = rc[:2]
    for k, v in zip(("m2_cwx", "m2_cwb", "m2_cwc", "m2_cbx", "m2_cbb", "m2_cbc", "m2_dtb", "m2_alog", "m2_dsk", "m2_norm"), rc[2:]):
        gl[k] = v
    gy5, gb["s5_w_glu"], gl["s5_b_glu"] = s5_glu_bwd(n + "s5glu_bwd", a["y5"], big["s5_w_glu"], lay["s5_b_glu"], gys, 0)
    r5 = s5_bwd(n + "s5_bwd", a["proj"], a["ops"], lay["s5_cre"], lay["s5_cim"], lay["s5_d"], a["sv5"], gy5, gproj)
    gproj, gops, gl["s5_cre"], gl["s5_cim"], gl["s5_d"] = r5[0], r5[1:7], r5[7], r5[8], r5[9]
    gl["s5_are"], gl["s5_aim"], gl["s5_lstep"], gl["s5_bre"], gl["s5_bim"] = s5_ops_bwd(n + "s5ops_bwd", a["raw"], gops)
    gproj = small_into(n + "gsmall", dsm_c, dsm_d, gproj)
    zero_b = jnp.zeros((1, PW), f32)
    dh1_p, gmn_p, gb["w_in"], _ = normlin_bwd(n + "inproj_bwd", a["h1"], lay["mix_norm"], big["w_in"], zero_b, gproj, tn=896)
    dh1_g, gmn_g, gb["w_gate"], gl["b_gate"] = normlin_bwd(n + "gate_bwd", a["h1"], lay["mix_norm"], big["w_gate"], lay["b_gate"], ggpre, tn=1024)
    gh1 = addn(n + "gh1", [(gh2, None)] + _parts(dh1_p) + _parts(dh1_g), t)
    dh0, gl["ffn1_norm"], gb["ffn1_wg"], gb["ffn1_wu"], gb["ffn1_wo"] = ffn_bwd(
        n + "ffn1_bwd", a["h0"], lay["ffn1_norm"], big["ffn1_wg"], big["ffn1_wu"], big["ffn1_wo"], gh1)
    gh0 = addn(n + "gh0", [(gh1, None)] + _parts(dh0), t)
    gl["mix_norm"] = gmn_p + gmn_g
    return gh0, gb, gl


def _local_step(x, p, target, bigs, smalls, final_norm):
    h = x
    acts = []
    for i in range(DEPTH):
        h, a = _layer_fwd(i, h, p[i], smalls[i], bigs[i])
        acts.append(a)
    fg = final_norm.reshape(1, -1)
    loss, gh, gfn = final_loss("final_loss", h, fg, target)
    gbs, gss = [None] * DEPTH, [None] * DEPTH
    for i in reversed(range(DEPTH)):
        gh, gb, gl = _layer_bwd(i, acts[i], smalls[i], bigs[i], gh)
        _, pull = jax.vjp(_layer_layout, smalls[i])
        lay_g = {k: gl[k] for k in acts[i]["lay"]}
        gs = pull(lay_g)[0]
        gs = dict(gs)
        gs["lru_conv_w"] = gl["lru_conv_w"]
        gs["gdn_conv_w"] = gl["gdn_conv_w"]
        gbs[i], gss[i] = gb, gs
    return loss, gh, gbs, gss, gfn.reshape(-1)


MESH_AXES = ("x", "y", "c")
PACK_W = 1024
PACK_ROWS = 256

W_NAMES = ("ffn1_norm", "ffn1_w_in", "ffn1_w_out", "mix_norm", "w_in", "w_gate", "b_gate", "s5_log_step", "s5_a_re",
           "s5_a_im", "s5_b_re", "s5_b_im", "s5_c_re", "s5_c_im", "s5_d", "s5_w_glu", "s5_b_glu", "lru_conv_w",
           "lru_conv_b", "lru_w_r", "lru_b_r", "lru_w_i", "lru_b_i", "lru_lambda", "m2_conv_w", "m2_conv_b",
           "m2_dt_bias", "m2_a_log", "m2_d", "m2_norm", "gdn_conv_w", "gdn_dt_bias", "gdn_a_log", "gdn_norm",
           "w_branch", "w_out", "ffn2_norm", "ffn2_w_in", "ffn2_w_out", "ple_norm", "ple_w_gate", "ple_w_proj",
           "final_norm")
COL_SHARDED = ("ffn1_w_in", "w_in", "w_gate", "lru_conv_w", "m2_conv_w", "gdn_conv_w", "w_branch", "ffn2_w_in", "ple_w_proj")
ROW_SHARDED = ("ffn1_w_out", "s5_w_glu", "w_out", "ffn2_w_out", "ple_w_gate")
BIG_NAMES = ("ffn1_w_in", "ffn1_w_out", "w_in", "w_gate", "s5_w_glu", "w_branch", "w_out", "ffn2_w_in", "ffn2_w_out",
             "ple_w_gate", "ple_w_proj")
CONV_NAMES = ("lru_conv_w", "m2_conv_w", "gdn_conv_w")
SHARDED = BIG_NAMES + CONV_NAMES
REPLICATED = tuple(k for k in W_NAMES if k not in SHARDED)


def _gathered_to_full(name, g):
    if name in COL_SHARDED:
        g = jnp.moveaxis(g, 0, -2)
        return g.reshape(g.shape[:-2] + (g.shape[-2] * g.shape[-1],))
    g = jnp.moveaxis(g, 0, 1)
    return g.reshape((g.shape[0], g.shape[1] * g.shape[2]) + g.shape[3:])


def _full_to_scattered(name, w):
    if name in COL_SHARDED:
        w = w.reshape(w.shape[:-1] + (N_DEV, w.shape[-1] // N_DEV))
        return jnp.moveaxis(w, -2, 0)
    w = w.reshape((w.shape[0], N_DEV, w.shape[1] // N_DEV) + w.shape[2:])
    return jnp.moveaxis(w, 1, 0)


def _pack(arrs, lead=0):
    lshape = arrs[0].shape[:lead]
    flat = jnp.concatenate([a.reshape(lshape + (-1,)) for a in arrs], axis=-1)
    n = flat.shape[-1]
    rows = -(-n // (PACK_W * PACK_ROWS)) * PACK_ROWS
    flat = jnp.pad(flat, [(0, 0)] * lead + [(0, rows * PACK_W - n)])
    return flat.reshape(lshape + (rows, PACK_W))


def _unpack(buf, shapes, lead=0):
    lshape = buf.shape[:lead]
    flat = buf.reshape(lshape + (-1,))
    out, off = [], 0
    for sh in shapes:
        n = math.prod(sh)
        out.append(flat[..., off:off + n].reshape(lshape + tuple(sh)))
        off += n
    return out


def _peer(k):
    mx, my, mc = (lax.axis_index(a) for a in MESH_AXES)
    px = 1 - mx if k & 4 else mx
    py = 1 - my if k & 2 else my
    pc = 1 - mc if k & 1 else mc
    return (px, py, pc), 4 * px + 2 * py + pc


def all_gather(name, xs):
    n = len(xs)

    def body(*refs):
        x_refs, out_refs = refs[:n], refs[n:2 * n]
        send_sems, recv_sems, local_sems = refs[2 * n:]
        mx, my, mc = (lax.axis_index(a) for a in MESH_AXES)
        me, sibling = (mx, my, mc), (mx, my, 1 - mc)
        chips = [(1 - mx, my), (mx, 1 - my), (1 - mx, 1 - my)]

        def slot(i, px, py, pc):
            return out_refs[i].at[4 * px + 2 * py + pc]

        def copy(k, i, block, to, src=None):
            return pltpu.make_async_remote_copy(
                src_ref=slot(i, *block) if src is None else src, dst_ref=slot(i, *block),
                send_sem=send_sems.at[k, i], recv_sem=recv_sems.at[k, i], device_id=to, device_id_type=pl.DeviceIdType.MESH)

        mine = [pltpu.make_async_copy(x_refs[i], slot(i, *me), local_sems.at[i]) for i in range(n)]
        first = []
        for i in range(n):
            mine[i].start()
            first.append(copy(0, i, me, sibling, src=x_refs[i]))
            first += [copy(1 + j, i, me, (*chip, mc), src=x_refs[i]) for j, chip in enumerate(chips)]
        for cp in first:
            cp.start()
        passed = []
        for i in range(n):
            for j, chip in enumerate(chips):
                copy(1 + j, i, (*chip, mc), me).wait_recv()
                cp = copy(4 + j, i, (*chip, mc), sibling)
                cp.start()
                passed.append(cp)
        for i in range(n):
            copy(0, i, sibling, me).wait_recv()
            for j, chip in enumerate(chips):
                copy(4 + j, i, (*chip, 1 - mc), me).wait_recv()
        for cp in first + passed:
            cp.wait_send()
        for cp in mine:
            cp.wait()

    res = pl.pallas_call(
        body, name=name, out_shape=[jax.ShapeDtypeStruct((N_DEV,) + x.shape, x.dtype) for x in xs],
        in_specs=[_ANY] * n, out_specs=[_ANY] * n,
        scratch_shapes=[pltpu.SemaphoreType.DMA((7, n)), pltpu.SemaphoreType.DMA((7, n)), pltpu.SemaphoreType.DMA((n,))],
    )(*xs)
    return list(res)


def exchange(name, gs):
    n = len(gs)

    def body(*refs):
        g_refs, out_refs = refs[:n], refs[n:2 * n]
        send_sems, recv_sems, local_sems = refs[2 * n:]
        mx, my, mc = (lax.axis_index(a) for a in MESH_AXES)
        me = 4 * mx + 2 * my + mc
        mine, copies = [], []
        for i in range(n):
            cp = pltpu.make_async_copy(g_refs[i].at[me], out_refs[i].at[me], local_sems.at[i])
            cp.start()
            mine.append(cp)
        for k in range(1, N_DEV):
            peer, pidx = _peer(k)
            for i in range(n):
                cp = pltpu.make_async_remote_copy(
                    src_ref=g_refs[i].at[pidx], dst_ref=out_refs[i].at[me], send_sem=send_sems.at[k - 1, i],
                    recv_sem=recv_sems.at[k - 1, i], device_id=peer, device_id_type=pl.DeviceIdType.MESH)
                cp.start()
                copies.append(cp)
        for k in range(1, N_DEV):
            peer, pidx = _peer(k)
            for i in range(n):
                pltpu.make_async_remote_copy(
                    src_ref=g_refs[i].at[pidx], dst_ref=out_refs[i].at[pidx], send_sem=send_sems.at[k - 1, i],
                    recv_sem=recv_sems.at[k - 1, i], device_id=peer, device_id_type=pl.DeviceIdType.MESH).wait_recv()
        for cp in copies:
            cp.wait_send()
        for cp in mine:
            cp.wait()

    res = pl.pallas_call(
        body, name=name, out_shape=[jax.ShapeDtypeStruct(g.shape, g.dtype) for g in gs], in_specs=[_ANY] * n, out_specs=[_ANY] * n,
        scratch_shapes=[pltpu.SemaphoreType.DMA((7, n)), pltpu.SemaphoreType.DMA((7, n)), pltpu.SemaphoreType.DMA((n,))],
    )(*gs)
    return list(res)


def sum_slots(name, buf):
    return addn(name, _parts(buf), PACK_ROWS)


def _adamw_math(ww, gg, mm_, vv):
    m2 = ADAM_B1 * mm_ + (1.0 - ADAM_B1) * gg
    v2 = ADAM_B2 * vv + (1.0 - ADAM_B2) * (gg * gg)
    m_hat = m2 / (1.0 - ADAM_B1 ** ADAM_STEP)
    v_hat = v2 / (1.0 - ADAM_B2 ** ADAM_STEP)
    delta = -ADAM_LR * (m_hat / (jnp.sqrt(v_hat) + ADAM_EPS) + ADAM_WD * ww)
    return delta, m2, v2


def adamw(name, w, g, m, v):
    spec = pl.BlockSpec((PACK_ROWS, PACK_W), lambda i: (i, 0))
    return run_fwd(name, _adamw_math, (w.shape[0] // PACK_ROWS,), [w, g, m, v], [spec] * 4, [_out(w.shape, spec)] * 3)[0]


def reduce_adamw(name, slots, w, m, v):
    shape = w.shape
    r, c = shape[-2:]
    a = math.prod(shape[:-2])
    tr = r
    while tr * c * 4 > (1 << 20) and tr % 16 == 0:
        tr //= 2
    s3 = slots.reshape((N_DEV, a, r, c))
    w3, m3, v3 = (t.reshape((a, r, c)) for t in (w, m, v))

    def fn(*t):
        g = t[0].astype(f32)
        for d in range(1, N_DEV):
            g = g + t[d].astype(f32)
        return (g,) + _adamw_math(t[N_DEV], g, t[N_DEV + 1], t[N_DEV + 2])

    specs = [pl.BlockSpec((None, None, tr, c), lambda i, j, _d=d: (_d, i, j, 0)) for d in range(N_DEV)]
    spec = pl.BlockSpec((None, tr, c), lambda i, j: (i, j, 0))
    res = run_fwd(name, fn, (a, r // tr), [s3] * N_DEV + [w3, m3, v3], specs + [spec] * 3, [_out((a, r, c), spec)] * 4)[0]
    return [t.reshape(shape) for t in res]


def kernel(*args):
    nw = len(W_NAMES)
    x, p = args[0], args[1]
    w = dict(zip(W_NAMES, args[2:2 + nw]))
    target = args[2 + nw]
    m = dict(zip(W_NAMES, args[3 + nw:3 + 2 * nw]))
    v = dict(zip(W_NAMES, args[3 + 2 * nw:3 + 3 * nw]))

    gathered = all_gather("ag_weights", [w[k].astype(bf16) for k in BIG_NAMES] + [w[k] for k in CONV_NAMES])
    full = {k: _gathered_to_full(k, g) for k, g in zip(SHARDED, gathered)}

    bigs = [big_layout({k: full[k][i] for k in BIG_NAMES}) for i in range(DEPTH)]
    smalls = []
    for i in range(DEPTH):
        sm = {k: w[k][i] for k in _SMALL_KEYS if k not in CONV_NAMES}
        sm.update({k: full[k][i] for k in CONV_NAMES})
        smalls.append(sm)
    loss, gx, gbs, gss, gfn = _local_step(x[0], p[:, 0], target[0], bigs, smalls, w["final_norm"])
    loss = lax.psum(loss[0, 0], MESH_AXES)

    gfull = [dict(big_unlayout(gbs[i]), **gss[i]) for i in range(DEPTH)]
    stack = lambda k: jnp.stack([gfull[i][k] for i in range(DEPTH)])
    slots = exchange("rs_sharded", [_full_to_scattered(k, stack(k)).astype(bf16) for k in SHARDED])
    outs = {}
    kinds = ("grad", "delta", "new_m", "new_v")
    for k, sl in zip(SHARDED, slots):
        for kind, a in zip(kinds, reduce_adamw("adamw_" + k, sl, w[k], m[k], v[k])):
            outs[kind + "_" + k] = a

    shapes = [w[k].shape for k in REPLICATED]
    g_rep = sum_slots("sum_replicated", all_gather("ag_replicated", [_pack([gfn if k == "final_norm" else stack(k) for k in REPLICATED])])[0])
    res = adamw("adamw_replicated", _pack([w[k] for k in REPLICATED]), g_rep, _pack([m[k] for k in REPLICATED]), _pack([v[k] for k in REPLICATED]))
    for kind, buf in zip(kinds, [g_rep] + list(res)):
        for k, a in zip(REPLICATED, _unpack(buf, shapes)):
            outs[kind + "_" + k] = a
    return (loss, gx[None]) + tuple(outs[kind + "_" + k] for kind in kinds for k in W_NAMES)
```

```python
import functools
import math

import jax
import jax.numpy as jnp
from jax import lax
from jax.experimental import pallas as pl
from jax.experimental.pallas import tpu as pltpu

f32 = jnp.float32
bf16 = jnp.bfloat16

EPS = 1e-6
DEPTH = 2
D_MODEL = 1024
FFN_DIM = 2816
BW = 512
IN_WIDTH = 5136
PW = 5376
SMALL_OFF = 5120
CHUNK = 64
LS = 256
LRU_C = 8.0
N_DEV = 8
VMEM_LIMIT_BYTES = 56 * 1024 * 1024

ADAM_LR, ADAM_B1, ADAM_B2, ADAM_EPS, ADAM_WD, ADAM_STEP = 0.001, 0.9, 0.999, 1e-08, 0.01, 10


_NN, _NT, _TN = ((1,), (0,)), ((1,), (1,)), ((0,), (0,))


def _pieces(x, n):
    parts, r = [], x
    for i in range(n):
        p = r.astype(bf16)
        parts.append(p)
        if i + 1 < n:
            r = r - p.astype(f32)
    return parts


def _dg(a, b, dims, mode):
    sa, sb = mode
    pa, pb = _pieces(a, sa), _pieces(b, sb)
    out = None
    for i in reversed(range(sa)):
        for j in reversed(range(sb)):
            if i + j < max(sa, sb):
                d = lax.dot_general(pa[i], pb[j], (dims, ((), ())), preferred_element_type=f32)
                out = d if out is None else out + d
    return out


def _make_mm(mode):
    sa, sb = mode
    cot = lambda s_other: 1 if mode == (1, 1) else (3 if s_other == 1 else 2)
    m_g_b, m_a_g, m_g_a, m_b_g = (cot(sb), sb), (sa, cot(sa)), (cot(sa), sa), (sb, cot(sb))

    @jax.custom_vjp
    def nn(a, b):
        return _dg(a, b, _NN, mode)

    @jax.custom_vjp
    def nt(a, b):
        return _dg(a, b, _NT, mode)

    @jax.custom_vjp
    def tn(a, b):
        return _dg(a, b, _TN, mode)

    nn.defvjp(lambda a, b: (nn(a, b), (a, b)), lambda r, g: (_dg(g, r[1], _NT, m_g_b), _dg(r[0], g, _TN, m_a_g)))
    nt.defvjp(lambda a, b: (nt(a, b), (a, b)), lambda r, g: (_dg(g, r[1], _NN, m_g_b), _dg(g, r[0], _TN, m_g_a)))
    tn.defvjp(lambda a, b: (tn(a, b), (a, b)), lambda r, g: (_dg(r[1], g, _NT, m_b_g), _dg(r[0], g, _NN, m_a_g)))
    return nn, nt, tn


mm, mm_nt, mm_tn = _make_mm((1, 1))
xmm, xmm_nt, xmm_tn = _make_mm((2, 2))
lmm, lmm_nt, lmm_tn = _make_mm((1, 3))
rmm, rmm_nt, rmm_tn = _make_mm((3, 1))


@jax.custom_vjp
def _tri_inv(m):
    n = m.shape[0]
    eye = (_rows((n, n)) == _lanes((n, n))).astype(f32)
    blk = (_rows((n, n)) // 16) == (_lanes((n, n)) // 16)
    x = lambda a, b: _dg(a, b, _NN, (2, 2))
    nb = jnp.where(blk, m, 0.0)
    p = -nb
    t = eye + p
    for _ in range(3):
        p = x(p, p)
        t = t + x(t, p)
    q = x(t, m - nb)
    imq = eye - q
    return x(imq + x(imq, x(q, q)), t)


def _tri_inv_fwd(m):
    t = _tri_inv(m)
    return t, t


def _tri_inv_bwd(t, g):
    return (-_dg(_dg(t, g, _TN, (2, 2)), t, _NT, (2, 2)),)


_tri_inv.defvjp(_tri_inv_fwd, _tri_inv_bwd)


def _rows(shape):
    return lax.broadcasted_iota(jnp.int32, shape, 0)


def _lanes(shape):
    return lax.broadcasted_iota(jnp.int32, shape, 1)


def _rms(x, g):
    return x * lax.rsqrt(jnp.mean(x * x, axis=-1, keepdims=True) + EPS) * g


def _silu(x):
    return x * jax.nn.sigmoid(x)


def _gelu(x):
    return 0.5 * x * (1.0 + jnp.tanh(0.7978845608028654 * (x + 0.044715 * x * x * x)))


def _softplus(x):
    return jnp.maximum(x, 0.0) + jnp.log1p(jnp.exp(-jnp.abs(x)))


def _expm1(x):
    p = x * (1.0 + x * (0.5 + x * (1.0 / 6 + x * (1.0 / 24 + x * (1.0 / 120 + x * (1.0 / 720 + x * (1.0 / 5040)))))))
    return jnp.where(x > -0.3, p, jnp.exp(x) - 1.0)


def _pick_row(x, r):
    return jnp.sum(jnp.where(_rows(x.shape) == r, x, 0.0), axis=0, keepdims=True)


def _pick_lane(x, c):
    return jnp.sum(jnp.where(_lanes(x.shape) == c, x, 0.0), axis=1, keepdims=True)


def _shift_up(g, j):
    n = g.shape[0]
    return jnp.where(_rows(g.shape) < n - j, pltpu.roll(g, n - j, 0), 0.0)


@functools.partial(jax.custom_vjp, nondiff_argnums=(1, 2))
def _shift(x, j, fill):
    return jnp.where(_rows(x.shape) >= j, pltpu.roll(x, j, 0), fill)


_shift.defvjp(lambda x, j, fill: (_shift(x, j, fill), None), lambda j, fill, _, g: (_shift_up(g, j),))


@functools.partial(jax.custom_vjp, nondiff_argnums=(2,))
def _shift_halo(x, prev8, j):
    xr = pltpu.roll(x, j, 0)
    pr = pltpu.roll(prev8, j, 0)
    top = jnp.where(_rows(pr.shape) < j, pr, xr[:8])
    return jnp.concatenate([top, xr[8:]], axis=0)


def _shift_halo_bwd(j, _, g):
    g8 = g[:8]
    dprev = jnp.where(_rows(g8.shape) >= 8 - j, pltpu.roll(g8, 8 - j, 0), 0.0)
    return _shift_up(g, j), dprev


_shift_halo.defvjp(lambda x, p, j: (_shift_halo(x, p, j), None), _shift_halo_bwd)


def _conv4(x, prev8, w, b):
    y = _pick_row(w, 3) * x
    for k in range(3):
        y = y + _pick_row(w, k) * _shift_halo(x, prev8, 3 - k)
    return y if b is None else y + b


def _cmul(ar, ai, br, bi):
    return ar * br - ai * bi, ar * bi + ai * br


_ANY = pl.BlockSpec(memory_space=pl.ANY)


def _params(grid):
    return pltpu.CompilerParams(dimension_semantics=("arbitrary",) * len(grid), vmem_limit_bytes=VMEM_LIMIT_BYTES)


def _first(axes):
    ok = pl.program_id(axes[0]) == 0
    for a in axes[1:]:
        ok = jnp.logical_and(ok, pl.program_id(a) == 0)
    return ok


def _store(ref, val, acc):
    val = val.astype(ref.dtype)
    if acc is None:
        ref[...] = val
        return
    first = _first(acc)

    @pl.when(first)
    def _():
        ref[...] = val

    @pl.when(jnp.logical_not(first))
    def _():
        ref[...] += val


def _full(a):
    nd = a.ndim
    return pl.BlockSpec(a.shape, lambda *g: (0,) * nd)


def _out(shape, spec, acc=None, dtype=f32):
    return dict(shape=tuple(shape), spec=spec, acc=acc, dtype=dtype)


def run_fwd(name, fn, grid, ins, in_specs, outs, carry=None):
    n_in, n_out = len(ins), len(outs)
    cshapes = carry["shapes"] if carry else []
    nc = len(cshapes)
    ng = len(grid)

    def body(*refs):
        in_refs = refs[:n_in]
        out_refs = refs[n_in:n_in + n_out]
        save_refs = refs[n_in + n_out:n_in + n_out + nc]
        c_refs = refs[n_in + n_out + nc:]
        vals = [r[...] for r in in_refs]
        if carry:
            @pl.when(pl.program_id(carry["axis"]) == 0)
            def _():
                for c in c_refs:
                    c[...] = jnp.zeros(c.shape, f32)
            cin = tuple(c[...] for c in c_refs)
            for s, v in zip(save_refs, cin):
                s[...] = v
            cout, res = fn(cin, *vals)
            for c, v in zip(c_refs, cout):
                c[...] = v
        else:
            res = fn(*vals)
        for o, r, d in zip(out_refs, res, outs):
            _store(o, r, d["acc"])

    out_shape = [jax.ShapeDtypeStruct(d["shape"], d["dtype"]) for d in outs]
    out_specs = [d["spec"] for d in outs]
    for cs in cshapes:
        out_shape.append(jax.ShapeDtypeStruct(tuple(grid) + tuple(cs), f32))
        out_specs.append(pl.BlockSpec((None,) * ng + tuple(cs), lambda *g, _n=len(cs): tuple(g) + (0,) * _n))
    res = pl.pallas_call(
        body, name=name, grid=grid, in_specs=list(in_specs), out_specs=out_specs, out_shape=out_shape,
        scratch_shapes=[pltpu.VMEM(tuple(cs), f32) for cs in cshapes], compiler_params=_params(grid),
    )(*ins)
    return list(res[:n_out]), list(res[n_out:])


def run_bwd(name, fn, grid, ins, in_specs, gouts, gout_specs, wants, carry=None, into=None):
    n_in, n_g, n_w = len(ins), len(gouts), len(wants)
    n_a = 0 if into is None else 1
    saved = carry["saved"] if carry else []
    nc = len(saved)
    ng = len(grid)

    def body(*refs):
        in_refs = refs[:n_in]
        g_refs = refs[n_in:n_in + n_g]
        s_refs = refs[n_in + n_g:n_in + n_g + nc]
        w_refs = refs[n_in + n_g + nc + n_a:n_in + n_g + nc + n_a + n_w]
        dc_refs = refs[n_in + n_g + nc + n_a + n_w:]
        vals = [r[...].astype(f32) for r in in_refs]
        gs = tuple(r[...].astype(f32) for r in g_refs)
        if carry:
            @pl.when(pl.program_id(carry["axis"]) == 0)
            def _():
                for c in dc_refs:
                    c[...] = jnp.zeros(c.shape, f32)
            cin = tuple(s[...] for s in s_refs)
            _, vjp = jax.vjp(fn, cin, *vals)
            grads = vjp((tuple(c[...] for c in dc_refs), gs))
            for c, v in zip(dc_refs, grads[0]):
                c[...] = v
            dvals = grads[1:]
        else:
            _, vjp = jax.vjp(fn, *vals)
            dvals = vjp(gs)
        for o, d in zip(w_refs, wants):
            idx = d["idx"]
            val = dvals[idx] if isinstance(idx, int) else jnp.concatenate([dvals[j] for j in idx], axis=1)
            _store(o, val, d["acc"])

    rev = carry["rev"] if carry else None
    s_specs = []
    for a in saved:
        n = a.ndim - ng
        s_specs.append(pl.BlockSpec((None,) * ng + tuple(a.shape[ng:]), lambda *g, _n=n: tuple(rev(g)) + (0,) * _n))
    res = pl.pallas_call(
        body, name=name, grid=grid, in_specs=list(in_specs) + list(gout_specs) + s_specs + [_ANY] * n_a,
        out_specs=[d["spec"] for d in wants],
        out_shape=[jax.ShapeDtypeStruct(d["shape"], d["dtype"]) for d in wants],
        input_output_aliases={n_in + n_g + nc: 0} if n_a else {},
        scratch_shapes=[pltpu.VMEM(tuple(a.shape[ng:]), f32) for a in saved], compiler_params=_params(grid),
    )(*ins, *gouts, *saved, *([into] if n_a else []))
    return list(res)


def _want(idx, shape, spec, acc=None):
    d = _out(shape, spec, acc)
    d["idx"] = idx
    return d


def addn(name, items, t):
    s, w = items[0][0].shape[-2:]
    specs = []
    for a, j in items:
        if j is None:
            specs.append(pl.BlockSpec((t, w), lambda i: (i, 0)))
        else:
            specs.append(pl.BlockSpec((None, t, w), lambda i, _j=j: (_j, i, 0)))

    def fn(*xs):
        y = xs[0]
        for x in xs[1:]:
            y = y + x
        return (y,)

    return run_fwd(name, fn, (s // t,), [a for a, _ in items], specs,
                   [_out((s, w), pl.BlockSpec((t, w), lambda i: (i, 0)))])[0][0]


def _parts(a):
    return [(a, j) for j in range(a.shape[0])]


def _ffn_tile(j_axis, residual):
    def fn(h, g, wg, wu, wo):
        n = _rms(h, g)
        act = _silu(mm(n, wg)) * mm(n, wu)
        y = 0.5 * mm(act, wo)
        if residual:
            y = y + (pl.program_id(j_axis) == 0).astype(f32) * h
        return (y,)
    return fn


def ffn_fwd(name, h, g, wg, wu, wo, t=512, tf=1408):
    s, d = h.shape
    f = wg.shape[1]
    specs = [pl.BlockSpec((t, d), lambda i, j: (i, 0)), _full(g), pl.BlockSpec((d, tf), lambda i, j: (0, j)),
             pl.BlockSpec((d, tf), lambda i, j: (0, j)), pl.BlockSpec((tf, d), lambda i, j: (j, 0))]
    outs = [_out((s, d), pl.BlockSpec((t, d), lambda i, j: (i, 0)), acc=(1,))]
    return run_fwd(name, _ffn_tile(1, True), (s // t, f // tf), [h, g, wg, wu, wo], specs, outs)[0][0]


def ffn_bwd(name, h, g, wg, wu, wo, gout, t=512, tf=256):
    s, d = h.shape
    f = wg.shape[1]
    nj = f // tf
    specs = [pl.BlockSpec((t, d), lambda j, i: (i, 0)), _full(g), pl.BlockSpec((d, tf), lambda j, i: (0, j)),
             pl.BlockSpec((d, tf), lambda j, i: (0, j)), pl.BlockSpec((tf, d), lambda j, i: (j, 0))]
    wants = [_want(0, (nj, s, d), pl.BlockSpec((None, t, d), lambda j, i: (j, i, 0))),
             _want(1, g.shape, _full(g), acc=(0, 1)),
             _want(2, wg.shape, pl.BlockSpec((d, tf), lambda j, i: (0, j)), acc=(1,)),
             _want(3, wu.shape, pl.BlockSpec((d, tf), lambda j, i: (0, j)), acc=(1,)),
             _want(4, wo.shape, pl.BlockSpec((tf, d), lambda j, i: (j, 0)), acc=(1,))]
    return run_bwd(name, _ffn_tile(0, False), (nj, s // t), [h, g, wg, wu, wo], specs,
                   [gout], [pl.BlockSpec((t, d), lambda j, i: (i, 0))], wants)


def _normlin_tile(h, g, w, b):
    return (mm(_rms(h, g), w) + b,)


def normlin_fwd(name, h, g, w, b, t=512, tn=1024):
    s, d = h.shape
    n = w.shape[1]
    specs = [pl.BlockSpec((t, d), lambda i, j: (i, 0)), _full(g), pl.BlockSpec((d, tn), lambda i, j: (0, j)),
             pl.BlockSpec((1, tn), lambda i, j: (0, j))]
    outs = [_out((s, n), pl.BlockSpec((t, tn), lambda i, j: (i, j)))]
    return run_fwd(name, _normlin_tile, (s // t, n // tn), [h, g, w, b], specs, outs)[0][0]


def normlin_bwd(name, h, g, w, b, gout, t=256, tn=1024):
    s, d = h.shape
    n = w.shape[1]
    nj = n // tn
    specs = [pl.BlockSpec((t, d), lambda j, i: (i, 0)), _full(g), pl.BlockSpec((d, tn), lambda j, i: (0, j)),
             pl.BlockSpec((1, tn), lambda j, i: (0, j))]
    wants = [_want(0, (nj, s, d), pl.BlockSpec((None, t, d), lambda j, i: (j, i, 0))),
             _want(1, g.shape, _full(g), acc=(0, 1)),
             _want(2, w.shape, pl.BlockSpec((d, tn), lambda j, i: (0, j)), acc=(1,)),
             _want(3, b.shape, pl.BlockSpec((1, tn), lambda j, i: (0, j)), acc=(1,))]
    return run_bwd(name, _normlin_tile, (nj, s // t), [h, g, w, b], specs,
                   [gout], [pl.BlockSpec((t, tn), lambda j, i: (i, j))], wants)


def _merge_tile(n_axis, residual):
    def fn(h, y, gp, wb, wo):
        part = mm(jax.nn.sigmoid(gp) * mm(y, wb), wo)
        if residual:
            part = part + (pl.program_id(n_axis) == 0).astype(f32) * h
        return (part,)
    return fn


def merge_fwd(name, h, ys, gpre, wb, wo, t=512):
    s, d = h.shape
    specs = [pl.BlockSpec((t, d), lambda i, n: (i, 0)), pl.BlockSpec((t, BW), lambda i, n: (i, n)),
             pl.BlockSpec((t, d), lambda i, n: (i, n)), pl.BlockSpec((None, BW, d), lambda i, n: (n, 0, 0)), _full(wo)]
    outs = [_out((s, d), pl.BlockSpec((t, d), lambda i, n: (i, 0)), acc=(1,))]
    return run_fwd(name, _merge_tile(1, True), (s // t, 4), [h, ys, gpre, wb, wo], specs, outs)[0][0]


def merge_bwd(name, h, ys, gpre, wb, wo, gout, t=256):
    s, d = h.shape
    specs = [pl.BlockSpec((t, d), lambda n, i: (i, 0)), pl.BlockSpec((t, BW), lambda n, i: (i, n)),
             pl.BlockSpec((t, d), lambda n, i: (i, n)), pl.BlockSpec((None, BW, d), lambda n, i: (n, 0, 0)), _full(wo)]
    wants = [_want(1, ys.shape, pl.BlockSpec((t, BW), lambda n, i: (i, n))),
             _want(2, gpre.shape, pl.BlockSpec((t, d), lambda n, i: (i, n))),
             _want(3, wb.shape, pl.BlockSpec((None, BW, d), lambda n, i: (n, 0, 0)), acc=(1,)),
             _want(4, wo.shape, _full(wo), acc=(0, 1))]
    return run_bwd(name, _merge_tile(0, False), (4, s // t), [h, ys, gpre, wb, wo], specs,
                   [gout], [pl.BlockSpec((t, d), lambda n, i: (i, 0))], wants)


def _ple_tile(residual):
    def fn(h, pe, g, wgate, wproj):
        y = jax.nn.sigmoid(mm(_rms(h, g), wgate)) * mm(pe, wproj)
        return (y + h,) if residual else (y,)
    return fn


def ple_fwd(name, h, pe, g, wgate, wproj, t=512):
    s, d = h.shape
    specs = [pl.BlockSpec((t, d), lambda i: (i, 0)), pl.BlockSpec((t, pe.shape[1]), lambda i: (i, 0)),
             _full(g), _full(wgate), _full(wproj)]
    return run_fwd(name, _ple_tile(True), (s // t,), [h, pe, g, wgate, wproj], specs,
                   [_out((s, d), pl.BlockSpec((t, d), lambda i: (i, 0)))])[0][0]


def ple_bwd(name, h, pe, g, wgate, wproj, gout, t=256):
    s, d = h.shape
    specs = [pl.BlockSpec((t, d), lambda i: (i, 0)), pl.BlockSpec((t, pe.shape[1]), lambda i: (i, 0)),
             _full(g), _full(wgate), _full(wproj)]
    wants = [_want(0, h.shape, pl.BlockSpec((t, d), lambda i: (i, 0))), _want(2, g.shape, _full(g), acc=(0,)),
             _want(3, wgate.shape, _full(wgate), acc=(0,)), _want(4, wproj.shape, _full(wproj), acc=(0,))]
    return run_bwd(name, _ple_tile(True), (s // t,), [h, pe, g, wgate, wproj], specs,
                   [gout], [pl.BlockSpec((t, d), lambda i: (i, 0))], wants)


def final_loss(name, h, g, target, t=512):
    s, d = h.shape

    def fn(hh, gg, tt):
        def loss_fn(a, b):
            err = _rms(a, b) - tt
            return 0.5 * jnp.sum(jnp.mean(err * err, axis=-1, keepdims=True), axis=0, keepdims=True)
        loss, vjp = jax.vjp(loss_fn, hh, gg)
        dh, dgain = vjp(jnp.ones((1, 1), f32))
        return loss, dh, dgain

    specs = [pl.BlockSpec((t, d), lambda i: (i, 0)), _full(g), pl.BlockSpec((t, d), lambda i: (i, 0))]
    outs = [_out((1, 1), pl.BlockSpec((1, 1), lambda i: (0, 0)), acc=(0,)),
            _out((s, d), pl.BlockSpec((t, d), lambda i: (i, 0))), _out(g.shape, _full(g), acc=(0,))]
    return run_fwd(name, fn, (s // t,), [h, g, target], specs, outs)[0]


def _rev(nc, rev):
    return (lambda c: nc - 1 - c) if rev else (lambda c: c)


def _s5_ops_tile(are, aim, lstep, bre, bim):
    step = jnp.exp(lstep)
    mag = jnp.exp(are * step)
    ab_re, ab_im = mag * jnp.cos(aim * step), mag * jnp.sin(aim * step)
    den = are * are + aim * aim
    num_re = ab_re - 1.0
    f_re = (num_re * are + ab_im * aim) / den
    f_im = (ab_im * are - num_re * aim) / den
    bb_re = f_re * bre - f_im * bim
    bb_im = f_re * bim + f_im * bre
    pr = jnp.broadcast_to(ab_re, (LS, ab_re.shape[1]))
    pi = jnp.broadcast_to(ab_im, (LS, ab_im.shape[1]))
    k = 1
    while k < LS:
        pr, pi = _cmul(pr, pi, _shift(pr, k, 1.0), _shift(pi, k, 0.0))
        k *= 2
    return ab_re, ab_im, bb_re, bb_im, pr, pi


def _s5_ops_specs(arrs):
    return [pl.BlockSpec((None,) + a.shape[1:], lambda gb: (gb, 0, 0)) for a in arrs]


def s5_ops_fwd(name, raw):
    shapes = [(4, 1, 512), (4, 1, 512), (4, 128, 512), (4, 128, 512), (4, LS, 512), (4, LS, 512)]
    outs = [_out(sh, pl.BlockSpec((None,) + sh[1:], lambda gb: (gb, 0, 0))) for sh in shapes]
    return run_fwd(name, _s5_ops_tile, (4,), raw, _s5_ops_specs(raw), outs)[0]


def s5_ops_bwd(name, raw, gops):
    wants = [_want(i, a.shape, pl.BlockSpec((None,) + a.shape[1:], lambda gb: (gb, 0, 0))) for i, a in enumerate(raw)]
    return run_bwd(name, _s5_ops_tile, (4,), raw, _s5_ops_specs(raw), gops, _s5_ops_specs(gops), wants)


def _s5_tile(carry, u, ab_re, ab_im, bb_re, bb_im, pw_re, pw_im, c_re, c_im, dskip):
    h_re, h_im = carry
    xr, xi = mm(u, bb_re), mm(u, bb_im)
    pr, pi = ab_re, ab_im
    k = 1
    while k < LS:
        sr, si = _cmul(pr, pi, _shift(xr, k, 0.0), _shift(xi, k, 0.0))
        xr, xi = xr + sr, xi + si
        pr, pi = _cmul(pr, pi, pr, pi)
        k *= 2
    cr, ci = _cmul(pw_re, pw_im, h_re, h_im)
    xr, xi = xr + cr, xi + ci
    y = mm(xr, c_re) - mm(xi, c_im) + dskip * u
    return (_pick_row(xr, LS - 1), _pick_row(xi, LS - 1)), (y,)


def _s5_io(u, ops, c_re, c_im, dskip, nc, rev):
    cm = _rev(nc, rev)
    ins = [u] + list(ops) + [c_re, c_im, dskip]
    specs = [pl.BlockSpec((LS, 128), lambda gb, c: (cm(c), 36 + gb))]
    specs += [pl.BlockSpec((None,) + a.shape[1:], lambda gb, c: (gb, 0, 0)) for a in list(ops) + [c_re, c_im]]
    specs += [pl.BlockSpec((1, 128), lambda gb, c: (0, gb))]
    return ins, specs, cm


def s5_fwd(name, proj, ops, c_re, c_im, dskip):
    s = proj.shape[0]
    nc = s // LS
    ins, specs, cm = _s5_io(proj, ops, c_re, c_im, dskip, nc, False)
    outs = [_out((s, BW), pl.BlockSpec((LS, 128), lambda gb, c: (c, gb)))]
    (y,), saved = run_fwd(name, _s5_tile, (4, nc), ins, specs, outs, carry=dict(shapes=[(1, 512), (1, 512)], axis=1))
    return y, saved


def s5_bwd(name, proj, ops, c_re, c_im, dskip, saved, gy, gproj):
    s = proj.shape[0]
    nc = s // LS
    ins, specs, cm = _s5_io(proj, ops, c_re, c_im, dskip, nc, True)
    wants = [_want(0, (s, PW), pl.BlockSpec((LS, 128), lambda gb, c: (cm(c), 36 + gb)))]
    for i, a in enumerate(list(ops) + [c_re, c_im]):
        wants.append(_want(1 + i, a.shape, pl.BlockSpec((None,) + a.shape[1:], lambda gb, c: (gb, 0, 0)), acc=(1,)))
    wants.append(_want(9, dskip.shape, pl.BlockSpec((1, 128), lambda gb, c: (0, gb)), acc=(1,)))
    return run_bwd(name, _s5_tile, (4, nc), ins, specs, [gy], [pl.BlockSpec((LS, 128), lambda gb, c: (cm(c), gb))],
                   wants, carry=dict(axis=1, saved=saved, rev=lambda g: (g[0], cm(g[1]))), into=gproj)


def _s5_glu_tile(y, w, b):
    z = _gelu(y)
    return (z * jax.nn.sigmoid(mm(z, w) + b),)


def s5_glu_fwd(name, y, w, b, t=512):
    s = y.shape[0]
    spec = pl.BlockSpec((t, BW), lambda i: (i, 0))
    return run_fwd(name, _s5_glu_tile, (s // t,), [y, w, b], [spec, _full(w), _full(b)], [_out((s, BW), spec)])[0][0]


def s5_glu_bwd(name, y, w, b, gout, gout_col, t=512):
    s = y.shape[0]
    spec = pl.BlockSpec((t, BW), lambda i: (i, 0))
    wants = [_want(0, y.shape, spec), _want(1, w.shape, _full(w), acc=(0,)), _want(2, b.shape, _full(b), acc=(0,))]
    return run_bwd(name, _s5_glu_tile, (s // t,), [y, w, b], [spec, _full(w), _full(b)], [gout],
                   [pl.BlockSpec((t, BW), lambda i: (i, gout_col))], wants)


def _lru_tile(carry, xb, gate, cw, cb, wr, br, wi, bi, lam):
    h_in, prev8 = carry
    xc = _conv4(xb, prev8, cw, cb)
    r = jax.nn.sigmoid(mm(xc, wr) + br)
    ig = jax.nn.sigmoid(mm(xc, wi) + bi)
    log_a = -LRU_C * r * _softplus(-lam)
    a = jnp.exp(log_a)
    b = jnp.sqrt(-_expm1(2.0 * log_a)) * (ig * xc)
    k = 1
    while k < LS:
        b = b + a * _shift(b, k, 0.0)
        a = a * _shift(a, k, 1.0)
        k *= 2
    h = b + a * h_in
    return (_pick_row(h, LS - 1), xb[LS - 8:, :]), (h * _gelu(gate),)


def _lru_io(proj, ws, nc, rev):
    cm = _rev(nc, rev)
    ins = [proj, proj] + list(ws)
    specs = [pl.BlockSpec((LS, BW), lambda c: (cm(c), 4)), pl.BlockSpec((LS, BW), lambda c: (cm(c), 5))]
    specs += [_full(a) for a in ws]
    return ins, specs, cm


def lru_fwd(name, proj, ws):
    s = proj.shape[0]
    nc = s // LS
    ins, specs, cm = _lru_io(proj, ws, nc, False)
    outs = [_out((s, BW), pl.BlockSpec((LS, BW), lambda c: (c, 0)))]
    (y,), saved = run_fwd(name, _lru_tile, (nc,), ins, specs, outs, carry=dict(shapes=[(1, BW), (8, BW)], axis=0))
    return y, saved


def lru_bwd(name, proj, ws, saved, gy, gy_col, gproj):
    s = proj.shape[0]
    nc = s // LS
    ins, specs, cm = _lru_io(proj, ws, nc, True)
    wants = [_want((0, 1), (s, PW), pl.BlockSpec((LS, 2 * BW), lambda c: (cm(c), 2)))]
    wants += [_want(2 + i, a.shape, _full(a), acc=(0,)) for i, a in enumerate(ws)]
    return run_bwd(name, _lru_tile, (nc,), ins, specs, [gy], [pl.BlockSpec((LS, BW), lambda c: (cm(c), gy_col))], wants,
                   carry=dict(axis=0, saved=saved, rev=lambda g: (cm(g[0]),)), into=gproj)


def _causal(n):
    return _rows((n, n)) >= _lanes((n, n))


def _decay(col, rowv):
    causal = _causal(col.shape[0])
    return jnp.where(causal, jnp.exp(jnp.where(causal, col - rowv, 0.0)), 0.0)


def _m2_tile(carry, z, xs_raw, b_raw, c_raw, small, cwx, cwb, cwc, cbx, cbb, cbc, dtb, alog, dsk, ng):
    state, px, pb, pc = carry
    n = CHUNK
    xs = _silu(_conv4(xs_raw, px, cwx, cbx))
    bm = _silu(_conv4(b_raw, pb, cwb, cbb))
    cmx = _silu(_conv4(c_raw, pc, cwc, cbc))
    expand = (_lanes((16, BW)) // 64 == _rows((16, BW))).astype(f32)
    tri = _causal(n).astype(f32)
    triu = (_rows((n, n)) <= _lanes((n, n))).astype(f32)
    dt = _softplus(small + dtb)
    da = dt * (-jnp.exp(alog))
    cs = lmm(tri, da)
    cs_t = rmm_tn(da, triu)
    cs_w = rmm(cs, expand)
    last_w = _pick_row(cs_w, n - 1)
    xdt = xs * rmm(dt, expand)
    g0 = _lanes((n, BW)) < 256
    bm0, bm1, cm0, cm1 = bm[:, :128], bm[:, 128:], cmx[:, :128], cmx[:, 128:]
    cb0, cb1 = mm_nt(cm0, bm0), mm_nt(cm1, bm1)
    y = jnp.where(g0, mm(cm0, state), mm(cm1, state)) * jnp.exp(cs_w)
    for h in range(8):
        sc = (cb0 if h < 4 else cb1) * _decay(_pick_lane(cs, h), _pick_row(cs_t, h))
        y = y + jnp.where(_lanes((n, BW)) // 64 == h, mm(sc, xdt), 0.0)
    xd = xdt * jnp.exp(last_w - cs_w)
    g0s = _lanes((128, BW)) < 256
    state_out = state * jnp.exp(last_w) + jnp.where(g0s, mm_tn(bm0, xd), mm_tn(bm1, xd))
    y = (y + dsk * xs) * _silu(z)
    return (state_out, xs_raw[n - 8:, :], b_raw[n - 8:, :], c_raw[n - 8:, :]), (_rms(y, ng),)


def _m2_io(proj, small, ws, nc, rev):
    cm = _rev(nc, rev)
    n = CHUNK
    ins = [proj, proj, proj, proj, small] + list(ws)
    specs = [pl.BlockSpec((n, BW), lambda c: (cm(c), 6)), pl.BlockSpec((n, BW), lambda c: (cm(c), 7)),
             pl.BlockSpec((n, 256), lambda c: (cm(c), 16)), pl.BlockSpec((n, 256), lambda c: (cm(c), 17)),
             pl.BlockSpec((n, 16), lambda c: (cm(c), 0))]
    specs += [_full(a) for a in ws]
    return ins, specs, cm


_M2_CARRY = [(128, BW), (8, BW), (8, 256), (8, 256)]


def m2_fwd(name, proj, small, ws):
    s = proj.shape[0]
    nc = s // CHUNK
    ins, specs, cm = _m2_io(proj, small, ws, nc, False)
    outs = [_out((s, BW), pl.BlockSpec((CHUNK, BW), lambda c: (c, 0)))]
    (y,), saved = run_fwd(name, _m2_tile, (nc,), ins, specs, outs, carry=dict(shapes=_M2_CARRY, axis=0))
    return y, saved


def m2_bwd(name, proj, small, ws, saved, gy, gy_col, gproj):
    s = proj.shape[0]
    nc = s // CHUNK
    ins, specs, cm = _m2_io(proj, small, ws, nc, True)
    n = CHUNK
    wants = [_want((0, 1, 2, 3), (s, PW), pl.BlockSpec((n, 3 * BW), lambda c: (cm(c), 2))),
             _want(4, (s, 16), pl.BlockSpec((n, 16), lambda c: (cm(c), 0)))]
    wants += [_want(5 + i, a.shape, _full(a), acc=(0,)) for i, a in enumerate(ws)]
    return run_bwd(name, _m2_tile, (nc,), ins, specs, [gy], [pl.BlockSpec((n, BW), lambda c: (cm(c), gy_col))], wants,
                   carry=dict(axis=0, saved=saved, rev=lambda g: (cm(g[0]),)), into=gproj)


def _l2n(x):
    return x * lax.rsqrt(jnp.sum(x * x, axis=-1, keepdims=True) + EPS)


def _gdn_tile(carry, q_raw, k_raw, v_raw, gate, small, cwq, cwk, cwv, dtb, alog, ng):
    state, pq, pk, pv = carry
    n, nh = CHUNK, 4
    nn_ = n * nh
    qc = _silu(_conv4(q_raw, pq, cwq, None))
    kc = _silu(_conv4(k_raw, pk, cwk, None))
    vc = _silu(_conv4(v_raw, pv, cwv, None))
    beta16 = jax.nn.sigmoid(small)
    g16 = -jnp.exp(alog) * _softplus(small + dtb)
    tri = _causal(n).astype(f32)
    triu = (_rows((n, n)) <= _lanes((n, n))).astype(f32)
    cs16 = lmm(tri, g16)
    cs16_t = rmm_tn(g16, triu)
    lanes_of = lambda h: slice(128 * h, 128 * (h + 1))
    rows_of = lambda h: slice(n * h, n * (h + 1))
    stack = lambda f: jnp.concatenate([f(h) for h in range(nh)], axis=0)
    q = stack(lambda h: _l2n(qc[:, lanes_of(h)]) * (128 ** -0.5))
    k = stack(lambda h: _l2n(kc[:, lanes_of(h)]))
    v = stack(lambda h: vc[:, lanes_of(h)])
    beta = stack(lambda h: _pick_lane(beta16, 8 + h))
    col = stack(lambda h: _pick_lane(cs16, 12 + h))
    last_h = [_pick_row(_pick_lane(cs16, 12 + h), n - 1) for h in range(nh)]
    last = stack(lambda h: jnp.broadcast_to(last_h[h], (n, 1)))
    last_w = jnp.concatenate([jnp.broadcast_to(last_h[h], (1, 128)) for h in range(nh)], axis=1)
    spread = (_rows((n, nn_)) == _lanes((n, nn_)) % n).astype(f32)
    cs_w = rmm(cs16_t, spread)
    rowv = jnp.sum(jnp.where(_rows((16, nn_)) == 12 + _lanes((16, nn_)) // n, cs_w, 0.0), axis=0, keepdims=True)
    same = (_rows((nn_, nn_)) // n) == (_lanes((nn_, nn_)) // n)
    causal = jnp.logical_and(same, _rows((nn_, nn_)) >= _lanes((nn_, nn_)))
    strict = jnp.logical_and(same, _rows((nn_, nn_)) > _lanes((nn_, nn_)))
    decay = jnp.where(causal, jnp.exp(jnp.where(causal, col - rowv, 0.0)), 0.0)
    kb = k * beta
    t = _tri_inv(jnp.where(strict, mm_nt(kb, k) * decay, 0.0))
    e_col = jnp.exp(col)
    uw = xmm(t, jnp.concatenate([v * beta, kb * e_col], axis=1))
    u, w = uw[:, :128], uw[:, 128:]
    qk = mm_nt(q, k) * decay
    own = lambda r: stack(lambda h: r[rows_of(h), lanes_of(h)])
    v_new = u - own(mm(w, state))
    o = own(mm(q * e_col, state)) + mm(qk, v_new)
    zero = jnp.zeros((n, 128), f32)
    v_blocks = stack(lambda h: jnp.concatenate([v_new[rows_of(h), :] if j == h else zero for j in range(nh)], axis=1))
    state_out = state * jnp.exp(last_w) + mm_tn(k * jnp.exp(last - col), v_blocks)
    gt = stack(lambda h: gate[:, lanes_of(h)])
    out = _rms(o, ng) * _silu(gt)
    out = jnp.concatenate([out[rows_of(h), :] for h in range(nh)], axis=1)
    return (state_out, q_raw[n - 8:, :], k_raw[n - 8:, :], v_raw[n - 8:, :]), (out,)


def _gdn_io(proj, small, cw, dtb, alog, ng, nc, rev):
    cm = _rev(nc, rev)
    n = CHUNK
    ins = [proj, proj, proj, proj, small, cw, cw, cw, dtb, alog, ng]
    specs = [pl.BlockSpec((n, BW), lambda c, _j=j: (cm(c), _j)) for j in (0, 1, 2, 3)]
    specs += [pl.BlockSpec((n, 16), lambda c: (cm(c), 0))]
    specs += [pl.BlockSpec((4, BW), lambda c, _j=j: (0, _j)) for j in (0, 1, 2)]
    specs += [_full(dtb), _full(alog), _full(ng)]
    return ins, specs, cm


_GDN_CARRY = [(128, BW)] + [(8, BW)] * 3


def gdn_fwd(name, proj, small, cw, dtb, alog, ng):
    s = proj.shape[0]
    nc = s // CHUNK
    ins, specs, cm = _gdn_io(proj, small, cw, dtb, alog, ng, nc, False)
    outs = [_out((s, BW), pl.BlockSpec((CHUNK, BW), lambda c: (c, 0)))]
    (y,), saved = run_fwd(name, _gdn_tile, (nc,), ins, specs, outs, carry=dict(shapes=_GDN_CARRY, axis=0))
    return y, saved


def gdn_bwd(name, proj, small, cw, dtb, alog, ng, saved, gy, gy_col):
    s = proj.shape[0]
    nc = s // CHUNK
    n = CHUNK
    ins, specs, cm = _gdn_io(proj, small, cw, dtb, alog, ng, nc, True)
    wants = [_want((0, 1, 2, 3), (s, PW), pl.BlockSpec((n, 4 * BW), lambda c: (cm(c), 0))),
             _want(4, (s, 16), pl.BlockSpec((n, 16), lambda c: (cm(c), 0)))]
    wants += [_want(5 + i, (4, BW), pl.BlockSpec((4, BW), lambda c: (0, 0)), acc=(0,)) for i in range(3)]
    wants += [_want(8, dtb.shape, _full(dtb), acc=(0,)), _want(9, alog.shape, _full(alog), acc=(0,)),
              _want(10, ng.shape, _full(ng), acc=(0,))]
    return run_bwd(name, _gdn_tile, (nc,), ins, specs, [gy], [pl.BlockSpec((n, BW), lambda c: (cm(c), gy_col))], wants,
                   carry=dict(axis=0, saved=saved, rev=lambda g: (cm(g[0]),)))


def _perm_cols(w):
    pad = jnp.zeros(w.shape[:-1] + (PW - IN_WIDTH,), w.dtype)
    return jnp.concatenate([w[..., 3080:5128], w[..., 512:3072], w[..., :512], w[..., 3072:3080], w[..., 5128:5136], pad], axis=-1)


def _unperm_cols(g):
    return jnp.concatenate([g[..., 4608:5120], g[..., 2048:4608], g[..., 5120:5128], g[..., :2048], g[..., 5128:5136]], axis=-1)


def _bd(blocks):
    n, a, b = blocks.shape
    eye = jnp.eye(n, dtype=blocks.dtype)
    return jnp.einsum("nab,nm->namb", blocks, eye).reshape(n * a, n * b)


def _layer_layout(lw):
    o = {}
    row = lambda a: a.reshape(1, -1)
    o["b_gate"] = row(lw["b_gate"])
    o["s5_are"] = lw["s5_a_re"].reshape(4, 1, 512)
    o["s5_aim"] = lw["s5_a_im"].reshape(4, 1, 512)
    o["s5_lstep"] = jnp.repeat(lw["s5_log_step"], 64).reshape(4, 1, 512)
    bt = lambda b: jax.vmap(_bd)(jnp.swapaxes(b, 1, 2).reshape(4, 8, 16, 64))
    o["s5_bre"], o["s5_bim"] = bt(lw["s5_b_re"]), bt(lw["s5_b_im"])
    ct = lambda c: jax.vmap(_bd)(jnp.swapaxes(c, 1, 2).reshape(4, 8, 64, 16))
    o["s5_cre"], o["s5_cim"] = ct(lw["s5_c_re"]), ct(lw["s5_c_im"])
    o["s5_d"] = row(lw["s5_d"])
    o["s5_b_glu"] = row(lw["s5_b_glu"])
    o["lru_conv_b"], o["lru_b_r"], o["lru_b_i"], o["lru_lambda"] = (row(lw[k]) for k in ("lru_conv_b", "lru_b_r", "lru_b_i", "lru_lambda"))
    o["lru_wr"], o["lru_wi"] = _bd(lw["lru_w_r"]), _bd(lw["lru_w_i"])
    cw, cb = lw["m2_conv_w"], lw["m2_conv_b"]
    o["m2_cwx"], o["m2_cwb"], o["m2_cwc"] = cw[:, :512], cw[:, 512:768], cw[:, 768:]
    o["m2_cbx"], o["m2_cbb"], o["m2_cbc"] = row(cb[:512]), row(cb[512:768]), row(cb[768:])
    o["m2_dtb"] = jnp.pad(lw["m2_dt_bias"], (0, 8)).reshape(1, 16)
    o["m2_alog"] = jnp.pad(lw["m2_a_log"], (0, 8)).reshape(1, 16)
    o["m2_dsk"] = jnp.repeat(lw["m2_d"], 64).reshape(1, 512)
    o["m2_norm"] = row(lw["m2_norm"])
    o["gdn_dtb"] = jnp.pad(lw["gdn_dt_bias"], (12, 0)).reshape(1, 16)
    o["gdn_alog"] = jnp.pad(lw["gdn_a_log"], (12, 0)).reshape(1, 16)
    o["gdn_norm"] = row(lw["gdn_norm"])
    for k in ("ffn1_norm", "mix_norm", "ffn2_norm", "ple_norm"):
        o[k] = row(lw[k])
    return o


_BIG_PLAIN = ("w_gate", "s5_w_glu", "w_branch", "w_out", "ple_w_gate", "ple_w_proj")


def big_layout(w):
    o = {k: w[k] for k in _BIG_PLAIN}
    for f in ("ffn1", "ffn2"):
        o[f + "_wg"], o[f + "_wu"] = w[f + "_w_in"][:, :FFN_DIM], w[f + "_w_in"][:, FFN_DIM:]
        o[f + "_wo"] = w[f + "_w_out"]
    o["w_in"] = _perm_cols(w["w_in"])
    return {k: v.astype(bf16) for k, v in o.items()}


def big_unlayout(g):
    o = {k: g[k] for k in _BIG_PLAIN}
    for f in ("ffn1", "ffn2"):
        o[f + "_w_in"] = jnp.concatenate([g[f + "_wg"], g[f + "_wu"]], axis=1)
        o[f + "_w_out"] = g[f + "_wo"]
    o["w_in"] = _unperm_cols(g["w_in"])
    return o


_SMALL_KEYS = ("b_gate", "s5_log_step", "s5_a_re", "s5_a_im", "s5_b_re", "s5_b_im", "s5_c_re", "s5_c_im", "s5_d",
               "s5_b_glu", "lru_conv_b", "lru_w_r", "lru_b_r", "lru_w_i", "lru_b_i", "lru_lambda", "m2_conv_w",
               "m2_conv_b", "m2_dt_bias", "m2_a_log", "m2_d", "m2_norm", "gdn_dt_bias", "gdn_a_log", "gdn_norm",
               "ffn1_norm", "mix_norm", "ffn2_norm", "ple_norm")


def _layer_fwd(i, h0, pe, lw, big):
    n = f"l{i}_"
    lay = _layer_layout(lw)
    a = {"h0": h0, "lay": lay}
    h1 = ffn_fwd(n + "ffn1_fwd", h0, lay["ffn1_norm"], big["ffn1_wg"], big["ffn1_wu"], big["ffn1_wo"])
    zero_b = jnp.zeros((1, PW), f32)
    proj = normlin_fwd(n + "inproj_fwd", h1, lay["mix_norm"], big["w_in"], zero_b, tn=896)
    gpre = normlin_fwd(n + "gate_fwd", h1, lay["mix_norm"], big["w_gate"], lay["b_gate"], tn=1024)
    small = proj[:, SMALL_OFF:SMALL_OFF + 16]
    raw = [lay["s5_are"], lay["s5_aim"], lay["s5_lstep"], lay["s5_bre"], lay["s5_bim"]]
    ops = s5_ops_fwd(n + "s5ops_fwd", raw)
    y5, sv5 = s5_fwd(n + "s5_fwd", proj, ops, lay["s5_cre"], lay["s5_cim"], lay["s5_d"])
    ya = s5_glu_fwd(n + "s5glu_fwd", y5, big["s5_w_glu"], lay["s5_b_glu"])
    lru_ws = [lw["lru_conv_w"], lay["lru_conv_b"], lay["lru_wr"], lay["lru_b_r"], lay["lru_wi"], lay["lru_b_i"], lay["lru_lambda"]]
    yb, svb = lru_fwd(n + "lru_fwd", proj, lru_ws)
    m2_ws = [lay[k] for k in ("m2_cwx", "m2_cwb", "m2_cwc", "m2_cbx", "m2_cbb", "m2_cbc", "m2_dtb", "m2_alog", "m2_dsk", "m2_norm")]
    yc, svc = m2_fwd(n + "m2_fwd", proj, small, m2_ws)
    yd, svd = gdn_fwd(n + "gdn_fwd", proj, small, lw["gdn_conv_w"], lay["gdn_dtb"], lay["gdn_alog"], lay["gdn_norm"])
    ys = jnp.concatenate([ya, yb, yc, yd], axis=1)
    h2 = merge_fwd(n + "merge_fwd", h1, ys, gpre, big["w_branch"], big["w_out"])
    h3 = ffn_fwd(n + "ffn2_fwd", h2, lay["ffn2_norm"], big["ffn2_wg"], big["ffn2_wu"], big["ffn2_wo"])
    h4 = ple_fwd(n + "ple_fwd", h3, pe, lay["ple_norm"], big["ple_w_gate"], big["ple_w_proj"])
    a.update(h1=h1, proj=proj, gpre=gpre, small=small, raw=raw, ops=ops, y5=y5, sv5=sv5, lru_ws=lru_ws, svb=svb,
             m2_ws=m2_ws, svc=svc, svd=svd, ys=ys, h2=h2, h3=h3, pe=pe)
    return h4, a


def small_into(name, a, b, gproj, t=512):
    s, w = a.shape
    pad_w = PW - SMALL_OFF

    def body(a_ref, b_ref, _, o_ref):
        place = (_rows((w, pad_w)) == _lanes((w, pad_w))).astype(f32)
        o_ref[...] = _dg(a_ref[...] + b_ref[...], place, _NN, (3, 1))

    row = pl.BlockSpec((t, w), lambda i: (i, 0))
    return pl.pallas_call(
        body, name=name, grid=(s // t,), in_specs=[row, row, _ANY],
        out_specs=pl.BlockSpec((t, pad_w), lambda i: (i, SMALL_OFF // pad_w)),
        out_shape=jax.ShapeDtypeStruct(gproj.shape, gproj.dtype), input_output_aliases={2: 0}, compiler_params=_params((1,)),
    )(a, b, gproj)


def _layer_bwd(i, a, lw, big, gh4):
    n = f"l{i}_"
    lay = a["lay"]
    gb, gl = {}, {}
    s = gh4.shape[0]
    t = 512
    gh3, gl["ple_norm"], gb["ple_w_gate"], gb["ple_w_proj"] = ple_bwd(
        n + "ple_bwd", a["h3"], a["pe"], lay["ple_norm"], big["ple_w_gate"], big["ple_w_proj"], gh4)
    dh2, gl["ffn2_norm"], gb["ffn2_wg"], gb["ffn2_wu"], gb["ffn2_wo"] = ffn_bwd(
        n + "ffn2_bwd", a["h2"], lay["ffn2_norm"], big["ffn2_wg"], big["ffn2_wu"], big["ffn2_wo"], gh3)
    gh2 = addn(n + "gh2", [(gh3, None)] + _parts(dh2), t)
    gys, ggpre, gb["w_branch"], gb["w_out"] = merge_bwd(
        n + "merge_bwd", a["h1"], a["ys"], a["gpre"], big["w_branch"], big["w_out"], gh2)
    rd = gdn_bwd(n + "gdn_bwd", a["proj"], a["small"], lw["gdn_conv_w"], lay["gdn_dtb"], lay["gdn_alog"], lay["gdn_norm"], a["svd"], gys, 3)
    gproj, dsm_d = rd[:2]
    gl["gdn_conv_w"] = jnp.concatenate(rd[2:5], axis=1)
    gl["gdn_dtb"], gl["gdn_alog"], gl["gdn_norm"] = rd[5:]
    rb = lru_bwd(n + "lru_bwd", a["proj"], a["lru_ws"], a["svb"], gys, 1, gproj)
    gproj = rb[0]
    gl["lru_conv_w"], gl["lru_conv_b"], gl["lru_wr"], gl["lru_b_r"], gl["lru_wi"], gl["lru_b_i"], gl["lru_lambda"] = rb[1:]
    rc = m2_bwd(n + "m2_bwd", a["proj"], a["small"], a["m2_ws"], a["svc"], gys, 2, gproj)
    gproj, dsm_c = rc[:2]
    for k, v in zip(("m2_cwx", "m2_cwb", "m2_cwc", "m2_cbx", "m2_cbb", "m2_cbc", "m2_dtb", "m2_alog", "m2_dsk", "m2_norm"), rc[2:]):
        gl[k] = v
    gy5, gb["s5_w_glu"], gl["s5_b_glu"] = s5_glu_bwd(n + "s5glu_bwd", a["y5"], big["s5_w_glu"], lay["s5_b_glu"], gys, 0)
    r5 = s5_bwd(n + "s5_bwd", a["proj"], a["ops"], lay["s5_cre"], lay["s5_cim"], lay["s5_d"], a["sv5"], gy5, gproj)
    gproj, gops, gl["s5_cre"], gl["s5_cim"], gl["s5_d"] = r5[0], r5[1:7], r5[7], r5[8], r5[9]
    gl["s5_are"], gl["s5_aim"], gl["s5_lstep"], gl["s5_bre"], gl["s5_bim"] = s5_ops_bwd(n + "s5ops_bwd", a["raw"], gops)
    gproj = small_into(n + "gsmall", dsm_c, dsm_d, gproj)
    zero_b = jnp.zeros((1, PW), f32)
    dh1_p, gmn_p, gb["w_in"], _ = normlin_bwd(n + "inproj_bwd", a["h1"], lay["mix_norm"], big["w_in"], zero_b, gproj, tn=896)
    dh1_g, gmn_g, gb["w_gate"], gl["b_gate"] = normlin_bwd(n + "gate_bwd", a["h1"], lay["mix_norm"], big["w_gate"], lay["b_gate"], ggpre, tn=1024)
    gh1 = addn(n + "gh1", [(gh2, None)] + _parts(dh1_p) + _parts(dh1_g), t)
    dh0, gl["ffn1_norm"], gb["ffn1_wg"], gb["ffn1_wu"], gb["ffn1_wo"] = ffn_bwd(
        n + "ffn1_bwd", a["h0"], lay["ffn1_norm"], big["ffn1_wg"], big["ffn1_wu"], big["ffn1_wo"], gh1)
    gh0 = addn(n + "gh0", [(gh1, None)] + _parts(dh0), t)
    gl["mix_norm"] = gmn_p + gmn_g
    return gh0, gb, gl


def _local_step(x, p, target, bigs, smalls, final_norm):
    h = x
    acts = []
    for i in range(DEPTH):
        h, a = _layer_fwd(i, h, p[i], smalls[i], bigs[i])
        acts.append(a)
    fg = final_norm.reshape(1, -1)
    loss, gh, gfn = final_loss("final_loss", h, fg, target)
    gbs, gss = [None] * DEPTH, [None] * DEPTH
    for i in reversed(range(DEPTH)):
        gh, gb, gl = _layer_bwd(i, acts[i], smalls[i], bigs[i], gh)
        _, pull = jax.vjp(_layer_layout, smalls[i])
        lay_g = {k: gl[k] for k in acts[i]["lay"]}
        gs = pull(lay_g)[0]
        gs = dict(gs)
        gs["lru_conv_w"] = gl["lru_conv_w"]
        gs["gdn_conv_w"] = gl["gdn_conv_w"]
        gbs[i], gss[i] = gb, gs
    return loss, gh, gbs, gss, gfn.reshape(-1)


MESH_AXES = ("x", "y", "c")
PACK_W = 1024
PACK_ROWS = 256

W_NAMES = ("ffn1_norm", "ffn1_w_in", "ffn1_w_out", "mix_norm", "w_in", "w_gate", "b_gate", "s5_log_step", "s5_a_re",
           "s5_a_im", "s5_b_re", "s5_b_im", "s5_c_re", "s5_c_im", "s5_d", "s5_w_glu", "s5_b_glu", "lru_conv_w",
           "lru_conv_b", "lru_w_r", "lru_b_r", "lru_w_i", "lru_b_i", "lru_lambda", "m2_conv_w", "m2_conv_b",
           "m2_dt_bias", "m2_a_log", "m2_d", "m2_norm", "gdn_conv_w", "gdn_dt_bias", "gdn_a_log", "gdn_norm",
           "w_branch", "w_out", "ffn2_norm", "ffn2_w_in", "ffn2_w_out", "ple_norm", "ple_w_gate", "ple_w_proj",
           "final_norm")
COL_SHARDED = ("ffn1_w_in", "w_in", "w_gate", "lru_conv_w", "m2_conv_w", "gdn_conv_w", "w_branch", "ffn2_w_in", "ple_w_proj")
ROW_SHARDED = ("ffn1_w_out", "s5_w_glu", "w_out", "ffn2_w_out", "ple_w_gate")
BIG_NAMES = ("ffn1_w_in", "ffn1_w_out", "w_in", "w_gate", "s5_w_glu", "w_branch", "w_out", "ffn2_w_in", "ffn2_w_out",
             "ple_w_gate", "ple_w_proj")
CONV_NAMES = ("lru_conv_w", "m2_conv_w", "gdn_conv_w")
SHARDED = BIG_NAMES + CONV_NAMES
REPLICATED = tuple(k for k in W_NAMES if k not in SHARDED)


def _gathered_to_full(name, g):
    if name in COL_SHARDED:
        g = jnp.moveaxis(g, 0, -2)
        return g.reshape(g.shape[:-2] + (g.shape[-2] * g.shape[-1],))
    g = jnp.moveaxis(g, 0, 1)
    return g.reshape((g.shape[0], g.shape[1] * g.shape[2]) + g.shape[3:])


def _full_to_scattered(name, w):
    if name in COL_SHARDED:
        w = w.reshape(w.shape[:-1] + (N_DEV, w.shape[-1] // N_DEV))
        return jnp.moveaxis(w, -2, 0)
    w = w.reshape((w.shape[0], N_DEV, w.shape[1] // N_DEV) + w.shape[2:])
    return jnp.moveaxis(w, 1, 0)


def _pack(arrs):
    pieces = []
    for a in arrs:
        k = -(-a.size // PACK_W)
        pieces.append(jnp.pad(a.reshape(-1), (0, k * PACK_W - a.size)).reshape(k, PACK_W))
    buf = jnp.concatenate(pieces, axis=0)
    return jnp.pad(buf, ((0, -buf.shape[0] % PACK_ROWS), (0, 0)))


def _unpack(buf, shapes):
    out, r = [], 0
    for sh in shapes:
        n = math.prod(sh)
        k = -(-n // PACK_W)
        out.append(buf[r:r + k].reshape(-1)[:n].reshape(sh))
        r += k
    return out


def _peer(k):
    mx, my, mc = (lax.axis_index(a) for a in MESH_AXES)
    px = 1 - mx if k & 4 else mx
    py = 1 - my if k & 2 else my
    pc = 1 - mc if k & 1 else mc
    return (px, py, pc), 4 * px + 2 * py + pc


def all_gather(name, xs):
    n = len(xs)

    def body(*refs):
        x_refs, out_refs = refs[:n], refs[n:2 * n]
        send_sems, recv_sems, local_sems = refs[2 * n:]
        mx, my, mc = (lax.axis_index(a) for a in MESH_AXES)
        me, sibling = (mx, my, mc), (mx, my, 1 - mc)
        chips = [(1 - mx, my), (mx, 1 - my), (1 - mx, 1 - my)]

        def slot(i, px, py, pc):
            return out_refs[i].at[4 * px + 2 * py + pc]

        def copy(k, i, block, to, src=None):
            return pltpu.make_async_remote_copy(
                src_ref=slot(i, *block) if src is None else src, dst_ref=slot(i, *block),
                send_sem=send_sems.at[k, i], recv_sem=recv_sems.at[k, i], device_id=to, device_id_type=pl.DeviceIdType.MESH)

        mine = [pltpu.make_async_copy(x_refs[i], slot(i, *me), local_sems.at[i]) for i in range(n)]
        first = []
        for i in range(n):
            mine[i].start()
            first.append(copy(0, i, me, sibling, src=x_refs[i]))
            first += [copy(1 + j, i, me, (*chip, mc), src=x_refs[i]) for j, chip in enumerate(chips)]
        for cp in first:
            cp.start()
        passed = []
        for i in range(n):
            for j, chip in enumerate(chips):
                copy(1 + j, i, (*chip, mc), me).wait_recv()
                cp = copy(4 + j, i, (*chip, mc), sibling)
                cp.start()
                passed.append(cp)
        for i in range(n):
            copy(0, i, sibling, me).wait_recv()
            for j, chip in enumerate(chips):
                copy(4 + j, i, (*chip, 1 - mc), me).wait_recv()
        for cp in first + passed:
            cp.wait_send()
        for cp in mine:
            cp.wait()

    res = pl.pallas_call(
        body, name=name, out_shape=[jax.ShapeDtypeStruct((N_DEV,) + x.shape, x.dtype) for x in xs],
        in_specs=[_ANY] * n, out_specs=[_ANY] * n,
        scratch_shapes=[pltpu.SemaphoreType.DMA((7, n)), pltpu.SemaphoreType.DMA((7, n)), pltpu.SemaphoreType.DMA((n,))],
    )(*xs)
    return list(res)


def exchange(name, gs):
    n = len(gs)

    def body(*refs):
        g_refs, out_refs = refs[:n], refs[n:2 * n]
        send_sems, recv_sems, local_sems = refs[2 * n:]
        mx, my, mc = (lax.axis_index(a) for a in MESH_AXES)
        me = 4 * mx + 2 * my + mc
        mine, copies = [], []
        for i in range(n):
            cp = pltpu.make_async_copy(g_refs[i].at[me], out_refs[i].at[me], local_sems.at[i])
            cp.start()
            mine.append(cp)
        for k in range(1, N_DEV):
            peer, pidx = _peer(k)
            for i in range(n):
                cp = pltpu.make_async_remote_copy(
                    src_ref=g_refs[i].at[pidx], dst_ref=out_refs[i].at[me], send_sem=send_sems.at[k - 1, i],
                    recv_sem=recv_sems.at[k - 1, i], device_id=peer, device_id_type=pl.DeviceIdType.MESH)
                cp.start()
                copies.append(cp)
        for k in range(1, N_DEV):
            peer, pidx = _peer(k)
            for i in range(n):
                pltpu.make_async_remote_copy(
                    src_ref=g_refs[i].at[pidx], dst_ref=out_refs[i].at[pidx], send_sem=send_sems.at[k - 1, i],
                    recv_sem=recv_sems.at[k - 1, i], device_id=peer, device_id_type=pl.DeviceIdType.MESH).wait_recv()
        for cp in copies:
            cp.wait_send()
        for cp in mine:
            cp.wait()

    res = pl.pallas_call(
        body, name=name, out_shape=[jax.ShapeDtypeStruct(g.shape, g.dtype) for g in gs], in_specs=[_ANY] * n, out_specs=[_ANY] * n,
        scratch_shapes=[pltpu.SemaphoreType.DMA((7, n)), pltpu.SemaphoreType.DMA((7, n)), pltpu.SemaphoreType.DMA((n,))],
    )(*gs)
    return list(res)


def sum_slots(name, buf):
    return addn(name, _parts(buf), PACK_ROWS)


def _adamw_math(ww, gg, mm_, vv):
    m2 = ADAM_B1 * mm_ + (1.0 - ADAM_B1) * gg
    v2 = ADAM_B2 * vv + (1.0 - ADAM_B2) * (gg * gg)
    m_hat = m2 / (1.0 - ADAM_B1 ** ADAM_STEP)
    v_hat = v2 / (1.0 - ADAM_B2 ** ADAM_STEP)
    delta = -ADAM_LR * (m_hat / (jnp.sqrt(v_hat) + ADAM_EPS) + ADAM_WD * ww)
    return delta, m2, v2


def adamw(name, w, g, m, v):
    spec = pl.BlockSpec((PACK_ROWS, PACK_W), lambda i: (i, 0))
    return run_fwd(name, _adamw_math, (w.shape[0] // PACK_ROWS,), [w, g, m, v], [spec] * 4, [_out(w.shape, spec)] * 3)[0]


def reduce_adamw(name, slots, w, m, v):
    shape = w.shape
    r, c = shape[-2:]
    a = math.prod(shape[:-2])
    tr = r
    while tr * c * 4 > (1 << 20) and tr % 16 == 0:
        tr //= 2
    s3 = slots.reshape((N_DEV, a, r, c))
    w3, m3, v3 = (t.reshape((a, r, c)) for t in (w, m, v))

    def fn(*t):
        g = t[0].astype(f32)
        for d in range(1, N_DEV):
            g = g + t[d].astype(f32)
        return (g,) + _adamw_math(t[N_DEV], g, t[N_DEV + 1], t[N_DEV + 2])

    specs = [pl.BlockSpec((None, None, tr, c), lambda i, j, _d=d: (_d, i, j, 0)) for d in range(N_DEV)]
    spec = pl.BlockSpec((None, tr, c), lambda i, j: (i, j, 0))
    res = run_fwd(name, fn, (a, r // tr), [s3] * N_DEV + [w3, m3, v3], specs + [spec] * 3, [_out((a, r, c), spec)] * 4)[0]
    return [t.reshape(shape) for t in res]


def kernel(*args):
    nw = len(W_NAMES)
    x, p = args[0], args[1]
    w = dict(zip(W_NAMES, args[2:2 + nw]))
    target = args[2 + nw]
    m = dict(zip(W_NAMES, args[3 + nw:3 + 2 * nw]))
    v = dict(zip(W_NAMES, args[3 + 2 * nw:3 + 3 * nw]))

    gathered = all_gather("ag_weights", [w[k].astype(bf16) for k in BIG_NAMES] + [w[k] for k in CONV_NAMES])
    full = {k: _gathered_to_full(k, g) for k, g in zip(SHARDED, gathered)}

    bigs = [big_layout({k: full[k][i] for k in BIG_NAMES}) for i in range(DEPTH)]
    smalls = []
    for i in range(DEPTH):
        sm = {k: w[k][i] for k in _SMALL_KEYS if k not in CONV_NAMES}
        sm.update({k: full[k][i] for k in CONV_NAMES})
        smalls.append(sm)
    loss, gx, gbs, gss, gfn = _local_step(x[0], p[:, 0], target[0], bigs, smalls, w["final_norm"])
    loss = lax.psum(loss[0, 0], MESH_AXES)

    gfull = [dict(big_unlayout(gbs[i]), **gss[i]) for i in range(DEPTH)]
    stack = lambda k: jnp.stack([gfull[i][k] for i in range(DEPTH)])
    slots = exchange("rs_sharded", [_full_to_scattered(k, stack(k)).astype(bf16) for k in SHARDED])
    outs = {}
    kinds = ("grad", "delta", "new_m", "new_v")
    for k, sl in zip(SHARDED, slots):
        for kind, a in zip(kinds, reduce_adamw("adamw_" + k, sl, w[k], m[k], v[k])):
            outs[kind + "_" + k] = a

    shapes = [w[k].shape for k in REPLICATED]
    g_rep = sum_slots("sum_replicated", all_gather("ag_replicated", [_pack([gfn if k == "final_norm" else stack(k) for k in REPLICATED])])[0])
    res = adamw("adamw_replicated", _pack([w[k] for k in REPLICATED]), g_rep, _pack([m[k] for k in REPLICATED]), _pack([v[k] for k in REPLICATED]))
    for kind, buf in zip(kinds, [g_rep] + list(res)):
        for k, a in zip(REPLICATED, _unpack(buf, shapes)):
            outs[kind + "_" + k] = a
    return (loss, gx[None]) + tuple(outs[kind + "_" + k] for kind in kinds for k in W_NAMES)
```

```python
import functools
import math

import jax
import jax.numpy as jnp
from jax import lax
from jax.experimental import pallas as pl
from jax.experimental.pallas import tpu as pltpu

f32 = jnp.float32
bf16 = jnp.bfloat16

EPS = 1e-6
DEPTH = 2
D_MODEL = 1024
FFN_DIM = 2816
BW = 512
IN_WIDTH = 5136
PW = 5376
SMALL_OFF = 5120
CHUNK = 64
LS = 256
LRU_C = 8.0
N_DEV = 8
VMEM_LIMIT_BYTES = 56 * 1024 * 1024

ADAM_LR, ADAM_B1, ADAM_B2, ADAM_EPS, ADAM_WD, ADAM_STEP = 0.001, 0.9, 0.999, 1e-08, 0.01, 10


_NN, _NT, _TN = ((1,), (0,)), ((1,), (1,)), ((0,), (0,))


def _pieces(x, n):
    parts, r = [], x
    for i in range(n):
        p = r.astype(bf16)
        parts.append(p)
        if i + 1 < n:
            r = r - p.astype(f32)
    return parts


def _dg(a, b, dims, mode):
    sa, sb = mode
    pa, pb = _pieces(a, sa), _pieces(b, sb)
    out = None
    for i in reversed(range(sa)):
        for j in reversed(range(sb)):
            if i + j < max(sa, sb):
                d = lax.dot_general(pa[i], pb[j], (dims, ((), ())), preferred_element_type=f32)
                out = d if out is None else out + d
    return out


def _make_mm(mode):
    sa, sb = mode
    cot = lambda s_other: 1 if mode == (1, 1) else (3 if s_other == 1 else 2)
    m_g_b, m_a_g, m_g_a, m_b_g = (cot(sb), sb), (sa, cot(sa)), (cot(sa), sa), (sb, cot(sb))

    @jax.custom_vjp
    def nn(a, b):
        return _dg(a, b, _NN, mode)

    @jax.custom_vjp
    def nt(a, b):
        return _dg(a, b, _NT, mode)

    @jax.custom_vjp
    def tn(a, b):
        return _dg(a, b, _TN, mode)

    nn.defvjp(lambda a, b: (nn(a, b), (a, b)), lambda r, g: (_dg(g, r[1], _NT, m_g_b), _dg(r[0], g, _TN, m_a_g)))
    nt.defvjp(lambda a, b: (nt(a, b), (a, b)), lambda r, g: (_dg(g, r[1], _NN, m_g_b), _dg(g, r[0], _TN, m_g_a)))
    tn.defvjp(lambda a, b: (tn(a, b), (a, b)), lambda r, g: (_dg(r[1], g, _NT, m_b_g), _dg(r[0], g, _NN, m_a_g)))
    return nn, nt, tn


mm, mm_nt, mm_tn = _make_mm((1, 1))
xmm, xmm_nt, xmm_tn = _make_mm((2, 2))
lmm, lmm_nt, lmm_tn = _make_mm((1, 3))
rmm, rmm_nt, rmm_tn = _make_mm((3, 1))


@jax.custom_vjp
def _tri_inv(m):
    n = m.shape[0]
    eye = (_rows((n, n)) == _lanes((n, n))).astype(f32)
    blk = (_rows((n, n)) // 16) == (_lanes((n, n)) // 16)
    x = lambda a, b: _dg(a, b, _NN, (2, 2))
    nb = jnp.where(blk, m, 0.0)
    p = -nb
    t = eye + p
    for _ in range(3):
        p = x(p, p)
        t = t + x(t, p)
    q = x(t, m - nb)
    imq = eye - q
    return x(imq + x(imq, x(q, q)), t)


def _tri_inv_fwd(m):
    t = _tri_inv(m)
    return t, t


def _tri_inv_bwd(t, g):
    return (-_dg(_dg(t, g, _TN, (2, 2)), t, _NT, (2, 2)),)


_tri_inv.defvjp(_tri_inv_fwd, _tri_inv_bwd)


def _rows(shape):
    return lax.broadcasted_iota(jnp.int32, shape, 0)


def _lanes(shape):
    return lax.broadcasted_iota(jnp.int32, shape, 1)


def _rms(x, g):
    return x * lax.rsqrt(jnp.mean(x * x, axis=-1, keepdims=True) + EPS) * g


def _silu(x):
    return x * jax.nn.sigmoid(x)


def _gelu(x):
    return 0.5 * x * (1.0 + jnp.tanh(0.7978845608028654 * (x + 0.044715 * x * x * x)))


def _softplus(x):
    return jnp.maximum(x, 0.0) + jnp.log1p(jnp.exp(-jnp.abs(x)))


def _expm1(x):
    p = x * (1.0 + x * (0.5 + x * (1.0 / 6 + x * (1.0 / 24 + x * (1.0 / 120 + x * (1.0 / 720 + x * (1.0 / 5040)))))))
    return jnp.where(x > -0.3, p, jnp.exp(x) - 1.0)


def _pick_row(x, r):
    return jnp.sum(jnp.where(_rows(x.shape) == r, x, 0.0), axis=0, keepdims=True)


def _pick_lane(x, c):
    return jnp.sum(jnp.where(_lanes(x.shape) == c, x, 0.0), axis=1, keepdims=True)


def _shift_up(g, j):
    n = g.shape[0]
    return jnp.where(_rows(g.shape) < n - j, pltpu.roll(g, n - j, 0), 0.0)


@functools.partial(jax.custom_vjp, nondiff_argnums=(1, 2))
def _shift(x, j, fill):
    return jnp.where(_rows(x.shape) >= j, pltpu.roll(x, j, 0), fill)


_shift.defvjp(lambda x, j, fill: (_shift(x, j, fill), None), lambda j, fill, _, g: (_shift_up(g, j),))


@functools.partial(jax.custom_vjp, nondiff_argnums=(2,))
def _shift_halo(x, prev8, j):
    xr = pltpu.roll(x, j, 0)
    pr = pltpu.roll(prev8, j, 0)
    top = jnp.where(_rows(pr.shape) < j, pr, xr[:8])
    return jnp.concatenate([top, xr[8:]], axis=0)


def _shift_halo_bwd(j, _, g):
    g8 = g[:8]
    dprev = jnp.where(_rows(g8.shape) >= 8 - j, pltpu.roll(g8, 8 - j, 0), 0.0)
    return _shift_up(g, j), dprev


_shift_halo.defvjp(lambda x, p, j: (_shift_halo(x, p, j), None), _shift_halo_bwd)


def _conv4(x, prev8, w, b):
    y = _pick_row(w, 3) * x
    for k in range(3):
        y = y + _pick_row(w, k) * _shift_halo(x, prev8, 3 - k)
    return y if b is None else y + b


def _cmul(ar, ai, br, bi):
    return ar * br - ai * bi, ar * bi + ai * br


_ANY = pl.BlockSpec(memory_space=pl.ANY)


def _params(grid):
    return pltpu.CompilerParams(dimension_semantics=("arbitrary",) * len(grid), vmem_limit_bytes=VMEM_LIMIT_BYTES)


def _first(axes):
    ok = pl.program_id(axes[0]) == 0
    for a in axes[1:]:
        ok = jnp.logical_and(ok, pl.program_id(a) == 0)
    return ok


def _store(ref, val, acc):
    val = val.astype(ref.dtype)
    if acc is None:
        ref[...] = val
        return
    first = _first(acc)

    @pl.when(first)
    def _():
        ref[...] = val

    @pl.when(jnp.logical_not(first))
    def _():
        ref[...] += val


def _full(a):
    nd = a.ndim
    return pl.BlockSpec(a.shape, lambda *g: (0,) * nd)


def _out(shape, spec, acc=None, dtype=f32):
    return dict(shape=tuple(shape), spec=spec, acc=acc, dtype=dtype)


def run_fwd(name, fn, grid, ins, in_specs, outs, carry=None):
    n_in, n_out = len(ins), len(outs)
    cshapes = carry["shapes"] if carry else []
    nc = len(cshapes)
    ng = len(grid)

    def body(*refs):
        in_refs = refs[:n_in]
        out_refs = refs[n_in:n_in + n_out]
        save_refs = refs[n_in + n_out:n_in + n_out + nc]
        c_refs = refs[n_in + n_out + nc:]
        vals = [r[...] for r in in_refs]
        if carry:
            @pl.when(pl.program_id(carry["axis"]) == 0)
            def _():
                for c in c_refs:
                    c[...] = jnp.zeros(c.shape, f32)
            cin = tuple(c[...] for c in c_refs)
            for s, v in zip(save_refs, cin):
                s[...] = v
            cout, res = fn(cin, *vals)
            for c, v in zip(c_refs, cout):
                c[...] = v
        else:
            res = fn(*vals)
        for o, r, d in zip(out_refs, res, outs):
            _store(o, r, d["acc"])

    out_shape = [jax.ShapeDtypeStruct(d["shape"], d["dtype"]) for d in outs]
    out_specs = [d["spec"] for d in outs]
    for cs in cshapes:
        out_shape.append(jax.ShapeDtypeStruct(tuple(grid) + tuple(cs), f32))
        out_specs.append(pl.BlockSpec((None,) * ng + tuple(cs), lambda *g, _n=len(cs): tuple(g) + (0,) * _n))
    res = pl.pallas_call(
        body, name=name, grid=grid, in_specs=list(in_specs), out_specs=out_specs, out_shape=out_shape,
        scratch_shapes=[pltpu.VMEM(tuple(cs), f32) for cs in cshapes], compiler_params=_params(grid),
    )(*ins)
    return list(res[:n_out]), list(res[n_out:])


def run_bwd(name, fn, grid, ins, in_specs, gouts, gout_specs, wants, carry=None, into=None, ride=None):
    n_in, n_g, n_w = len(ins), len(gouts), len(wants)
    n_a = 0 if into is None else 1
    ride = list(ride or [])
    n_r = len(ride)
    saved = carry["saved"] if carry else []
    nc = len(saved)
    ng = len(grid)

    def body(*refs):
        in_refs = refs[:n_in]
        g_refs = refs[n_in:n_in + n_g]
        s_refs = refs[n_in + n_g:n_in + n_g + nc]
        base = n_in + n_g + nc + n_a
        r_in = refs[base:base + n_r]
        w_refs = refs[base + n_r:base + n_r + n_w]
        r_out = refs[base + n_r + n_w:base + 2 * n_r + n_w]
        dc_refs = refs[base + 2 * n_r + n_w:base + 2 * n_r + n_w + nc]
        if n_r:
            steps = math.prod(grid)
            step = pl.program_id(0)
            for ax in range(1, ng):
                step = step * grid[ax] + pl.program_id(ax)
            r_args = (r_in, r_out) + tuple(refs[base + 2 * n_r + n_w + nc:])
            pl.when(step == 0)(lambda: _exchange_start(*r_args))
        vals = [r[...].astype(f32) for r in in_refs]
        gs = tuple(r[...].astype(f32) for r in g_refs)
        if carry:
            @pl.when(pl.program_id(carry["axis"]) == 0)
            def _():
                for c in dc_refs:
                    c[...] = jnp.zeros(c.shape, f32)
            cin = tuple(s[...] for s in s_refs)
            _, vjp = jax.vjp(fn, cin, *vals)
            grads = vjp((tuple(c[...] for c in dc_refs), gs))
            for c, v in zip(dc_refs, grads[0]):
                c[...] = v
            dvals = grads[1:]
        else:
            _, vjp = jax.vjp(fn, *vals)
            dvals = vjp(gs)
        for o, d in zip(w_refs, wants):
            idx = d["idx"]
            val = dvals[idx] if isinstance(idx, int) else jnp.concatenate([dvals[j] for j in idx], axis=1)
            _store(o, val, d["acc"])
        if n_r:
            pl.when(step == steps - 1)(lambda: _exchange_wait(*r_args))

    rev = carry["rev"] if carry else None
    s_specs = []
    for a in saved:
        n = a.ndim - ng
        s_specs.append(pl.BlockSpec((None,) * ng + tuple(a.shape[ng:]), lambda *g, _n=n: tuple(rev(g)) + (0,) * _n))
    res = pl.pallas_call(
        body, name=name, grid=grid, in_specs=list(in_specs) + list(gout_specs) + s_specs + [_ANY] * (n_a + n_r),
        out_specs=[d["spec"] for d in wants] + [_ANY] * n_r,
        out_shape=[jax.ShapeDtypeStruct(d["shape"], d["dtype"]) for d in wants] + [jax.ShapeDtypeStruct(g.shape, g.dtype) for g in ride],
        input_output_aliases={n_in + n_g + nc: 0} if n_a else {},
        scratch_shapes=[pltpu.VMEM(tuple(a.shape[ng:]), f32) for a in saved] + (_exchange_sems(n_r) if n_r else []),
        compiler_params=_params(grid),
    )(*ins, *gouts, *saved, *([into] if n_a else []), *ride)
    return list(res)


def _want(idx, shape, spec, acc=None):
    d = _out(shape, spec, acc)
    d["idx"] = idx
    return d


def addn(name, items, t):
    s, w = items[0][0].shape[-2:]
    specs = []
    for a, j in items:
        if j is None:
            specs.append(pl.BlockSpec((t, w), lambda i: (i, 0)))
        else:
            specs.append(pl.BlockSpec((None, t, w), lambda i, _j=j: (_j, i, 0)))

    def fn(*xs):
        y = xs[0]
        for x in xs[1:]:
            y = y + x
        return (y,)

    return run_fwd(name, fn, (s // t,), [a for a, _ in items], specs,
                   [_out((s, w), pl.BlockSpec((t, w), lambda i: (i, 0)))])[0][0]


def _parts(a):
    return [(a, j) for j in range(a.shape[0])]


def _ffn_core(n, wg, wu, wo):
    return (0.5 * mm(_silu(mm(n, wg)) * mm(n, wu), wo),)


def ffn_fwd(name, h, g, wg, wu, wo, t=512, tf=1408):
    s, d = h.shape
    f = wg.shape[1]

    def fn(hh, gg, a, b, c):
        n = _rms(hh, gg)
        return _ffn_core(n, a, b, c)[0] + (pl.program_id(1) == 0).astype(f32) * hh, n

    specs = [pl.BlockSpec((t, d), lambda i, j: (i, 0)), _full(g), pl.BlockSpec((d, tf), lambda i, j: (0, j)),
             pl.BlockSpec((d, tf), lambda i, j: (0, j)), pl.BlockSpec((tf, d), lambda i, j: (j, 0))]
    outs = [_out((s, d), pl.BlockSpec((t, d), lambda i, j: (i, 0)), acc=(1,)),
            _out((s, d), pl.BlockSpec((t, d), lambda i, j: (i, 0)), dtype=bf16)]
    return run_fwd(name, fn, (s // t, f // tf), [h, g, wg, wu, wo], specs, outs)[0]


def ffn_bwd(name, n, wg, wu, wo, gout, t=512, tf=256):
    s, d = n.shape
    f = wg.shape[1]
    nj = f // tf
    specs = [pl.BlockSpec((t, d), lambda j, i: (i, 0)), pl.BlockSpec((d, tf), lambda j, i: (0, j)),
             pl.BlockSpec((d, tf), lambda j, i: (0, j)), pl.BlockSpec((tf, d), lambda j, i: (j, 0))]
    wants = [_want(0, (nj, s, d), pl.BlockSpec((None, t, d), lambda j, i: (j, i, 0))),
             _want(1, wg.shape, pl.BlockSpec((d, tf), lambda j, i: (0, j)), acc=(1,)),
             _want(2, wu.shape, pl.BlockSpec((d, tf), lambda j, i: (0, j)), acc=(1,)),
             _want(3, wo.shape, pl.BlockSpec((tf, d), lambda j, i: (j, 0)), acc=(1,))]
    return run_bwd(name, _ffn_core, (nj, s // t), [n, wg, wu, wo], specs,
                   [gout], [pl.BlockSpec((t, d), lambda j, i: (i, 0))], wants)


def norm_bwd(name, h, g, base, parts, t=256):
    s, d = h.shape

    def fn(hh, gg, bb, *ps):
        dn = ps[0]
        for p in ps[1:]:
            dn = dn + p
        _, vjp = jax.vjp(_rms, hh, gg)
        dh, dg = vjp(dn)
        return bb + dh, dg

    row = pl.BlockSpec((t, d), lambda i: (i, 0))
    specs = [row, _full(g), row] + [pl.BlockSpec((None, t, d), lambda i, _j=j: (_j, i, 0)) for _, j in parts]
    outs = [_out((s, d), row), _out(g.shape, _full(g), acc=(0,))]
    return run_fwd(name, fn, (s // t,), [h, g, base] + [a for a, _ in parts], specs, outs)[0]


def _lin_tile(u, w, b):
    return (mm(u, w) + b,)


def normlin_fwd(name, h, g, w, b, t=512, tn=1024):
    s, d = h.shape
    n = w.shape[1]

    def fn(hh, gg, ww, bb):
        u = _rms(hh, gg)
        return mm(u, ww) + bb, u

    specs = [pl.BlockSpec((t, d), lambda i, j: (i, 0)), _full(g), pl.BlockSpec((d, tn), lambda i, j: (0, j)),
             pl.BlockSpec((1, tn), lambda i, j: (0, j))]
    outs = [_out((s, n), pl.BlockSpec((t, tn), lambda i, j: (i, j))),
            _out((s, d), pl.BlockSpec((t, d), lambda i, j: (i, 0)), dtype=bf16)]
    return run_fwd(name, fn, (s // t, n // tn), [h, g, w, b], specs, outs)[0]


def lin_fwd(name, u, w, b, t=512, tn=1024):
    s, d = u.shape
    n = w.shape[1]
    specs = [pl.BlockSpec((t, d), lambda i, j: (i, 0)), pl.BlockSpec((d, tn), lambda i, j: (0, j)),
             pl.BlockSpec((1, tn), lambda i, j: (0, j))]
    outs = [_out((s, n), pl.BlockSpec((t, tn), lambda i, j: (i, j)))]
    return run_fwd(name, _lin_tile, (s // t, n // tn), [u, w, b], specs, outs)[0][0]


def lin_bwd(name, u, w, b, gout, t=256, tn=1024):
    s, d = u.shape
    n = w.shape[1]
    nj = n // tn
    specs = [pl.BlockSpec((t, d), lambda j, i: (i, 0)), pl.BlockSpec((d, tn), lambda j, i: (0, j)),
             pl.BlockSpec((1, tn), lambda j, i: (0, j))]
    wants = [_want(0, (nj, s, d), pl.BlockSpec((None, t, d), lambda j, i: (j, i, 0))),
             _want(1, w.shape, pl.BlockSpec((d, tn), lambda j, i: (0, j)), acc=(1,)),
             _want(2, b.shape, pl.BlockSpec((1, tn), lambda j, i: (0, j)), acc=(1,))]
    return run_bwd(name, _lin_tile, (nj, s // t), [u, w, b], specs,
                   [gout], [pl.BlockSpec((t, tn), lambda j, i: (i, j))], wants)


def _merge_tile(n_axis, residual):
    def fn(h, y, gp, wb, wo):
        part = mm(jax.nn.sigmoid(gp) * mm(y, wb), wo)
        if residual:
            part = part + (pl.program_id(n_axis) == 0).astype(f32) * h
        return (part,)
    return fn


def merge_fwd(name, h, ys, gpre, wb, wo, t=512):
    s, d = h.shape
    specs = [pl.BlockSpec((t, d), lambda i, n: (i, 0)), pl.BlockSpec((t, BW), lambda i, n: (i, n)),
             pl.BlockSpec((t, d), lambda i, n: (i, n)), pl.BlockSpec((None, BW, d), lambda i, n: (n, 0, 0)), _full(wo)]
    outs = [_out((s, d), pl.BlockSpec((t, d), lambda i, n: (i, 0)), acc=(1,))]
    return run_fwd(name, _merge_tile(1, True), (s // t, 4), [h, ys, gpre, wb, wo], specs, outs)[0][0]


def merge_bwd(name, h, ys, gpre, wb, wo, gout, t=256):
    s, d = h.shape
    specs = [pl.BlockSpec((t, d), lambda n, i: (i, 0)), pl.BlockSpec((t, BW), lambda n, i: (i, n)),
             pl.BlockSpec((t, d), lambda n, i: (i, n)), pl.BlockSpec((None, BW, d), lambda n, i: (n, 0, 0)), _full(wo)]
    wants = [_want(1, ys.shape, pl.BlockSpec((t, BW), lambda n, i: (i, n))),
             _want(2, gpre.shape, pl.BlockSpec((t, d), lambda n, i: (i, n))),
             _want(3, wb.shape, pl.BlockSpec((None, BW, d), lambda n, i: (n, 0, 0)), acc=(1,)),
             _want(4, wo.shape, _full(wo), acc=(0, 1))]
    return run_bwd(name, _merge_tile(0, False), (4, s // t), [h, ys, gpre, wb, wo], specs,
                   [gout], [pl.BlockSpec((t, d), lambda n, i: (i, 0))], wants)


def _ple_tile(residual):
    def fn(h, pe, g, wgate, wproj):
        y = jax.nn.sigmoid(mm(_rms(h, g), wgate)) * mm(pe, wproj)
        return (y + h,) if residual else (y,)
    return fn


def ple_fwd(name, h, pe, g, wgate, wproj, t=512):
    s, d = h.shape
    specs = [pl.BlockSpec((t, d), lambda i: (i, 0)), pl.BlockSpec((t, pe.shape[1]), lambda i: (i, 0)),
             _full(g), _full(wgate), _full(wproj)]
    return run_fwd(name, _ple_tile(True), (s // t,), [h, pe, g, wgate, wproj], specs,
                   [_out((s, d), pl.BlockSpec((t, d), lambda i: (i, 0)))])[0][0]


def ple_bwd(name, h, pe, g, wgate, wproj, gout, t=256):
    s, d = h.shape
    specs = [pl.BlockSpec((t, d), lambda i: (i, 0)), pl.BlockSpec((t, pe.shape[1]), lambda i: (i, 0)),
             _full(g), _full(wgate), _full(wproj)]
    wants = [_want(0, h.shape, pl.BlockSpec((t, d), lambda i: (i, 0))), _want(2, g.shape, _full(g), acc=(0,)),
             _want(3, wgate.shape, _full(wgate), acc=(0,)), _want(4, wproj.shape, _full(wproj), acc=(0,))]
    return run_bwd(name, _ple_tile(True), (s // t,), [h, pe, g, wgate, wproj], specs,
                   [gout], [pl.BlockSpec((t, d), lambda i: (i, 0))], wants)


def final_loss(name, h, g, target, t=512):
    s, d = h.shape

    def fn(hh, gg, tt):
        def loss_fn(a, b):
            err = _rms(a, b) - tt
            return 0.5 * jnp.sum(jnp.mean(err * err, axis=-1, keepdims=True), axis=0, keepdims=True)
        loss, vjp = jax.vjp(loss_fn, hh, gg)
        dh, dgain = vjp(jnp.ones((1, 1), f32))
        return loss, dh, dgain

    specs = [pl.BlockSpec((t, d), lambda i: (i, 0)), _full(g), pl.BlockSpec((t, d), lambda i: (i, 0))]
    outs = [_out((1, 1), pl.BlockSpec((1, 1), lambda i: (0, 0)), acc=(0,)),
            _out((s, d), pl.BlockSpec((t, d), lambda i: (i, 0))), _out(g.shape, _full(g), acc=(0,))]
    return run_fwd(name, fn, (s // t,), [h, g, target], specs, outs)[0]


def _rev(nc, rev):
    return (lambda c: nc - 1 - c) if rev else (lambda c: c)


def _s5_ops_tile(are, aim, lstep, bre, bim):
    step = jnp.exp(lstep)
    mag = jnp.exp(are * step)
    ab_re, ab_im = mag * jnp.cos(aim * step), mag * jnp.sin(aim * step)
    den = are * are + aim * aim
    num_re = ab_re - 1.0
    f_re = (num_re * are + ab_im * aim) / den
    f_im = (ab_im * are - num_re * aim) / den
    bb_re = f_re * bre - f_im * bim
    bb_im = f_re * bim + f_im * bre
    pr = jnp.broadcast_to(ab_re, (LS, ab_re.shape[1]))
    pi = jnp.broadcast_to(ab_im, (LS, ab_im.shape[1]))
    k = 1
    while k < LS:
        pr, pi = _cmul(pr, pi, _shift(pr, k, 1.0), _shift(pi, k, 0.0))
        k *= 2
    return ab_re, ab_im, bb_re, bb_im, pr, pi


def _s5_ops_specs(arrs):
    return [pl.BlockSpec((None,) + a.shape[1:], lambda gb: (gb, 0, 0)) for a in arrs]


def s5_ops_fwd(name, raw):
    shapes = [(4, 1, 512), (4, 1, 512), (4, 128, 512), (4, 128, 512), (4, LS, 512), (4, LS, 512)]
    outs = [_out(sh, pl.BlockSpec((None,) + sh[1:], lambda gb: (gb, 0, 0))) for sh in shapes]
    return run_fwd(name, _s5_ops_tile, (4,), raw, _s5_ops_specs(raw), outs)[0]


def s5_ops_bwd(name, raw, gops):
    wants = [_want(i, a.shape, pl.BlockSpec((None,) + a.shape[1:], lambda gb: (gb, 0, 0))) for i, a in enumerate(raw)]
    return run_bwd(name, _s5_ops_tile, (4,), raw, _s5_ops_specs(raw), gops, _s5_ops_specs(gops), wants)


def _s5_tile(carry, u, ab_re, ab_im, bb_re, bb_im, pw_re, pw_im, c_re, c_im, dskip):
    h_re, h_im = carry
    xr, xi = mm(u, bb_re), mm(u, bb_im)
    pr, pi = ab_re, ab_im
    k = 1
    while k < LS:
        sr, si = _cmul(pr, pi, _shift(xr, k, 0.0), _shift(xi, k, 0.0))
        xr, xi = xr + sr, xi + si
        pr, pi = _cmul(pr, pi, pr, pi)
        k *= 2
    cr, ci = _cmul(pw_re, pw_im, h_re, h_im)
    xr, xi = xr + cr, xi + ci
    y = mm(xr, c_re) - mm(xi, c_im) + dskip * u
    return (_pick_row(xr, LS - 1), _pick_row(xi, LS - 1)), (y,)


def _s5_io(u, ops, c_re, c_im, dskip, nc, rev):
    cm = _rev(nc, rev)
    ins = [u] + list(ops) + [c_re, c_im, dskip]
    specs = [pl.BlockSpec((LS, 128), lambda gb, c: (cm(c), 36 + gb))]
    specs += [pl.BlockSpec((None,) + a.shape[1:], lambda gb, c: (gb, 0, 0)) for a in list(ops) + [c_re, c_im]]
    specs += [pl.BlockSpec((1, 128), lambda gb, c: (0, gb))]
    return ins, specs, cm


def s5_fwd(name, proj, ops, c_re, c_im, dskip):
    s = proj.shape[0]
    nc = s // LS
    ins, specs, cm = _s5_io(proj, ops, c_re, c_im, dskip, nc, False)
    outs = [_out((s, BW), pl.BlockSpec((LS, 128), lambda gb, c: (c, gb)))]
    (y,), saved = run_fwd(name, _s5_tile, (4, nc), ins, specs, outs, carry=dict(shapes=[(1, 512), (1, 512)], axis=1))
    return y, saved


def s5_bwd(name, proj, ops, c_re, c_im, dskip, saved, gy, gproj):
    s = proj.shape[0]
    nc = s // LS
    ins, specs, cm = _s5_io(proj, ops, c_re, c_im, dskip, nc, True)
    wants = [_want(0, (s, PW), pl.BlockSpec((LS, 128), lambda gb, c: (cm(c), 36 + gb)))]
    for i, a in enumerate(list(ops) + [c_re, c_im]):
        wants.append(_want(1 + i, a.shape, pl.BlockSpec((None,) + a.shape[1:], lambda gb, c: (gb, 0, 0)), acc=(1,)))
    wants.append(_want(9, dskip.shape, pl.BlockSpec((1, 128), lambda gb, c: (0, gb)), acc=(1,)))
    return run_bwd(name, _s5_tile, (4, nc), ins, specs, [gy], [pl.BlockSpec((LS, 128), lambda gb, c: (cm(c), gb))],
                   wants, carry=dict(axis=1, saved=saved, rev=lambda g: (g[0], cm(g[1]))), into=gproj)


def _s5_glu_tile(y, w, b):
    z = _gelu(y)
    return (z * jax.nn.sigmoid(mm(z, w) + b),)


def s5_glu_fwd(name, y, w, b, t=512):
    s = y.shape[0]
    spec = pl.BlockSpec((t, BW), lambda i: (i, 0))
    return run_fwd(name, _s5_glu_tile, (s // t,), [y, w, b], [spec, _full(w), _full(b)], [_out((s, BW), spec)])[0][0]


def s5_glu_bwd(name, y, w, b, gout, gout_col, t=512):
    s = y.shape[0]
    spec = pl.BlockSpec((t, BW), lambda i: (i, 0))
    wants = [_want(0, y.shape, spec), _want(1, w.shape, _full(w), acc=(0,)), _want(2, b.shape, _full(b), acc=(0,))]
    return run_bwd(name, _s5_glu_tile, (s // t,), [y, w, b], [spec, _full(w), _full(b)], [gout],
                   [pl.BlockSpec((t, BW), lambda i: (i, gout_col))], wants)


def _lru_tile(carry, xb, gate, cw, cb, wr, br, wi, bi, lam):
    h_in, prev8 = carry
    xc = _conv4(xb, prev8, cw, cb)
    r = jax.nn.sigmoid(mm(xc, wr) + br)
    ig = jax.nn.sigmoid(mm(xc, wi) + bi)
    log_a = -LRU_C * r * _softplus(-lam)
    a = jnp.exp(log_a)
    b = jnp.sqrt(-_expm1(2.0 * log_a)) * (ig * xc)
    k = 1
    while k < LS:
        b = b + a * _shift(b, k, 0.0)
        a = a * _shift(a, k, 1.0)
        k *= 2
    h = b + a * h_in
    return (_pick_row(h, LS - 1), xb[LS - 8:, :]), (h * _gelu(gate),)


def _lru_io(proj, ws, nc, rev):
    cm = _rev(nc, rev)
    ins = [proj, proj] + list(ws)
    specs = [pl.BlockSpec((LS, BW), lambda c: (cm(c), 4)), pl.BlockSpec((LS, BW), lambda c: (cm(c), 5))]
    specs += [_full(a) for a in ws]
    return ins, specs, cm


def lru_fwd(name, proj, ws):
    s = proj.shape[0]
    nc = s // LS
    ins, specs, cm = _lru_io(proj, ws, nc, False)
    outs = [_out((s, BW), pl.BlockSpec((LS, BW), lambda c: (c, 0)))]
    (y,), saved = run_fwd(name, _lru_tile, (nc,), ins, specs, outs, carry=dict(shapes=[(1, BW), (8, BW)], axis=0))
    return y, saved


def lru_bwd(name, proj, ws, saved, gy, gy_col, gproj):
    s = proj.shape[0]
    nc = s // LS
    ins, specs, cm = _lru_io(proj, ws, nc, True)
    wants = [_want((0, 1), (s, PW), pl.BlockSpec((LS, 2 * BW), lambda c: (cm(c), 2)))]
    wants += [_want(2 + i, a.shape, _full(a), acc=(0,)) for i, a in enumerate(ws)]
    return run_bwd(name, _lru_tile, (nc,), ins, specs, [gy], [pl.BlockSpec((LS, BW), lambda c: (cm(c), gy_col))], wants,
                   carry=dict(axis=0, saved=saved, rev=lambda g: (cm(g[0]),)), into=gproj)


def _causal(n):
    return _rows((n, n)) >= _lanes((n, n))


def _decay(col, rowv):
    causal = _causal(col.shape[0])
    return jnp.where(causal, jnp.exp(jnp.where(causal, col - rowv, 0.0)), 0.0)


def _m2_tile(carry, z, xs_raw, b_raw, c_raw, small, cwx, cwb, cwc, cbx, cbb, cbc, dtb, alog, dsk, ng):
    state, px, pb, pc = carry
    n = CHUNK
    xs = _silu(_conv4(xs_raw, px, cwx, cbx))
    bm = _silu(_conv4(b_raw, pb, cwb, cbb))
    cmx = _silu(_conv4(c_raw, pc, cwc, cbc))
    expand = (_lanes((16, BW)) // 64 == _rows((16, BW))).astype(f32)
    tri = _causal(n).astype(f32)
    triu = (_rows((n, n)) <= _lanes((n, n))).astype(f32)
    dt = _softplus(small + dtb)
    da = dt * (-jnp.exp(alog))
    cs = lmm(tri, da)
    cs_t = rmm_tn(da, triu)
    cs_w = rmm(cs, expand)
    last_w = _pick_row(cs_w, n - 1)
    xdt = xs * rmm(dt, expand)
    g0 = _lanes((n, BW)) < 256
    bm0, bm1, cm0, cm1 = bm[:, :128], bm[:, 128:], cmx[:, :128], cmx[:, 128:]
    cb0, cb1 = mm_nt(cm0, bm0), mm_nt(cm1, bm1)
    y = jnp.where(g0, mm(cm0, state), mm(cm1, state)) * jnp.exp(cs_w)
    for h in range(8):
        sc = (cb0 if h < 4 else cb1) * _decay(_pick_lane(cs, h), _pick_row(cs_t, h))
        y = y + jnp.where(_lanes((n, BW)) // 64 == h, mm(sc, xdt), 0.0)
    xd = xdt * jnp.exp(last_w - cs_w)
    g0s = _lanes((128, BW)) < 256
    state_out = state * jnp.exp(last_w) + jnp.where(g0s, mm_tn(bm0, xd), mm_tn(bm1, xd))
    y = (y + dsk * xs) * _silu(z)
    return (state_out, xs_raw[n - 8:, :], b_raw[n - 8:, :], c_raw[n - 8:, :]), (_rms(y, ng),)


def _m2_io(proj, small, ws, nc, rev):
    cm = _rev(nc, rev)
    n = CHUNK
    ins = [proj, proj, proj, proj, small] + list(ws)
    specs = [pl.BlockSpec((n, BW), lambda c: (cm(c), 6)), pl.BlockSpec((n, BW), lambda c: (cm(c), 7)),
             pl.BlockSpec((n, 256), lambda c: (cm(c), 16)), pl.BlockSpec((n, 256), lambda c: (cm(c), 17)),
             pl.BlockSpec((n, 16), lambda c: (cm(c), 0))]
    specs += [_full(a) for a in ws]
    return ins, specs, cm


_M2_CARRY = [(128, BW), (8, BW), (8, 256), (8, 256)]


def m2_fwd(name, proj, small, ws):
    s = proj.shape[0]
    nc = s // CHUNK
    ins, specs, cm = _m2_io(proj, small, ws, nc, False)
    outs = [_out((s, BW), pl.BlockSpec((CHUNK, BW), lambda c: (c, 0)))]
    (y,), saved = run_fwd(name, _m2_tile, (nc,), ins, specs, outs, carry=dict(shapes=_M2_CARRY, axis=0))
    return y, saved


def m2_bwd(name, proj, small, ws, saved, gy, gy_col, gproj):
    s = proj.shape[0]
    nc = s // CHUNK
    ins, specs, cm = _m2_io(proj, small, ws, nc, True)
    n = CHUNK
    wants = [_want((0, 1, 2, 3), (s, PW), pl.BlockSpec((n, 3 * BW), lambda c: (cm(c), 2))),
             _want(4, (s, 16), pl.BlockSpec((n, 16), lambda c: (cm(c), 0)))]
    wants += [_want(5 + i, a.shape, _full(a), acc=(0,)) for i, a in enumerate(ws)]
    return run_bwd(name, _m2_tile, (nc,), ins, specs, [gy], [pl.BlockSpec((n, BW), lambda c: (cm(c), gy_col))], wants,
                   carry=dict(axis=0, saved=saved, rev=lambda g: (cm(g[0]),)), into=gproj)


def _l2n(x):
    return x * lax.rsqrt(jnp.sum(x * x, axis=-1, keepdims=True) + EPS)


def _gdn_tile(carry, q_raw, k_raw, v_raw, gate, small, cwq, cwk, cwv, dtb, alog, ng):
    state, pq, pk, pv = carry
    n, nh = CHUNK, 4
    nn_ = n * nh
    qc = _silu(_conv4(q_raw, pq, cwq, None))
    kc = _silu(_conv4(k_raw, pk, cwk, None))
    vc = _silu(_conv4(v_raw, pv, cwv, None))
    beta16 = jax.nn.sigmoid(small)
    g16 = -jnp.exp(alog) * _softplus(small + dtb)
    tri = _causal(n).astype(f32)
    triu = (_rows((n, n)) <= _lanes((n, n))).astype(f32)
    cs16 = lmm(tri, g16)
    cs16_t = rmm_tn(g16, triu)
    lanes_of = lambda h: slice(128 * h, 128 * (h + 1))
    rows_of = lambda h: slice(n * h, n * (h + 1))
    stack = lambda f: jnp.concatenate([f(h) for h in range(nh)], axis=0)
    q = stack(lambda h: _l2n(qc[:, lanes_of(h)]) * (128 ** -0.5))
    k = stack(lambda h: _l2n(kc[:, lanes_of(h)]))
    v = stack(lambda h: vc[:, lanes_of(h)])
    beta = stack(lambda h: _pick_lane(beta16, 8 + h))
    col = stack(lambda h: _pick_lane(cs16, 12 + h))
    last_h = [_pick_row(_pick_lane(cs16, 12 + h), n - 1) for h in range(nh)]
    last = stack(lambda h: jnp.broadcast_to(last_h[h], (n, 1)))
    last_w = jnp.concatenate([jnp.broadcast_to(last_h[h], (1, 128)) for h in range(nh)], axis=1)
    spread = (_rows((n, nn_)) == _lanes((n, nn_)) % n).astype(f32)
    cs_w = rmm(cs16_t, spread)
    rowv = jnp.sum(jnp.where(_rows((16, nn_)) == 12 + _lanes((16, nn_)) // n, cs_w, 0.0), axis=0, keepdims=True)
    same = (_rows((nn_, nn_)) // n) == (_lanes((nn_, nn_)) // n)
    causal = jnp.logical_and(same, _rows((nn_, nn_)) >= _lanes((nn_, nn_)))
    strict = jnp.logical_and(same, _rows((nn_, nn_)) > _lanes((nn_, nn_)))
    decay = jnp.where(causal, jnp.exp(jnp.where(causal, col - rowv, 0.0)), 0.0)
    kb = k * beta
    t = _tri_inv(jnp.where(strict, mm_nt(kb, k) * decay, 0.0))
    e_col = jnp.exp(col)
    uw = xmm(t, jnp.concatenate([v * beta, kb * e_col], axis=1))
    u, w = uw[:, :128], uw[:, 128:]
    qk = mm_nt(q, k) * decay
    own = lambda r: stack(lambda h: r[rows_of(h), lanes_of(h)])
    v_new = u - own(mm(w, state))
    o = own(mm(q * e_col, state)) + mm(qk, v_new)
    zero = jnp.zeros((n, 128), f32)
    v_blocks = stack(lambda h: jnp.concatenate([v_new[rows_of(h), :] if j == h else zero for j in range(nh)], axis=1))
    state_out = state * jnp.exp(last_w) + mm_tn(k * jnp.exp(last - col), v_blocks)
    gt = stack(lambda h: gate[:, lanes_of(h)])
    out = _rms(o, ng) * _silu(gt)
    out = jnp.concatenate([out[rows_of(h), :] for h in range(nh)], axis=1)
    return (state_out, q_raw[n - 8:, :], k_raw[n - 8:, :], v_raw[n - 8:, :]), (out,)


def _gdn_io(proj, small, cw, dtb, alog, ng, nc, rev):
    cm = _rev(nc, rev)
    n = CHUNK
    ins = [proj, proj, proj, proj, small, cw, cw, cw, dtb, alog, ng]
    specs = [pl.BlockSpec((n, BW), lambda c, _j=j: (cm(c), _j)) for j in (0, 1, 2, 3)]
    specs += [pl.BlockSpec((n, 16), lambda c: (cm(c), 0))]
    specs += [pl.BlockSpec((4, BW), lambda c, _j=j: (0, _j)) for j in (0, 1, 2)]
    specs += [_full(dtb), _full(alog), _full(ng)]
    return ins, specs, cm


_GDN_CARRY = [(128, BW)] + [(8, BW)] * 3


def gdn_fwd(name, proj, small, cw, dtb, alog, ng):
    s = proj.shape[0]
    nc = s // CHUNK
    ins, specs, cm = _gdn_io(proj, small, cw, dtb, alog, ng, nc, False)
    outs = [_out((s, BW), pl.BlockSpec((CHUNK, BW), lambda c: (c, 0)))]
    (y,), saved = run_fwd(name, _gdn_tile, (nc,), ins, specs, outs, carry=dict(shapes=_GDN_CARRY, axis=0))
    return y, saved


def gdn_bwd(name, proj, small, cw, dtb, alog, ng, saved, gy, gy_col, ride=None):
    s = proj.shape[0]
    nc = s // CHUNK
    n = CHUNK
    ins, specs, cm = _gdn_io(proj, small, cw, dtb, alog, ng, nc, True)
    wants = [_want((0, 1, 2, 3), (s, PW), pl.BlockSpec((n, 4 * BW), lambda c: (cm(c), 0))),
             _want(4, (s, 16), pl.BlockSpec((n, 16), lambda c: (cm(c), 0)))]
    wants += [_want(5 + i, (4, BW), pl.BlockSpec((4, BW), lambda c: (0, 0)), acc=(0,)) for i in range(3)]
    wants += [_want(8, dtb.shape, _full(dtb), acc=(0,)), _want(9, alog.shape, _full(alog), acc=(0,)),
              _want(10, ng.shape, _full(ng), acc=(0,))]
    return run_bwd(name, _gdn_tile, (nc,), ins, specs, [gy], [pl.BlockSpec((n, BW), lambda c: (cm(c), gy_col))], wants,
                   carry=dict(axis=0, saved=saved, rev=lambda g: (cm(g[0]),)), ride=ride)


def _perm_cols(w):
    pad = jnp.zeros(w.shape[:-1] + (PW - IN_WIDTH,), w.dtype)
    return jnp.concatenate([w[..., 3080:5128], w[..., 512:3072], w[..., :512], w[..., 3072:3080], w[..., 5128:5136], pad], axis=-1)


def _unperm_cols(g):
    return jnp.concatenate([g[..., 4608:5120], g[..., 2048:4608], g[..., 5120:5128], g[..., :2048], g[..., 5128:5136]], axis=-1)


def _bd(blocks):
    n, a, b = blocks.shape
    eye = jnp.eye(n, dtype=blocks.dtype)
    return jnp.einsum("nab,nm->namb", blocks, eye).reshape(n * a, n * b)


def _layer_layout(lw):
    o = {}
    row = lambda a: a.reshape(1, -1)
    o["b_gate"] = row(lw["b_gate"])
    o["s5_are"] = lw["s5_a_re"].reshape(4, 1, 512)
    o["s5_aim"] = lw["s5_a_im"].reshape(4, 1, 512)
    o["s5_lstep"] = jnp.repeat(lw["s5_log_step"], 64).reshape(4, 1, 512)
    bt = lambda b: jax.vmap(_bd)(jnp.swapaxes(b, 1, 2).reshape(4, 8, 16, 64))
    o["s5_bre"], o["s5_bim"] = bt(lw["s5_b_re"]), bt(lw["s5_b_im"])
    ct = lambda c: jax.vmap(_bd)(jnp.swapaxes(c, 1, 2).reshape(4, 8, 64, 16))
    o["s5_cre"], o["s5_cim"] = ct(lw["s5_c_re"]), ct(lw["s5_c_im"])
    o["s5_d"] = row(lw["s5_d"])
    o["s5_b_glu"] = row(lw["s5_b_glu"])
    o["lru_conv_b"], o["lru_b_r"], o["lru_b_i"], o["lru_lambda"] = (row(lw[k]) for k in ("lru_conv_b", "lru_b_r", "lru_b_i", "lru_lambda"))
    o["lru_wr"], o["lru_wi"] = _bd(lw["lru_w_r"]), _bd(lw["lru_w_i"])
    cw, cb = lw["m2_conv_w"], lw["m2_conv_b"]
    o["m2_cwx"], o["m2_cwb"], o["m2_cwc"] = cw[:, :512], cw[:, 512:768], cw[:, 768:]
    o["m2_cbx"], o["m2_cbb"], o["m2_cbc"] = row(cb[:512]), row(cb[512:768]), row(cb[768:])
    o["m2_dtb"] = jnp.pad(lw["m2_dt_bias"], (0, 8)).reshape(1, 16)
    o["m2_alog"] = jnp.pad(lw["m2_a_log"], (0, 8)).reshape(1, 16)
    o["m2_dsk"] = jnp.repeat(lw["m2_d"], 64).reshape(1, 512)
    o["m2_norm"] = row(lw["m2_norm"])
    o["gdn_dtb"] = jnp.pad(lw["gdn_dt_bias"], (12, 0)).reshape(1, 16)
    o["gdn_alog"] = jnp.pad(lw["gdn_a_log"], (12, 0)).reshape(1, 16)
    o["gdn_norm"] = row(lw["gdn_norm"])
    for k in ("ffn1_norm", "mix_norm", "ffn2_norm", "ple_norm"):
        o[k] = row(lw[k])
    return o


_BIG_PLAIN = ("w_gate", "s5_w_glu", "w_branch", "w_out", "ple_w_gate", "ple_w_proj")


def big_layout(w):
    o = {k: w[k] for k in _BIG_PLAIN}
    for f in ("ffn1", "ffn2"):
        o[f + "_wg"], o[f + "_wu"] = w[f + "_w_in"][:, :FFN_DIM], w[f + "_w_in"][:, FFN_DIM:]
        o[f + "_wo"] = w[f + "_w_out"]
    o["w_in"] = _perm_cols(w["w_in"])
    return {k: v.astype(bf16) for k, v in o.items()}


def big_unlayout(g):
    o = {k: g[k] for k in _BIG_PLAIN}
    for f in ("ffn1", "ffn2"):
        o[f + "_w_in"] = jnp.concatenate([g[f + "_wg"], g[f + "_wu"]], axis=1)
        o[f + "_w_out"] = g[f + "_wo"]
    o["w_in"] = _unperm_cols(g["w_in"])
    return o


_SMALL_KEYS = ("b_gate", "s5_log_step", "s5_a_re", "s5_a_im", "s5_b_re", "s5_b_im", "s5_c_re", "s5_c_im", "s5_d",
               "s5_b_glu", "lru_conv_b", "lru_w_r", "lru_b_r", "lru_w_i", "lru_b_i", "lru_lambda", "m2_conv_w",
               "m2_conv_b", "m2_dt_bias", "m2_a_log", "m2_d", "m2_norm", "gdn_dt_bias", "gdn_a_log", "gdn_norm",
               "ffn1_norm", "mix_norm", "ffn2_norm", "ple_norm")


def _layer_fwd(i, h0, pe, lw, big):
    n = f"l{i}_"
    lay = _layer_layout(lw)
    a = {"h0": h0, "lay": lay}
    h1, n1 = ffn_fwd(n + "ffn1_fwd", h0, lay["ffn1_norm"], big["ffn1_wg"], big["ffn1_wu"], big["ffn1_wo"])
    zero_b = jnp.zeros((1, PW), f32)
    proj, u = normlin_fwd(n + "inproj_fwd", h1, lay["mix_norm"], big["w_in"], zero_b, tn=896)
    gpre = lin_fwd(n + "gate_fwd", u, big["w_gate"], lay["b_gate"], tn=1024)
    small = proj[:, SMALL_OFF:SMALL_OFF + 16]
    raw = [lay["s5_are"], lay["s5_aim"], lay["s5_lstep"], lay["s5_bre"], lay["s5_bim"]]
    ops = s5_ops_fwd(n + "s5ops_fwd", raw)
    y5, sv5 = s5_fwd(n + "s5_fwd", proj, ops, lay["s5_cre"], lay["s5_cim"], lay["s5_d"])
    ya = s5_glu_fwd(n + "s5glu_fwd", y5, big["s5_w_glu"], lay["s5_b_glu"])
    lru_ws = [lw["lru_conv_w"], lay["lru_conv_b"], lay["lru_wr"], lay["lru_b_r"], lay["lru_wi"], lay["lru_b_i"], lay["lru_lambda"]]
    yb, svb = lru_fwd(n + "lru_fwd", proj, lru_ws)
    m2_ws = [lay[k] for k in ("m2_cwx", "m2_cwb", "m2_cwc", "m2_cbx", "m2_cbb", "m2_cbc", "m2_dtb", "m2_alog", "m2_dsk", "m2_norm")]
    yc, svc = m2_fwd(n + "m2_fwd", proj, small, m2_ws)
    yd, svd = gdn_fwd(n + "gdn_fwd", proj, small, lw["gdn_conv_w"], lay["gdn_dtb"], lay["gdn_alog"], lay["gdn_norm"])
    ys = jnp.concatenate([ya, yb, yc, yd], axis=1)
    h2 = merge_fwd(n + "merge_fwd", h1, ys, gpre, big["w_branch"], big["w_out"])
    h3, n2 = ffn_fwd(n + "ffn2_fwd", h2, lay["ffn2_norm"], big["ffn2_wg"], big["ffn2_wu"], big["ffn2_wo"])
    h4 = ple_fwd(n + "ple_fwd", h3, pe, lay["ple_norm"], big["ple_w_gate"], big["ple_w_proj"])
    a.update(h1=h1, n1=n1, u=u, n2=n2, proj=proj, gpre=gpre, small=small, raw=raw, ops=ops, y5=y5, sv5=sv5, lru_ws=lru_ws, svb=svb,
             m2_ws=m2_ws, svc=svc, svd=svd, ys=ys, h2=h2, h3=h3, pe=pe)
    return h4, a


def small_into(name, a, b, gproj, t=512):
    s, w = a.shape
    pad_w = PW - SMALL_OFF

    def body(a_ref, b_ref, _, o_ref):
        place = (_rows((w, pad_w)) == _lanes((w, pad_w))).astype(f32)
        o_ref[...] = _dg(a_ref[...] + b_ref[...], place, _NN, (3, 1))

    row = pl.BlockSpec((t, w), lambda i: (i, 0))
    return pl.pallas_call(
        body, name=name, grid=(s // t,), in_specs=[row, row, _ANY],
        out_specs=pl.BlockSpec((t, pad_w), lambda i: (i, SMALL_OFF // pad_w)),
        out_shape=jax.ShapeDtypeStruct(gproj.shape, gproj.dtype), input_output_aliases={2: 0}, compiler_params=_params((1,)),
    )(a, b, gproj)


def _layer_bwd(i, a, lw, big, gh4, ride=None):
    n = f"l{i}_"
    lay = a["lay"]
    gb, gl = {}, {}
    s = gh4.shape[0]
    t = 512
    gh3, gl["ple_norm"], gb["ple_w_gate"], gb["ple_w_proj"] = ple_bwd(
        n + "ple_bwd", a["h3"], a["pe"], lay["ple_norm"], big["ple_w_gate"], big["ple_w_proj"], gh4)
    dn2, gb["ffn2_wg"], gb["ffn2_wu"], gb["ffn2_wo"] = ffn_bwd(
        n + "ffn2_bwd", a["n2"], big["ffn2_wg"], big["ffn2_wu"], big["ffn2_wo"], gh3)
    gh2, gl["ffn2_norm"] = norm_bwd(n + "gh2", a["h2"], lay["ffn2_norm"], gh3, _parts(dn2))
    gys, ggpre, gb["w_branch"], gb["w_out"] = merge_bwd(
        n + "merge_bwd", a["h1"], a["ys"], a["gpre"], big["w_branch"], big["w_out"], gh2)
    rd = gdn_bwd(n + "gdn_bwd", a["proj"], a["small"], lw["gdn_conv_w"], lay["gdn_dtb"], lay["gdn_alog"], lay["gdn_norm"], a["svd"], gys, 3, ride)
    gproj, dsm_d = rd[:2]
    gl["gdn_conv_w"] = jnp.concatenate(rd[2:5], axis=1)
    gl["gdn_dtb"], gl["gdn_alog"], gl["gdn_norm"] = rd[5:8]
    rode = rd[8:]
    rb = lru_bwd(n + "lru_bwd", a["proj"], a["lru_ws"], a["svb"], gys, 1, gproj)
    gproj = rb[0]
    gl["lru_conv_w"], gl["lru_conv_b"], gl["lru_wr"], gl["lru_b_r"], gl["lru_wi"], gl["lru_b_i"], gl["lru_lambda"] = rb[1:]
    rc = m2_bwd(n + "m2_bwd", a["proj"], a["small"], a["m2_ws"], a["svc"], gys, 2, gproj)
    gproj, dsm_c = rc[:2]
    for k, v in zip(("m2_cwx", "m2_cwb", "m2_cwc", "m2_cbx", "m2_cbb", "m2_cbc", "m2_dtb", "m2_alog", "m2_dsk", "m2_norm"), rc[2:]):
        gl[k] = v
    gy5, gb["s5_w_glu"], gl["s5_b_glu"] = s5_glu_bwd(n + "s5glu_bwd", a["y5"], big["s5_w_glu"], lay["s5_b_glu"], gys, 0)
    r5 = s5_bwd(n + "s5_bwd", a["proj"], a["ops"], lay["s5_cre"], lay["s5_cim"], lay["s5_d"], a["sv5"], gy5, gproj)
    gproj, gops, gl["s5_cre"], gl["s5_cim"], gl["s5_d"] = r5[0], r5[1:7], r5[7], r5[8], r5[9]
    gl["s5_are"], gl["s5_aim"], gl["s5_lstep"], gl["s5_bre"], gl["s5_bim"] = s5_ops_bwd(n + "s5ops_bwd", a["raw"], gops)
    gproj = small_into(n + "gsmall", dsm_c, dsm_d, gproj)
    zero_b = jnp.zeros((1, PW), f32)
    du_p, gb["w_in"], _ = lin_bwd(n + "inproj_bwd", a["u"], big["w_in"], zero_b, gproj, tn=896)
    du_g, gb["w_gate"], gl["b_gate"] = lin_bwd(n + "gate_bwd", a["u"], big["w_gate"], lay["b_gate"], ggpre, tn=1024)
    gh1, gl["mix_norm"] = norm_bwd(n + "gh1", a["h1"], lay["mix_norm"], gh2, _parts(du_p) + _parts(du_g))
    dn1, gb["ffn1_wg"], gb["ffn1_wu"], gb["ffn1_wo"] = ffn_bwd(
        n + "ffn1_bwd", a["n1"], big["ffn1_wg"], big["ffn1_wu"], big["ffn1_wo"], gh1)
    gh0, gl["ffn1_norm"] = norm_bwd(n + "gh0", a["h0"], lay["ffn1_norm"], gh1, _parts(dn1))
    return gh0, gb, gl, rode


def _local_step(x, p, target, bigs, smalls, final_norm, ride_of=None):
    h = x
    acts = []
    for i in range(DEPTH):
        h, a = _layer_fwd(i, h, p[i], smalls[i], bigs[i])
        acts.append(a)
    fg = final_norm.reshape(1, -1)
    loss, gh, gfn = final_loss("final_loss", h, fg, target)
    gbs, gss, rode = [None] * DEPTH, [None] * DEPTH, [None] * DEPTH
    for i in reversed(range(DEPTH)):
        ride = ride_of(gbs[i + 1], gss[i + 1]) if ride_of is not None and i + 1 < DEPTH else None
        gh, gb, gl, got = _layer_bwd(i, acts[i], smalls[i], bigs[i], gh, ride)
        if ride is not None:
            rode[i + 1] = got
        _, pull = jax.vjp(_layer_layout, smalls[i])
        lay_g = {k: gl[k] for k in acts[i]["lay"]}
        gs = pull(lay_g)[0]
        gs = dict(gs)
        gs["lru_conv_w"] = gl["lru_conv_w"]
        gs["gdn_conv_w"] = gl["gdn_conv_w"]
        gbs[i], gss[i] = gb, gs
    return loss, gh, gbs, gss, gfn.reshape(-1), rode


MESH_AXES = ("x", "y", "c")
PACK_W = 1024
PACK_ROWS = 256

W_NAMES = ("ffn1_norm", "ffn1_w_in", "ffn1_w_out", "mix_norm", "w_in", "w_gate", "b_gate", "s5_log_step", "s5_a_re",
           "s5_a_im", "s5_b_re", "s5_b_im", "s5_c_re", "s5_c_im", "s5_d", "s5_w_glu", "s5_b_glu", "lru_conv_w",
           "lru_conv_b", "lru_w_r", "lru_b_r", "lru_w_i", "lru_b_i", "lru_lambda", "m2_conv_w", "m2_conv_b",
           "m2_dt_bias", "m2_a_log", "m2_d", "m2_norm", "gdn_conv_w", "gdn_dt_bias", "gdn_a_log", "gdn_norm",
           "w_branch", "w_out", "ffn2_norm", "ffn2_w_in", "ffn2_w_out", "ple_norm", "ple_w_gate", "ple_w_proj",
           "final_norm")
COL_SHARDED = ("ffn1_w_in", "w_in", "w_gate", "lru_conv_w", "m2_conv_w", "gdn_conv_w", "w_branch", "ffn2_w_in", "ple_w_proj")
ROW_SHARDED = ("ffn1_w_out", "s5_w_glu", "w_out", "ffn2_w_out", "ple_w_gate")
BIG_NAMES = ("ffn1_w_in", "ffn1_w_out", "w_in", "w_gate", "s5_w_glu", "w_branch", "w_out", "ffn2_w_in", "ffn2_w_out",
             "ple_w_gate", "ple_w_proj")
CONV_NAMES = ("lru_conv_w", "m2_conv_w", "gdn_conv_w")
SHARDED = BIG_NAMES + CONV_NAMES
REPLICATED = tuple(k for k in W_NAMES if k not in SHARDED)


def _gathered_to_full(name, g):
    if name in COL_SHARDED:
        g = jnp.moveaxis(g, 0, -2)
        return g.reshape(g.shape[:-2] + (g.shape[-2] * g.shape[-1],))
    g = jnp.moveaxis(g, 0, 1)
    return g.reshape((g.shape[0], g.shape[1] * g.shape[2]) + g.shape[3:])


def _full_to_scattered(name, w):
    if name in COL_SHARDED:
        w = w.reshape(w.shape[:-1] + (N_DEV, w.shape[-1] // N_DEV))
        return jnp.moveaxis(w, -2, 0)
    w = w.reshape((w.shape[0], N_DEV, w.shape[1] // N_DEV) + w.shape[2:])
    return jnp.moveaxis(w, 1, 0)


def _pack(arrs):
    pieces = []
    for a in arrs:
        k = -(-a.size // PACK_W)
        pieces.append(jnp.pad(a.reshape(-1), (0, k * PACK_W - a.size)).reshape(k, PACK_W))
    buf = jnp.concatenate(pieces, axis=0)
    return jnp.pad(buf, ((0, -buf.shape[0] % PACK_ROWS), (0, 0)))


def _unpack(buf, shapes):
    out, r = [], 0
    for sh in shapes:
        n = math.prod(sh)
        k = -(-n // PACK_W)
        out.append(buf[r:r + k].reshape(-1)[:n].reshape(sh))
        r += k
    return out


def _peer(k):
    mx, my, mc = (lax.axis_index(a) for a in MESH_AXES)
    px = 1 - mx if k & 4 else mx
    py = 1 - my if k & 2 else my
    pc = 1 - mc if k & 1 else mc
    return (px, py, pc), 4 * px + 2 * py + pc


def all_gather(name, xs):
    n = len(xs)

    def body(*refs):
        x_refs, out_refs = refs[:n], refs[n:2 * n]
        send_sems, recv_sems, local_sems = refs[2 * n:]
        mx, my, mc = (lax.axis_index(a) for a in MESH_AXES)
        me, sibling = (mx, my, mc), (mx, my, 1 - mc)
        chips = [(1 - mx, my), (mx, 1 - my), (1 - mx, 1 - my)]

        def slot(i, px, py, pc):
            return out_refs[i].at[4 * px + 2 * py + pc]

        def copy(k, i, block, to, src=None):
            return pltpu.make_async_remote_copy(
                src_ref=slot(i, *block) if src is None else src, dst_ref=slot(i, *block),
                send_sem=send_sems.at[k, i], recv_sem=recv_sems.at[k, i], device_id=to, device_id_type=pl.DeviceIdType.MESH)

        mine = [pltpu.make_async_copy(x_refs[i], slot(i, *me), local_sems.at[i]) for i in range(n)]
        first = []
        for i in range(n):
            mine[i].start()
            first.append(copy(0, i, me, sibling, src=x_refs[i]))
            first += [copy(1 + j, i, me, (*chip, mc), src=x_refs[i]) for j, chip in enumerate(chips)]
        for cp in first:
            cp.start()
        passed = []
        for i in range(n):
            for j, chip in enumerate(chips):
                copy(1 + j, i, (*chip, mc), me).wait_recv()
                cp = copy(4 + j, i, (*chip, mc), sibling)
                cp.start()
                passed.append(cp)
        for i in range(n):
            copy(0, i, sibling, me).wait_recv()
            for j, chip in enumerate(chips):
                copy(4 + j, i, (*chip, 1 - mc), me).wait_recv()
        for cp in first + passed:
            cp.wait_send()
        for cp in mine:
            cp.wait()

    res = pl.pallas_call(
        body, name=name, out_shape=[jax.ShapeDtypeStruct((N_DEV,) + x.shape, x.dtype) for x in xs],
        in_specs=[_ANY] * n, out_specs=[_ANY] * n,
        scratch_shapes=[pltpu.SemaphoreType.DMA((7, n)), pltpu.SemaphoreType.DMA((7, n)), pltpu.SemaphoreType.DMA((n,))],
    )(*xs)
    return list(res)


def _exchange_copies(g_refs, out_refs, send_sems, recv_sems, local_sems, with_incoming=True):
    mx, my, mc = (lax.axis_index(a) for a in MESH_AXES)
    me = 4 * mx + 2 * my + mc
    n = len(g_refs)
    local = [pltpu.make_async_copy(g_refs[i].at[me], out_refs[i].at[me], local_sems.at[i]) for i in range(n)]
    outgoing, incoming = [], []
    for k in range(1, N_DEV):
        peer, pidx = _peer(k)
        for i in range(n):
            sems = dict(send_sem=send_sems.at[k - 1, i], recv_sem=recv_sems.at[k - 1, i], device_id=peer,
                        device_id_type=pl.DeviceIdType.MESH)
            outgoing.append(pltpu.make_async_remote_copy(src_ref=g_refs[i].at[pidx], dst_ref=out_refs[i].at[me], **sems))
            if with_incoming:
                incoming.append(pltpu.make_async_remote_copy(src_ref=g_refs[i].at[pidx], dst_ref=out_refs[i].at[pidx], **sems))
    return local, outgoing, incoming


def _exchange_start(*refs):
    local, outgoing, _ = _exchange_copies(*refs, with_incoming=False)
    for cp in local + outgoing:
        cp.start()


def _exchange_wait(*refs):
    local, outgoing, incoming = _exchange_copies(*refs)
    for cp in incoming:
        cp.wait_recv()
    for cp in outgoing:
        cp.wait_send()
    for cp in local:
        cp.wait()


def _exchange_sems(n):
    return [pltpu.SemaphoreType.DMA((7, n)), pltpu.SemaphoreType.DMA((7, n)), pltpu.SemaphoreType.DMA((n,))]


def exchange(name, gs):
    n = len(gs)

    def body(*refs):
        args = (refs[:n], refs[n:2 * n]) + tuple(refs[2 * n:])
        _exchange_start(*args)
        _exchange_wait(*args)

    res = pl.pallas_call(
        body, name=name, out_shape=[jax.ShapeDtypeStruct(g.shape, g.dtype) for g in gs], in_specs=[_ANY] * n, out_specs=[_ANY] * n,
        scratch_shapes=_exchange_sems(n),
    )(*gs)
    return list(res)


def sum_slots(name, buf):
    return addn(name, _parts(buf), PACK_ROWS)


def _adamw_math(ww, gg, mm_, vv):
    m2 = ADAM_B1 * mm_ + (1.0 - ADAM_B1) * gg
    v2 = ADAM_B2 * vv + (1.0 - ADAM_B2) * (gg * gg)
    m_hat = m2 / (1.0 - ADAM_B1 ** ADAM_STEP)
    v_hat = v2 / (1.0 - ADAM_B2 ** ADAM_STEP)
    delta = -ADAM_LR * (m_hat / (jnp.sqrt(v_hat) + ADAM_EPS) + ADAM_WD * ww)
    return delta, m2, v2


def adamw(name, w, g, m, v):
    spec = pl.BlockSpec((PACK_ROWS, PACK_W), lambda i: (i, 0))
    return run_fwd(name, _adamw_math, (w.shape[0] // PACK_ROWS,), [w, g, m, v], [spec] * 4, [_out(w.shape, spec)] * 3)[0]


def reduce_adamw(name, slots, w, m, v):
    shape = w.shape
    r, c = shape[-2:]
    a = math.prod(shape[:-2])
    per = a // DEPTH
    tr = r
    while tr * c * 4 > (1 << 20) and tr % 16 == 0:
        tr //= 2
    s3 = [s.reshape((N_DEV, per, r, c)) for s in slots]
    w3, m3, v3 = (t.reshape((a, r, c)) for t in (w, m, v))

    def fn(*t):
        layer = pl.program_id(0) // per
        g = None
        for l in range(DEPTH):
            gl = t[l * N_DEV].astype(f32)
            for d in range(1, N_DEV):
                gl = gl + t[l * N_DEV + d].astype(f32)
            g = gl if g is None else jnp.where(layer == l, gl, g)
        k = DEPTH * N_DEV
        return (g,) + _adamw_math(t[k], g, t[k + 1], t[k + 2])

    specs, ins = [], []
    for l in range(DEPTH):
        for d in range(N_DEV):
            specs.append(pl.BlockSpec((None, None, tr, c), lambda i, j, _d=d, _l=l: (_d, jnp.clip(i - _l * per, 0, per - 1), j, 0)))
            ins.append(s3[l])
    spec = pl.BlockSpec((None, tr, c), lambda i, j: (i, j, 0))
    res = run_fwd(name, fn, (a, r // tr), ins + [w3, m3, v3], specs + [spec] * 3, [_out((a, r, c), spec)] * 4)[0]
    return [t.reshape(shape) for t in res]


def kernel(*args):
    nw = len(W_NAMES)
    x, p = args[0], args[1]
    w = dict(zip(W_NAMES, args[2:2 + nw]))
    target = args[2 + nw]
    m = dict(zip(W_NAMES, args[3 + nw:3 + 2 * nw]))
    v = dict(zip(W_NAMES, args[3 + 2 * nw:3 + 3 * nw]))

    gathered = all_gather("ag_weights", [w[k].astype(bf16) for k in BIG_NAMES] + [w[k] for k in CONV_NAMES])
    full = {k: _gathered_to_full(k, g) for k, g in zip(SHARDED, gathered)}

    bigs = [big_layout({k: full[k][i] for k in BIG_NAMES}) for i in range(DEPTH)]
    smalls = []
    for i in range(DEPTH):
        sm = {k: w[k][i] for k in _SMALL_KEYS if k not in CONV_NAMES}
        sm.update({k: full[k][i] for k in CONV_NAMES})
        smalls.append(sm)
    def scattered(gb, gs):
        gf = dict(big_unlayout(gb), **gs)
        return [_full_to_scattered(k, gf[k][None]).astype(bf16) for k in SHARDED]

    loss, gx, gbs, gss, gfn, rode = _local_step(x[0], p[:, 0], target[0], bigs, smalls, w["final_norm"], scattered)
    loss = lax.psum(loss[0, 0], MESH_AXES)

    rode[0] = exchange("rs_layer0", scattered(gbs[0], gss[0]))
    gfull = [dict(big_unlayout(gbs[i]), **gss[i]) for i in range(DEPTH)]
    stack = lambda k: jnp.stack([gfull[i][k] for i in range(DEPTH)])
    outs = {}
    kinds = ("grad", "delta", "new_m", "new_v")
    for j, k in enumerate(SHARDED):
        for kind, a in zip(kinds, reduce_adamw("adamw_" + k, [rode[i][j] for i in range(DEPTH)], w[k], m[k], v[k])):
            outs[kind + "_" + k] = a

    shapes = [w[k].shape for k in REPLICATED]
    g_rep = sum_slots("sum_replicated", all_gather("ag_replicated", [_pack([gfn if k == "final_norm" else stack(k) for k in REPLICATED])])[0])
    res = adamw("adamw_replicated", _pack([w[k] for k in REPLICATED]), g_rep, _pack([m[k] for k in REPLICATED]), _pack([v[k] for k in REPLICATED]))
    for kind, buf in zip(kinds, [g_rep] + list(res)):
        for k, a in zip(REPLICATED, _unpack(buf, shapes)):
            outs[kind + "_" + k] = a
    return (loss, gx[None]) + tuple(outs[kind + "_" + k] for kind in kinds for k in W_NAMES)
```

```python
import functools
import math

import jax
import jax.numpy as jnp
from jax import lax
from jax.experimental import pallas as pl
from jax.experimental.pallas import tpu as pltpu

f32 = jnp.float32
bf16 = jnp.bfloat16

EPS = 1e-6
DEPTH = 2
D_MODEL = 1024
FFN_DIM = 2816
BW = 512
IN_WIDTH = 5136
PW = 5376
SMALL_OFF = 5120
CHUNK = 64
LS = 256
LRU_C = 8.0
N_DEV = 8
VMEM_LIMIT_BYTES = 56 * 1024 * 1024

ADAM_LR, ADAM_B1, ADAM_B2, ADAM_EPS, ADAM_WD, ADAM_STEP = 0.001, 0.9, 0.999, 1e-08, 0.01, 10


_NN, _NT, _TN = ((1,), (0,)), ((1,), (1,)), ((0,), (0,))


def _pieces(x, n):
    parts, r = [], x
    for i in range(n):
        p = r.astype(bf16)
        parts.append(p)
        if i + 1 < n:
            r = r - p.astype(f32)
    return parts


def _dg(a, b, dims, mode):
    sa, sb = mode
    pa, pb = _pieces(a, sa), _pieces(b, sb)
    out = None
    for i in reversed(range(sa)):
        for j in reversed(range(sb)):
            if i + j < max(sa, sb):
                d = lax.dot_general(pa[i], pb[j], (dims, ((), ())), preferred_element_type=f32)
                out = d if out is None else out + d
    return out


def _make_mm(mode):
    sa, sb = mode
    cot = lambda s_other: 1 if mode == (1, 1) else (3 if s_other == 1 else 2)
    m_g_b, m_a_g, m_g_a, m_b_g = (cot(sb), sb), (sa, cot(sa)), (cot(sa), sa), (sb, cot(sb))

    @jax.custom_vjp
    def nn(a, b):
        return _dg(a, b, _NN, mode)

    @jax.custom_vjp
    def nt(a, b):
        return _dg(a, b, _NT, mode)

    @jax.custom_vjp
    def tn(a, b):
        return _dg(a, b, _TN, mode)

    nn.defvjp(lambda a, b: (nn(a, b), (a, b)), lambda r, g: (_dg(g, r[1], _NT, m_g_b), _dg(r[0], g, _TN, m_a_g)))
    nt.defvjp(lambda a, b: (nt(a, b), (a, b)), lambda r, g: (_dg(g, r[1], _NN, m_g_b), _dg(g, r[0], _TN, m_g_a)))
    tn.defvjp(lambda a, b: (tn(a, b), (a, b)), lambda r, g: (_dg(r[1], g, _NT, m_b_g), _dg(r[0], g, _NN, m_a_g)))
    return nn, nt, tn


mm, mm_nt, mm_tn = _make_mm((1, 1))
xmm, xmm_nt, xmm_tn = _make_mm((2, 2))
lmm, lmm_nt, lmm_tn = _make_mm((1, 3))
rmm, rmm_nt, rmm_tn = _make_mm((3, 1))


@jax.custom_vjp
def _tri_inv(m):
    n = m.shape[0]
    eye = (_rows((n, n)) == _lanes((n, n))).astype(f32)
    blk = (_rows((n, n)) // 16) == (_lanes((n, n)) // 16)
    x = lambda a, b: _dg(a, b, _NN, (2, 2))
    nb = jnp.where(blk, m, 0.0)
    p = -nb
    t = eye + p
    for _ in range(3):
        p = x(p, p)
        t = t + x(t, p)
    q = x(t, m - nb)
    imq = eye - q
    return x(imq + x(imq, x(q, q)), t)


def _tri_inv_fwd(m):
    t = _tri_inv(m)
    return t, t


def _tri_inv_bwd(t, g):
    return (-_dg(_dg(t, g, _TN, (2, 2)), t, _NT, (2, 2)),)


_tri_inv.defvjp(_tri_inv_fwd, _tri_inv_bwd)


def _rows(shape):
    return lax.broadcasted_iota(jnp.int32, shape, 0)


def _lanes(shape):
    return lax.broadcasted_iota(jnp.int32, shape, 1)


def _rms(x, g):
    return x * lax.rsqrt(jnp.mean(x * x, axis=-1, keepdims=True) + EPS) * g


def _silu(x):
    return x * jax.nn.sigmoid(x)


def _gelu(x):
    return 0.5 * x * (1.0 + jnp.tanh(0.7978845608028654 * (x + 0.044715 * x * x * x)))


def _softplus(x):
    return jnp.maximum(x, 0.0) + jnp.log1p(jnp.exp(-jnp.abs(x)))


def _expm1(x):
    p = x * (1.0 + x * (0.5 + x * (1.0 / 6 + x * (1.0 / 24 + x * (1.0 / 120 + x * (1.0 / 720 + x * (1.0 / 5040)))))))
    return jnp.where(x > -0.3, p, jnp.exp(x) - 1.0)


def _pick_row(x, r):
    return jnp.sum(jnp.where(_rows(x.shape) == r, x, 0.0), axis=0, keepdims=True)


def _pick_lane(x, c):
    return jnp.sum(jnp.where(_lanes(x.shape) == c, x, 0.0), axis=1, keepdims=True)


def _shift_up(g, j):
    n = g.shape[0]
    return jnp.where(_rows(g.shape) < n - j, pltpu.roll(g, n - j, 0), 0.0)


@functools.partial(jax.custom_vjp, nondiff_argnums=(1, 2))
def _shift(x, j, fill):
    return jnp.where(_rows(x.shape) >= j, pltpu.roll(x, j, 0), fill)


_shift.defvjp(lambda x, j, fill: (_shift(x, j, fill), None), lambda j, fill, _, g: (_shift_up(g, j),))


@functools.partial(jax.custom_vjp, nondiff_argnums=(2,))
def _shift_halo(x, prev8, j):
    xr = pltpu.roll(x, j, 0)
    pr = pltpu.roll(prev8, j, 0)
    top = jnp.where(_rows(pr.shape) < j, pr, xr[:8])
    return jnp.concatenate([top, xr[8:]], axis=0)


def _shift_halo_bwd(j, _, g):
    g8 = g[:8]
    dprev = jnp.where(_rows(g8.shape) >= 8 - j, pltpu.roll(g8, 8 - j, 0), 0.0)
    return _shift_up(g, j), dprev


_shift_halo.defvjp(lambda x, p, j: (_shift_halo(x, p, j), None), _shift_halo_bwd)


def _conv4(x, prev8, w, b):
    y = _pick_row(w, 3) * x
    for k in range(3):
        y = y + _pick_row(w, k) * _shift_halo(x, prev8, 3 - k)
    return y if b is None else y + b


def _cmul(ar, ai, br, bi):
    return ar * br - ai * bi, ar * bi + ai * br


_ANY = pl.BlockSpec(memory_space=pl.ANY)


def _params(grid):
    return pltpu.CompilerParams(dimension_semantics=("arbitrary",) * len(grid), vmem_limit_bytes=VMEM_LIMIT_BYTES)


def _first(axes):
    ok = pl.program_id(axes[0]) == 0
    for a in axes[1:]:
        ok = jnp.logical_and(ok, pl.program_id(a) == 0)
    return ok


def _store(ref, val, acc):
    val = val.astype(ref.dtype)
    if acc is None:
        ref[...] = val
        return
    first = _first(acc)

    @pl.when(first)
    def _():
        ref[...] = val

    @pl.when(jnp.logical_not(first))
    def _():
        ref[...] += val


def _full(a):
    nd = a.ndim
    return pl.BlockSpec(a.shape, lambda *g: (0,) * nd)


def _out(shape, spec, acc=None, dtype=f32):
    return dict(shape=tuple(shape), spec=spec, acc=acc, dtype=dtype)


def run_fwd(name, fn, grid, ins, in_specs, outs, carry=None):
    n_in, n_out = len(ins), len(outs)
    cshapes = carry["shapes"] if carry else []
    nc = len(cshapes)
    ng = len(grid)

    def body(*refs):
        in_refs = refs[:n_in]
        out_refs = refs[n_in:n_in + n_out]
        save_refs = refs[n_in + n_out:n_in + n_out + nc]
        c_refs = refs[n_in + n_out + nc:]
        vals = [r[...] for r in in_refs]
        if carry:
            @pl.when(pl.program_id(carry["axis"]) == 0)
            def _():
                for c in c_refs:
                    c[...] = jnp.zeros(c.shape, f32)
            cin = tuple(c[...] for c in c_refs)
            for s, v in zip(save_refs, cin):
                s[...] = v
            cout, res = fn(cin, *vals)
            for c, v in zip(c_refs, cout):
                c[...] = v
        else:
            res = fn(*vals)
        for o, r, d in zip(out_refs, res, outs):
            _store(o, r, d["acc"])

    out_shape = [jax.ShapeDtypeStruct(d["shape"], d["dtype"]) for d in outs]
    out_specs = [d["spec"] for d in outs]
    for cs in cshapes:
        out_shape.append(jax.ShapeDtypeStruct(tuple(grid) + tuple(cs), f32))
        out_specs.append(pl.BlockSpec((None,) * ng + tuple(cs), lambda *g, _n=len(cs): tuple(g) + (0,) * _n))
    res = pl.pallas_call(
        body, name=name, grid=grid, in_specs=list(in_specs), out_specs=out_specs, out_shape=out_shape,
        scratch_shapes=[pltpu.VMEM(tuple(cs), f32) for cs in cshapes], compiler_params=_params(grid),
    )(*ins)
    return list(res[:n_out]), list(res[n_out:])


def run_bwd(name, fn, grid, ins, in_specs, gouts, gout_specs, wants, carry=None, into=None, ride=None):
    n_in, n_g, n_w = len(ins), len(gouts), len(wants)
    n_a = 0 if into is None else 1
    ride = list(ride or [])
    n_r = len(ride)
    saved = carry["saved"] if carry else []
    nc = len(saved)
    ng = len(grid)

    def body(*refs):
        in_refs = refs[:n_in]
        g_refs = refs[n_in:n_in + n_g]
        s_refs = refs[n_in + n_g:n_in + n_g + nc]
        base = n_in + n_g + nc + n_a
        r_in = refs[base:base + n_r]
        w_refs = refs[base + n_r:base + n_r + n_w]
        r_out = refs[base + n_r + n_w:base + 2 * n_r + n_w]
        dc_refs = refs[base + 2 * n_r + n_w:base + 2 * n_r + n_w + nc]
        if n_r:
            steps = math.prod(grid)
            step = pl.program_id(0)
            for ax in range(1, ng):
                step = step * grid[ax] + pl.program_id(ax)
            r_args = (r_in, r_out) + tuple(refs[base + 2 * n_r + n_w + nc:])
            pl.when(step == 0)(lambda: _exchange_start(*r_args))
        vals = [r[...].astype(f32) for r in in_refs]
        gs = tuple(r[...].astype(f32) for r in g_refs)
        if carry:
            @pl.when(pl.program_id(carry["axis"]) == 0)
            def _():
                for c in dc_refs:
                    c[...] = jnp.zeros(c.shape, f32)
            cin = tuple(s[...] for s in s_refs)
            _, vjp = jax.vjp(fn, cin, *vals)
            grads = vjp((tuple(c[...] for c in dc_refs), gs))
            for c, v in zip(dc_refs, grads[0]):
                c[...] = v
            dvals = grads[1:]
        else:
            _, vjp = jax.vjp(fn, *vals)
            dvals = vjp(gs)
        for o, d in zip(w_refs, wants):
            idx = d["idx"]
            val = dvals[idx] if isinstance(idx, int) else jnp.concatenate([dvals[j] for j in idx], axis=1)
            _store(o, val, d["acc"])
        if n_r:
            pl.when(step == steps - 1)(lambda: _exchange_wait(*r_args))

    rev = carry["rev"] if carry else None
    s_specs = []
    for a in saved:
        n = a.ndim - ng
        s_specs.append(pl.BlockSpec((None,) * ng + tuple(a.shape[ng:]), lambda *g, _n=n: tuple(rev(g)) + (0,) * _n))
    res = pl.pallas_call(
        body, name=name, grid=grid, in_specs=list(in_specs) + list(gout_specs) + s_specs + [_ANY] * (n_a + n_r),
        out_specs=[d["spec"] for d in wants] + [_ANY] * n_r,
        out_shape=[jax.ShapeDtypeStruct(d["shape"], d["dtype"]) for d in wants] + [jax.ShapeDtypeStruct(g.shape, g.dtype) for g in ride],
        input_output_aliases={n_in + n_g + nc: 0} if n_a else {},
        scratch_shapes=[pltpu.VMEM(tuple(a.shape[ng:]), f32) for a in saved] + (_exchange_sems(n_r) if n_r else []),
        compiler_params=_params(grid),
    )(*ins, *gouts, *saved, *([into] if n_a else []), *ride)
    return list(res)


def _want(idx, shape, spec, acc=None):
    d = _out(shape, spec, acc)
    d["idx"] = idx
    return d


def addn(name, items, t):
    s, w = items[0][0].shape[-2:]
    specs = []
    for a, j in items:
        if j is None:
            specs.append(pl.BlockSpec((t, w), lambda i: (i, 0)))
        else:
            specs.append(pl.BlockSpec((None, t, w), lambda i, _j=j: (_j, i, 0)))

    def fn(*xs):
        y = xs[0]
        for x in xs[1:]:
            y = y + x
        return (y,)

    return run_fwd(name, fn, (s // t,), [a for a, _ in items], specs,
                   [_out((s, w), pl.BlockSpec((t, w), lambda i: (i, 0)))])[0][0]


def _parts(a):
    return [(a, j) for j in range(a.shape[0])]


def _ffn_core(n, wg, wu, wo):
    return (0.5 * mm(_silu(mm(n, wg)) * mm(n, wu), wo),)


def ffn_fwd(name, h, g, wg, wu, wo, t=512, tf=1408):
    s, d = h.shape
    f = wg.shape[1]

    def fn(hh, gg, a, b, c):
        n = _rms(hh, gg)
        return _ffn_core(n, a, b, c)[0] + (pl.program_id(1) == 0).astype(f32) * hh, n

    specs = [pl.BlockSpec((t, d), lambda i, j: (i, 0)), _full(g), pl.BlockSpec((d, tf), lambda i, j: (0, j)),
             pl.BlockSpec((d, tf), lambda i, j: (0, j)), pl.BlockSpec((tf, d), lambda i, j: (j, 0))]
    outs = [_out((s, d), pl.BlockSpec((t, d), lambda i, j: (i, 0)), acc=(1,)),
            _out((s, d), pl.BlockSpec((t, d), lambda i, j: (i, 0)), dtype=bf16)]
    return run_fwd(name, fn, (s // t, f // tf), [h, g, wg, wu, wo], specs, outs)[0]


def ffn_bwd(name, n, wg, wu, wo, gout, t=512, tf=256, ride=None):
    s, d = n.shape
    f = wg.shape[1]
    nj = f // tf
    specs = [pl.BlockSpec((t, d), lambda j, i: (i, 0)), pl.BlockSpec((d, tf), lambda j, i: (0, j)),
             pl.BlockSpec((d, tf), lambda j, i: (0, j)), pl.BlockSpec((tf, d), lambda j, i: (j, 0))]
    wants = [_want(0, (nj, s, d), pl.BlockSpec((None, t, d), lambda j, i: (j, i, 0))),
             _want(1, wg.shape, pl.BlockSpec((d, tf), lambda j, i: (0, j)), acc=(1,)),
             _want(2, wu.shape, pl.BlockSpec((d, tf), lambda j, i: (0, j)), acc=(1,)),
             _want(3, wo.shape, pl.BlockSpec((tf, d), lambda j, i: (j, 0)), acc=(1,))]
    return run_bwd(name, _ffn_core, (nj, s // t), [n, wg, wu, wo], specs,
                   [gout], [pl.BlockSpec((t, d), lambda j, i: (i, 0))], wants, ride=ride)


def norm_bwd(name, h, g, base, parts, t=256):
    s, d = h.shape

    def fn(hh, gg, bb, *ps):
        dn = ps[0]
        for p in ps[1:]:
            dn = dn + p
        _, vjp = jax.vjp(_rms, hh, gg)
        dh, dg = vjp(dn)
        return bb + dh, dg

    row = pl.BlockSpec((t, d), lambda i: (i, 0))
    specs = [row, _full(g), row] + [pl.BlockSpec((None, t, d), lambda i, _j=j: (_j, i, 0)) for _, j in parts]
    outs = [_out((s, d), row), _out(g.shape, _full(g), acc=(0,))]
    return run_fwd(name, fn, (s // t,), [h, g, base] + [a for a, _ in parts], specs, outs)[0]


def _lin_tile(u, w, b):
    return (mm(u, w) + b,)


def normlin_fwd(name, h, g, w, b, t=512, tn=1024):
    s, d = h.shape
    n = w.shape[1]

    def fn(hh, gg, ww, bb):
        u = _rms(hh, gg)
        return mm(u, ww) + bb, u

    specs = [pl.BlockSpec((t, d), lambda i, j: (i, 0)), _full(g), pl.BlockSpec((d, tn), lambda i, j: (0, j)),
             pl.BlockSpec((1, tn), lambda i, j: (0, j))]
    outs = [_out((s, n), pl.BlockSpec((t, tn), lambda i, j: (i, j))),
            _out((s, d), pl.BlockSpec((t, d), lambda i, j: (i, 0)), dtype=bf16)]
    return run_fwd(name, fn, (s // t, n // tn), [h, g, w, b], specs, outs)[0]


def lin_fwd(name, u, w, b, t=512, tn=1024):
    s, d = u.shape
    n = w.shape[1]
    specs = [pl.BlockSpec((t, d), lambda i, j: (i, 0)), pl.BlockSpec((d, tn), lambda i, j: (0, j)),
             pl.BlockSpec((1, tn), lambda i, j: (0, j))]
    outs = [_out((s, n), pl.BlockSpec((t, tn), lambda i, j: (i, j)))]
    return run_fwd(name, _lin_tile, (s // t, n // tn), [u, w, b], specs, outs)[0][0]


def lin_bwd(name, u, w, b, gout, t=256, tn=1024):
    s, d = u.shape
    n = w.shape[1]
    nj = n // tn
    specs = [pl.BlockSpec((t, d), lambda j, i: (i, 0)), pl.BlockSpec((d, tn), lambda j, i: (0, j)),
             pl.BlockSpec((1, tn), lambda j, i: (0, j))]
    wants = [_want(0, (nj, s, d), pl.BlockSpec((None, t, d), lambda j, i: (j, i, 0))),
             _want(1, w.shape, pl.BlockSpec((d, tn), lambda j, i: (0, j)), acc=(1,)),
             _want(2, b.shape, pl.BlockSpec((1, tn), lambda j, i: (0, j)), acc=(1,))]
    return run_bwd(name, _lin_tile, (nj, s // t), [u, w, b], specs,
                   [gout], [pl.BlockSpec((t, tn), lambda j, i: (i, j))], wants)


def _merge_tile(n_axis, residual):
    def fn(h, y, gp, wb, wo):
        part = mm(jax.nn.sigmoid(gp) * mm(y, wb), wo)
        if residual:
            part = part + (pl.program_id(n_axis) == 0).astype(f32) * h
        return (part,)
    return fn


def merge_fwd(name, h, ys, gpre, wb, wo, t=512):
    s, d = h.shape
    specs = [pl.BlockSpec((t, d), lambda i, n: (i, 0)), pl.BlockSpec((t, BW), lambda i, n: (i, n)),
             pl.BlockSpec((t, d), lambda i, n: (i, n)), pl.BlockSpec((None, BW, d), lambda i, n: (n, 0, 0)), _full(wo)]
    outs = [_out((s, d), pl.BlockSpec((t, d), lambda i, n: (i, 0)), acc=(1,))]
    return run_fwd(name, _merge_tile(1, True), (s // t, 4), [h, ys, gpre, wb, wo], specs, outs)[0][0]


def merge_bwd(name, h, ys, gpre, wb, wo, gout, t=256):
    s, d = h.shape
    specs = [pl.BlockSpec((t, d), lambda n, i: (i, 0)), pl.BlockSpec((t, BW), lambda n, i: (i, n)),
             pl.BlockSpec((t, d), lambda n, i: (i, n)), pl.BlockSpec((None, BW, d), lambda n, i: (n, 0, 0)), _full(wo)]
    wants = [_want(1, ys.shape, pl.BlockSpec((t, BW), lambda n, i: (i, n))),
             _want(2, gpre.shape, pl.BlockSpec((t, d), lambda n, i: (i, n))),
             _want(3, wb.shape, pl.BlockSpec((None, BW, d), lambda n, i: (n, 0, 0)), acc=(1,)),
             _want(4, wo.shape, _full(wo), acc=(0, 1))]
    return run_bwd(name, _merge_tile(0, False), (4, s // t), [h, ys, gpre, wb, wo], specs,
                   [gout], [pl.BlockSpec((t, d), lambda n, i: (i, 0))], wants)


def _ple_tile(residual):
    def fn(h, pe, g, wgate, wproj):
        y = jax.nn.sigmoid(mm(_rms(h, g), wgate)) * mm(pe, wproj)
        return (y + h,) if residual else (y,)
    return fn


def ple_fwd(name, h, pe, g, wgate, wproj, t=512):
    s, d = h.shape
    specs = [pl.BlockSpec((t, d), lambda i: (i, 0)), pl.BlockSpec((t, pe.shape[1]), lambda i: (i, 0)),
             _full(g), _full(wgate), _full(wproj)]
    return run_fwd(name, _ple_tile(True), (s // t,), [h, pe, g, wgate, wproj], specs,
                   [_out((s, d), pl.BlockSpec((t, d), lambda i: (i, 0)))])[0][0]


def ple_bwd(name, h, pe, g, wgate, wproj, gout, t=256):
    s, d = h.shape
    specs = [pl.BlockSpec((t, d), lambda i: (i, 0)), pl.BlockSpec((t, pe.shape[1]), lambda i: (i, 0)),
             _full(g), _full(wgate), _full(wproj)]
    wants = [_want(0, h.shape, pl.BlockSpec((t, d), lambda i: (i, 0))), _want(2, g.shape, _full(g), acc=(0,)),
             _want(3, wgate.shape, _full(wgate), acc=(0,)), _want(4, wproj.shape, _full(wproj), acc=(0,))]
    return run_bwd(name, _ple_tile(True), (s // t,), [h, pe, g, wgate, wproj], specs,
                   [gout], [pl.BlockSpec((t, d), lambda i: (i, 0))], wants)


def final_loss(name, h, g, target, t=512):
    s, d = h.shape

    def fn(hh, gg, tt):
        def loss_fn(a, b):
            err = _rms(a, b) - tt
            return 0.5 * jnp.sum(jnp.mean(err * err, axis=-1, keepdims=True), axis=0, keepdims=True)
        loss, vjp = jax.vjp(loss_fn, hh, gg)
        dh, dgain = vjp(jnp.ones((1, 1), f32))
        return loss, dh, dgain

    specs = [pl.BlockSpec((t, d), lambda i: (i, 0)), _full(g), pl.BlockSpec((t, d), lambda i: (i, 0))]
    outs = [_out((1, 1), pl.BlockSpec((1, 1), lambda i: (0, 0)), acc=(0,)),
            _out((s, d), pl.BlockSpec((t, d), lambda i: (i, 0))), _out(g.shape, _full(g), acc=(0,))]
    return run_fwd(name, fn, (s // t,), [h, g, target], specs, outs)[0]


def _rev(nc, rev):
    return (lambda c: nc - 1 - c) if rev else (lambda c: c)


def _s5_ops_tile(are, aim, lstep, bre, bim):
    step = jnp.exp(lstep)
    mag = jnp.exp(are * step)
    ab_re, ab_im = mag * jnp.cos(aim * step), mag * jnp.sin(aim * step)
    den = are * are + aim * aim
    num_re = ab_re - 1.0
    f_re = (num_re * are + ab_im * aim) / den
    f_im = (ab_im * are - num_re * aim) / den
    bb_re = f_re * bre - f_im * bim
    bb_im = f_re * bim + f_im * bre
    pr = jnp.broadcast_to(ab_re, (LS, ab_re.shape[1]))
    pi = jnp.broadcast_to(ab_im, (LS, ab_im.shape[1]))
    k = 1
    while k < LS:
        pr, pi = _cmul(pr, pi, _shift(pr, k, 1.0), _shift(pi, k, 0.0))
        k *= 2
    return ab_re, ab_im, bb_re, bb_im, pr, pi


def _s5_ops_specs(arrs):
    return [pl.BlockSpec((None,) + a.shape[1:], lambda gb: (gb, 0, 0)) for a in arrs]


def s5_ops_fwd(name, raw):
    shapes = [(4, 1, 512), (4, 1, 512), (4, 128, 512), (4, 128, 512), (4, LS, 512), (4, LS, 512)]
    outs = [_out(sh, pl.BlockSpec((None,) + sh[1:], lambda gb: (gb, 0, 0))) for sh in shapes]
    return run_fwd(name, _s5_ops_tile, (4,), raw, _s5_ops_specs(raw), outs)[0]


def s5_ops_bwd(name, raw, gops):
    wants = [_want(i, a.shape, pl.BlockSpec((None,) + a.shape[1:], lambda gb: (gb, 0, 0))) for i, a in enumerate(raw)]
    return run_bwd(name, _s5_ops_tile, (4,), raw, _s5_ops_specs(raw), gops, _s5_ops_specs(gops), wants)


def _lti_scan_raw(xr, xi, pr, pi, up):
    n = xr.shape[0]
    move = (lambda a, k: _shift_up(a, k)) if up else (lambda a, k: jnp.where(_rows(a.shape) >= k, pltpu.roll(a, k, 0), 0.0))
    k = 1
    while k < n:
        sr, si = _cmul(pr, pi, move(xr, k), move(xi, k))
        xr, xi = xr + sr, xi + si
        pr, pi = _cmul(pr, pi, pr, pi)
        k *= 2
    return xr, xi


@jax.custom_vjp
def _lti_scan(xr, xi, pr, pi):
    return _lti_scan_raw(xr, xi, pr, pi, False)


def _lti_scan_fwd(xr, xi, pr, pi):
    sr, si = _lti_scan_raw(xr, xi, pr, pi, False)
    return (sr, si), (sr, si, pr, pi)


def _lti_scan_bwd(res, g):
    sr, si, pr, pi = res
    mr, mi = _lti_scan_raw(g[0], g[1], pr, -pi, True)
    qr = jnp.where(_rows(sr.shape) >= 1, pltpu.roll(sr, 1, 0), 0.0)
    qi = jnp.where(_rows(si.shape) >= 1, pltpu.roll(si, 1, 0), 0.0)
    dpr = jnp.sum(qr * mr + qi * mi, axis=0, keepdims=True)
    dpi = jnp.sum(qr * mi - qi * mr, axis=0, keepdims=True)
    return mr, mi, dpr, dpi


_lti_scan.defvjp(_lti_scan_fwd, _lti_scan_bwd)


def _s5_tile(carry, u, ab_re, ab_im, bb_re, bb_im, pw_re, pw_im, c_re, c_im, dskip):
    h_re, h_im = carry
    xr, xi = _lti_scan(mm(u, bb_re), mm(u, bb_im), ab_re, ab_im)
    cr, ci = _cmul(pw_re, pw_im, h_re, h_im)
    xr, xi = xr + cr, xi + ci
    y = mm(xr, c_re) - mm(xi, c_im) + dskip * u
    return (_pick_row(xr, LS - 1), _pick_row(xi, LS - 1)), (y,)


def _s5_io(u, ops, c_re, c_im, dskip, nc, rev):
    cm = _rev(nc, rev)
    ins = [u] + list(ops) + [c_re, c_im, dskip]
    specs = [pl.BlockSpec((LS, 128), lambda gb, c: (cm(c), 36 + gb))]
    specs += [pl.BlockSpec((None,) + a.shape[1:], lambda gb, c: (gb, 0, 0)) for a in list(ops) + [c_re, c_im]]
    specs += [pl.BlockSpec((1, 128), lambda gb, c: (0, gb))]
    return ins, specs, cm


def s5_fwd(name, proj, ops, c_re, c_im, dskip):
    s = proj.shape[0]
    nc = s // LS
    ins, specs, cm = _s5_io(proj, ops, c_re, c_im, dskip, nc, False)
    outs = [_out((s, BW), pl.BlockSpec((LS, 128), lambda gb, c: (c, gb)))]
    (y,), saved = run_fwd(name, _s5_tile, (4, nc), ins, specs, outs, carry=dict(shapes=[(1, 512), (1, 512)], axis=1))
    return y, saved


def s5_bwd(name, proj, ops, c_re, c_im, dskip, saved, gy, gproj, ride=None):
    s = proj.shape[0]
    nc = s // LS
    ins, specs, cm = _s5_io(proj, ops, c_re, c_im, dskip, nc, True)
    wants = [_want(0, (s, PW), pl.BlockSpec((LS, 128), lambda gb, c: (cm(c), 36 + gb)))]
    for i, a in enumerate(list(ops) + [c_re, c_im]):
        wants.append(_want(1 + i, a.shape, pl.BlockSpec((None,) + a.shape[1:], lambda gb, c: (gb, 0, 0)), acc=(1,)))
    wants.append(_want(9, dskip.shape, pl.BlockSpec((1, 128), lambda gb, c: (0, gb)), acc=(1,)))
    return run_bwd(name, _s5_tile, (4, nc), ins, specs, [gy], [pl.BlockSpec((LS, 128), lambda gb, c: (cm(c), gb))],
                   wants, carry=dict(axis=1, saved=saved, rev=lambda g: (g[0], cm(g[1]))), into=gproj, ride=ride)


def _s5_glu_tile(y, w, b):
    z = _gelu(y)
    return (z * jax.nn.sigmoid(mm(z, w) + b),)


def s5_glu_fwd(name, y, w, b, t=512):
    s = y.shape[0]
    spec = pl.BlockSpec((t, BW), lambda i: (i, 0))
    return run_fwd(name, _s5_glu_tile, (s // t,), [y, w, b], [spec, _full(w), _full(b)], [_out((s, BW), spec)])[0][0]


def s5_glu_bwd(name, y, w, b, gout, gout_col, t=512):
    s = y.shape[0]
    spec = pl.BlockSpec((t, BW), lambda i: (i, 0))
    wants = [_want(0, y.shape, spec), _want(1, w.shape, _full(w), acc=(0,)), _want(2, b.shape, _full(b), acc=(0,))]
    return run_bwd(name, _s5_glu_tile, (s // t,), [y, w, b], [spec, _full(w), _full(b)], [gout],
                   [pl.BlockSpec((t, BW), lambda i: (i, gout_col))], wants)


def _lru_tile(carry, xb, gate, cw, cb, wr, br, wi, bi, lam):
    h_in, prev8 = carry
    xc = _conv4(xb, prev8, cw, cb)
    r = jax.nn.sigmoid(mm(xc, wr) + br)
    ig = jax.nn.sigmoid(mm(xc, wi) + bi)
    log_a = -LRU_C * r * _softplus(-lam)
    a = jnp.exp(log_a)
    b = jnp.sqrt(-_expm1(2.0 * log_a)) * (ig * xc)
    k = 1
    while k < LS:
        b = b + a * _shift(b, k, 0.0)
        a = a * _shift(a, k, 1.0)
        k *= 2
    h = b + a * h_in
    return (_pick_row(h, LS - 1), xb[LS - 8:, :]), (h * _gelu(gate),)


def _lru_io(proj, ws, nc, rev):
    cm = _rev(nc, rev)
    ins = [proj, proj] + list(ws)
    specs = [pl.BlockSpec((LS, BW), lambda c: (cm(c), 4)), pl.BlockSpec((LS, BW), lambda c: (cm(c), 5))]
    specs += [_full(a) for a in ws]
    return ins, specs, cm


def lru_fwd(name, proj, ws):
    s = proj.shape[0]
    nc = s // LS
    ins, specs, cm = _lru_io(proj, ws, nc, False)
    outs = [_out((s, BW), pl.BlockSpec((LS, BW), lambda c: (c, 0)))]
    (y,), saved = run_fwd(name, _lru_tile, (nc,), ins, specs, outs, carry=dict(shapes=[(1, BW), (8, BW)], axis=0))
    return y, saved


def lru_bwd(name, proj, ws, saved, gy, gy_col, gproj):
    s = proj.shape[0]
    nc = s // LS
    ins, specs, cm = _lru_io(proj, ws, nc, True)
    wants = [_want((0, 1), (s, PW), pl.BlockSpec((LS, 2 * BW), lambda c: (cm(c), 2)))]
    wants += [_want(2 + i, a.shape, _full(a), acc=(0,)) for i, a in enumerate(ws)]
    return run_bwd(name, _lru_tile, (nc,), ins, specs, [gy], [pl.BlockSpec((LS, BW), lambda c: (cm(c), gy_col))], wants,
                   carry=dict(axis=0, saved=saved, rev=lambda g: (cm(g[0]),)), into=gproj)


def _causal(n):
    return _rows((n, n)) >= _lanes((n, n))


def _decay(col, rowv):
    causal = _causal(col.shape[0])
    return jnp.where(causal, jnp.exp(jnp.where(causal, col - rowv, 0.0)), 0.0)


def _m2_tile(carry, z, xs_raw, b_raw, c_raw, small, cwx, cwb, cwc, cbx, cbb, cbc, dtb, alog, dsk, ng):
    state, px, pb, pc = carry
    n = CHUNK
    xs = _silu(_conv4(xs_raw, px, cwx, cbx))
    bm = _silu(_conv4(b_raw, pb, cwb, cbb))
    cmx = _silu(_conv4(c_raw, pc, cwc, cbc))
    expand = (_lanes((16, BW)) // 64 == _rows((16, BW))).astype(f32)
    tri = _causal(n).astype(f32)
    triu = (_rows((n, n)) <= _lanes((n, n))).astype(f32)
    dt = _softplus(small + dtb)
    da = dt * (-jnp.exp(alog))
    cs = lmm(tri, da)
    cs_t = rmm_tn(da, triu)
    cs_w = rmm(cs, expand)
    last_w = _pick_row(cs_w, n - 1)
    xdt = xs * rmm(dt, expand)
    g0 = _lanes((n, BW)) < 256
    bm0, bm1, cm0, cm1 = bm[:, :128], bm[:, 128:], cmx[:, :128], cmx[:, 128:]
    cb0, cb1 = mm_nt(cm0, bm0), mm_nt(cm1, bm1)
    y = jnp.where(g0, mm(cm0, state), mm(cm1, state)) * jnp.exp(cs_w)
    for h in range(8):
        sc = (cb0 if h < 4 else cb1) * _decay(_pick_lane(cs, h), _pick_row(cs_t, h))
        y = y + jnp.where(_lanes((n, BW)) // 64 == h, mm(sc, xdt), 0.0)
    xd = xdt * jnp.exp(last_w - cs_w)
    g0s = _lanes((128, BW)) < 256
    state_out = state * jnp.exp(last_w) + jnp.where(g0s, mm_tn(bm0, xd), mm_tn(bm1, xd))
    y = (y + dsk * xs) * _silu(z)
    return (state_out, xs_raw[n - 8:, :], b_raw[n - 8:, :], c_raw[n - 8:, :]), (_rms(y, ng),)


def _m2_io(proj, small, ws, nc, rev):
    cm = _rev(nc, rev)
    n = CHUNK
    ins = [proj, proj, proj, proj, small] + list(ws)
    specs = [pl.BlockSpec((n, BW), lambda c: (cm(c), 6)), pl.BlockSpec((n, BW), lambda c: (cm(c), 7)),
             pl.BlockSpec((n, 256), lambda c: (cm(c), 16)), pl.BlockSpec((n, 256), lambda c: (cm(c), 17)),
             pl.BlockSpec((n, 16), lambda c: (cm(c), 0))]
    specs += [_full(a) for a in ws]
    return ins, specs, cm


_M2_CARRY = [(128, BW), (8, BW), (8, 256), (8, 256)]


def m2_fwd(name, proj, small, ws):
    s = proj.shape[0]
    nc = s // CHUNK
    ins, specs, cm = _m2_io(proj, small, ws, nc, False)
    outs = [_out((s, BW), pl.BlockSpec((CHUNK, BW), lambda c: (c, 0)))]
    (y,), saved = run_fwd(name, _m2_tile, (nc,), ins, specs, outs, carry=dict(shapes=_M2_CARRY, axis=0))
    return y, saved


def m2_bwd(name, proj, small, ws, saved, gy, gy_col, gproj):
    s = proj.shape[0]
    nc = s // CHUNK
    ins, specs, cm = _m2_io(proj, small, ws, nc, True)
    n = CHUNK
    wants = [_want((0, 1, 2, 3), (s, PW), pl.BlockSpec((n, 3 * BW), lambda c: (cm(c), 2))),
             _want(4, (s, 16), pl.BlockSpec((n, 16), lambda c: (cm(c), 0)))]
    wants += [_want(5 + i, a.shape, _full(a), acc=(0,)) for i, a in enumerate(ws)]
    return run_bwd(name, _m2_tile, (nc,), ins, specs, [gy], [pl.BlockSpec((n, BW), lambda c: (cm(c), gy_col))], wants,
                   carry=dict(axis=0, saved=saved, rev=lambda g: (cm(g[0]),)), into=gproj)


def _l2n(x):
    return x * lax.rsqrt(jnp.sum(x * x, axis=-1, keepdims=True) + EPS)


def _gdn_tile(carry, q_raw, k_raw, v_raw, gate, small, cwq, cwk, cwv, dtb, alog, ng):
    state, pq, pk, pv = carry
    n, nh = CHUNK, 4
    nn_ = n * nh
    qc = _silu(_conv4(q_raw, pq, cwq, None))
    kc = _silu(_conv4(k_raw, pk, cwk, None))
    vc = _silu(_conv4(v_raw, pv, cwv, None))
    beta16 = jax.nn.sigmoid(small)
    g16 = -jnp.exp(alog) * _softplus(small + dtb)
    tri = _causal(n).astype(f32)
    triu = (_rows((n, n)) <= _lanes((n, n))).astype(f32)
    cs16 = lmm(tri, g16)
    cs16_t = rmm_tn(g16, triu)
    lanes_of = lambda h: slice(128 * h, 128 * (h + 1))
    rows_of = lambda h: slice(n * h, n * (h + 1))
    stack = lambda f: jnp.concatenate([f(h) for h in range(nh)], axis=0)
    q = stack(lambda h: _l2n(qc[:, lanes_of(h)]) * (128 ** -0.5))
    k = stack(lambda h: _l2n(kc[:, lanes_of(h)]))
    v = stack(lambda h: vc[:, lanes_of(h)])
    beta = stack(lambda h: _pick_lane(beta16, 8 + h))
    col = stack(lambda h: _pick_lane(cs16, 12 + h))
    last_h = [_pick_row(_pick_lane(cs16, 12 + h), n - 1) for h in range(nh)]
    last = stack(lambda h: jnp.broadcast_to(last_h[h], (n, 1)))
    last_w = jnp.concatenate([jnp.broadcast_to(last_h[h], (1, 128)) for h in range(nh)], axis=1)
    spread = (_rows((n, nn_)) == _lanes((n, nn_)) % n).astype(f32)
    cs_w = rmm(cs16_t, spread)
    rowv = jnp.sum(jnp.where(_rows((16, nn_)) == 12 + _lanes((16, nn_)) // n, cs_w, 0.0), axis=0, keepdims=True)
    same = (_rows((nn_, nn_)) // n) == (_lanes((nn_, nn_)) // n)
    causal = jnp.logical_and(same, _rows((nn_, nn_)) >= _lanes((nn_, nn_)))
    strict = jnp.logical_and(same, _rows((nn_, nn_)) > _lanes((nn_, nn_)))
    decay = jnp.where(causal, jnp.exp(jnp.where(causal, col - rowv, 0.0)), 0.0)
    kb = k * beta
    t = _tri_inv(jnp.where(strict, mm_nt(kb, k) * decay, 0.0))
    e_col = jnp.exp(col)
    uw = xmm(t, jnp.concatenate([v * beta, kb * e_col], axis=1))
    u, w = uw[:, :128], uw[:, 128:]
    qk = mm_nt(q, k) * decay
    own = lambda r: stack(lambda h: r[rows_of(h), lanes_of(h)])
    v_new = u - own(mm(w, state))
    o = own(mm(q * e_col, state)) + mm(qk, v_new)
    zero = jnp.zeros((n, 128), f32)
    v_blocks = stack(lambda h: jnp.concatenate([v_new[rows_of(h), :] if j == h else zero for j in range(nh)], axis=1))
    state_out = state * jnp.exp(last_w) + mm_tn(k * jnp.exp(last - col), v_blocks)
    gt = stack(lambda h: gate[:, lanes_of(h)])
    out = _rms(o, ng) * _silu(gt)
    out = jnp.concatenate([out[rows_of(h), :] for h in range(nh)], axis=1)
    return (state_out, q_raw[n - 8:, :], k_raw[n - 8:, :], v_raw[n - 8:, :]), (out,)


def _gdn_io(proj, small, cw, dtb, alog, ng, nc, rev):
    cm = _rev(nc, rev)
    n = CHUNK
    ins = [proj, proj, proj, proj, small, cw, cw, cw, dtb, alog, ng]
    specs = [pl.BlockSpec((n, BW), lambda c, _j=j: (cm(c), _j)) for j in (0, 1, 2, 3)]
    specs += [pl.BlockSpec((n, 16), lambda c: (cm(c), 0))]
    specs += [pl.BlockSpec((4, BW), lambda c, _j=j: (0, _j)) for j in (0, 1, 2)]
    specs += [_full(dtb), _full(alog), _full(ng)]
    return ins, specs, cm


_GDN_CARRY = [(128, BW)] + [(8, BW)] * 3


def gdn_fwd(name, proj, small, cw, dtb, alog, ng):
    s = proj.shape[0]
    nc = s // CHUNK
    ins, specs, cm = _gdn_io(proj, small, cw, dtb, alog, ng, nc, False)
    outs = [_out((s, BW), pl.BlockSpec((CHUNK, BW), lambda c: (c, 0)))]
    (y,), saved = run_fwd(name, _gdn_tile, (nc,), ins, specs, outs, carry=dict(shapes=_GDN_CARRY, axis=0))
    return y, saved


def gdn_bwd(name, proj, small, cw, dtb, alog, ng, saved, gy, gy_col, ride=None):
    s = proj.shape[0]
    nc = s // CHUNK
    n = CHUNK
    ins, specs, cm = _gdn_io(proj, small, cw, dtb, alog, ng, nc, True)
    wants = [_want((0, 1, 2, 3), (s, PW), pl.BlockSpec((n, 4 * BW), lambda c: (cm(c), 0))),
             _want(4, (s, 16), pl.BlockSpec((n, 16), lambda c: (cm(c), 0)))]
    wants += [_want(5 + i, (4, BW), pl.BlockSpec((4, BW), lambda c: (0, 0)), acc=(0,)) for i in range(3)]
    wants += [_want(8, dtb.shape, _full(dtb), acc=(0,)), _want(9, alog.shape, _full(alog), acc=(0,)),
              _want(10, ng.shape, _full(ng), acc=(0,))]
    return run_bwd(name, _gdn_tile, (nc,), ins, specs, [gy], [pl.BlockSpec((n, BW), lambda c: (cm(c), gy_col))], wants,
                   carry=dict(axis=0, saved=saved, rev=lambda g: (cm(g[0]),)), ride=ride)


def _perm_cols(w):
    pad = jnp.zeros(w.shape[:-1] + (PW - IN_WIDTH,), w.dtype)
    return jnp.concatenate([w[..., 3080:5128], w[..., 512:3072], w[..., :512], w[..., 3072:3080], w[..., 5128:5136], pad], axis=-1)


def _unperm_cols(g):
    return jnp.concatenate([g[..., 4608:5120], g[..., 2048:4608], g[..., 5120:5128], g[..., :2048], g[..., 5128:5136]], axis=-1)


def _bd(blocks):
    n, a, b = blocks.shape
    eye = jnp.eye(n, dtype=blocks.dtype)
    return jnp.einsum("nab,nm->namb", blocks, eye).reshape(n * a, n * b)


def _layer_layout(lw):
    o = {}
    row = lambda a: a.reshape(1, -1)
    o["b_gate"] = row(lw["b_gate"])
    o["s5_are"] = lw["s5_a_re"].reshape(4, 1, 512)
    o["s5_aim"] = lw["s5_a_im"].reshape(4, 1, 512)
    o["s5_lstep"] = jnp.repeat(lw["s5_log_step"], 64).reshape(4, 1, 512)
    bt = lambda b: jax.vmap(_bd)(jnp.swapaxes(b, 1, 2).reshape(4, 8, 16, 64))
    o["s5_bre"], o["s5_bim"] = bt(lw["s5_b_re"]), bt(lw["s5_b_im"])
    ct = lambda c: jax.vmap(_bd)(jnp.swapaxes(c, 1, 2).reshape(4, 8, 64, 16))
    o["s5_cre"], o["s5_cim"] = ct(lw["s5_c_re"]), ct(lw["s5_c_im"])
    o["s5_d"] = row(lw["s5_d"])
    o["s5_b_glu"] = row(lw["s5_b_glu"])
    o["lru_conv_b"], o["lru_b_r"], o["lru_b_i"], o["lru_lambda"] = (row(lw[k]) for k in ("lru_conv_b", "lru_b_r", "lru_b_i", "lru_lambda"))
    o["lru_wr"], o["lru_wi"] = _bd(lw["lru_w_r"]), _bd(lw["lru_w_i"])
    cw, cb = lw["m2_conv_w"], lw["m2_conv_b"]
    o["m2_cwx"], o["m2_cwb"], o["m2_cwc"] = cw[:, :512], cw[:, 512:768], cw[:, 768:]
    o["m2_cbx"], o["m2_cbb"], o["m2_cbc"] = row(cb[:512]), row(cb[512:768]), row(cb[768:])
    o["m2_dtb"] = jnp.pad(lw["m2_dt_bias"], (0, 8)).reshape(1, 16)
    o["m2_alog"] = jnp.pad(lw["m2_a_log"], (0, 8)).reshape(1, 16)
    o["m2_dsk"] = jnp.repeat(lw["m2_d"], 64).reshape(1, 512)
    o["m2_norm"] = row(lw["m2_norm"])
    o["gdn_dtb"] = jnp.pad(lw["gdn_dt_bias"], (12, 0)).reshape(1, 16)
    o["gdn_alog"] = jnp.pad(lw["gdn_a_log"], (12, 0)).reshape(1, 16)
    o["gdn_norm"] = row(lw["gdn_norm"])
    for k in ("ffn1_norm", "mix_norm", "ffn2_norm", "ple_norm"):
        o[k] = row(lw[k])
    return o


_BIG_PLAIN = ("w_gate", "s5_w_glu", "w_branch", "w_out", "ple_w_gate", "ple_w_proj")


def big_layout(w):
    o = {k: w[k] for k in _BIG_PLAIN}
    for f in ("ffn1", "ffn2"):
        o[f + "_wg"], o[f + "_wu"] = w[f + "_w_in"][:, :FFN_DIM], w[f + "_w_in"][:, FFN_DIM:]
        o[f + "_wo"] = w[f + "_w_out"]
    o["w_in"] = _perm_cols(w["w_in"])
    return {k: v.astype(bf16) for k, v in o.items()}


def big_unlayout(g):
    o = {k: g[k] for k in _BIG_PLAIN}
    for f in ("ffn1", "ffn2"):
        o[f + "_w_in"] = jnp.concatenate([g[f + "_wg"], g[f + "_wu"]], axis=1)
        o[f + "_w_out"] = g[f + "_wo"]
    o["w_in"] = _unperm_cols(g["w_in"])
    return o


_SMALL_KEYS = ("b_gate", "s5_log_step", "s5_a_re", "s5_a_im", "s5_b_re", "s5_b_im", "s5_c_re", "s5_c_im", "s5_d",
               "s5_b_glu", "lru_conv_b", "lru_w_r", "lru_b_r", "lru_w_i", "lru_b_i", "lru_lambda", "m2_conv_w",
               "m2_conv_b", "m2_dt_bias", "m2_a_log", "m2_d", "m2_norm", "gdn_dt_bias", "gdn_a_log", "gdn_norm",
               "ffn1_norm", "mix_norm", "ffn2_norm", "ple_norm")


def _layer_fwd(i, h0, pe, lw, big):
    n = f"l{i}_"
    lay = _layer_layout(lw)
    a = {"h0": h0, "lay": lay}
    h1, n1 = ffn_fwd(n + "ffn1_fwd", h0, lay["ffn1_norm"], big["ffn1_wg"], big["ffn1_wu"], big["ffn1_wo"])
    zero_b = jnp.zeros((1, PW), f32)
    proj, u = normlin_fwd(n + "inproj_fwd", h1, lay["mix_norm"], big["w_in"], zero_b, tn=896)
    gpre = lin_fwd(n + "gate_fwd", u, big["w_gate"], lay["b_gate"], tn=1024)
    small = proj[:, SMALL_OFF:SMALL_OFF + 16]
    raw = [lay["s5_are"], lay["s5_aim"], lay["s5_lstep"], lay["s5_bre"], lay["s5_bim"]]
    ops = s5_ops_fwd(n + "s5ops_fwd", raw)
    y5, sv5 = s5_fwd(n + "s5_fwd", proj, ops, lay["s5_cre"], lay["s5_cim"], lay["s5_d"])
    ya = s5_glu_fwd(n + "s5glu_fwd", y5, big["s5_w_glu"], lay["s5_b_glu"])
    lru_ws = [lw["lru_conv_w"], lay["lru_conv_b"], lay["lru_wr"], lay["lru_b_r"], lay["lru_wi"], lay["lru_b_i"], lay["lru_lambda"]]
    yb, svb = lru_fwd(n + "lru_fwd", proj, lru_ws)
    m2_ws = [lay[k] for k in ("m2_cwx", "m2_cwb", "m2_cwc", "m2_cbx", "m2_cbb", "m2_cbc", "m2_dtb", "m2_alog", "m2_dsk", "m2_norm")]
    yc, svc = m2_fwd(n + "m2_fwd", proj, small, m2_ws)
    yd, svd = gdn_fwd(n + "gdn_fwd", proj, small, lw["gdn_conv_w"], lay["gdn_dtb"], lay["gdn_alog"], lay["gdn_norm"])
    ys = jnp.concatenate([ya, yb, yc, yd], axis=1)
    h2 = merge_fwd(n + "merge_fwd", h1, ys, gpre, big["w_branch"], big["w_out"])
    h3, n2 = ffn_fwd(n + "ffn2_fwd", h2, lay["ffn2_norm"], big["ffn2_wg"], big["ffn2_wu"], big["ffn2_wo"])
    h4 = ple_fwd(n + "ple_fwd", h3, pe, lay["ple_norm"], big["ple_w_gate"], big["ple_w_proj"])
    a.update(h1=h1, n1=n1, u=u, n2=n2, proj=proj, gpre=gpre, small=small, raw=raw, ops=ops, y5=y5, sv5=sv5, lru_ws=lru_ws, svb=svb,
             m2_ws=m2_ws, svc=svc, svd=svd, ys=ys, h2=h2, h3=h3, pe=pe)
    return h4, a


def small_into(name, a, b, gproj, t=512):
    s, w = a.shape
    pad_w = PW - SMALL_OFF

    def body(a_ref, b_ref, _, o_ref):
        place = (_rows((w, pad_w)) == _lanes((w, pad_w))).astype(f32)
        o_ref[...] = _dg(a_ref[...] + b_ref[...], place, _NN, (3, 1))

    row = pl.BlockSpec((t, w), lambda i: (i, 0))
    return pl.pallas_call(
        body, name=name, grid=(s // t,), in_specs=[row, row, _ANY],
        out_specs=pl.BlockSpec((t, pad_w), lambda i: (i, SMALL_OFF // pad_w)),
        out_shape=jax.ShapeDtypeStruct(gproj.shape, gproj.dtype), input_output_aliases={2: 0}, compiler_params=_params((1,)),
    )(a, b, gproj)


RIDE_S5 = ("ffn2_w_in", "ffn2_w_out", "w_branch", "w_out", "ple_w_gate", "ple_w_proj", "s5_w_glu")
RIDE_FFN1 = ("w_in", "w_gate", "lru_conv_w", "m2_conv_w", "gdn_conv_w")
RIDE_LAST = ("ffn1_w_in", "ffn1_w_out")


def _grad_of(k, gb, small):
    if k in ("ffn1_w_in", "ffn2_w_in"):
        return jnp.concatenate([gb[k[:4] + "_wg"], gb[k[:4] + "_wu"]], axis=1)
    if k in ("ffn1_w_out", "ffn2_w_out"):
        return gb[k[:4] + "_wo"]
    if k == "w_in":
        return _unperm_cols(gb[k])
    if k == "m2_conv_w" and k not in small:
        return jnp.concatenate([small["m2_cwx"], small["m2_cwb"], small["m2_cwc"]], axis=1)
    return gb[k] if k in gb else small[k]


def _scattered(names, gb, small):
    return [_full_to_scattered(k, _grad_of(k, gb, small)[None]).astype(bf16) for k in names]


def _layer_bwd(i, a, lw, big, gh4, ride=None):
    n = f"l{i}_"
    lay = a["lay"]
    gb, gl = {}, {}
    s = gh4.shape[0]
    t = 512
    gh3, gl["ple_norm"], gb["ple_w_gate"], gb["ple_w_proj"] = ple_bwd(
        n + "ple_bwd", a["h3"], a["pe"], lay["ple_norm"], big["ple_w_gate"], big["ple_w_proj"], gh4)
    dn2, gb["ffn2_wg"], gb["ffn2_wu"], gb["ffn2_wo"] = ffn_bwd(
        n + "ffn2_bwd", a["n2"], big["ffn2_wg"], big["ffn2_wu"], big["ffn2_wo"], gh3)
    gh2, gl["ffn2_norm"] = norm_bwd(n + "gh2", a["h2"], lay["ffn2_norm"], gh3, _parts(dn2))
    gys, ggpre, gb["w_branch"], gb["w_out"] = merge_bwd(
        n + "merge_bwd", a["h1"], a["ys"], a["gpre"], big["w_branch"], big["w_out"], gh2)
    rd = gdn_bwd(n + "gdn_bwd", a["proj"], a["small"], lw["gdn_conv_w"], lay["gdn_dtb"], lay["gdn_alog"], lay["gdn_norm"], a["svd"], gys, 3, ride)
    gproj, dsm_d = rd[:2]
    gl["gdn_conv_w"] = jnp.concatenate(rd[2:5], axis=1)
    gl["gdn_dtb"], gl["gdn_alog"], gl["gdn_norm"] = rd[5:8]
    rode = rd[8:]
    rb = lru_bwd(n + "lru_bwd", a["proj"], a["lru_ws"], a["svb"], gys, 1, gproj)
    gproj = rb[0]
    gl["lru_conv_w"], gl["lru_conv_b"], gl["lru_wr"], gl["lru_b_r"], gl["lru_wi"], gl["lru_b_i"], gl["lru_lambda"] = rb[1:]
    rc = m2_bwd(n + "m2_bwd", a["proj"], a["small"], a["m2_ws"], a["svc"], gys, 2, gproj)
    gproj, dsm_c = rc[:2]
    for k, v in zip(("m2_cwx", "m2_cwb", "m2_cwc", "m2_cbx", "m2_cbb", "m2_cbc", "m2_dtb", "m2_alog", "m2_dsk", "m2_norm"), rc[2:]):
        gl[k] = v
    gy5, gb["s5_w_glu"], gl["s5_b_glu"] = s5_glu_bwd(n + "s5glu_bwd", a["y5"], big["s5_w_glu"], lay["s5_b_glu"], gys, 0)
    own = {}
    r5 = s5_bwd(n + "s5_bwd", a["proj"], a["ops"], lay["s5_cre"], lay["s5_cim"], lay["s5_d"], a["sv5"], gy5, gproj,
                _scattered(RIDE_S5, gb, gl) if ride is not None else None)
    gproj, gops, gl["s5_cre"], gl["s5_cim"], gl["s5_d"] = r5[0], r5[1:7], r5[7], r5[8], r5[9]
    own.update(zip(RIDE_S5, r5[10:]))
    gl["s5_are"], gl["s5_aim"], gl["s5_lstep"], gl["s5_bre"], gl["s5_bim"] = s5_ops_bwd(n + "s5ops_bwd", a["raw"], gops)
    gproj = small_into(n + "gsmall", dsm_c, dsm_d, gproj)
    zero_b = jnp.zeros((1, PW), f32)
    du_p, gb["w_in"], _ = lin_bwd(n + "inproj_bwd", a["u"], big["w_in"], zero_b, gproj, tn=896)
    du_g, gb["w_gate"], gl["b_gate"] = lin_bwd(n + "gate_bwd", a["u"], big["w_gate"], lay["b_gate"], ggpre, tn=1024)
    gh1, gl["mix_norm"] = norm_bwd(n + "gh1", a["h1"], lay["mix_norm"], gh2, _parts(du_p) + _parts(du_g))
    rf = ffn_bwd(n + "ffn1_bwd", a["n1"], big["ffn1_wg"], big["ffn1_wu"], big["ffn1_wo"], gh1,
                 ride=_scattered(RIDE_FFN1, gb, gl) if ride is not None else None)
    dn1, gb["ffn1_wg"], gb["ffn1_wu"], gb["ffn1_wo"] = rf[:4]
    own.update(zip(RIDE_FFN1, rf[4:]))
    gh0, gl["ffn1_norm"] = norm_bwd(n + "gh0", a["h0"], lay["ffn1_norm"], gh1, _parts(dn1))
    return gh0, gb, gl, (rode, own)


def _local_step(x, p, target, bigs, smalls, final_norm, ride_of=None):
    h = x
    acts = []
    for i in range(DEPTH):
        h, a = _layer_fwd(i, h, p[i], smalls[i], bigs[i])
        acts.append(a)
    fg = final_norm.reshape(1, -1)
    loss, gh, gfn = final_loss("final_loss", h, fg, target)
    gbs, gss, rode = [None] * DEPTH, [None] * DEPTH, [None] * DEPTH
    for i in reversed(range(DEPTH)):
        ride = ride_of(gbs[i + 1], gss[i + 1]) if ride_of is not None and i + 1 < DEPTH else None
        gh, gb, gl, (got, own) = _layer_bwd(i, acts[i], smalls[i], bigs[i], gh, ride)
        if ride is not None:
            rode[i + 1], rode[i] = got, own
        _, pull = jax.vjp(_layer_layout, smalls[i])
        lay_g = {k: gl[k] for k in acts[i]["lay"]}
        gs = pull(lay_g)[0]
        gs = dict(gs)
        gs["lru_conv_w"] = gl["lru_conv_w"]
        gs["gdn_conv_w"] = gl["gdn_conv_w"]
        gbs[i], gss[i] = gb, gs
    return loss, gh, gbs, gss, gfn.reshape(-1), rode


MESH_AXES = ("x", "y", "c")
PACK_W = 1024
PACK_ROWS = 256

W_NAMES = ("ffn1_norm", "ffn1_w_in", "ffn1_w_out", "mix_norm", "w_in", "w_gate", "b_gate", "s5_log_step", "s5_a_re",
           "s5_a_im", "s5_b_re", "s5_b_im", "s5_c_re", "s5_c_im", "s5_d", "s5_w_glu", "s5_b_glu", "lru_conv_w",
           "lru_conv_b", "lru_w_r", "lru_b_r", "lru_w_i", "lru_b_i", "lru_lambda", "m2_conv_w", "m2_conv_b",
           "m2_dt_bias", "m2_a_log", "m2_d", "m2_norm", "gdn_conv_w", "gdn_dt_bias", "gdn_a_log", "gdn_norm",
           "w_branch", "w_out", "ffn2_norm", "ffn2_w_in", "ffn2_w_out", "ple_norm", "ple_w_gate", "ple_w_proj",
           "final_norm")
COL_SHARDED = ("ffn1_w_in", "w_in", "w_gate", "lru_conv_w", "m2_conv_w", "gdn_conv_w", "w_branch", "ffn2_w_in", "ple_w_proj")
ROW_SHARDED = ("ffn1_w_out", "s5_w_glu", "w_out", "ffn2_w_out", "ple_w_gate")
BIG_NAMES = ("ffn1_w_in", "ffn1_w_out", "w_in", "w_gate", "s5_w_glu", "w_branch", "w_out", "ffn2_w_in", "ffn2_w_out",
             "ple_w_gate", "ple_w_proj")
CONV_NAMES = ("lru_conv_w", "m2_conv_w", "gdn_conv_w")
SHARDED = BIG_NAMES + CONV_NAMES
REPLICATED = tuple(k for k in W_NAMES if k not in SHARDED)


def _gathered_to_full(name, g):
    if name in COL_SHARDED:
        g = jnp.moveaxis(g, 0, -2)
        return g.reshape(g.shape[:-2] + (g.shape[-2] * g.shape[-1],))
    g = jnp.moveaxis(g, 0, 1)
    return g.reshape((g.shape[0], g.shape[1] * g.shape[2]) + g.shape[3:])


def _full_to_scattered(name, w):
    if name in COL_SHARDED:
        w = w.reshape(w.shape[:-1] + (N_DEV, w.shape[-1] // N_DEV))
        return jnp.moveaxis(w, -2, 0)
    w = w.reshape((w.shape[0], N_DEV, w.shape[1] // N_DEV) + w.shape[2:])
    return jnp.moveaxis(w, 1, 0)


def _pack(arrs):
    pieces = []
    for a in arrs:
        k = -(-a.size // PACK_W)
        pieces.append(jnp.pad(a.reshape(-1), (0, k * PACK_W - a.size)).reshape(k, PACK_W))
    buf = jnp.concatenate(pieces, axis=0)
    return jnp.pad(buf, ((0, -buf.shape[0] % PACK_ROWS), (0, 0)))


def _unpack(buf, shapes):
    out, r = [], 0
    for sh in shapes:
        n = math.prod(sh)
        k = -(-n // PACK_W)
        out.append(buf[r:r + k].reshape(-1)[:n].reshape(sh))
        r += k
    return out


def _peer(k):
    mx, my, mc = (lax.axis_index(a) for a in MESH_AXES)
    px = 1 - mx if k & 4 else mx
    py = 1 - my if k & 2 else my
    pc = 1 - mc if k & 1 else mc
    return (px, py, pc), 4 * px + 2 * py + pc


def all_gather(name, xs):
    n = len(xs)

    def body(*refs):
        x_refs, out_refs = refs[:n], refs[n:2 * n]
        send_sems, recv_sems, local_sems = refs[2 * n:]
        mx, my, mc = (lax.axis_index(a) for a in MESH_AXES)
        me, sibling = (mx, my, mc), (mx, my, 1 - mc)
        chips = [(1 - mx, my), (mx, 1 - my), (1 - mx, 1 - my)]

        def slot(i, px, py, pc):
            return out_refs[i].at[4 * px + 2 * py + pc]

        def copy(k, i, block, to, src=None):
            return pltpu.make_async_remote_copy(
                src_ref=slot(i, *block) if src is None else src, dst_ref=slot(i, *block),
                send_sem=send_sems.at[k, i], recv_sem=recv_sems.at[k, i], device_id=to, device_id_type=pl.DeviceIdType.MESH)

        mine = [pltpu.make_async_copy(x_refs[i], slot(i, *me), local_sems.at[i]) for i in range(n)]
        first = []
        for i in range(n):
            mine[i].start()
            first.append(copy(0, i, me, sibling, src=x_refs[i]))
            first += [copy(1 + j, i, me, (*chip, mc), src=x_refs[i]) for j, chip in enumerate(chips)]
        for cp in first:
            cp.start()
        passed = []
        for i in range(n):
            for j, chip in enumerate(chips):
                copy(1 + j, i, (*chip, mc), me).wait_recv()
                cp = copy(4 + j, i, (*chip, mc), sibling)
                cp.start()
                passed.append(cp)
        for i in range(n):
            copy(0, i, sibling, me).wait_recv()
            for j, chip in enumerate(chips):
                copy(4 + j, i, (*chip, 1 - mc), me).wait_recv()
        for cp in first + passed:
            cp.wait_send()
        for cp in mine:
            cp.wait()

    res = pl.pallas_call(
        body, name=name, out_shape=[jax.ShapeDtypeStruct((N_DEV,) + x.shape, x.dtype) for x in xs],
        in_specs=[_ANY] * n, out_specs=[_ANY] * n,
        scratch_shapes=[pltpu.SemaphoreType.DMA((7, n)), pltpu.SemaphoreType.DMA((7, n)), pltpu.SemaphoreType.DMA((n,))],
    )(*xs)
    return list(res)


def _exchange_copies(g_refs, out_refs, send_sems, recv_sems, local_sems, with_incoming=True):
    mx, my, mc = (lax.axis_index(a) for a in MESH_AXES)
    me = 4 * mx + 2 * my + mc
    n = len(g_refs)
    local = [pltpu.make_async_copy(g_refs[i].at[me], out_refs[i].at[me], local_sems.at[i]) for i in range(n)]
    outgoing, incoming = [], []
    for k in range(1, N_DEV):
        peer, pidx = _peer(k)
        for i in range(n):
            sems = dict(send_sem=send_sems.at[k - 1, i], recv_sem=recv_sems.at[k - 1, i], device_id=peer,
                        device_id_type=pl.DeviceIdType.MESH)
            outgoing.append(pltpu.make_async_remote_copy(src_ref=g_refs[i].at[pidx], dst_ref=out_refs[i].at[me], **sems))
            if with_incoming:
                incoming.append(pltpu.make_async_remote_copy(src_ref=g_refs[i].at[pidx], dst_ref=out_refs[i].at[pidx], **sems))
    return local, outgoing, incoming


def _exchange_start(*refs):
    local, outgoing, _ = _exchange_copies(*refs, with_incoming=False)
    for cp in local + outgoing:
        cp.start()


def _exchange_wait(*refs):
    local, outgoing, incoming = _exchange_copies(*refs)
    for cp in incoming:
        cp.wait_recv()
    for cp in outgoing:
        cp.wait_send()
    for cp in local:
        cp.wait()


def _exchange_sems(n):
    return [pltpu.SemaphoreType.DMA((7, n)), pltpu.SemaphoreType.DMA((7, n)), pltpu.SemaphoreType.DMA((n,))]


def exchange(name, gs):
    n = len(gs)

    def body(*refs):
        args = (refs[:n], refs[n:2 * n]) + tuple(refs[2 * n:])
        _exchange_start(*args)
        _exchange_wait(*args)

    res = pl.pallas_call(
        body, name=name, out_shape=[jax.ShapeDtypeStruct(g.shape, g.dtype) for g in gs], in_specs=[_ANY] * n, out_specs=[_ANY] * n,
        scratch_shapes=_exchange_sems(n),
    )(*gs)
    return list(res)


def sum_slots(name, buf):
    return addn(name, _parts(buf), PACK_ROWS)


def _adamw_math(ww, gg, mm_, vv):
    m2 = ADAM_B1 * mm_ + (1.0 - ADAM_B1) * gg
    v2 = ADAM_B2 * vv + (1.0 - ADAM_B2) * (gg * gg)
    m_hat = m2 / (1.0 - ADAM_B1 ** ADAM_STEP)
    v_hat = v2 / (1.0 - ADAM_B2 ** ADAM_STEP)
    delta = -ADAM_LR * (m_hat / (jnp.sqrt(v_hat) + ADAM_EPS) + ADAM_WD * ww)
    return delta, m2, v2


def adamw(name, w, g, m, v):
    spec = pl.BlockSpec((PACK_ROWS, PACK_W), lambda i: (i, 0))
    return run_fwd(name, _adamw_math, (w.shape[0] // PACK_ROWS,), [w, g, m, v], [spec] * 4, [_out(w.shape, spec)] * 3)[0]


def reduce_adamw(name, slots, w, m, v):
    shape = w.shape
    r, c = shape[-2:]
    a = math.prod(shape[:-2])
    per = a // DEPTH
    tr = r
    while tr * c * 4 > (1 << 20) and tr % 16 == 0:
        tr //= 2
    s3 = [s.reshape((N_DEV, per, r, c)) for s in slots]
    w3, m3, v3 = (t.reshape((a, r, c)) for t in (w, m, v))

    def fn(*t):
        layer = pl.program_id(0) // per
        g = None
        for l in range(DEPTH):
            gl = t[l * N_DEV].astype(f32)
            for d in range(1, N_DEV):
                gl = gl + t[l * N_DEV + d].astype(f32)
            g = gl if g is None else jnp.where(layer == l, gl, g)
        k = DEPTH * N_DEV
        return (g,) + _adamw_math(t[k], g, t[k + 1], t[k + 2])

    specs, ins = [], []
    for l in range(DEPTH):
        for d in range(N_DEV):
            specs.append(pl.BlockSpec((None, None, tr, c), lambda i, j, _d=d, _l=l: (_d, jnp.clip(i - _l * per, 0, per - 1), j, 0)))
            ins.append(s3[l])
    spec = pl.BlockSpec((None, tr, c), lambda i, j: (i, j, 0))
    res = run_fwd(name, fn, (a, r // tr), ins + [w3, m3, v3], specs + [spec] * 3, [_out((a, r, c), spec)] * 4)[0]
    return [t.reshape(shape) for t in res]


def kernel(*args):
    nw = len(W_NAMES)
    x, p = args[0], args[1]
    w = dict(zip(W_NAMES, args[2:2 + nw]))
    target = args[2 + nw]
    m = dict(zip(W_NAMES, args[3 + nw:3 + 2 * nw]))
    v = dict(zip(W_NAMES, args[3 + 2 * nw:3 + 3 * nw]))

    gathered = all_gather("ag_weights", [w[k].astype(bf16) for k in BIG_NAMES] + [w[k] for k in CONV_NAMES])
    full = {k: _gathered_to_full(k, g) for k, g in zip(SHARDED, gathered)}

    bigs = [big_layout({k: full[k][i] for k in BIG_NAMES}) for i in range(DEPTH)]
    smalls = []
    for i in range(DEPTH):
        sm = {k: w[k][i] for k in _SMALL_KEYS if k not in CONV_NAMES}
        sm.update({k: full[k][i] for k in CONV_NAMES})
        smalls.append(sm)
    loss, gx, gbs, gss, gfn, rode = _local_step(x[0], p[:, 0], target[0], bigs, smalls, w["final_norm"],
                                                lambda gb, gs: _scattered(SHARDED, gb, gs))
    loss = lax.psum(loss[0, 0], MESH_AXES)

    rode[0].update(zip(RIDE_LAST, exchange("rs_last", _scattered(RIDE_LAST, gbs[0], gss[0]))))
    rode[0] = [rode[0][k] for k in SHARDED]
    gfull = [dict(big_unlayout(gbs[i]), **gss[i]) for i in range(DEPTH)]
    stack = lambda k: jnp.stack([gfull[i][k] for i in range(DEPTH)])
    outs = {}
    kinds = ("grad", "delta", "new_m", "new_v")
    for j, k in enumerate(SHARDED):
        for kind, a in zip(kinds, reduce_adamw("adamw_" + k, [rode[i][j] for i in range(DEPTH)], w[k], m[k], v[k])):
            outs[kind + "_" + k] = a

    shapes = [w[k].shape for k in REPLICATED]
    g_rep = sum_slots("sum_replicated", all_gather("ag_replicated", [_pack([gfn if k == "final_norm" else stack(k) for k in REPLICATED])])[0])
    res = adamw("adamw_replicated", _pack([w[k] for k in REPLICATED]), g_rep, _pack([m[k] for k in REPLICATED]), _pack([v[k] for k in REPLICATED]))
    for kind, buf in zip(kinds, [g_rep] + list(res)):
        for k, a in zip(REPLICATED, _unpack(buf, shapes)):
            outs[kind + "_" + k] = a
    return (loss, gx[None]) + tuple(outs[kind + "_" + k] for kind in kinds for k in W_NAMES)
```

```python
import functools
import math

import jax
import jax.numpy as jnp
from jax import lax
from jax.experimental import pallas as pl
from jax.experimental.pallas import tpu as pltpu

f32 = jnp.float32
bf16 = jnp.bfloat16

EPS = 1e-6
DEPTH = 2
D_MODEL = 1024
FFN_DIM = 2816
BW = 512
IN_WIDTH = 5136
PW = 5376
SMALL_OFF = 5120
CHUNK = 64
LS = 256
LRU_C = 8.0
N_DEV = 8
VMEM_LIMIT_BYTES = 56 * 1024 * 1024

ADAM_LR, ADAM_B1, ADAM_B2, ADAM_EPS, ADAM_WD, ADAM_STEP = 0.001, 0.9, 0.999, 1e-08, 0.01, 10


_NN, _NT, _TN = ((1,), (0,)), ((1,), (1,)), ((0,), (0,))


def _pieces(x, n):
    parts, r = [], x
    for i in range(n):
        p = r.astype(bf16)
        parts.append(p)
        if i + 1 < n:
            r = r - p.astype(f32)
    return parts


def _dg(a, b, dims, mode):
    sa, sb = mode
    pa, pb = _pieces(a, sa), _pieces(b, sb)
    out = None
    for i in reversed(range(sa)):
        for j in reversed(range(sb)):
            if i + j < max(sa, sb):
                d = lax.dot_general(pa[i], pb[j], (dims, ((), ())), preferred_element_type=f32)
                out = d if out is None else out + d
    return out


def _make_mm(mode):
    sa, sb = mode
    cot = lambda s_other: 1 if mode == (1, 1) else (3 if s_other == 1 else 2)
    m_g_b, m_a_g, m_g_a, m_b_g = (cot(sb), sb), (sa, cot(sa)), (cot(sa), sa), (sb, cot(sb))

    @jax.custom_vjp
    def nn(a, b):
        return _dg(a, b, _NN, mode)

    @jax.custom_vjp
    def nt(a, b):
        return _dg(a, b, _NT, mode)

    @jax.custom_vjp
    def tn(a, b):
        return _dg(a, b, _TN, mode)

    nn.defvjp(lambda a, b: (nn(a, b), (a, b)), lambda r, g: (_dg(g, r[1], _NT, m_g_b), _dg(r[0], g, _TN, m_a_g)))
    nt.defvjp(lambda a, b: (nt(a, b), (a, b)), lambda r, g: (_dg(g, r[1], _NN, m_g_b), _dg(g, r[0], _TN, m_g_a)))
    tn.defvjp(lambda a, b: (tn(a, b), (a, b)), lambda r, g: (_dg(r[1], g, _NT, m_b_g), _dg(r[0], g, _NN, m_a_g)))
    return nn, nt, tn


mm, mm_nt, mm_tn = _make_mm((1, 1))
xmm, xmm_nt, xmm_tn = _make_mm((2, 2))
lmm, lmm_nt, lmm_tn = _make_mm((1, 3))
rmm, rmm_nt, rmm_tn = _make_mm((3, 1))


@jax.custom_vjp
def _tri_inv(m):
    n = m.shape[0]
    eye = (_rows((n, n)) == _lanes((n, n))).astype(f32)
    blk = (_rows((n, n)) // 16) == (_lanes((n, n)) // 16)
    x = lambda a, b: _dg(a, b, _NN, (2, 2))
    nb = jnp.where(blk, m, 0.0)
    p = -nb
    t = eye + p
    for _ in range(3):
        p = x(p, p)
        t = t + x(t, p)
    q = x(t, m - nb)
    imq = eye - q
    return x(imq + x(imq, x(q, q)), t)


def _tri_inv_fwd(m):
    t = _tri_inv(m)
    return t, t


def _tri_inv_bwd(t, g):
    return (-_dg(_dg(t, g, _TN, (2, 2)), t, _NT, (2, 2)),)


_tri_inv.defvjp(_tri_inv_fwd, _tri_inv_bwd)


def _rows(shape):
    return lax.broadcasted_iota(jnp.int32, shape, 0)


def _lanes(shape):
    return lax.broadcasted_iota(jnp.int32, shape, 1)


def _rms(x, g):
    return x * lax.rsqrt(jnp.mean(x * x, axis=-1, keepdims=True) + EPS) * g


def _silu(x):
    return x * jax.nn.sigmoid(x)


def _gelu(x):
    return 0.5 * x * (1.0 + jnp.tanh(0.7978845608028654 * (x + 0.044715 * x * x * x)))


def _softplus(x):
    return jnp.maximum(x, 0.0) + jnp.log1p(jnp.exp(-jnp.abs(x)))


def _expm1(x):
    p = x * (1.0 + x * (0.5 + x * (1.0 / 6 + x * (1.0 / 24 + x * (1.0 / 120 + x * (1.0 / 720 + x * (1.0 / 5040)))))))
    return jnp.where(x > -0.3, p, jnp.exp(x) - 1.0)


def _pick_row(x, r):
    return jnp.sum(jnp.where(_rows(x.shape) == r, x, 0.0), axis=0, keepdims=True)


def _pick_lane(x, c):
    return jnp.sum(jnp.where(_lanes(x.shape) == c, x, 0.0), axis=1, keepdims=True)


def _shift_up(g, j):
    n = g.shape[0]
    return jnp.where(_rows(g.shape) < n - j, pltpu.roll(g, n - j, 0), 0.0)


@functools.partial(jax.custom_vjp, nondiff_argnums=(1, 2))
def _shift(x, j, fill):
    return jnp.where(_rows(x.shape) >= j, pltpu.roll(x, j, 0), fill)


_shift.defvjp(lambda x, j, fill: (_shift(x, j, fill), None), lambda j, fill, _, g: (_shift_up(g, j),))


@functools.partial(jax.custom_vjp, nondiff_argnums=(2,))
def _shift_halo(x, prev8, j):
    xr = pltpu.roll(x, j, 0)
    pr = pltpu.roll(prev8, j, 0)
    top = jnp.where(_rows(pr.shape) < j, pr, xr[:8])
    return jnp.concatenate([top, xr[8:]], axis=0)


def _shift_halo_bwd(j, _, g):
    g8 = g[:8]
    dprev = jnp.where(_rows(g8.shape) >= 8 - j, pltpu.roll(g8, 8 - j, 0), 0.0)
    return _shift_up(g, j), dprev


_shift_halo.defvjp(lambda x, p, j: (_shift_halo(x, p, j), None), _shift_halo_bwd)


def _conv4(x, prev8, w, b):
    y = _pick_row(w, 3) * x
    for k in range(3):
        y = y + _pick_row(w, k) * _shift_halo(x, prev8, 3 - k)
    return y if b is None else y + b


def _cmul(ar, ai, br, bi):
    return ar * br - ai * bi, ar * bi + ai * br


_ANY = pl.BlockSpec(memory_space=pl.ANY)


def _params(grid):
    return pltpu.CompilerParams(dimension_semantics=("arbitrary",) * len(grid), vmem_limit_bytes=VMEM_LIMIT_BYTES)


def _first(axes):
    ok = pl.program_id(axes[0]) == 0
    for a in axes[1:]:
        ok = jnp.logical_and(ok, pl.program_id(a) == 0)
    return ok


def _store(ref, val, acc):
    val = val.astype(ref.dtype)
    if acc is None:
        ref[...] = val
        return
    first = _first(acc)

    @pl.when(first)
    def _():
        ref[...] = val

    @pl.when(jnp.logical_not(first))
    def _():
        ref[...] += val


def _full(a):
    nd = a.ndim
    return pl.BlockSpec(a.shape, lambda *g: (0,) * nd)


def _out(shape, spec, acc=None, dtype=f32):
    return dict(shape=tuple(shape), spec=spec, acc=acc, dtype=dtype)


def _grid_step(grid):
    step = pl.program_id(0)
    for ax in range(1, len(grid)):
        step = step * grid[ax] + pl.program_id(ax)
    return step


def run_fwd(name, fn, grid, ins, in_specs, outs, carry=None, ride=None):
    n_in, n_out = len(ins), len(outs)
    cshapes = carry["shapes"] if carry else []
    nc = len(cshapes)
    ng = len(grid)
    n_r = len(ride) if ride else 0

    def body(*refs):
        in_refs = refs[:n_in]
        out_refs = refs[n_in + n_r:n_in + n_r + n_out]
        save_refs = refs[n_in + n_r + n_out:n_in + n_r + n_out + nc]
        c_refs = refs[n_in + 2 * n_r + n_out + nc:n_in + 2 * n_r + n_out + 2 * nc]
        if n_r:
            step = _grid_step(grid)
            r_args = (refs[n_in:n_in + n_r], refs[n_in + n_r + n_out + nc:n_in + 2 * n_r + n_out + nc]) \
                + tuple(refs[n_in + 2 * n_r + n_out + 2 * nc:])
            pl.when(step == 0)(lambda: _exchange_start(*r_args))
        vals = [r[...] for r in in_refs]
        if carry:
            @pl.when(pl.program_id(carry["axis"]) == 0)
            def _():
                for c in c_refs:
                    c[...] = jnp.zeros(c.shape, f32)
            cin = tuple(c[...] for c in c_refs)
            for s, v in zip(save_refs, cin):
                s[...] = v
            cout, res = fn(cin, *vals)
            for c, v in zip(c_refs, cout):
                c[...] = v
        else:
            res = fn(*vals)
        for o, r, d in zip(out_refs, res, outs):
            _store(o, r, d["acc"])
        if n_r:
            pl.when(step == math.prod(grid) - 1)(lambda: _exchange_wait(*r_args))

    out_shape = [jax.ShapeDtypeStruct(d["shape"], d["dtype"]) for d in outs]
    out_specs = [d["spec"] for d in outs]
    for cs in cshapes:
        out_shape.append(jax.ShapeDtypeStruct(tuple(grid) + tuple(cs), f32))
        out_specs.append(pl.BlockSpec((None,) * ng + tuple(cs), lambda *g, _n=len(cs): tuple(g) + (0,) * _n))
    if n_r:
        out_shape += [jax.ShapeDtypeStruct(g.shape, g.dtype) for g in ride]
        out_specs += [_ANY] * n_r
    res = pl.pallas_call(
        body, name=name, grid=grid, in_specs=list(in_specs) + [_ANY] * n_r, out_specs=out_specs, out_shape=out_shape,
        scratch_shapes=[pltpu.VMEM(tuple(cs), f32) for cs in cshapes] + (_exchange_sems(n_r) if n_r else []),
        compiler_params=_params(grid),
    )(*ins, *(ride or []))
    if n_r:
        return list(res[:n_out]), list(res[n_out:n_out + nc]), list(res[n_out + nc:])
    return list(res[:n_out]), list(res[n_out:])


def run_bwd(name, fn, grid, ins, in_specs, gouts, gout_specs, wants, carry=None, into=None, ride=None):
    n_in, n_g, n_w = len(ins), len(gouts), len(wants)
    n_a = 0 if into is None else 1
    ride = list(ride or [])
    n_r = len(ride)
    saved = carry["saved"] if carry else []
    nc = len(saved)
    ng = len(grid)

    def body(*refs):
        in_refs = refs[:n_in]
        g_refs = refs[n_in:n_in + n_g]
        s_refs = refs[n_in + n_g:n_in + n_g + nc]
        base = n_in + n_g + nc + n_a
        r_in = refs[base:base + n_r]
        w_refs = refs[base + n_r:base + n_r + n_w]
        r_out = refs[base + n_r + n_w:base + 2 * n_r + n_w]
        dc_refs = refs[base + 2 * n_r + n_w:base + 2 * n_r + n_w + nc]
        if n_r:
            steps = math.prod(grid)
            step = pl.program_id(0)
            for ax in range(1, ng):
                step = step * grid[ax] + pl.program_id(ax)
            r_args = (r_in, r_out) + tuple(refs[base + 2 * n_r + n_w + nc:])
            pl.when(step == 0)(lambda: _exchange_start(*r_args))
        vals = [r[...].astype(f32) for r in in_refs]
        gs = tuple(r[...].astype(f32) for r in g_refs)
        if carry:
            @pl.when(pl.program_id(carry["axis"]) == 0)
            def _():
                for c in dc_refs:
                    c[...] = jnp.zeros(c.shape, f32)
            cin = tuple(s[...] for s in s_refs)
            _, vjp = jax.vjp(fn, cin, *vals)
            grads = vjp((tuple(c[...] for c in dc_refs), gs))
            for c, v in zip(dc_refs, grads[0]):
                c[...] = v
            dvals = grads[1:]
        else:
            _, vjp = jax.vjp(fn, *vals)
            dvals = vjp(gs)
        for o, d in zip(w_refs, wants):
            idx = d["idx"]
            val = dvals[idx] if isinstance(idx, int) else jnp.concatenate([dvals[j] for j in idx], axis=1)
            _store(o, val, d["acc"])
        if n_r:
            pl.when(step == steps - 1)(lambda: _exchange_wait(*r_args))

    rev = carry["rev"] if carry else None
    s_specs = []
    for a in saved:
        n = a.ndim - ng
        s_specs.append(pl.BlockSpec((None,) * ng + tuple(a.shape[ng:]), lambda *g, _n=n: tuple(rev(g)) + (0,) * _n))
    res = pl.pallas_call(
        body, name=name, grid=grid, in_specs=list(in_specs) + list(gout_specs) + s_specs + [_ANY] * (n_a + n_r),
        out_specs=[d["spec"] for d in wants] + [_ANY] * n_r,
        out_shape=[jax.ShapeDtypeStruct(d["shape"], d["dtype"]) for d in wants] + [jax.ShapeDtypeStruct(g.shape, g.dtype) for g in ride],
        input_output_aliases={n_in + n_g + nc: 0} if n_a else {},
        scratch_shapes=[pltpu.VMEM(tuple(a.shape[ng:]), f32) for a in saved] + (_exchange_sems(n_r) if n_r else []),
        compiler_params=_params(grid),
    )(*ins, *gouts, *saved, *([into] if n_a else []), *ride)
    return list(res)


def _want(idx, shape, spec, acc=None, dtype=f32):
    d = _out(shape, spec, acc, dtype)
    d["idx"] = idx
    return d


def addn(name, items, t):
    s, w = items[0][0].shape[-2:]
    specs = []
    for a, j in items:
        if j is None:
            specs.append(pl.BlockSpec((t, w), lambda i: (i, 0)))
        else:
            specs.append(pl.BlockSpec((None, t, w), lambda i, _j=j: (_j, i, 0)))

    def fn(*xs):
        y = xs[0]
        for x in xs[1:]:
            y = y + x
        return (y,)

    return run_fwd(name, fn, (s // t,), [a for a, _ in items], specs,
                   [_out((s, w), pl.BlockSpec((t, w), lambda i: (i, 0)))])[0][0]


def _parts(a):
    return [(a, j) for j in range(a.shape[0])]


def _ffn_core(n, wg, wu, wo):
    return (0.5 * mm(_silu(mm(n, wg)) * mm(n, wu), wo),)


def ffn_fwd(name, h, g, wg, wu, wo, t=512, tf=1408):
    s, d = h.shape
    f = wg.shape[1]

    def fn(hh, gg, a, b, c):
        n = _rms(hh, gg)
        return _ffn_core(n, a, b, c)[0] + (pl.program_id(1) == 0).astype(f32) * hh, n

    specs = [pl.BlockSpec((t, d), lambda i, j: (i, 0)), _full(g), pl.BlockSpec((d, tf), lambda i, j: (0, j)),
             pl.BlockSpec((d, tf), lambda i, j: (0, j)), pl.BlockSpec((tf, d), lambda i, j: (j, 0))]
    outs = [_out((s, d), pl.BlockSpec((t, d), lambda i, j: (i, 0)), acc=(1,)),
            _out((s, d), pl.BlockSpec((t, d), lambda i, j: (i, 0)), dtype=bf16)]
    return run_fwd(name, fn, (s // t, f // tf), [h, g, wg, wu, wo], specs, outs)[0]


def ffn_bwd(name, n, wg, wu, wo, gout, t=512, tf=256, ride=None):
    s, d = n.shape
    f = wg.shape[1]
    nj = f // tf
    specs = [pl.BlockSpec((t, d), lambda j, i: (i, 0)), pl.BlockSpec((d, tf), lambda j, i: (0, j)),
             pl.BlockSpec((d, tf), lambda j, i: (0, j)), pl.BlockSpec((tf, d), lambda j, i: (j, 0))]
    wants = [_want(0, (nj, s, d), pl.BlockSpec((None, t, d), lambda j, i: (j, i, 0)), dtype=bf16),
             _want(1, wg.shape, pl.BlockSpec((d, tf), lambda j, i: (0, j)), acc=(1,)),
             _want(2, wu.shape, pl.BlockSpec((d, tf), lambda j, i: (0, j)), acc=(1,)),
             _want(3, wo.shape, pl.BlockSpec((tf, d), lambda j, i: (j, 0)), acc=(1,))]
    return run_bwd(name, _ffn_core, (nj, s // t), [n, wg, wu, wo], specs,
                   [gout], [pl.BlockSpec((t, d), lambda j, i: (i, 0))], wants, ride=ride)


def norm_bwd(name, h, g, base, parts, t=256):
    s, d = h.shape

    def fn(hh, gg, bb, *ps):
        dn = ps[0].astype(f32)
        for p in ps[1:]:
            dn = dn + p.astype(f32)
        _, vjp = jax.vjp(_rms, hh, gg)
        dh, dg = vjp(dn)
        return bb + dh, dg

    row = pl.BlockSpec((t, d), lambda i: (i, 0))
    specs = [row, _full(g), row] + [pl.BlockSpec((None, t, d), lambda i, _j=j: (_j, i, 0)) for _, j in parts]
    outs = [_out((s, d), row), _out(g.shape, _full(g), acc=(0,))]
    return run_fwd(name, fn, (s // t,), [h, g, base] + [a for a, _ in parts], specs, outs)[0]


def _lin_tile(u, w, b):
    return (mm(u, w) + b,)


def normlin_fwd(name, h, g, w, b, t=512, tn=1024):
    s, d = h.shape
    n = w.shape[1]

    def fn(hh, gg, ww, bb):
        u = _rms(hh, gg)
        return mm(u, ww) + bb, u

    specs = [pl.BlockSpec((t, d), lambda i, j: (i, 0)), _full(g), pl.BlockSpec((d, tn), lambda i, j: (0, j)),
             pl.BlockSpec((1, tn), lambda i, j: (0, j))]
    outs = [_out((s, n), pl.BlockSpec((t, tn), lambda i, j: (i, j))),
            _out((s, d), pl.BlockSpec((t, d), lambda i, j: (i, 0)), dtype=bf16)]
    return run_fwd(name, fn, (s // t, n // tn), [h, g, w, b], specs, outs)[0]


def lin_fwd(name, u, w, b, t=512, tn=1024):
    s, d = u.shape
    n = w.shape[1]
    specs = [pl.BlockSpec((t, d), lambda i, j: (i, 0)), pl.BlockSpec((d, tn), lambda i, j: (0, j)),
             pl.BlockSpec((1, tn), lambda i, j: (0, j))]
    outs = [_out((s, n), pl.BlockSpec((t, tn), lambda i, j: (i, j)))]
    return run_fwd(name, _lin_tile, (s // t, n // tn), [u, w, b], specs, outs)[0][0]


def lin_bwd(name, u, w, b, gout, t=256, tn=1024):
    s, d = u.shape
    n = w.shape[1]
    nj = n // tn
    specs = [pl.BlockSpec((t, d), lambda j, i: (i, 0)), pl.BlockSpec((d, tn), lambda j, i: (0, j)),
             pl.BlockSpec((1, tn), lambda j, i: (0, j))]
    wants = [_want(0, (nj, s, d), pl.BlockSpec((None, t, d), lambda j, i: (j, i, 0)), dtype=bf16),
             _want(1, w.shape, pl.BlockSpec((d, tn), lambda j, i: (0, j)), acc=(1,)),
             _want(2, b.shape, pl.BlockSpec((1, tn), lambda j, i: (0, j)), acc=(1,))]
    return run_bwd(name, _lin_tile, (nj, s // t), [u, w, b], specs,
                   [gout], [pl.BlockSpec((t, tn), lambda j, i: (i, j))], wants)


def _merge_tile(n_axis, residual):
    def fn(h, y, gp, wb, wo):
        part = mm(jax.nn.sigmoid(gp) * mm(y, wb), wo)
        if residual:
            part = part + (pl.program_id(n_axis) == 0).astype(f32) * h
        return (part,)
    return fn


def merge_fwd(name, h, ys, gpre, wb, wo, t=512):
    s, d = h.shape
    specs = [pl.BlockSpec((t, d), lambda i, n: (i, 0)), pl.BlockSpec((t, BW), lambda i, n: (i, n)),
             pl.BlockSpec((t, d), lambda i, n: (i, n)), pl.BlockSpec((None, BW, d), lambda i, n: (n, 0, 0)), _full(wo)]
    outs = [_out((s, d), pl.BlockSpec((t, d), lambda i, n: (i, 0)), acc=(1,))]
    return run_fwd(name, _merge_tile(1, True), (s // t, 4), [h, ys, gpre, wb, wo], specs, outs)[0][0]


def merge_bwd(name, h, ys, gpre, wb, wo, gout, t=256):
    s, d = h.shape
    specs = [pl.BlockSpec((t, d), lambda n, i: (i, 0)), pl.BlockSpec((t, BW), lambda n, i: (i, n)),
             pl.BlockSpec((t, d), lambda n, i: (i, n)), pl.BlockSpec((None, BW, d), lambda n, i: (n, 0, 0)), _full(wo)]
    wants = [_want(1, ys.shape, pl.BlockSpec((t, BW), lambda n, i: (i, n))),
             _want(2, gpre.shape, pl.BlockSpec((t, d), lambda n, i: (i, n))),
             _want(3, wb.shape, pl.BlockSpec((None, BW, d), lambda n, i: (n, 0, 0)), acc=(1,)),
             _want(4, wo.shape, _full(wo), acc=(0, 1))]
    return run_bwd(name, _merge_tile(0, False), (4, s // t), [h, ys, gpre, wb, wo], specs,
                   [gout], [pl.BlockSpec((t, d), lambda n, i: (i, 0))], wants)


def _ple_tile(residual):
    def fn(h, pe, g, wgate, wproj):
        y = jax.nn.sigmoid(mm(_rms(h, g), wgate)) * mm(pe, wproj)
        return (y + h,) if residual else (y,)
    return fn


def ple_fwd(name, h, pe, g, wgate, wproj, t=512):
    s, d = h.shape
    specs = [pl.BlockSpec((t, d), lambda i: (i, 0)), pl.BlockSpec((t, pe.shape[1]), lambda i: (i, 0)),
             _full(g), _full(wgate), _full(wproj)]
    return run_fwd(name, _ple_tile(True), (s // t,), [h, pe, g, wgate, wproj], specs,
                   [_out((s, d), pl.BlockSpec((t, d), lambda i: (i, 0)))])[0][0]


def ple_bwd(name, h, pe, g, wgate, wproj, gout, t=256):
    s, d = h.shape
    specs = [pl.BlockSpec((t, d), lambda i: (i, 0)), pl.BlockSpec((t, pe.shape[1]), lambda i: (i, 0)),
             _full(g), _full(wgate), _full(wproj)]
    wants = [_want(0, h.shape, pl.BlockSpec((t, d), lambda i: (i, 0))), _want(2, g.shape, _full(g), acc=(0,)),
             _want(3, wgate.shape, _full(wgate), acc=(0,)), _want(4, wproj.shape, _full(wproj), acc=(0,))]
    return run_bwd(name, _ple_tile(True), (s // t,), [h, pe, g, wgate, wproj], specs,
                   [gout], [pl.BlockSpec((t, d), lambda i: (i, 0))], wants)


def final_loss(name, h, g, target, t=512):
    s, d = h.shape

    def fn(hh, gg, tt):
        def loss_fn(a, b):
            err = _rms(a, b) - tt
            return 0.5 * jnp.sum(jnp.mean(err * err, axis=-1, keepdims=True), axis=0, keepdims=True)
        loss, vjp = jax.vjp(loss_fn, hh, gg)
        dh, dgain = vjp(jnp.ones((1, 1), f32))
        return loss, dh, dgain

    specs = [pl.BlockSpec((t, d), lambda i: (i, 0)), _full(g), pl.BlockSpec((t, d), lambda i: (i, 0))]
    outs = [_out((1, 1), pl.BlockSpec((1, 1), lambda i: (0, 0)), acc=(0,)),
            _out((s, d), pl.BlockSpec((t, d), lambda i: (i, 0))), _out(g.shape, _full(g), acc=(0,))]
    return run_fwd(name, fn, (s // t,), [h, g, target], specs, outs)[0]


def _rev(nc, rev):
    return (lambda c: nc - 1 - c) if rev else (lambda c: c)


def _s5_ops_tile(are, aim, lstep, bre, bim):
    step = jnp.exp(lstep)
    mag = jnp.exp(are * step)
    ab_re, ab_im = mag * jnp.cos(aim * step), mag * jnp.sin(aim * step)
    den = are * are + aim * aim
    num_re = ab_re - 1.0
    f_re = (num_re * are + ab_im * aim) / den
    f_im = (ab_im * are - num_re * aim) / den
    bb_re = f_re * bre - f_im * bim
    bb_im = f_re * bim + f_im * bre
    pr = jnp.broadcast_to(ab_re, (LS, ab_re.shape[1]))
    pi = jnp.broadcast_to(ab_im, (LS, ab_im.shape[1]))
    k = 1
    while k < LS:
        pr, pi = _cmul(pr, pi, _shift(pr, k, 1.0), _shift(pi, k, 0.0))
        k *= 2
    return ab_re, ab_im, bb_re, bb_im, pr, pi


def _s5_ops_specs(arrs):
    return [pl.BlockSpec((None,) + a.shape[1:], lambda gb: (gb, 0, 0)) for a in arrs]


def s5_ops_fwd(name, raw):
    shapes = [(4, 1, 512), (4, 1, 512), (4, 128, 512), (4, 128, 512), (4, LS, 512), (4, LS, 512)]
    outs = [_out(sh, pl.BlockSpec((None,) + sh[1:], lambda gb: (gb, 0, 0))) for sh in shapes]
    return run_fwd(name, _s5_ops_tile, (4,), raw, _s5_ops_specs(raw), outs)[0]


def s5_ops_bwd(name, raw, gops):
    wants = [_want(i, a.shape, pl.BlockSpec((None,) + a.shape[1:], lambda gb: (gb, 0, 0))) for i, a in enumerate(raw)]
    return run_bwd(name, _s5_ops_tile, (4,), raw, _s5_ops_specs(raw), gops, _s5_ops_specs(gops), wants)


def _lti_scan_raw(xr, xi, pr, pi, up):
    n = xr.shape[0]
    move = (lambda a, k: _shift_up(a, k)) if up else (lambda a, k: jnp.where(_rows(a.shape) >= k, pltpu.roll(a, k, 0), 0.0))
    k = 1
    while k < n:
        sr, si = _cmul(pr, pi, move(xr, k), move(xi, k))
        xr, xi = xr + sr, xi + si
        pr, pi = _cmul(pr, pi, pr, pi)
        k *= 2
    return xr, xi


@jax.custom_vjp
def _lti_scan(xr, xi, pr, pi):
    return _lti_scan_raw(xr, xi, pr, pi, False)


def _lti_scan_fwd(xr, xi, pr, pi):
    sr, si = _lti_scan_raw(xr, xi, pr, pi, False)
    return (sr, si), (sr, si, pr, pi)


def _lti_scan_bwd(res, g):
    sr, si, pr, pi = res
    mr, mi = _lti_scan_raw(g[0], g[1], pr, -pi, True)
    qr = jnp.where(_rows(sr.shape) >= 1, pltpu.roll(sr, 1, 0), 0.0)
    qi = jnp.where(_rows(si.shape) >= 1, pltpu.roll(si, 1, 0), 0.0)
    dpr = jnp.sum(qr * mr + qi * mi, axis=0, keepdims=True)
    dpi = jnp.sum(qr * mi - qi * mr, axis=0, keepdims=True)
    return mr, mi, dpr, dpi


_lti_scan.defvjp(_lti_scan_fwd, _lti_scan_bwd)


def _s5_tile(carry, u, ab_re, ab_im, bb_re, bb_im, pw_re, pw_im, c_re, c_im, dskip):
    h_re, h_im = carry
    xr, xi = _lti_scan(mm(u, bb_re), mm(u, bb_im), ab_re, ab_im)
    cr, ci = _cmul(pw_re, pw_im, h_re, h_im)
    xr, xi = xr + cr, xi + ci
    y = mm(xr, c_re) - mm(xi, c_im) + dskip * u
    return (_pick_row(xr, LS - 1), _pick_row(xi, LS - 1)), (y,)


def _s5_io(u, ops, c_re, c_im, dskip, nc, rev):
    cm = _rev(nc, rev)
    ins = [u] + list(ops) + [c_re, c_im, dskip]
    specs = [pl.BlockSpec((LS, 128), lambda gb, c: (cm(c), 36 + gb))]
    specs += [pl.BlockSpec((None,) + a.shape[1:], lambda gb, c: (gb, 0, 0)) for a in list(ops) + [c_re, c_im]]
    specs += [pl.BlockSpec((1, 128), lambda gb, c: (0, gb))]
    return ins, specs, cm


def s5_fwd(name, proj, ops, c_re, c_im, dskip, ride=None):
    s = proj.shape[0]
    nc = s // LS
    ins, specs, cm = _s5_io(proj, ops, c_re, c_im, dskip, nc, False)
    outs = [_out((s, BW), pl.BlockSpec((LS, 128), lambda gb, c: (c, gb)))]
    res = run_fwd(name, _s5_tile, (4, nc), ins, specs, outs, carry=dict(shapes=[(1, 512), (1, 512)], axis=1), ride=ride)
    return (res[0][0],) + tuple(res[1:])


def s5_bwd(name, proj, ops, c_re, c_im, dskip, saved, gy, gproj, ride=None):
    s = proj.shape[0]
    nc = s // LS
    ins, specs, cm = _s5_io(proj, ops, c_re, c_im, dskip, nc, True)
    wants = [_want(0, (s, PW), pl.BlockSpec((LS, 128), lambda gb, c: (cm(c), 36 + gb)))]
    for i, a in enumerate(list(ops) + [c_re, c_im]):
        wants.append(_want(1 + i, a.shape, pl.BlockSpec((None,) + a.shape[1:], lambda gb, c: (gb, 0, 0)), acc=(1,)))
    wants.append(_want(9, dskip.shape, pl.BlockSpec((1, 128), lambda gb, c: (0, gb)), acc=(1,)))
    return run_bwd(name, _s5_tile, (4, nc), ins, specs, [gy], [pl.BlockSpec((LS, 128), lambda gb, c: (cm(c), gb))],
                   wants, carry=dict(axis=1, saved=saved, rev=lambda g: (g[0], cm(g[1]))), into=gproj, ride=ride)


def _s5_glu_tile(y, w, b):
    z = _gelu(y)
    return (z * jax.nn.sigmoid(mm(z, w) + b),)


def s5_glu_fwd(name, y, w, b, t=512):
    s = y.shape[0]
    spec = pl.BlockSpec((t, BW), lambda i: (i, 0))
    return run_fwd(name, _s5_glu_tile, (s // t,), [y, w, b], [spec, _full(w), _full(b)], [_out((s, BW), spec)])[0][0]


def s5_glu_bwd(name, y, w, b, gout, gout_col, t=512):
    s = y.shape[0]
    spec = pl.BlockSpec((t, BW), lambda i: (i, 0))
    wants = [_want(0, y.shape, spec), _want(1, w.shape, _full(w), acc=(0,)), _want(2, b.shape, _full(b), acc=(0,))]
    return run_bwd(name, _s5_glu_tile, (s // t,), [y, w, b], [spec, _full(w), _full(b)], [gout],
                   [pl.BlockSpec((t, BW), lambda i: (i, gout_col))], wants)


def _lru_tile(carry, xb, gate, cw, cb, wr, br, wi, bi, lam):
    h_in, prev8 = carry
    xc = _conv4(xb, prev8, cw, cb)
    r = jax.nn.sigmoid(mm(xc, wr) + br)
    ig = jax.nn.sigmoid(mm(xc, wi) + bi)
    log_a = -LRU_C * r * _softplus(-lam)
    a = jnp.exp(log_a)
    b = jnp.sqrt(-_expm1(2.0 * log_a)) * (ig * xc)
    k = 1
    while k < LS:
        b = b + a * _shift(b, k, 0.0)
        a = a * _shift(a, k, 1.0)
        k *= 2
    h = b + a * h_in
    return (_pick_row(h, LS - 1), xb[LS - 8:, :]), (h * _gelu(gate),)


def _lru_io(proj, ws, nc, rev):
    cm = _rev(nc, rev)
    ins = [proj, proj] + list(ws)
    specs = [pl.BlockSpec((LS, BW), lambda c: (cm(c), 4)), pl.BlockSpec((LS, BW), lambda c: (cm(c), 5))]
    specs += [_full(a) for a in ws]
    return ins, specs, cm


def lru_fwd(name, proj, ws):
    s = proj.shape[0]
    nc = s // LS
    ins, specs, cm = _lru_io(proj, ws, nc, False)
    outs = [_out((s, BW), pl.BlockSpec((LS, BW), lambda c: (c, 0)))]
    (y,), saved = run_fwd(name, _lru_tile, (nc,), ins, specs, outs, carry=dict(shapes=[(1, BW), (8, BW)], axis=0))
    return y, saved


def lru_bwd(name, proj, ws, saved, gy, gy_col, gproj):
    s = proj.shape[0]
    nc = s // LS
    ins, specs, cm = _lru_io(proj, ws, nc, True)
    wants = [_want((0, 1), (s, PW), pl.BlockSpec((LS, 2 * BW), lambda c: (cm(c), 2)))]
    wants += [_want(2 + i, a.shape, _full(a), acc=(0,)) for i, a in enumerate(ws)]
    return run_bwd(name, _lru_tile, (nc,), ins, specs, [gy], [pl.BlockSpec((LS, BW), lambda c: (cm(c), gy_col))], wants,
                   carry=dict(axis=0, saved=saved, rev=lambda g: (cm(g[0]),)), into=gproj)


def _causal(n):
    return _rows((n, n)) >= _lanes((n, n))


def _decay(col, rowv):
    causal = _causal(col.shape[0])
    return jnp.where(causal, jnp.exp(jnp.where(causal, col - rowv, 0.0)), 0.0)


def _m2_tile(carry, z, xs_raw, b_raw, c_raw, small, cwx, cwb, cwc, cbx, cbb, cbc, dtb, alog, dsk, ng):
    state, px, pb, pc = carry
    n = CHUNK
    xs = _silu(_conv4(xs_raw, px, cwx, cbx))
    bm = _silu(_conv4(b_raw, pb, cwb, cbb))
    cmx = _silu(_conv4(c_raw, pc, cwc, cbc))
    expand = (_lanes((16, BW)) // 64 == _rows((16, BW))).astype(f32)
    tri = _causal(n).astype(f32)
    triu = (_rows((n, n)) <= _lanes((n, n))).astype(f32)
    dt = _softplus(small + dtb)
    da = dt * (-jnp.exp(alog))
    cs = lmm(tri, da)
    cs_t = rmm_tn(da, triu)
    cs_w = rmm(cs, expand)
    last_w = _pick_row(cs_w, n - 1)
    xdt = xs * rmm(dt, expand)
    g0 = _lanes((n, BW)) < 256
    bm0, bm1, cm0, cm1 = bm[:, :128], bm[:, 128:], cmx[:, :128], cmx[:, 128:]
    cb0, cb1 = mm_nt(cm0, bm0), mm_nt(cm1, bm1)
    y = jnp.where(g0, mm(cm0, state), mm(cm1, state)) * jnp.exp(cs_w)
    for h in range(8):
        sc = (cb0 if h < 4 else cb1) * _decay(_pick_lane(cs, h), _pick_row(cs_t, h))
        y = y + jnp.where(_lanes((n, BW)) // 64 == h, mm(sc, xdt), 0.0)
    xd = xdt * jnp.exp(last_w - cs_w)
    g0s = _lanes((128, BW)) < 256
    state_out = state * jnp.exp(last_w) + jnp.where(g0s, mm_tn(bm0, xd), mm_tn(bm1, xd))
    y = (y + dsk * xs) * _silu(z)
    return (state_out, xs_raw[n - 8:, :], b_raw[n - 8:, :], c_raw[n - 8:, :]), (_rms(y, ng),)


def _m2_io(proj, small, ws, nc, rev):
    cm = _rev(nc, rev)
    n = CHUNK
    ins = [proj, proj, proj, proj, small] + list(ws)
    specs = [pl.BlockSpec((n, BW), lambda c: (cm(c), 6)), pl.BlockSpec((n, BW), lambda c: (cm(c), 7)),
             pl.BlockSpec((n, 256), lambda c: (cm(c), 16)), pl.BlockSpec((n, 256), lambda c: (cm(c), 17)),
             pl.BlockSpec((n, 16), lambda c: (cm(c), 0))]
    specs += [_full(a) for a in ws]
    return ins, specs, cm


_M2_CARRY = [(128, BW), (8, BW), (8, 256), (8, 256)]


def m2_fwd(name, proj, small, ws):
    s = proj.shape[0]
    nc = s // CHUNK
    ins, specs, cm = _m2_io(proj, small, ws, nc, False)
    outs = [_out((s, BW), pl.BlockSpec((CHUNK, BW), lambda c: (c, 0)))]
    (y,), saved = run_fwd(name, _m2_tile, (nc,), ins, specs, outs, carry=dict(shapes=_M2_CARRY, axis=0))
    return y, saved


def m2_bwd(name, proj, small, ws, saved, gy, gy_col, gproj):
    s = proj.shape[0]
    nc = s // CHUNK
    ins, specs, cm = _m2_io(proj, small, ws, nc, True)
    n = CHUNK
    wants = [_want((0, 1, 2, 3), (s, PW), pl.BlockSpec((n, 3 * BW), lambda c: (cm(c), 2))),
             _want(4, (s, 16), pl.BlockSpec((n, 16), lambda c: (cm(c), 0)))]
    wants += [_want(5 + i, a.shape, _full(a), acc=(0,)) for i, a in enumerate(ws)]
    return run_bwd(name, _m2_tile, (nc,), ins, specs, [gy], [pl.BlockSpec((n, BW), lambda c: (cm(c), gy_col))], wants,
                   carry=dict(axis=0, saved=saved, rev=lambda g: (cm(g[0]),)), into=gproj)


def _l2n(x):
    return x * lax.rsqrt(jnp.sum(x * x, axis=-1, keepdims=True) + EPS)


def _gdn_tile(carry, q_raw, k_raw, v_raw, gate, small, cwq, cwk, cwv, dtb, alog, ng):
    state, pq, pk, pv = carry
    n, nh = CHUNK, 4
    nn_ = n * nh
    qc = _silu(_conv4(q_raw, pq, cwq, None))
    kc = _silu(_conv4(k_raw, pk, cwk, None))
    vc = _silu(_conv4(v_raw, pv, cwv, None))
    beta16 = jax.nn.sigmoid(small)
    g16 = -jnp.exp(alog) * _softplus(small + dtb)
    tri = _causal(n).astype(f32)
    triu = (_rows((n, n)) <= _lanes((n, n))).astype(f32)
    cs16 = lmm(tri, g16)
    cs16_t = rmm_tn(g16, triu)
    lanes_of = lambda h: slice(128 * h, 128 * (h + 1))
    rows_of = lambda h: slice(n * h, n * (h + 1))
    stack = lambda f: jnp.concatenate([f(h) for h in range(nh)], axis=0)
    q = stack(lambda h: _l2n(qc[:, lanes_of(h)]) * (128 ** -0.5))
    k = stack(lambda h: _l2n(kc[:, lanes_of(h)]))
    v = stack(lambda h: vc[:, lanes_of(h)])
    beta = stack(lambda h: _pick_lane(beta16, 8 + h))
    col = stack(lambda h: _pick_lane(cs16, 12 + h))
    last_h = [_pick_row(_pick_lane(cs16, 12 + h), n - 1) for h in range(nh)]
    last = stack(lambda h: jnp.broadcast_to(last_h[h], (n, 1)))
    last_w = jnp.concatenate([jnp.broadcast_to(last_h[h], (1, 128)) for h in range(nh)], axis=1)
    spread = (_rows((n, nn_)) == _lanes((n, nn_)) % n).astype(f32)
    cs_w = rmm(cs16_t, spread)
    rowv = jnp.sum(jnp.where(_rows((16, nn_)) == 12 + _lanes((16, nn_)) // n, cs_w, 0.0), axis=0, keepdims=True)
    same = (_rows((nn_, nn_)) // n) == (_lanes((nn_, nn_)) // n)
    causal = jnp.logical_and(same, _rows((nn_, nn_)) >= _lanes((nn_, nn_)))
    strict = jnp.logical_and(same, _rows((nn_, nn_)) > _lanes((nn_, nn_)))
    decay = jnp.where(causal, jnp.exp(jnp.where(causal, col - rowv, 0.0)), 0.0)
    kb = k * beta
    t = _tri_inv(jnp.where(strict, mm_nt(kb, k) * decay, 0.0))
    e_col = jnp.exp(col)
    uw = xmm(t, jnp.concatenate([v * beta, kb * e_col], axis=1))
    u, w = uw[:, :128], uw[:, 128:]
    qk = mm_nt(q, k) * decay
    own = lambda r: stack(lambda h: r[rows_of(h), lanes_of(h)])
    v_new = u - own(mm(w, state))
    o = own(mm(q * e_col, state)) + mm(qk, v_new)
    zero = jnp.zeros((n, 128), f32)
    v_blocks = stack(lambda h: jnp.concatenate([v_new[rows_of(h), :] if j == h else zero for j in range(nh)], axis=1))
    state_out = state * jnp.exp(last_w) + mm_tn(k * jnp.exp(last - col), v_blocks)
    gt = stack(lambda h: gate[:, lanes_of(h)])
    out = _rms(o, ng) * _silu(gt)
    out = jnp.concatenate([out[rows_of(h), :] for h in range(nh)], axis=1)
    return (state_out, q_raw[n - 8:, :], k_raw[n - 8:, :], v_raw[n - 8:, :]), (out,)


def _gdn_io(proj, small, cw, dtb, alog, ng, nc, rev):
    cm = _rev(nc, rev)
    n = CHUNK
    ins = [proj, proj, proj, proj, small, cw, cw, cw, dtb, alog, ng]
    specs = [pl.BlockSpec((n, BW), lambda c, _j=j: (cm(c), _j)) for j in (0, 1, 2, 3)]
    specs += [pl.BlockSpec((n, 16), lambda c: (cm(c), 0))]
    specs += [pl.BlockSpec((4, BW), lambda c, _j=j: (0, _j)) for j in (0, 1, 2)]
    specs += [_full(dtb), _full(alog), _full(ng)]
    return ins, specs, cm


_GDN_CARRY = [(128, BW)] + [(8, BW)] * 3


def gdn_fwd(name, proj, small, cw, dtb, alog, ng, ride=None):
    s = proj.shape[0]
    nc = s // CHUNK
    ins, specs, cm = _gdn_io(proj, small, cw, dtb, alog, ng, nc, False)
    outs = [_out((s, BW), pl.BlockSpec((CHUNK, BW), lambda c: (c, 0)))]
    res = run_fwd(name, _gdn_tile, (nc,), ins, specs, outs, carry=dict(shapes=_GDN_CARRY, axis=0), ride=ride)
    return (res[0][0],) + tuple(res[1:])


def gdn_bwd(name, proj, small, cw, dtb, alog, ng, saved, gy, gy_col, ride=None):
    s = proj.shape[0]
    nc = s // CHUNK
    n = CHUNK
    ins, specs, cm = _gdn_io(proj, small, cw, dtb, alog, ng, nc, True)
    wants = [_want((0, 1, 2, 3), (s, PW), pl.BlockSpec((n, 4 * BW), lambda c: (cm(c), 0))),
             _want(4, (s, 16), pl.BlockSpec((n, 16), lambda c: (cm(c), 0)))]
    wants += [_want(5 + i, (4, BW), pl.BlockSpec((4, BW), lambda c: (0, 0)), acc=(0,)) for i in range(3)]
    wants += [_want(8, dtb.shape, _full(dtb), acc=(0,)), _want(9, alog.shape, _full(alog), acc=(0,)),
              _want(10, ng.shape, _full(ng), acc=(0,))]
    return run_bwd(name, _gdn_tile, (nc,), ins, specs, [gy], [pl.BlockSpec((n, BW), lambda c: (cm(c), gy_col))], wants,
                   carry=dict(axis=0, saved=saved, rev=lambda g: (cm(g[0]),)), ride=ride)


def _perm_cols(w):
    pad = jnp.zeros(w.shape[:-1] + (PW - IN_WIDTH,), w.dtype)
    return jnp.concatenate([w[..., 3080:5128], w[..., 512:3072], w[..., :512], w[..., 3072:3080], w[..., 5128:5136], pad], axis=-1)


def _unperm_cols(g):
    return jnp.concatenate([g[..., 4608:5120], g[..., 2048:4608], g[..., 5120:5128], g[..., :2048], g[..., 5128:5136]], axis=-1)


def _bd(blocks):
    n, a, b = blocks.shape
    eye = jnp.eye(n, dtype=blocks.dtype)
    return jnp.einsum("nab,nm->namb", blocks, eye).reshape(n * a, n * b)


def _layer_layout(lw):
    o = {}
    row = lambda a: a.reshape(1, -1)
    o["b_gate"] = row(lw["b_gate"])
    o["s5_are"] = lw["s5_a_re"].reshape(4, 1, 512)
    o["s5_aim"] = lw["s5_a_im"].reshape(4, 1, 512)
    o["s5_lstep"] = jnp.repeat(lw["s5_log_step"], 64).reshape(4, 1, 512)
    bt = lambda b: jax.vmap(_bd)(jnp.swapaxes(b, 1, 2).reshape(4, 8, 16, 64))
    o["s5_bre"], o["s5_bim"] = bt(lw["s5_b_re"]), bt(lw["s5_b_im"])
    ct = lambda c: jax.vmap(_bd)(jnp.swapaxes(c, 1, 2).reshape(4, 8, 64, 16))
    o["s5_cre"], o["s5_cim"] = ct(lw["s5_c_re"]), ct(lw["s5_c_im"])
    o["s5_d"] = row(lw["s5_d"])
    o["s5_b_glu"] = row(lw["s5_b_glu"])
    o["lru_conv_b"], o["lru_b_r"], o["lru_b_i"], o["lru_lambda"] = (row(lw[k]) for k in ("lru_conv_b", "lru_b_r", "lru_b_i", "lru_lambda"))
    o["lru_wr"], o["lru_wi"] = _bd(lw["lru_w_r"]), _bd(lw["lru_w_i"])
    cw, cb = lw["m2_conv_w"], lw["m2_conv_b"]
    o["m2_cwx"], o["m2_cwb"], o["m2_cwc"] = cw[:, :512], cw[:, 512:768], cw[:, 768:]
    o["m2_cbx"], o["m2_cbb"], o["m2_cbc"] = row(cb[:512]), row(cb[512:768]), row(cb[768:])
    o["m2_dtb"] = jnp.pad(lw["m2_dt_bias"], (0, 8)).reshape(1, 16)
    o["m2_alog"] = jnp.pad(lw["m2_a_log"], (0, 8)).reshape(1, 16)
    o["m2_dsk"] = jnp.repeat(lw["m2_d"], 64).reshape(1, 512)
    o["m2_norm"] = row(lw["m2_norm"])
    o["gdn_dtb"] = jnp.pad(lw["gdn_dt_bias"], (12, 0)).reshape(1, 16)
    o["gdn_alog"] = jnp.pad(lw["gdn_a_log"], (12, 0)).reshape(1, 16)
    o["gdn_norm"] = row(lw["gdn_norm"])
    for k in ("ffn1_norm", "mix_norm", "ffn2_norm", "ple_norm"):
        o[k] = row(lw[k])
    return o


_BIG_PLAIN = ("w_gate", "s5_w_glu", "w_branch", "w_out", "ple_w_gate", "ple_w_proj")


def big_layout(w):
    o = {k: w[k] for k in _BIG_PLAIN}
    for f in ("ffn1", "ffn2"):
        o[f + "_wg"], o[f + "_wu"] = w[f + "_w_in"][:, :FFN_DIM], w[f + "_w_in"][:, FFN_DIM:]
        o[f + "_wo"] = w[f + "_w_out"]
    o["w_in"] = _perm_cols(w["w_in"])
    return {k: v.astype(bf16) for k, v in o.items()}


def big_unlayout(g):
    o = {k: g[k] for k in _BIG_PLAIN}
    for f in ("ffn1", "ffn2"):
        o[f + "_w_in"] = jnp.concatenate([g[f + "_wg"], g[f + "_wu"]], axis=1)
        o[f + "_w_out"] = g[f + "_wo"]
    o["w_in"] = _unperm_cols(g["w_in"])
    return o


_SMALL_KEYS = ("b_gate", "s5_log_step", "s5_a_re", "s5_a_im", "s5_b_re", "s5_b_im", "s5_c_re", "s5_c_im", "s5_d",
               "s5_b_glu", "lru_conv_b", "lru_w_r", "lru_b_r", "lru_w_i", "lru_b_i", "lru_lambda", "m2_conv_w",
               "m2_conv_b", "m2_dt_bias", "m2_a_log", "m2_d", "m2_norm", "gdn_dt_bias", "gdn_a_log", "gdn_norm",
               "ffn1_norm", "mix_norm", "ffn2_norm", "ple_norm")


def _layer_fwd(i, h0, pe, lw, big, ride=None):
    ride = ride or (None, None)
    n = f"l{i}_"
    lay = _layer_layout(lw)
    a = {"h0": h0, "lay": lay}
    h1, n1 = ffn_fwd(n + "ffn1_fwd", h0, lay["ffn1_norm"], big["ffn1_wg"], big["ffn1_wu"], big["ffn1_wo"])
    zero_b = jnp.zeros((1, PW), f32)
    proj, u = normlin_fwd(n + "inproj_fwd", h1, lay["mix_norm"], big["w_in"], zero_b, tn=896)
    gpre = lin_fwd(n + "gate_fwd", u, big["w_gate"], lay["b_gate"], tn=1024)
    small = proj[:, SMALL_OFF:SMALL_OFF + 16]
    raw = [lay["s5_are"], lay["s5_aim"], lay["s5_lstep"], lay["s5_bre"], lay["s5_bim"]]
    ops = s5_ops_fwd(n + "s5ops_fwd", raw)
    y5, sv5, *rode5 = s5_fwd(n + "s5_fwd", proj, ops, lay["s5_cre"], lay["s5_cim"], lay["s5_d"], ride[0])
    ya = s5_glu_fwd(n + "s5glu_fwd", y5, big["s5_w_glu"], lay["s5_b_glu"])
    lru_ws = [lw["lru_conv_w"], lay["lru_conv_b"], lay["lru_wr"], lay["lru_b_r"], lay["lru_wi"], lay["lru_b_i"], lay["lru_lambda"]]
    yb, svb = lru_fwd(n + "lru_fwd", proj, lru_ws)
    m2_ws = [lay[k] for k in ("m2_cwx", "m2_cwb", "m2_cwc", "m2_cbx", "m2_cbb", "m2_cbc", "m2_dtb", "m2_alog", "m2_dsk", "m2_norm")]
    yc, svc = m2_fwd(n + "m2_fwd", proj, small, m2_ws)
    yd, svd, *roded = gdn_fwd(n + "gdn_fwd", proj, small, lw["gdn_conv_w"], lay["gdn_dtb"], lay["gdn_alog"], lay["gdn_norm"], ride[1])
    ys = jnp.concatenate([ya, yb, yc, yd], axis=1)
    h2 = merge_fwd(n + "merge_fwd", h1, ys, gpre, big["w_branch"], big["w_out"])
    h3, n2 = ffn_fwd(n + "ffn2_fwd", h2, lay["ffn2_norm"], big["ffn2_wg"], big["ffn2_wu"], big["ffn2_wo"])
    h4 = ple_fwd(n + "ple_fwd", h3, pe, lay["ple_norm"], big["ple_w_gate"], big["ple_w_proj"])
    a.update(h1=h1, n1=n1, u=u, n2=n2, proj=proj, gpre=gpre, small=small, raw=raw, ops=ops, y5=y5, sv5=sv5, lru_ws=lru_ws, svb=svb,
             m2_ws=m2_ws, svc=svc, svd=svd, ys=ys, h2=h2, h3=h3, pe=pe)
    return h4, a, (rode5[0] if rode5 else None, roded[0] if roded else None)


def small_into(name, a, b, gproj, t=512):
    s, w = a.shape
    pad_w = PW - SMALL_OFF

    def body(a_ref, b_ref, _, o_ref):
        place = (_rows((w, pad_w)) == _lanes((w, pad_w))).astype(f32)
        o_ref[...] = _dg(a_ref[...] + b_ref[...], place, _NN, (3, 1))

    row = pl.BlockSpec((t, w), lambda i: (i, 0))
    return pl.pallas_call(
        body, name=name, grid=(s // t,), in_specs=[row, row, _ANY],
        out_specs=pl.BlockSpec((t, pad_w), lambda i: (i, SMALL_OFF // pad_w)),
        out_shape=jax.ShapeDtypeStruct(gproj.shape, gproj.dtype), input_output_aliases={2: 0}, compiler_params=_params((1,)),
    )(a, b, gproj)


RIDE_S5 = ("ffn2_w_in", "ffn2_w_out", "w_branch", "w_out", "ple_w_gate", "ple_w_proj", "s5_w_glu")
RIDE_FFN1 = ("w_in", "w_gate", "lru_conv_w", "m2_conv_w", "gdn_conv_w")
RIDE_LAST = ("ffn1_w_in", "ffn1_w_out")


def _grad_of(k, gb, small):
    if k in ("ffn1_w_in", "ffn2_w_in"):
        return jnp.concatenate([gb[k[:4] + "_wg"], gb[k[:4] + "_wu"]], axis=1)
    if k in ("ffn1_w_out", "ffn2_w_out"):
        return gb[k[:4] + "_wo"]
    if k == "w_in":
        return _unperm_cols(gb[k])
    if k == "m2_conv_w" and k not in small:
        return jnp.concatenate([small["m2_cwx"], small["m2_cwb"], small["m2_cwc"]], axis=1)
    return gb[k] if k in gb else small[k]


def _scattered(names, gb, small):
    return [_full_to_scattered(k, _grad_of(k, gb, small)[None]).astype(bf16) for k in names]


def _layer_bwd(i, a, lw, big, gh4, ride=None):
    n = f"l{i}_"
    lay = a["lay"]
    gb, gl = {}, {}
    s = gh4.shape[0]
    t = 512
    gh3, gl["ple_norm"], gb["ple_w_gate"], gb["ple_w_proj"] = ple_bwd(
        n + "ple_bwd", a["h3"], a["pe"], lay["ple_norm"], big["ple_w_gate"], big["ple_w_proj"], gh4)
    dn2, gb["ffn2_wg"], gb["ffn2_wu"], gb["ffn2_wo"] = ffn_bwd(
        n + "ffn2_bwd", a["n2"], big["ffn2_wg"], big["ffn2_wu"], big["ffn2_wo"], gh3)
    gh2, gl["ffn2_norm"] = norm_bwd(n + "gh2", a["h2"], lay["ffn2_norm"], gh3, _parts(dn2))
    gys, ggpre, gb["w_branch"], gb["w_out"] = merge_bwd(
        n + "merge_bwd", a["h1"], a["ys"], a["gpre"], big["w_branch"], big["w_out"], gh2)
    rd = gdn_bwd(n + "gdn_bwd", a["proj"], a["small"], lw["gdn_conv_w"], lay["gdn_dtb"], lay["gdn_alog"], lay["gdn_norm"], a["svd"], gys, 3, ride)
    gproj, dsm_d = rd[:2]
    gl["gdn_conv_w"] = jnp.concatenate(rd[2:5], axis=1)
    gl["gdn_dtb"], gl["gdn_alog"], gl["gdn_norm"] = rd[5:8]
    rode = rd[8:]
    rb = lru_bwd(n + "lru_bwd", a["proj"], a["lru_ws"], a["svb"], gys, 1, gproj)
    gproj = rb[0]
    gl["lru_conv_w"], gl["lru_conv_b"], gl["lru_wr"], gl["lru_b_r"], gl["lru_wi"], gl["lru_b_i"], gl["lru_lambda"] = rb[1:]
    rc = m2_bwd(n + "m2_bwd", a["proj"], a["small"], a["m2_ws"], a["svc"], gys, 2, gproj)
    gproj, dsm_c = rc[:2]
    for k, v in zip(("m2_cwx", "m2_cwb", "m2_cwc", "m2_cbx", "m2_cbb", "m2_cbc", "m2_dtb", "m2_alog", "m2_dsk", "m2_norm"), rc[2:]):
        gl[k] = v
    gy5, gb["s5_w_glu"], gl["s5_b_glu"] = s5_glu_bwd(n + "s5glu_bwd", a["y5"], big["s5_w_glu"], lay["s5_b_glu"], gys, 0)
    own = {}
    r5 = s5_bwd(n + "s5_bwd", a["proj"], a["ops"], lay["s5_cre"], lay["s5_cim"], lay["s5_d"], a["sv5"], gy5, gproj,
                _scattered(RIDE_S5, gb, gl) if ride is not None else None)
    gproj, gops, gl["s5_cre"], gl["s5_cim"], gl["s5_d"] = r5[0], r5[1:7], r5[7], r5[8], r5[9]
    own.update(zip(RIDE_S5, r5[10:]))
    gl["s5_are"], gl["s5_aim"], gl["s5_lstep"], gl["s5_bre"], gl["s5_bim"] = s5_ops_bwd(n + "s5ops_bwd", a["raw"], gops)
    gproj = small_into(n + "gsmall", dsm_c, dsm_d, gproj)
    zero_b = jnp.zeros((1, PW), f32)
    du_p, gb["w_in"], _ = lin_bwd(n + "inproj_bwd", a["u"], big["w_in"], zero_b, gproj, tn=896)
    du_g, gb["w_gate"], gl["b_gate"] = lin_bwd(n + "gate_bwd", a["u"], big["w_gate"], lay["b_gate"], ggpre, tn=1024)
    gh1, gl["mix_norm"] = norm_bwd(n + "gh1", a["h1"], lay["mix_norm"], gh2, _parts(du_p) + _parts(du_g))
    rf = ffn_bwd(n + "ffn1_bwd", a["n1"], big["ffn1_wg"], big["ffn1_wu"], big["ffn1_wo"], gh1,
                 ride=_scattered(RIDE_FFN1, gb, gl) if ride is not None else None)
    dn1, gb["ffn1_wg"], gb["ffn1_wu"], gb["ffn1_wo"] = rf[:4]
    own.update(zip(RIDE_FFN1, rf[4:]))
    gh0, gl["ffn1_norm"] = norm_bwd(n + "gh0", a["h0"], lay["ffn1_norm"], gh1, _parts(dn1))
    return gh0, gb, gl, (rode, own)


def _local_step(x, p, target, bigs, smalls, final_norm, ride_of=None, next_layer=None):
    h = x
    acts = []
    bigs, smalls = list(bigs), list(smalls)
    for i in range(DEPTH):
        h, a, got = _layer_fwd(i, h, p[i], smalls[i], bigs[i], next_layer[0] if next_layer and i == DEPTH - 2 else None)
        if next_layer and i == DEPTH - 2:
            bigs[i + 1], smalls[i + 1] = next_layer[1](got)
        acts.append(a)
    fg = final_norm.reshape(1, -1)
    loss, gh, gfn = final_loss("final_loss", h, fg, target)
    gbs, gss, rode = [None] * DEPTH, [None] * DEPTH, [None] * DEPTH
    for i in reversed(range(DEPTH)):
        ride = ride_of(gbs[i + 1], gss[i + 1]) if ride_of is not None and i + 1 < DEPTH else None
        gh, gb, gl, (got, own) = _layer_bwd(i, acts[i], smalls[i], bigs[i], gh, ride)
        if ride is not None:
            rode[i + 1], rode[i] = got, own
        _, pull = jax.vjp(_layer_layout, smalls[i])
        lay_g = {k: gl[k] for k in acts[i]["lay"]}
        gs = pull(lay_g)[0]
        gs = dict(gs)
        gs["lru_conv_w"] = gl["lru_conv_w"]
        gs["gdn_conv_w"] = gl["gdn_conv_w"]
        gbs[i], gss[i] = gb, gs
    return loss, gh, gbs, gss, gfn.reshape(-1), rode


MESH_AXES = ("x", "y", "c")
PACK_W = 1024
PACK_ROWS = 256

W_NAMES = ("ffn1_norm", "ffn1_w_in", "ffn1_w_out", "mix_norm", "w_in", "w_gate", "b_gate", "s5_log_step", "s5_a_re",
           "s5_a_im", "s5_b_re", "s5_b_im", "s5_c_re", "s5_c_im", "s5_d", "s5_w_glu", "s5_b_glu", "lru_conv_w",
           "lru_conv_b", "lru_w_r", "lru_b_r", "lru_w_i", "lru_b_i", "lru_lambda", "m2_conv_w", "m2_conv_b",
           "m2_dt_bias", "m2_a_log", "m2_d", "m2_norm", "gdn_conv_w", "gdn_dt_bias", "gdn_a_log", "gdn_norm",
           "w_branch", "w_out", "ffn2_norm", "ffn2_w_in", "ffn2_w_out", "ple_norm", "ple_w_gate", "ple_w_proj",
           "final_norm")
COL_SHARDED = ("ffn1_w_in", "w_in", "w_gate", "lru_conv_w", "m2_conv_w", "gdn_conv_w", "w_branch", "ffn2_w_in", "ple_w_proj")
ROW_SHARDED = ("ffn1_w_out", "s5_w_glu", "w_out", "ffn2_w_out", "ple_w_gate")
BIG_NAMES = ("ffn1_w_in", "ffn1_w_out", "w_in", "w_gate", "s5_w_glu", "w_branch", "w_out", "ffn2_w_in", "ffn2_w_out",
             "ple_w_gate", "ple_w_proj")
CONV_NAMES = ("lru_conv_w", "m2_conv_w", "gdn_conv_w")
SHARDED = BIG_NAMES + CONV_NAMES
REPLICATED = tuple(k for k in W_NAMES if k not in SHARDED)


def _gathered_to_full(name, g):
    if name in COL_SHARDED:
        g = jnp.moveaxis(g, 0, -2)
        return g.reshape(g.shape[:-2] + (g.shape[-2] * g.shape[-1],))
    g = jnp.moveaxis(g, 0, 1)
    return g.reshape((g.shape[0], g.shape[1] * g.shape[2]) + g.shape[3:])


def _full_to_scattered(name, w):
    if name in COL_SHARDED:
        w = w.reshape(w.shape[:-1] + (N_DEV, w.shape[-1] // N_DEV))
        return jnp.moveaxis(w, -2, 0)
    w = w.reshape((w.shape[0], N_DEV, w.shape[1] // N_DEV) + w.shape[2:])
    return jnp.moveaxis(w, 1, 0)


def _pack(arrs):
    pieces = []
    for a in arrs:
        k = -(-a.size // PACK_W)
        pieces.append(jnp.pad(a.reshape(-1), (0, k * PACK_W - a.size)).reshape(k, PACK_W))
    buf = jnp.concatenate(pieces, axis=0)
    return jnp.pad(buf, ((0, -buf.shape[0] % PACK_ROWS), (0, 0)))


def _unpack(buf, shapes):
    out, r = [], 0
    for sh in shapes:
        n = math.prod(sh)
        k = -(-n // PACK_W)
        out.append(buf[r:r + k].reshape(-1)[:n].reshape(sh))
        r += k
    return out


def _peer(k):
    mx, my, mc = (lax.axis_index(a) for a in MESH_AXES)
    px = 1 - mx if k & 4 else mx
    py = 1 - my if k & 2 else my
    pc = 1 - mc if k & 1 else mc
    return (px, py, pc), 4 * px + 2 * py + pc


def all_gather(name, xs):
    n = len(xs)

    def body(*refs):
        x_refs, out_refs = refs[:n], refs[n:2 * n]
        send_sems, recv_sems, local_sems = refs[2 * n:]
        mx, my, mc = (lax.axis_index(a) for a in MESH_AXES)
        me, sibling = (mx, my, mc), (mx, my, 1 - mc)
        chips = [(1 - mx, my), (mx, 1 - my), (1 - mx, 1 - my)]

        def slot(i, px, py, pc):
            return out_refs[i].at[4 * px + 2 * py + pc]

        def copy(k, i, block, to, src=None):
            return pltpu.make_async_remote_copy(
                src_ref=slot(i, *block) if src is None else src, dst_ref=slot(i, *block),
                send_sem=send_sems.at[k, i], recv_sem=recv_sems.at[k, i], device_id=to, device_id_type=pl.DeviceIdType.MESH)

        mine = [pltpu.make_async_copy(x_refs[i], slot(i, *me), local_sems.at[i]) for i in range(n)]
        first = []
        for i in range(n):
            mine[i].start()
            first.append(copy(0, i, me, sibling, src=x_refs[i]))
            first += [copy(1 + j, i, me, (*chip, mc), src=x_refs[i]) for j, chip in enumerate(chips)]
        for cp in first:
            cp.start()
        passed = []
        for i in range(n):
            for j, chip in enumerate(chips):
                copy(1 + j, i, (*chip, mc), me).wait_recv()
                cp = copy(4 + j, i, (*chip, mc), sibling)
                cp.start()
                passed.append(cp)
        for i in range(n):
            copy(0, i, sibling, me).wait_recv()
            for j, chip in enumerate(chips):
                copy(4 + j, i, (*chip, 1 - mc), me).wait_recv()
        for cp in first + passed:
            cp.wait_send()
        for cp in mine:
            cp.wait()

    res = pl.pallas_call(
        body, name=name, out_shape=[jax.ShapeDtypeStruct((N_DEV,) + x.shape, x.dtype) for x in xs],
        in_specs=[_ANY] * n, out_specs=[_ANY] * n,
        scratch_shapes=[pltpu.SemaphoreType.DMA((7, n)), pltpu.SemaphoreType.DMA((7, n)), pltpu.SemaphoreType.DMA((n,))],
    )(*xs)
    return list(res)


def _exchange_copies(g_refs, out_refs, send_sems, recv_sems, local_sems, with_incoming=True):
    mx, my, mc = (lax.axis_index(a) for a in MESH_AXES)
    me = 4 * mx + 2 * my + mc
    n = len(g_refs)
    local = [pltpu.make_async_copy(g_refs[i].at[me], out_refs[i].at[me], local_sems.at[i]) for i in range(n)]
    outgoing, incoming = [], []
    for k in range(1, N_DEV):
        peer, pidx = _peer(k)
        for i in range(n):
            sems = dict(send_sem=send_sems.at[k - 1, i], recv_sem=recv_sems.at[k - 1, i], device_id=peer,
                        device_id_type=pl.DeviceIdType.MESH)
            outgoing.append(pltpu.make_async_remote_copy(src_ref=g_refs[i].at[pidx], dst_ref=out_refs[i].at[me], **sems))
            if with_incoming:
                incoming.append(pltpu.make_async_remote_copy(src_ref=g_refs[i].at[pidx], dst_ref=out_refs[i].at[pidx], **sems))
    return local, outgoing, incoming


def _exchange_start(*refs):
    local, outgoing, _ = _exchange_copies(*refs, with_incoming=False)
    for cp in local + outgoing:
        cp.start()


def _exchange_wait(*refs):
    local, outgoing, incoming = _exchange_copies(*refs)
    for cp in incoming:
        cp.wait_recv()
    for cp in outgoing:
        cp.wait_send()
    for cp in local:
        cp.wait()


def _exchange_sems(n):
    return [pltpu.SemaphoreType.DMA((7, n)), pltpu.SemaphoreType.DMA((7, n)), pltpu.SemaphoreType.DMA((n,))]


def exchange(name, gs):
    n = len(gs)

    def body(*refs):
        args = (refs[:n], refs[n:2 * n]) + tuple(refs[2 * n:])
        _exchange_start(*args)
        _exchange_wait(*args)

    res = pl.pallas_call(
        body, name=name, out_shape=[jax.ShapeDtypeStruct(g.shape, g.dtype) for g in gs], in_specs=[_ANY] * n, out_specs=[_ANY] * n,
        scratch_shapes=_exchange_sems(n),
    )(*gs)
    return list(res)


def sum_slots(name, buf):
    return addn(name, _parts(buf), PACK_ROWS)


def _adamw_math(ww, gg, mm_, vv):
    m2 = ADAM_B1 * mm_ + (1.0 - ADAM_B1) * gg
    v2 = ADAM_B2 * vv + (1.0 - ADAM_B2) * (gg * gg)
    m_hat = m2 / (1.0 - ADAM_B1 ** ADAM_STEP)
    v_hat = v2 / (1.0 - ADAM_B2 ** ADAM_STEP)
    delta = -ADAM_LR * (m_hat / (jnp.sqrt(v_hat) + ADAM_EPS) + ADAM_WD * ww)
    return delta, m2, v2


def adamw(name, w, g, m, v):
    spec = pl.BlockSpec((PACK_ROWS, PACK_W), lambda i: (i, 0))
    return run_fwd(name, _adamw_math, (w.shape[0] // PACK_ROWS,), [w, g, m, v], [spec] * 4, [_out(w.shape, spec)] * 3)[0]


def reduce_adamw(name, slots, w, m, v):
    shape = w.shape
    r, c = shape[-2:]
    a = math.prod(shape[:-2])
    per = a // DEPTH
    tr = r
    while tr * c * 4 > (1 << 20) and tr % 16 == 0:
        tr //= 2
    s3 = [s.reshape((N_DEV, per, r, c)) for s in slots]
    w3, m3, v3 = (t.reshape((a, r, c)) for t in (w, m, v))

    def fn(*t):
        layer = pl.program_id(0) // per
        g = None
        for l in range(DEPTH):
            gl = t[l * N_DEV].astype(f32)
            for d in range(1, N_DEV):
                gl = gl + t[l * N_DEV + d].astype(f32)
            g = gl if g is None else jnp.where(layer == l, gl, g)
        k = DEPTH * N_DEV
        return (g,) + _adamw_math(t[k], g, t[k + 1], t[k + 2])

    specs, ins = [], []
    for l in range(DEPTH):
        for d in range(N_DEV):
            specs.append(pl.BlockSpec((None, None, tr, c), lambda i, j, _d=d, _l=l: (_d, jnp.clip(i - _l * per, 0, per - 1), j, 0)))
            ins.append(s3[l])
    spec = pl.BlockSpec((None, tr, c), lambda i, j: (i, j, 0))
    res = run_fwd(name, fn, (a, r // tr), ins + [w3, m3, v3], specs + [spec] * 3, [_out((a, r, c), spec)] * 4)[0]
    return [t.reshape(shape) for t in res]


def kernel(*args):
    nw = len(W_NAMES)
    x, p = args[0], args[1]
    w = dict(zip(W_NAMES, args[2:2 + nw]))
    target = args[2 + nw]
    m = dict(zip(W_NAMES, args[3 + nw:3 + 2 * nw]))
    v = dict(zip(W_NAMES, args[3 + 2 * nw:3 + 3 * nw]))

    wire = lambda k, i: w[k][i].astype(bf16) if k in BIG_NAMES else w[k][i]

    def layer_params(i, gathered):
        full = {k: _gathered_to_full(k, g[:, None])[0] for k, g in zip(SHARDED, gathered)}
        small = {k: w[k][i] for k in _SMALL_KEYS if k not in CONV_NAMES}
        small.update({k: full[k] for k in CONV_NAMES})
        return big_layout({k: full[k] for k in BIG_NAMES}), small

    big0, small0 = layer_params(0, all_gather("ag_layer0", [wire(k, 0) for k in SHARDED]))
    to_all = lambda k: jnp.broadcast_to(wire(k, DEPTH - 1)[None], (N_DEV,) + w[k].shape[1:])
    early = ("ffn1_w_in", "ffn1_w_out")
    late = tuple(k for k in SHARDED if k not in early)

    def last_layer(got):
        by_name = dict(zip(early + late, list(got[0]) + list(got[1])))
        return layer_params(DEPTH - 1, [by_name[k] for k in SHARDED])

    bigs, smalls = [big0, None], [small0, None]
    next_layer = (([to_all(k) for k in early], [to_all(k) for k in late]), last_layer)
    loss, gx, gbs, gss, gfn, rode = _local_step(x[0], p[:, 0], target[0], bigs, smalls, w["final_norm"],
                                                lambda gb, gs: _scattered(SHARDED, gb, gs), next_layer)
    loss = lax.psum(loss[0, 0], MESH_AXES)

    rode[0].update(zip(RIDE_LAST, exchange("rs_last", _scattered(RIDE_LAST, gbs[0], gss[0]))))
    rode[0] = [rode[0][k] for k in SHARDED]
    gfull = [dict(big_unlayout(gbs[i]), **gss[i]) for i in range(DEPTH)]
    stack = lambda k: jnp.stack([gfull[i][k] for i in range(DEPTH)])
    outs = {}
    kinds = ("grad", "delta", "new_m", "new_v")
    for j, k in enumerate(SHARDED):
        for kind, a in zip(kinds, reduce_adamw("adamw_" + k, [rode[i][j] for i in range(DEPTH)], w[k], m[k], v[k])):
            outs[kind + "_" + k] = a

    shapes = [w[k].shape for k in REPLICATED]
    g_rep = sum_slots("sum_replicated", all_gather("ag_replicated", [_pack([gfn if k == "final_norm" else stack(k) for k in REPLICATED])])[0])
    res = adamw("adamw_replicated", _pack([w[k] for k in REPLICATED]), g_rep, _pack([m[k] for k in REPLICATED]), _pack([v[k] for k in REPLICATED]))
    for kind, buf in zip(kinds, [g_rep] + list(res)):
        for k, a in zip(REPLICATED, _unpack(buf, shapes)):
            outs[kind + "_" + k] = a
    return (loss, gx[None]) + tuple(outs[kind + "_" + k] for kind in kinds for k in W_NAMES)
```

```python
import functools
import math

import jax
import jax.numpy as jnp
from jax import lax
from jax.experimental import pallas as pl
from jax.experimental.pallas import tpu as pltpu

f32 = jnp.float32
bf16 = jnp.bfloat16

EPS = 1e-6
DEPTH = 2
D_MODEL = 1024
FFN_DIM = 2816
BW = 512
IN_WIDTH = 5136
PW = 5376
SMALL_OFF = 5120
CHUNK = 64
LS = 256
LRU_C = 8.0
N_DEV = 8
VMEM_LIMIT_BYTES = 56 * 1024 * 1024

ADAM_LR, ADAM_B1, ADAM_B2, ADAM_EPS, ADAM_WD, ADAM_STEP = 0.001, 0.9, 0.999, 1e-08, 0.01, 10


_NN, _NT, _TN = ((1,), (0,)), ((1,), (1,)), ((0,), (0,))


def _pieces(x, n):
    parts, r = [], x
    for i in range(n):
        p = r.astype(bf16)
        parts.append(p)
        if i + 1 < n:
            r = r - p.astype(f32)
    return parts


def _dg(a, b, dims, mode):
    sa, sb = mode
    pa, pb = _pieces(a, sa), _pieces(b, sb)
    out = None
    for i in reversed(range(sa)):
        for j in reversed(range(sb)):
            if i + j < max(sa, sb):
                d = lax.dot_general(pa[i], pb[j], (dims, ((), ())), preferred_element_type=f32)
                out = d if out is None else out + d
    return out


def _make_mm(mode):
    sa, sb = mode
    cot = lambda s_other: 1 if mode == (1, 1) else (3 if s_other == 1 else 2)
    m_g_b, m_a_g, m_g_a, m_b_g = (cot(sb), sb), (sa, cot(sa)), (cot(sa), sa), (sb, cot(sb))

    @jax.custom_vjp
    def nn(a, b):
        return _dg(a, b, _NN, mode)

    @jax.custom_vjp
    def nt(a, b):
        return _dg(a, b, _NT, mode)

    @jax.custom_vjp
    def tn(a, b):
        return _dg(a, b, _TN, mode)

    nn.defvjp(lambda a, b: (nn(a, b), (a, b)), lambda r, g: (_dg(g, r[1], _NT, m_g_b), _dg(r[0], g, _TN, m_a_g)))
    nt.defvjp(lambda a, b: (nt(a, b), (a, b)), lambda r, g: (_dg(g, r[1], _NN, m_g_b), _dg(g, r[0], _TN, m_g_a)))
    tn.defvjp(lambda a, b: (tn(a, b), (a, b)), lambda r, g: (_dg(r[1], g, _NT, m_b_g), _dg(r[0], g, _NN, m_a_g)))
    return nn, nt, tn


mm, mm_nt, mm_tn = _make_mm((1, 1))
xmm, xmm_nt, xmm_tn = _make_mm((2, 2))
lmm, lmm_nt, lmm_tn = _make_mm((1, 3))
rmm, rmm_nt, rmm_tn = _make_mm((3, 1))


@jax.custom_vjp
def _tri_inv(m):
    n = m.shape[0]
    eye = (_rows((n, n)) == _lanes((n, n))).astype(f32)
    blk = (_rows((n, n)) // 16) == (_lanes((n, n)) // 16)
    x = lambda a, b: _dg(a, b, _NN, (2, 2))
    nb = jnp.where(blk, m, 0.0)
    p = -nb
    t = eye + p
    for _ in range(3):
        p = x(p, p)
        t = t + x(t, p)
    q = x(t, m - nb)
    imq = eye - q
    return x(imq + x(imq, x(q, q)), t)


def _tri_inv_fwd(m):
    t = _tri_inv(m)
    return t, t


def _tri_inv_bwd(t, g):
    return (-_dg(_dg(t, g, _TN, (2, 2)), t, _NT, (2, 2)),)


_tri_inv.defvjp(_tri_inv_fwd, _tri_inv_bwd)


def _rows(shape):
    return lax.broadcasted_iota(jnp.int32, shape, 0)


def _lanes(shape):
    return lax.broadcasted_iota(jnp.int32, shape, 1)


def _rms(x, g):
    return x * lax.rsqrt(jnp.mean(x * x, axis=-1, keepdims=True) + EPS) * g


def _silu(x):
    return x * jax.nn.sigmoid(x)


def _gelu(x):
    return 0.5 * x * (1.0 + jnp.tanh(0.7978845608028654 * (x + 0.044715 * x * x * x)))


def _softplus(x):
    return jnp.maximum(x, 0.0) + jnp.log1p(jnp.exp(-jnp.abs(x)))


def _expm1(x):
    p = x * (1.0 + x * (0.5 + x * (1.0 / 6 + x * (1.0 / 24 + x * (1.0 / 120 + x * (1.0 / 720 + x * (1.0 / 5040)))))))
    return jnp.where(x > -0.3, p, jnp.exp(x) - 1.0)


def _pick_row(x, r):
    return jnp.sum(jnp.where(_rows(x.shape) == r, x, 0.0), axis=0, keepdims=True)


def _pick_lane(x, c):
    return jnp.sum(jnp.where(_lanes(x.shape) == c, x, 0.0), axis=1, keepdims=True)


def _shift_up(g, j):
    n = g.shape[0]
    return jnp.where(_rows(g.shape) < n - j, pltpu.roll(g, n - j, 0), 0.0)


@functools.partial(jax.custom_vjp, nondiff_argnums=(1, 2))
def _shift(x, j, fill):
    return jnp.where(_rows(x.shape) >= j, pltpu.roll(x, j, 0), fill)


_shift.defvjp(lambda x, j, fill: (_shift(x, j, fill), None), lambda j, fill, _, g: (_shift_up(g, j),))


@functools.partial(jax.custom_vjp, nondiff_argnums=(2,))
def _shift_halo(x, prev8, j):
    xr = pltpu.roll(x, j, 0)
    pr = pltpu.roll(prev8, j, 0)
    top = jnp.where(_rows(pr.shape) < j, pr, xr[:8])
    return jnp.concatenate([top, xr[8:]], axis=0)


def _shift_halo_bwd(j, _, g):
    g8 = g[:8]
    dprev = jnp.where(_rows(g8.shape) >= 8 - j, pltpu.roll(g8, 8 - j, 0), 0.0)
    return _shift_up(g, j), dprev


_shift_halo.defvjp(lambda x, p, j: (_shift_halo(x, p, j), None), _shift_halo_bwd)


def _conv4(x, prev8, w, b):
    y = _pick_row(w, 3) * x
    for k in range(3):
        y = y + _pick_row(w, k) * _shift_halo(x, prev8, 3 - k)
    return y if b is None else y + b


def _cmul(ar, ai, br, bi):
    return ar * br - ai * bi, ar * bi + ai * br


_ANY = pl.BlockSpec(memory_space=pl.ANY)


def _params(grid):
    return pltpu.CompilerParams(dimension_semantics=("arbitrary",) * len(grid), vmem_limit_bytes=VMEM_LIMIT_BYTES)


def _first(axes):
    ok = pl.program_id(axes[0]) == 0
    for a in axes[1:]:
        ok = jnp.logical_and(ok, pl.program_id(a) == 0)
    return ok


def _store(ref, val, acc):
    val = val.astype(ref.dtype)
    if acc is None:
        ref[...] = val
        return
    first = _first(acc)

    @pl.when(first)
    def _():
        ref[...] = val

    @pl.when(jnp.logical_not(first))
    def _():
        ref[...] += val


def _full(a):
    nd = a.ndim
    return pl.BlockSpec(a.shape, lambda *g: (0,) * nd)


def _out(shape, spec, acc=None, dtype=f32):
    return dict(shape=tuple(shape), spec=spec, acc=acc, dtype=dtype)


def _grid_step(grid):
    step = pl.program_id(0)
    for ax in range(1, len(grid)):
        step = step * grid[ax] + pl.program_id(ax)
    return step


def run_fwd(name, fn, grid, ins, in_specs, outs, carry=None, ride=None):
    n_in, n_out = len(ins), len(outs)
    cshapes = carry["shapes"] if carry else []
    nc = len(cshapes)
    ng = len(grid)
    n_r = len(ride) if ride else 0

    def body(*refs):
        in_refs = refs[:n_in]
        out_refs = refs[n_in + n_r:n_in + n_r + n_out]
        save_refs = refs[n_in + n_r + n_out:n_in + n_r + n_out + nc]
        c_refs = refs[n_in + 2 * n_r + n_out + nc:n_in + 2 * n_r + n_out + 2 * nc]
        if n_r:
            step = _grid_step(grid)
            r_args = (refs[n_in:n_in + n_r], refs[n_in + n_r + n_out + nc:n_in + 2 * n_r + n_out + nc]) \
                + tuple(refs[n_in + 2 * n_r + n_out + 2 * nc:])
            pl.when(step == 0)(lambda: _exchange_start(*r_args))
        vals = [r[...] for r in in_refs]
        if carry:
            @pl.when(pl.program_id(carry["axis"]) == 0)
            def _():
                for c in c_refs:
                    c[...] = jnp.zeros(c.shape, f32)
            cin = tuple(c[...] for c in c_refs)
            for s, v in zip(save_refs, cin):
                s[...] = v
            cout, res = fn(cin, *vals)
            for c, v in zip(c_refs, cout):
                c[...] = v
        else:
            res = fn(*vals)
        for o, r, d in zip(out_refs, res, outs):
            _store(o, r, d["acc"])
        if n_r:
            pl.when(step == math.prod(grid) - 1)(lambda: _exchange_wait(*r_args))

    out_shape = [jax.ShapeDtypeStruct(d["shape"], d["dtype"]) for d in outs]
    out_specs = [d["spec"] for d in outs]
    for cs in cshapes:
        out_shape.append(jax.ShapeDtypeStruct(tuple(grid) + tuple(cs), f32))
        out_specs.append(pl.BlockSpec((None,) * ng + tuple(cs), lambda *g, _n=len(cs): tuple(g) + (0,) * _n))
    if n_r:
        out_shape += [jax.ShapeDtypeStruct(g.shape, g.dtype) for g in ride]
        out_specs += [_ANY] * n_r
    res = pl.pallas_call(
        body, name=name, grid=grid, in_specs=list(in_specs) + [_ANY] * n_r, out_specs=out_specs, out_shape=out_shape,
        scratch_shapes=[pltpu.VMEM(tuple(cs), f32) for cs in cshapes] + (_exchange_sems(n_r) if n_r else []),
        compiler_params=_params(grid),
    )(*ins, *(ride or []))
    if n_r:
        return list(res[:n_out]), list(res[n_out:n_out + nc]), list(res[n_out + nc:])
    return list(res[:n_out]), list(res[n_out:])


def run_bwd(name, fn, grid, ins, in_specs, gouts, gout_specs, wants, carry=None, into=None, ride=None):
    n_in, n_g, n_w = len(ins), len(gouts), len(wants)
    n_a = 0 if into is None else 1
    ride = list(ride or [])
    n_r = len(ride)
    saved = carry["saved"] if carry else []
    nc = len(saved)
    ng = len(grid)

    def body(*refs):
        in_refs = refs[:n_in]
        g_refs = refs[n_in:n_in + n_g]
        s_refs = refs[n_in + n_g:n_in + n_g + nc]
        base = n_in + n_g + nc + n_a
        r_in = refs[base:base + n_r]
        w_refs = refs[base + n_r:base + n_r + n_w]
        r_out = refs[base + n_r + n_w:base + 2 * n_r + n_w]
        dc_refs = refs[base + 2 * n_r + n_w:base + 2 * n_r + n_w + nc]
        if n_r:
            steps = math.prod(grid)
            step = pl.program_id(0)
            for ax in range(1, ng):
                step = step * grid[ax] + pl.program_id(ax)
            r_args = (r_in, r_out) + tuple(refs[base + 2 * n_r + n_w + nc:])
            pl.when(step == 0)(lambda: _exchange_start(*r_args))
        vals = [r[...].astype(f32) for r in in_refs]
        gs = tuple(r[...].astype(f32) for r in g_refs)
        if carry:
            @pl.when(pl.program_id(carry["axis"]) == 0)
            def _():
                for c in dc_refs:
                    c[...] = jnp.zeros(c.shape, f32)
            cin = tuple(s[...] for s in s_refs)
            _, vjp = jax.vjp(fn, cin, *vals)
            grads = vjp((tuple(c[...] for c in dc_refs), gs))
            for c, v in zip(dc_refs, grads[0]):
                c[...] = v
            dvals = grads[1:]
        else:
            _, vjp = jax.vjp(fn, *vals)
            dvals = vjp(gs)
        for o, d in zip(w_refs, wants):
            idx = d["idx"]
            val = dvals[idx] if isinstance(idx, int) else jnp.concatenate([dvals[j] for j in idx], axis=1)
            _store(o, val, d["acc"])
        if n_r:
            pl.when(step == steps - 1)(lambda: _exchange_wait(*r_args))

    rev = carry["rev"] if carry else None
    s_specs = []
    for a in saved:
        n = a.ndim - ng
        s_specs.append(pl.BlockSpec((None,) * ng + tuple(a.shape[ng:]), lambda *g, _n=n: tuple(rev(g)) + (0,) * _n))
    res = pl.pallas_call(
        body, name=name, grid=grid, in_specs=list(in_specs) + list(gout_specs) + s_specs + [_ANY] * (n_a + n_r),
        out_specs=[d["spec"] for d in wants] + [_ANY] * n_r,
        out_shape=[jax.ShapeDtypeStruct(d["shape"], d["dtype"]) for d in wants] + [jax.ShapeDtypeStruct(g.shape, g.dtype) for g in ride],
        input_output_aliases={n_in + n_g + nc: 0} if n_a else {},
        scratch_shapes=[pltpu.VMEM(tuple(a.shape[ng:]), f32) for a in saved] + (_exchange_sems(n_r) if n_r else []),
        compiler_params=_params(grid),
    )(*ins, *gouts, *saved, *([into] if n_a else []), *ride)
    return list(res)


def _want(idx, shape, spec, acc=None, dtype=f32):
    d = _out(shape, spec, acc, dtype)
    d["idx"] = idx
    return d


def addn(name, items, t):
    s, w = items[0][0].shape[-2:]
    specs = []
    for a, j in items:
        if j is None:
            specs.append(pl.BlockSpec((t, w), lambda i: (i, 0)))
        else:
            specs.append(pl.BlockSpec((None, t, w), lambda i, _j=j: (_j, i, 0)))

    def fn(*xs):
        y = xs[0]
        for x in xs[1:]:
            y = y + x
        return (y,)

    return run_fwd(name, fn, (s // t,), [a for a, _ in items], specs,
                   [_out((s, w), pl.BlockSpec((t, w), lambda i: (i, 0)))])[0][0]


def _parts(a):
    return [(a, j) for j in range(a.shape[0])]


def _ffn_core(n, wg, wu, wo):
    return (0.5 * mm(_silu(mm(n, wg)) * mm(n, wu), wo),)


def ffn_fwd(name, h, g, wg, wu, wo, t=512, tf=1408):
    s, d = h.shape
    f = wg.shape[1]

    def fn(hh, gg, a, b, c):
        n = _rms(hh, gg)
        return _ffn_core(n, a, b, c)[0] + (pl.program_id(1) == 0).astype(f32) * hh, n

    specs = [pl.BlockSpec((t, d), lambda i, j: (i, 0)), _full(g), pl.BlockSpec((d, tf), lambda i, j: (0, j)),
             pl.BlockSpec((d, tf), lambda i, j: (0, j)), pl.BlockSpec((tf, d), lambda i, j: (j, 0))]
    outs = [_out((s, d), pl.BlockSpec((t, d), lambda i, j: (i, 0)), acc=(1,)),
            _out((s, d), pl.BlockSpec((t, d), lambda i, j: (i, 0)), dtype=bf16)]
    return run_fwd(name, fn, (s // t, f // tf), [h, g, wg, wu, wo], specs, outs)[0]


def ffn_bwd(name, n, wg, wu, wo, gout, t=1024, tf=256, ride=None):
    s, d = n.shape
    t = min(t, s)
    f = wg.shape[1]
    nj = f // tf
    specs = [pl.BlockSpec((t, d), lambda j, i: (i, 0)), pl.BlockSpec((d, tf), lambda j, i: (0, j)),
             pl.BlockSpec((d, tf), lambda j, i: (0, j)), pl.BlockSpec((tf, d), lambda j, i: (j, 0))]
    wants = [_want(0, (nj, s, d), pl.BlockSpec((None, t, d), lambda j, i: (j, i, 0)), dtype=bf16),
             _want(1, wg.shape, pl.BlockSpec((d, tf), lambda j, i: (0, j)), acc=(1,)),
             _want(2, wu.shape, pl.BlockSpec((d, tf), lambda j, i: (0, j)), acc=(1,)),
             _want(3, wo.shape, pl.BlockSpec((tf, d), lambda j, i: (j, 0)), acc=(1,))]
    return run_bwd(name, _ffn_core, (nj, s // t), [n, wg, wu, wo], specs,
                   [gout], [pl.BlockSpec((t, d), lambda j, i: (i, 0))], wants, ride=ride)


def norm_bwd(name, h, g, base, parts, t=512):
    s, d = h.shape

    def fn(hh, gg, bb, *ps):
        dn = ps[0].astype(f32)
        for p in ps[1:]:
            dn = dn + p.astype(f32)
        _, vjp = jax.vjp(_rms, hh, gg)
        dh, dg = vjp(dn)
        return bb + dh, dg

    row = pl.BlockSpec((t, d), lambda i: (i, 0))
    specs = [row, _full(g), row] + [pl.BlockSpec((None, t, d), lambda i, _j=j: (_j, i, 0)) for _, j in parts]
    outs = [_out((s, d), row), _out(g.shape, _full(g), acc=(0,))]
    return run_fwd(name, fn, (s // t,), [h, g, base] + [a for a, _ in parts], specs, outs)[0]


def _lin_tile(u, w, b):
    return (mm(u, w) + b,)


def normlin_fwd(name, h, g, w, b, t=512, tn=1024):
    s, d = h.shape
    n = w.shape[1]

    def fn(hh, gg, ww, bb):
        u = _rms(hh, gg)
        return mm(u, ww) + bb, u

    specs = [pl.BlockSpec((t, d), lambda i, j: (i, 0)), _full(g), pl.BlockSpec((d, tn), lambda i, j: (0, j)),
             pl.BlockSpec((1, tn), lambda i, j: (0, j))]
    outs = [_out((s, n), pl.BlockSpec((t, tn), lambda i, j: (i, j))),
            _out((s, d), pl.BlockSpec((t, d), lambda i, j: (i, 0)), dtype=bf16)]
    return run_fwd(name, fn, (s // t, n // tn), [h, g, w, b], specs, outs)[0]


def lin_fwd(name, u, w, b, t=512, tn=1024):
    s, d = u.shape
    n = w.shape[1]
    specs = [pl.BlockSpec((t, d), lambda i, j: (i, 0)), pl.BlockSpec((d, tn), lambda i, j: (0, j)),
             pl.BlockSpec((1, tn), lambda i, j: (0, j))]
    outs = [_out((s, n), pl.BlockSpec((t, tn), lambda i, j: (i, j)))]
    return run_fwd(name, _lin_tile, (s // t, n // tn), [u, w, b], specs, outs)[0][0]


def lin_bwd(name, u, w, b, gout, t=512, tn=1024):
    s, d = u.shape
    n = w.shape[1]
    nj = n // tn
    specs = [pl.BlockSpec((t, d), lambda j, i: (i, 0)), pl.BlockSpec((d, tn), lambda j, i: (0, j)),
             pl.BlockSpec((1, tn), lambda j, i: (0, j))]
    wants = [_want(0, (nj, s, d), pl.BlockSpec((None, t, d), lambda j, i: (j, i, 0)), dtype=bf16),
             _want(1, w.shape, pl.BlockSpec((d, tn), lambda j, i: (0, j)), acc=(1,)),
             _want(2, b.shape, pl.BlockSpec((1, tn), lambda j, i: (0, j)), acc=(1,))]
    return run_bwd(name, _lin_tile, (nj, s // t), [u, w, b], specs,
                   [gout], [pl.BlockSpec((t, tn), lambda j, i: (i, j))], wants)


def _merge_tile(n_axis, residual):
    def fn(h, y, gp, wb, wo):
        part = mm(jax.nn.sigmoid(gp) * mm(y, wb), wo)
        if residual:
            part = part + (pl.program_id(n_axis) == 0).astype(f32) * h
        return (part,)
    return fn


def merge_fwd(name, h, ys, gpre, wb, wo, t=512):
    s, d = h.shape
    specs = [pl.BlockSpec((t, d), lambda i, n: (i, 0)), pl.BlockSpec((t, BW), lambda i, n: (i, n)),
             pl.BlockSpec((t, d), lambda i, n: (i, n)), pl.BlockSpec((None, BW, d), lambda i, n: (n, 0, 0)), _full(wo)]
    outs = [_out((s, d), pl.BlockSpec((t, d), lambda i, n: (i, 0)), acc=(1,))]
    return run_fwd(name, _merge_tile(1, True), (s // t, 4), [h, ys, gpre, wb, wo], specs, outs)[0][0]


def merge_bwd(name, h, ys, gpre, wb, wo, gout, t=256):
    s, d = h.shape
    specs = [pl.BlockSpec((t, d), lambda n, i: (i, 0)), pl.BlockSpec((t, BW), lambda n, i: (i, n)),
             pl.BlockSpec((t, d), lambda n, i: (i, n)), pl.BlockSpec((None, BW, d), lambda n, i: (n, 0, 0)), _full(wo)]
    wants = [_want(1, ys.shape, pl.BlockSpec((t, BW), lambda n, i: (i, n))),
             _want(2, gpre.shape, pl.BlockSpec((t, d), lambda n, i: (i, n))),
             _want(3, wb.shape, pl.BlockSpec((None, BW, d), lambda n, i: (n, 0, 0)), acc=(1,)),
             _want(4, wo.shape, _full(wo), acc=(0, 1))]
    return run_bwd(name, _merge_tile(0, False), (4, s // t), [h, ys, gpre, wb, wo], specs,
                   [gout], [pl.BlockSpec((t, d), lambda n, i: (i, 0))], wants)


def _ple_tile(residual):
    def fn(h, pe, g, wgate, wproj):
        y = jax.nn.sigmoid(mm(_rms(h, g), wgate)) * mm(pe, wproj)
        return (y + h,) if residual else (y,)
    return fn


def ple_fwd(name, h, pe, g, wgate, wproj, t=512):
    s, d = h.shape
    specs = [pl.BlockSpec((t, d), lambda i: (i, 0)), pl.BlockSpec((t, pe.shape[1]), lambda i: (i, 0)),
             _full(g), _full(wgate), _full(wproj)]
    return run_fwd(name, _ple_tile(True), (s // t,), [h, pe, g, wgate, wproj], specs,
                   [_out((s, d), pl.BlockSpec((t, d), lambda i: (i, 0)))])[0][0]


def ple_bwd(name, h, pe, g, wgate, wproj, gout, t=256):
    s, d = h.shape
    specs = [pl.BlockSpec((t, d), lambda i: (i, 0)), pl.BlockSpec((t, pe.shape[1]), lambda i: (i, 0)),
             _full(g), _full(wgate), _full(wproj)]
    wants = [_want(0, h.shape, pl.BlockSpec((t, d), lambda i: (i, 0))), _want(2, g.shape, _full(g), acc=(0,)),
             _want(3, wgate.shape, _full(wgate), acc=(0,)), _want(4, wproj.shape, _full(wproj), acc=(0,))]
    return run_bwd(name, _ple_tile(True), (s // t,), [h, pe, g, wgate, wproj], specs,
                   [gout], [pl.BlockSpec((t, d), lambda i: (i, 0))], wants)


def final_loss(name, h, g, target, t=512):
    s, d = h.shape

    def fn(hh, gg, tt):
        def loss_fn(a, b):
            err = _rms(a, b) - tt
            return 0.5 * jnp.sum(jnp.mean(err * err, axis=-1, keepdims=True), axis=0, keepdims=True)
        loss, vjp = jax.vjp(loss_fn, hh, gg)
        dh, dgain = vjp(jnp.ones((1, 1), f32))
        return loss, dh, dgain

    specs = [pl.BlockSpec((t, d), lambda i: (i, 0)), _full(g), pl.BlockSpec((t, d), lambda i: (i, 0))]
    outs = [_out((1, 1), pl.BlockSpec((1, 1), lambda i: (0, 0)), acc=(0,)),
            _out((s, d), pl.BlockSpec((t, d), lambda i: (i, 0))), _out(g.shape, _full(g), acc=(0,))]
    return run_fwd(name, fn, (s // t,), [h, g, target], specs, outs)[0]


def _rev(nc, rev):
    return (lambda c: nc - 1 - c) if rev else (lambda c: c)


def _s5_ops_tile(are, aim, lstep, bre, bim):
    step = jnp.exp(lstep)
    mag = jnp.exp(are * step)
    ab_re, ab_im = mag * jnp.cos(aim * step), mag * jnp.sin(aim * step)
    den = are * are + aim * aim
    num_re = ab_re - 1.0
    f_re = (num_re * are + ab_im * aim) / den
    f_im = (ab_im * are - num_re * aim) / den
    bb_re = f_re * bre - f_im * bim
    bb_im = f_re * bim + f_im * bre
    pr = jnp.broadcast_to(ab_re, (LS, ab_re.shape[1]))
    pi = jnp.broadcast_to(ab_im, (LS, ab_im.shape[1]))
    k = 1
    while k < LS:
        pr, pi = _cmul(pr, pi, _shift(pr, k, 1.0), _shift(pi, k, 0.0))
        k *= 2
    return ab_re, ab_im, bb_re, bb_im, pr, pi


def _s5_ops_specs(arrs):
    return [pl.BlockSpec((None,) + a.shape[1:], lambda gb: (gb, 0, 0)) for a in arrs]


def s5_ops_fwd(name, raw):
    shapes = [(4, 1, 512), (4, 1, 512), (4, 128, 512), (4, 128, 512), (4, LS, 512), (4, LS, 512)]
    outs = [_out(sh, pl.BlockSpec((None,) + sh[1:], lambda gb: (gb, 0, 0))) for sh in shapes]
    return run_fwd(name, _s5_ops_tile, (4,), raw, _s5_ops_specs(raw), outs)[0]


def s5_ops_bwd(name, raw, gops):
    wants = [_want(i, a.shape, pl.BlockSpec((None,) + a.shape[1:], lambda gb: (gb, 0, 0))) for i, a in enumerate(raw)]
    return run_bwd(name, _s5_ops_tile, (4,), raw, _s5_ops_specs(raw), gops, _s5_ops_specs(gops), wants)


def _lti_scan_raw(xr, xi, pr, pi, up):
    n = xr.shape[0]
    move = (lambda a, k: _shift_up(a, k)) if up else (lambda a, k: jnp.where(_rows(a.shape) >= k, pltpu.roll(a, k, 0), 0.0))
    k = 1
    while k < n:
        sr, si = _cmul(pr, pi, move(xr, k), move(xi, k))
        xr, xi = xr + sr, xi + si
        pr, pi = _cmul(pr, pi, pr, pi)
        k *= 2
    return xr, xi


@jax.custom_vjp
def _lti_scan(xr, xi, pr, pi):
    return _lti_scan_raw(xr, xi, pr, pi, False)


def _lti_scan_fwd(xr, xi, pr, pi):
    sr, si = _lti_scan_raw(xr, xi, pr, pi, False)
    return (sr, si), (sr, si, pr, pi)


def _lti_scan_bwd(res, g):
    sr, si, pr, pi = res
    mr, mi = _lti_scan_raw(g[0], g[1], pr, -pi, True)
    qr = jnp.where(_rows(sr.shape) >= 1, pltpu.roll(sr, 1, 0), 0.0)
    qi = jnp.where(_rows(si.shape) >= 1, pltpu.roll(si, 1, 0), 0.0)
    dpr = jnp.sum(qr * mr + qi * mi, axis=0, keepdims=True)
    dpi = jnp.sum(qr * mi - qi * mr, axis=0, keepdims=True)
    return mr, mi, dpr, dpi


_lti_scan.defvjp(_lti_scan_fwd, _lti_scan_bwd)


def _s5_tile(carry, u, ab_re, ab_im, bb_re, bb_im, pw_re, pw_im, c_re, c_im, dskip):
    h_re, h_im = carry
    xr, xi = _lti_scan(mm(u, bb_re), mm(u, bb_im), ab_re, ab_im)
    cr, ci = _cmul(pw_re, pw_im, h_re, h_im)
    xr, xi = xr + cr, xi + ci
    y = mm(xr, c_re) - mm(xi, c_im) + dskip * u
    return (_pick_row(xr, LS - 1), _pick_row(xi, LS - 1)), (y,)


def _s5_io(u, ops, c_re, c_im, dskip, nc, rev):
    cm = _rev(nc, rev)
    ins = [u] + list(ops) + [c_re, c_im, dskip]
    specs = [pl.BlockSpec((LS, 128), lambda gb, c: (cm(c), 36 + gb))]
    specs += [pl.BlockSpec((None,) + a.shape[1:], lambda gb, c: (gb, 0, 0)) for a in list(ops) + [c_re, c_im]]
    specs += [pl.BlockSpec((1, 128), lambda gb, c: (0, gb))]
    return ins, specs, cm


def s5_fwd(name, proj, ops, c_re, c_im, dskip, ride=None):
    s = proj.shape[0]
    nc = s // LS
    ins, specs, cm = _s5_io(proj, ops, c_re, c_im, dskip, nc, False)
    outs = [_out((s, BW), pl.BlockSpec((LS, 128), lambda gb, c: (c, gb)))]
    res = run_fwd(name, _s5_tile, (4, nc), ins, specs, outs, carry=dict(shapes=[(1, 512), (1, 512)], axis=1), ride=ride)
    return (res[0][0],) + tuple(res[1:])


def s5_bwd(name, proj, ops, c_re, c_im, dskip, saved, gy, gproj, ride=None):
    s = proj.shape[0]
    nc = s // LS
    ins, specs, cm = _s5_io(proj, ops, c_re, c_im, dskip, nc, True)
    wants = [_want(0, (s, PW), pl.BlockSpec((LS, 128), lambda gb, c: (cm(c), 36 + gb)))]
    for i, a in enumerate(list(ops) + [c_re, c_im]):
        wants.append(_want(1 + i, a.shape, pl.BlockSpec((None,) + a.shape[1:], lambda gb, c: (gb, 0, 0)), acc=(1,)))
    wants.append(_want(9, dskip.shape, pl.BlockSpec((1, 128), lambda gb, c: (0, gb)), acc=(1,)))
    return run_bwd(name, _s5_tile, (4, nc), ins, specs, [gy], [pl.BlockSpec((LS, 128), lambda gb, c: (cm(c), gb))],
                   wants, carry=dict(axis=1, saved=saved, rev=lambda g: (g[0], cm(g[1]))), into=gproj, ride=ride)


def _s5_glu_tile(y, w, b):
    z = _gelu(y)
    return (z * jax.nn.sigmoid(mm(z, w) + b),)


def s5_glu_fwd(name, y, w, b, t=512):
    s = y.shape[0]
    spec = pl.BlockSpec((t, BW), lambda i: (i, 0))
    return run_fwd(name, _s5_glu_tile, (s // t,), [y, w, b], [spec, _full(w), _full(b)], [_out((s, BW), spec)])[0][0]


def s5_glu_bwd(name, y, w, b, gout, gout_col, t=512):
    s = y.shape[0]
    spec = pl.BlockSpec((t, BW), lambda i: (i, 0))
    wants = [_want(0, y.shape, spec), _want(1, w.shape, _full(w), acc=(0,)), _want(2, b.shape, _full(b), acc=(0,))]
    return run_bwd(name, _s5_glu_tile, (s // t,), [y, w, b], [spec, _full(w), _full(b)], [gout],
                   [pl.BlockSpec((t, BW), lambda i: (i, gout_col))], wants)


def _lru_tile(carry, xb, gate, cw, cb, wr, br, wi, bi, lam):
    h_in, prev8 = carry
    xc = _conv4(xb, prev8, cw, cb)
    r = jax.nn.sigmoid(mm(xc, wr) + br)
    ig = jax.nn.sigmoid(mm(xc, wi) + bi)
    log_a = -LRU_C * r * _softplus(-lam)
    a = jnp.exp(log_a)
    b = jnp.sqrt(-_expm1(2.0 * log_a)) * (ig * xc)
    k = 1
    while k < LS:
        b = b + a * _shift(b, k, 0.0)
        a = a * _shift(a, k, 1.0)
        k *= 2
    h = b + a * h_in
    return (_pick_row(h, LS - 1), xb[LS - 8:, :]), (h * _gelu(gate),)


def _lru_io(proj, ws, nc, rev):
    cm = _rev(nc, rev)
    ins = [proj, proj] + list(ws)
    specs = [pl.BlockSpec((LS, BW), lambda c: (cm(c), 4)), pl.BlockSpec((LS, BW), lambda c: (cm(c), 5))]
    specs += [_full(a) for a in ws]
    return ins, specs, cm


def lru_fwd(name, proj, ws):
    s = proj.shape[0]
    nc = s // LS
    ins, specs, cm = _lru_io(proj, ws, nc, False)
    outs = [_out((s, BW), pl.BlockSpec((LS, BW), lambda c: (c, 0)))]
    (y,), saved = run_fwd(name, _lru_tile, (nc,), ins, specs, outs, carry=dict(shapes=[(1, BW), (8, BW)], axis=0))
    return y, saved


def lru_bwd(name, proj, ws, saved, gy, gy_col, gproj):
    s = proj.shape[0]
    nc = s // LS
    ins, specs, cm = _lru_io(proj, ws, nc, True)
    wants = [_want((0, 1), (s, PW), pl.BlockSpec((LS, 2 * BW), lambda c: (cm(c), 2)))]
    wants += [_want(2 + i, a.shape, _full(a), acc=(0,)) for i, a in enumerate(ws)]
    return run_bwd(name, _lru_tile, (nc,), ins, specs, [gy], [pl.BlockSpec((LS, BW), lambda c: (cm(c), gy_col))], wants,
                   carry=dict(axis=0, saved=saved, rev=lambda g: (cm(g[0]),)), into=gproj)


def _causal(n):
    return _rows((n, n)) >= _lanes((n, n))


def _decay(col, rowv):
    causal = _causal(col.shape[0])
    return jnp.where(causal, jnp.exp(jnp.where(causal, col - rowv, 0.0)), 0.0)


def _m2_tile(carry, z, xs_raw, b_raw, c_raw, small, cwx, cwb, cwc, cbx, cbb, cbc, dtb, alog, dsk, ng):
    state, px, pb, pc = carry
    n = CHUNK
    xs = _silu(_conv4(xs_raw, px, cwx, cbx))
    bm = _silu(_conv4(b_raw, pb, cwb, cbb))
    cmx = _silu(_conv4(c_raw, pc, cwc, cbc))
    expand = (_lanes((16, BW)) // 64 == _rows((16, BW))).astype(f32)
    tri = _causal(n).astype(f32)
    triu = (_rows((n, n)) <= _lanes((n, n))).astype(f32)
    dt = _softplus(small + dtb)
    da = dt * (-jnp.exp(alog))
    cs = lmm(tri, da)
    cs_t = rmm_tn(da, triu)
    cs_w = rmm(cs, expand)
    last_w = _pick_row(cs_w, n - 1)
    xdt = xs * rmm(dt, expand)
    g0 = _lanes((n, BW)) < 256
    bm0, bm1, cm0, cm1 = bm[:, :128], bm[:, 128:], cmx[:, :128], cmx[:, 128:]
    cb0, cb1 = mm_nt(cm0, bm0), mm_nt(cm1, bm1)
    y = jnp.where(g0, mm(cm0, state), mm(cm1, state)) * jnp.exp(cs_w)
    for h in range(8):
        sc = (cb0 if h < 4 else cb1) * _decay(_pick_lane(cs, h), _pick_row(cs_t, h))
        y = y + jnp.where(_lanes((n, BW)) // 64 == h, mm(sc, xdt), 0.0)
    xd = xdt * jnp.exp(last_w - cs_w)
    g0s = _lanes((128, BW)) < 256
    state_out = state * jnp.exp(last_w) + jnp.where(g0s, mm_tn(bm0, xd), mm_tn(bm1, xd))
    y = (y + dsk * xs) * _silu(z)
    return (state_out, xs_raw[n - 8:, :], b_raw[n - 8:, :], c_raw[n - 8:, :]), (_rms(y, ng),)


def _m2_io(proj, small, ws, nc, rev):
    cm = _rev(nc, rev)
    n = CHUNK
    ins = [proj, proj, proj, proj, small] + list(ws)
    specs = [pl.BlockSpec((n, BW), lambda c: (cm(c), 6)), pl.BlockSpec((n, BW), lambda c: (cm(c), 7)),
             pl.BlockSpec((n, 256), lambda c: (cm(c), 16)), pl.BlockSpec((n, 256), lambda c: (cm(c), 17)),
             pl.BlockSpec((n, 16), lambda c: (cm(c), 0))]
    specs += [_full(a) for a in ws]
    return ins, specs, cm


_M2_CARRY = [(128, BW), (8, BW), (8, 256), (8, 256)]


def m2_fwd(name, proj, small, ws):
    s = proj.shape[0]
    nc = s // CHUNK
    ins, specs, cm = _m2_io(proj, small, ws, nc, False)
    outs = [_out((s, BW), pl.BlockSpec((CHUNK, BW), lambda c: (c, 0)))]
    (y,), saved = run_fwd(name, _m2_tile, (nc,), ins, specs, outs, carry=dict(shapes=_M2_CARRY, axis=0))
    return y, saved


def m2_bwd(name, proj, small, ws, saved, gy, gy_col, gproj):
    s = proj.shape[0]
    nc = s // CHUNK
    ins, specs, cm = _m2_io(proj, small, ws, nc, True)
    n = CHUNK
    wants = [_want((0, 1, 2, 3), (s, PW), pl.BlockSpec((n, 3 * BW), lambda c: (cm(c), 2))),
             _want(4, (s, 16), pl.BlockSpec((n, 16), lambda c: (cm(c), 0)))]
    wants += [_want(5 + i, a.shape, _full(a), acc=(0,)) for i, a in enumerate(ws)]
    return run_bwd(name, _m2_tile, (nc,), ins, specs, [gy], [pl.BlockSpec((n, BW), lambda c: (cm(c), gy_col))], wants,
                   carry=dict(axis=0, saved=saved, rev=lambda g: (cm(g[0]),)), into=gproj)


def _l2n(x):
    return x * lax.rsqrt(jnp.sum(x * x, axis=-1, keepdims=True) + EPS)


def _gdn_tile(carry, q_raw, k_raw, v_raw, gate, small, cwq, cwk, cwv, dtb, alog, ng):
    state, pq, pk, pv = carry
    n, nh = CHUNK, 4
    nn_ = n * nh
    qc = _silu(_conv4(q_raw, pq, cwq, None))
    kc = _silu(_conv4(k_raw, pk, cwk, None))
    vc = _silu(_conv4(v_raw, pv, cwv, None))
    beta16 = jax.nn.sigmoid(small)
    g16 = -jnp.exp(alog) * _softplus(small + dtb)
    tri = _causal(n).astype(f32)
    triu = (_rows((n, n)) <= _lanes((n, n))).astype(f32)
    cs16 = lmm(tri, g16)
    cs16_t = rmm_tn(g16, triu)
    lanes_of = lambda h: slice(128 * h, 128 * (h + 1))
    rows_of = lambda h: slice(n * h, n * (h + 1))
    stack = lambda f: jnp.concatenate([f(h) for h in range(nh)], axis=0)
    q = stack(lambda h: _l2n(qc[:, lanes_of(h)]) * (128 ** -0.5))
    k = stack(lambda h: _l2n(kc[:, lanes_of(h)]))
    v = stack(lambda h: vc[:, lanes_of(h)])
    beta = stack(lambda h: _pick_lane(beta16, 8 + h))
    col = stack(lambda h: _pick_lane(cs16, 12 + h))
    last_h = [_pick_row(_pick_lane(cs16, 12 + h), n - 1) for h in range(nh)]
    last = stack(lambda h: jnp.broadcast_to(last_h[h], (n, 1)))
    last_w = jnp.concatenate([jnp.broadcast_to(last_h[h], (1, 128)) for h in range(nh)], axis=1)
    spread = (_rows((n, nn_)) == _lanes((n, nn_)) % n).astype(f32)
    cs_w = rmm(cs16_t, spread)
    rowv = jnp.sum(jnp.where(_rows((16, nn_)) == 12 + _lanes((16, nn_)) // n, cs_w, 0.0), axis=0, keepdims=True)
    same = (_rows((nn_, nn_)) // n) == (_lanes((nn_, nn_)) // n)
    causal = jnp.logical_and(same, _rows((nn_, nn_)) >= _lanes((nn_, nn_)))
    strict = jnp.logical_and(same, _rows((nn_, nn_)) > _lanes((nn_, nn_)))
    decay = jnp.where(causal, jnp.exp(jnp.where(causal, col - rowv, 0.0)), 0.0)
    kb = k * beta
    t = _tri_inv(jnp.where(strict, mm_nt(kb, k) * decay, 0.0))
    e_col = jnp.exp(col)
    uw = xmm(t, jnp.concatenate([v * beta, kb * e_col], axis=1))
    u, w = uw[:, :128], uw[:, 128:]
    qk = mm_nt(q, k) * decay
    own = lambda r: stack(lambda h: r[rows_of(h), lanes_of(h)])
    v_new = u - own(mm(w, state))
    o = own(mm(q * e_col, state)) + mm(qk, v_new)
    zero = jnp.zeros((n, 128), f32)
    v_blocks = stack(lambda h: jnp.concatenate([v_new[rows_of(h), :] if j == h else zero for j in range(nh)], axis=1))
    state_out = state * jnp.exp(last_w) + mm_tn(k * jnp.exp(last - col), v_blocks)
    gt = stack(lambda h: gate[:, lanes_of(h)])
    out = _rms(o, ng) * _silu(gt)
    out = jnp.concatenate([out[rows_of(h), :] for h in range(nh)], axis=1)
    return (state_out, q_raw[n - 8:, :], k_raw[n - 8:, :], v_raw[n - 8:, :]), (out,)


def _gdn_io(proj, small, cw, dtb, alog, ng, nc, rev):
    cm = _rev(nc, rev)
    n = CHUNK
    ins = [proj, proj, proj, proj, small, cw, cw, cw, dtb, alog, ng]
    specs = [pl.BlockSpec((n, BW), lambda c, _j=j: (cm(c), _j)) for j in (0, 1, 2, 3)]
    specs += [pl.BlockSpec((n, 16), lambda c: (cm(c), 0))]
    specs += [pl.BlockSpec((4, BW), lambda c, _j=j: (0, _j)) for j in (0, 1, 2)]
    specs += [_full(dtb), _full(alog), _full(ng)]
    return ins, specs, cm


_GDN_CARRY = [(128, BW)] + [(8, BW)] * 3


def gdn_fwd(name, proj, small, cw, dtb, alog, ng, ride=None):
    s = proj.shape[0]
    nc = s // CHUNK
    ins, specs, cm = _gdn_io(proj, small, cw, dtb, alog, ng, nc, False)
    outs = [_out((s, BW), pl.BlockSpec((CHUNK, BW), lambda c: (c, 0)))]
    res = run_fwd(name, _gdn_tile, (nc,), ins, specs, outs, carry=dict(shapes=_GDN_CARRY, axis=0), ride=ride)
    return (res[0][0],) + tuple(res[1:])


def gdn_bwd(name, proj, small, cw, dtb, alog, ng, saved, gy, gy_col, ride=None):
    s = proj.shape[0]
    nc = s // CHUNK
    n = CHUNK
    ins, specs, cm = _gdn_io(proj, small, cw, dtb, alog, ng, nc, True)
    wants = [_want((0, 1, 2, 3), (s, PW), pl.BlockSpec((n, 4 * BW), lambda c: (cm(c), 0))),
             _want(4, (s, 16), pl.BlockSpec((n, 16), lambda c: (cm(c), 0)))]
    wants += [_want(5 + i, (4, BW), pl.BlockSpec((4, BW), lambda c: (0, 0)), acc=(0,)) for i in range(3)]
    wants += [_want(8, dtb.shape, _full(dtb), acc=(0,)), _want(9, alog.shape, _full(alog), acc=(0,)),
              _want(10, ng.shape, _full(ng), acc=(0,))]
    return run_bwd(name, _gdn_tile, (nc,), ins, specs, [gy], [pl.BlockSpec((n, BW), lambda c: (cm(c), gy_col))], wants,
                   carry=dict(axis=0, saved=saved, rev=lambda g: (cm(g[0]),)), ride=ride)


def _perm_cols(w):
    pad = jnp.zeros(w.shape[:-1] + (PW - IN_WIDTH,), w.dtype)
    return jnp.concatenate([w[..., 3080:5128], w[..., 512:3072], w[..., :512], w[..., 3072:3080], w[..., 5128:5136], pad], axis=-1)


def _unperm_cols(g):
    return jnp.concatenate([g[..., 4608:5120], g[..., 2048:4608], g[..., 5120:5128], g[..., :2048], g[..., 5128:5136]], axis=-1)


def _bd(blocks):
    n, a, b = blocks.shape
    eye = jnp.eye(n, dtype=blocks.dtype)
    return jnp.einsum("nab,nm->namb", blocks, eye).reshape(n * a, n * b)


def _layer_layout(lw):
    o = {}
    row = lambda a: a.reshape(1, -1)
    o["b_gate"] = row(lw["b_gate"])
    o["s5_are"] = lw["s5_a_re"].reshape(4, 1, 512)
    o["s5_aim"] = lw["s5_a_im"].reshape(4, 1, 512)
    o["s5_lstep"] = jnp.repeat(lw["s5_log_step"], 64).reshape(4, 1, 512)
    bt = lambda b: jax.vmap(_bd)(jnp.swapaxes(b, 1, 2).reshape(4, 8, 16, 64))
    o["s5_bre"], o["s5_bim"] = bt(lw["s5_b_re"]), bt(lw["s5_b_im"])
    ct = lambda c: jax.vmap(_bd)(jnp.swapaxes(c, 1, 2).reshape(4, 8, 64, 16))
    o["s5_cre"], o["s5_cim"] = ct(lw["s5_c_re"]), ct(lw["s5_c_im"])
    o["s5_d"] = row(lw["s5_d"])
    o["s5_b_glu"] = row(lw["s5_b_glu"])
    o["lru_conv_b"], o["lru_b_r"], o["lru_b_i"], o["lru_lambda"] = (row(lw[k]) for k in ("lru_conv_b", "lru_b_r", "lru_b_i", "lru_lambda"))
    o["lru_wr"], o["lru_wi"] = _bd(lw["lru_w_r"]), _bd(lw["lru_w_i"])
    cw, cb = lw["m2_conv_w"], lw["m2_conv_b"]
    o["m2_cwx"], o["m2_cwb"], o["m2_cwc"] = cw[:, :512], cw[:, 512:768], cw[:, 768:]
    o["m2_cbx"], o["m2_cbb"], o["m2_cbc"] = row(cb[:512]), row(cb[512:768]), row(cb[768:])
    o["m2_dtb"] = jnp.pad(lw["m2_dt_bias"], (0, 8)).reshape(1, 16)
    o["m2_alog"] = jnp.pad(lw["m2_a_log"], (0, 8)).reshape(1, 16)
    o["m2_dsk"] = jnp.repeat(lw["m2_d"], 64).reshape(1, 512)
    o["m2_norm"] = row(lw["m2_norm"])
    o["gdn_dtb"] = jnp.pad(lw["gdn_dt_bias"], (12, 0)).reshape(1, 16)
    o["gdn_alog"] = jnp.pad(lw["gdn_a_log"], (12, 0)).reshape(1, 16)
    o["gdn_norm"] = row(lw["gdn_norm"])
    for k in ("ffn1_norm", "mix_norm", "ffn2_norm", "ple_norm"):
        o[k] = row(lw[k])
    return o


_BIG_PLAIN = ("w_gate", "s5_w_glu", "w_branch", "w_out", "ple_w_gate", "ple_w_proj")


def big_layout(w):
    o = {k: w[k] for k in _BIG_PLAIN}
    for f in ("ffn1", "ffn2"):
        o[f + "_wg"], o[f + "_wu"] = w[f + "_w_in"][:, :FFN_DIM], w[f + "_w_in"][:, FFN_DIM:]
        o[f + "_wo"] = w[f + "_w_out"]
    o["w_in"] = _perm_cols(w["w_in"])
    return {k: v.astype(bf16) for k, v in o.items()}


def big_unlayout(g):
    o = {k: g[k] for k in _BIG_PLAIN}
    for f in ("ffn1", "ffn2"):
        o[f + "_w_in"] = jnp.concatenate([g[f + "_wg"], g[f + "_wu"]], axis=1)
        o[f + "_w_out"] = g[f + "_wo"]
    o["w_in"] = _unperm_cols(g["w_in"])
    return o


_SMALL_KEYS = ("b_gate", "s5_log_step", "s5_a_re", "s5_a_im", "s5_b_re", "s5_b_im", "s5_c_re", "s5_c_im", "s5_d",
               "s5_b_glu", "lru_conv_b", "lru_w_r", "lru_b_r", "lru_w_i", "lru_b_i", "lru_lambda", "m2_conv_w",
               "m2_conv_b", "m2_dt_bias", "m2_a_log", "m2_d", "m2_norm", "gdn_dt_bias", "gdn_a_log", "gdn_norm",
               "ffn1_norm", "mix_norm", "ffn2_norm", "ple_norm")


def _layer_fwd(i, h0, pe, lw, big, ride=None):
    ride = ride or (None, None)
    n = f"l{i}_"
    lay = _layer_layout(lw)
    a = {"h0": h0, "lay": lay}
    h1, n1 = ffn_fwd(n + "ffn1_fwd", h0, lay["ffn1_norm"], big["ffn1_wg"], big["ffn1_wu"], big["ffn1_wo"])
    zero_b = jnp.zeros((1, PW), f32)
    proj, u = normlin_fwd(n + "inproj_fwd", h1, lay["mix_norm"], big["w_in"], zero_b, tn=896)
    gpre = lin_fwd(n + "gate_fwd", u, big["w_gate"], lay["b_gate"], tn=1024)
    small = proj[:, SMALL_OFF:SMALL_OFF + 16]
    raw = [lay["s5_are"], lay["s5_aim"], lay["s5_lstep"], lay["s5_bre"], lay["s5_bim"]]
    ops = s5_ops_fwd(n + "s5ops_fwd", raw)
    y5, sv5, *rode5 = s5_fwd(n + "s5_fwd", proj, ops, lay["s5_cre"], lay["s5_cim"], lay["s5_d"], ride[0])
    ya = s5_glu_fwd(n + "s5glu_fwd", y5, big["s5_w_glu"], lay["s5_b_glu"])
    lru_ws = [lw["lru_conv_w"], lay["lru_conv_b"], lay["lru_wr"], lay["lru_b_r"], lay["lru_wi"], lay["lru_b_i"], lay["lru_lambda"]]
    yb, svb = lru_fwd(n + "lru_fwd", proj, lru_ws)
    m2_ws = [lay[k] for k in ("m2_cwx", "m2_cwb", "m2_cwc", "m2_cbx", "m2_cbb", "m2_cbc", "m2_dtb", "m2_alog", "m2_dsk", "m2_norm")]
    yc, svc = m2_fwd(n + "m2_fwd", proj, small, m2_ws)
    yd, svd, *roded = gdn_fwd(n + "gdn_fwd", proj, small, lw["gdn_conv_w"], lay["gdn_dtb"], lay["gdn_alog"], lay["gdn_norm"], ride[1])
    ys = jnp.concatenate([ya, yb, yc, yd], axis=1)
    h2 = merge_fwd(n + "merge_fwd", h1, ys, gpre, big["w_branch"], big["w_out"])
    h3, n2 = ffn_fwd(n + "ffn2_fwd", h2, lay["ffn2_norm"], big["ffn2_wg"], big["ffn2_wu"], big["ffn2_wo"])
    h4 = ple_fwd(n + "ple_fwd", h3, pe, lay["ple_norm"], big["ple_w_gate"], big["ple_w_proj"])
    a.update(h1=h1, n1=n1, u=u, n2=n2, proj=proj, gpre=gpre, small=small, raw=raw, ops=ops, y5=y5, sv5=sv5, lru_ws=lru_ws, svb=svb,
             m2_ws=m2_ws, svc=svc, svd=svd, ys=ys, h2=h2, h3=h3, pe=pe)
    return h4, a, (rode5[0] if rode5 else None, roded[0] if roded else None)


def small_into(name, a, b, gproj, t=512):
    s, w = a.shape
    pad_w = PW - SMALL_OFF

    def body(a_ref, b_ref, _, o_ref):
        place = (_rows((w, pad_w)) == _lanes((w, pad_w))).astype(f32)
        o_ref[...] = _dg(a_ref[...] + b_ref[...], place, _NN, (3, 1))

    row = pl.BlockSpec((t, w), lambda i: (i, 0))
    return pl.pallas_call(
        body, name=name, grid=(s // t,), in_specs=[row, row, _ANY],
        out_specs=pl.BlockSpec((t, pad_w), lambda i: (i, SMALL_OFF // pad_w)),
        out_shape=jax.ShapeDtypeStruct(gproj.shape, gproj.dtype), input_output_aliases={2: 0}, compiler_params=_params((1,)),
    )(a, b, gproj)


RIDE_S5 = ("ffn2_w_in", "ffn2_w_out", "w_branch", "w_out", "ple_w_gate", "ple_w_proj", "s5_w_glu")
RIDE_FFN1 = ("w_in", "w_gate", "lru_conv_w", "m2_conv_w", "gdn_conv_w")
RIDE_LAST = ("ffn1_w_in", "ffn1_w_out")


def _grad_of(k, gb, small):
    if k in ("ffn1_w_in", "ffn2_w_in"):
        return jnp.concatenate([gb[k[:4] + "_wg"], gb[k[:4] + "_wu"]], axis=1)
    if k in ("ffn1_w_out", "ffn2_w_out"):
        return gb[k[:4] + "_wo"]
    if k == "w_in":
        return _unperm_cols(gb[k])
    if k == "m2_conv_w" and k not in small:
        return jnp.concatenate([small["m2_cwx"], small["m2_cwb"], small["m2_cwc"]], axis=1)
    return gb[k] if k in gb else small[k]


def _scattered(names, gb, small):
    return [_full_to_scattered(k, _grad_of(k, gb, small)[None]).astype(bf16) for k in names]


def _layer_bwd(i, a, lw, big, gh4, ride=None):
    n = f"l{i}_"
    lay = a["lay"]
    gb, gl = {}, {}
    s = gh4.shape[0]
    t = 512
    gh3, gl["ple_norm"], gb["ple_w_gate"], gb["ple_w_proj"] = ple_bwd(
        n + "ple_bwd", a["h3"], a["pe"], lay["ple_norm"], big["ple_w_gate"], big["ple_w_proj"], gh4)
    dn2, gb["ffn2_wg"], gb["ffn2_wu"], gb["ffn2_wo"] = ffn_bwd(
        n + "ffn2_bwd", a["n2"], big["ffn2_wg"], big["ffn2_wu"], big["ffn2_wo"], gh3)
    gh2, gl["ffn2_norm"] = norm_bwd(n + "gh2", a["h2"], lay["ffn2_norm"], gh3, _parts(dn2))
    gys, ggpre, gb["w_branch"], gb["w_out"] = merge_bwd(
        n + "merge_bwd", a["h1"], a["ys"], a["gpre"], big["w_branch"], big["w_out"], gh2)
    rd = gdn_bwd(n + "gdn_bwd", a["proj"], a["small"], lw["gdn_conv_w"], lay["gdn_dtb"], lay["gdn_alog"], lay["gdn_norm"], a["svd"], gys, 3, ride)
    gproj, dsm_d = rd[:2]
    gl["gdn_conv_w"] = jnp.concatenate(rd[2:5], axis=1)
    gl["gdn_dtb"], gl["gdn_alog"], gl["gdn_norm"] = rd[5:8]
    rode = rd[8:]
    rb = lru_bwd(n + "lru_bwd", a["proj"], a["lru_ws"], a["svb"], gys, 1, gproj)
    gproj = rb[0]
    gl["lru_conv_w"], gl["lru_conv_b"], gl["lru_wr"], gl["lru_b_r"], gl["lru_wi"], gl["lru_b_i"], gl["lru_lambda"] = rb[1:]
    rc = m2_bwd(n + "m2_bwd", a["proj"], a["small"], a["m2_ws"], a["svc"], gys, 2, gproj)
    gproj, dsm_c = rc[:2]
    for k, v in zip(("m2_cwx", "m2_cwb", "m2_cwc", "m2_cbx", "m2_cbb", "m2_cbc", "m2_dtb", "m2_alog", "m2_dsk", "m2_norm"), rc[2:]):
        gl[k] = v
    gy5, gb["s5_w_glu"], gl["s5_b_glu"] = s5_glu_bwd(n + "s5glu_bwd", a["y5"], big["s5_w_glu"], lay["s5_b_glu"], gys, 0)
    own = {}
    r5 = s5_bwd(n + "s5_bwd", a["proj"], a["ops"], lay["s5_cre"], lay["s5_cim"], lay["s5_d"], a["sv5"], gy5, gproj,
                _scattered(RIDE_S5, gb, gl) if ride is not None else None)
    gproj, gops, gl["s5_cre"], gl["s5_cim"], gl["s5_d"] = r5[0], r5[1:7], r5[7], r5[8], r5[9]
    own.update(zip(RIDE_S5, r5[10:]))
    gl["s5_are"], gl["s5_aim"], gl["s5_lstep"], gl["s5_bre"], gl["s5_bim"] = s5_ops_bwd(n + "s5ops_bwd", a["raw"], gops)
    gproj = small_into(n + "gsmall", dsm_c, dsm_d, gproj)
    zero_b = jnp.zeros((1, PW), f32)
    du_p, gb["w_in"], _ = lin_bwd(n + "inproj_bwd", a["u"], big["w_in"], zero_b, gproj, tn=896)
    du_g, gb["w_gate"], gl["b_gate"] = lin_bwd(n + "gate_bwd", a["u"], big["w_gate"], lay["b_gate"], ggpre, tn=1024)
    gh1, gl["mix_norm"] = norm_bwd(n + "gh1", a["h1"], lay["mix_norm"], gh2, _parts(du_p) + _parts(du_g))
    rf = ffn_bwd(n + "ffn1_bwd", a["n1"], big["ffn1_wg"], big["ffn1_wu"], big["ffn1_wo"], gh1,
                 ride=_scattered(RIDE_FFN1, gb, gl) if ride is not None else None)
    dn1, gb["ffn1_wg"], gb["ffn1_wu"], gb["ffn1_wo"] = rf[:4]
    own.update(zip(RIDE_FFN1, rf[4:]))
    gh0, gl["ffn1_norm"] = norm_bwd(n + "gh0", a["h0"], lay["ffn1_norm"], gh1, _parts(dn1))
    return gh0, gb, gl, (rode, own)


def _local_step(x, p, target, bigs, smalls, final_norm, ride_of=None, next_layer=None):
    h = x
    acts = []
    bigs, smalls = list(bigs), list(smalls)
    for i in range(DEPTH):
        h, a, got = _layer_fwd(i, h, p[i], smalls[i], bigs[i], next_layer[0] if next_layer and i == DEPTH - 2 else None)
        if next_layer and i == DEPTH - 2:
            bigs[i + 1], smalls[i + 1] = next_layer[1](got)
        acts.append(a)
    fg = final_norm.reshape(1, -1)
    loss, gh, gfn = final_loss("final_loss", h, fg, target)
    gbs, gss, rode = [None] * DEPTH, [None] * DEPTH, [None] * DEPTH
    for i in reversed(range(DEPTH)):
        ride = ride_of(gbs[i + 1], gss[i + 1]) if ride_of is not None and i + 1 < DEPTH else None
        gh, gb, gl, (got, own) = _layer_bwd(i, acts[i], smalls[i], bigs[i], gh, ride)
        if ride is not None:
            rode[i + 1], rode[i] = got, own
        _, pull = jax.vjp(_layer_layout, smalls[i])
        lay_g = {k: gl[k] for k in acts[i]["lay"]}
        gs = pull(lay_g)[0]
        gs = dict(gs)
        gs["lru_conv_w"] = gl["lru_conv_w"]
        gs["gdn_conv_w"] = gl["gdn_conv_w"]
        gbs[i], gss[i] = gb, gs
    return loss, gh, gbs, gss, gfn.reshape(-1), rode


MESH_AXES = ("x", "y", "c")
PACK_W = 1024
PACK_ROWS = 256

W_NAMES = ("ffn1_norm", "ffn1_w_in", "ffn1_w_out", "mix_norm", "w_in", "w_gate", "b_gate", "s5_log_step", "s5_a_re",
           "s5_a_im", "s5_b_re", "s5_b_im", "s5_c_re", "s5_c_im", "s5_d", "s5_w_glu", "s5_b_glu", "lru_conv_w",
           "lru_conv_b", "lru_w_r", "lru_b_r", "lru_w_i", "lru_b_i", "lru_lambda", "m2_conv_w", "m2_conv_b",
           "m2_dt_bias", "m2_a_log", "m2_d", "m2_norm", "gdn_conv_w", "gdn_dt_bias", "gdn_a_log", "gdn_norm",
           "w_branch", "w_out", "ffn2_norm", "ffn2_w_in", "ffn2_w_out", "ple_norm", "ple_w_gate", "ple_w_proj",
           "final_norm")
COL_SHARDED = ("ffn1_w_in", "w_in", "w_gate", "lru_conv_w", "m2_conv_w", "gdn_conv_w", "w_branch", "ffn2_w_in", "ple_w_proj")
ROW_SHARDED = ("ffn1_w_out", "s5_w_glu", "w_out", "ffn2_w_out", "ple_w_gate")
BIG_NAMES = ("ffn1_w_in", "ffn1_w_out", "w_in", "w_gate", "s5_w_glu", "w_branch", "w_out", "ffn2_w_in", "ffn2_w_out",
             "ple_w_gate", "ple_w_proj")
CONV_NAMES = ("lru_conv_w", "m2_conv_w", "gdn_conv_w")
SHARDED = BIG_NAMES + CONV_NAMES
REPLICATED = tuple(k for k in W_NAMES if k not in SHARDED)


def _gathered_to_full(name, g):
    if name in COL_SHARDED:
        g = jnp.moveaxis(g, 0, -2)
        return g.reshape(g.shape[:-2] + (g.shape[-2] * g.shape[-1],))
    g = jnp.moveaxis(g, 0, 1)
    return g.reshape((g.shape[0], g.shape[1] * g.shape[2]) + g.shape[3:])


def _full_to_scattered(name, w):
    if name in COL_SHARDED:
        w = w.reshape(w.shape[:-1] + (N_DEV, w.shape[-1] // N_DEV))
        return jnp.moveaxis(w, -2, 0)
    w = w.reshape((w.shape[0], N_DEV, w.shape[1] // N_DEV) + w.shape[2:])
    return jnp.moveaxis(w, 1, 0)


def _pack(arrs):
    pieces = []
    for a in arrs:
        k = -(-a.size // PACK_W)
        pieces.append(jnp.pad(a.reshape(-1), (0, k * PACK_W - a.size)).reshape(k, PACK_W))
    buf = jnp.concatenate(pieces, axis=0)
    return jnp.pad(buf, ((0, -buf.shape[0] % PACK_ROWS), (0, 0)))


def _unpack(buf, shapes):
    out, r = [], 0
    for sh in shapes:
        n = math.prod(sh)
        k = -(-n // PACK_W)
        out.append(buf[r:r + k].reshape(-1)[:n].reshape(sh))
        r += k
    return out


def _peer(k):
    mx, my, mc = (lax.axis_index(a) for a in MESH_AXES)
    px = 1 - mx if k & 4 else mx
    py = 1 - my if k & 2 else my
    pc = 1 - mc if k & 1 else mc
    return (px, py, pc), 4 * px + 2 * py + pc


def all_gather(name, xs):
    n = len(xs)

    def body(*refs):
        x_refs, out_refs = refs[:n], refs[n:2 * n]
        send_sems, recv_sems, local_sems = refs[2 * n:]
        mx, my, mc = (lax.axis_index(a) for a in MESH_AXES)
        me, sibling = (mx, my, mc), (mx, my, 1 - mc)
        chips = [(1 - mx, my), (mx, 1 - my), (1 - mx, 1 - my)]

        def slot(i, px, py, pc):
            return out_refs[i].at[4 * px + 2 * py + pc]

        def copy(k, i, block, to, src=None):
            return pltpu.make_async_remote_copy(
                src_ref=slot(i, *block) if src is None else src, dst_ref=slot(i, *block),
                send_sem=send_sems.at[k, i], recv_sem=recv_sems.at[k, i], device_id=to, device_id_type=pl.DeviceIdType.MESH)

        mine = [pltpu.make_async_copy(x_refs[i], slot(i, *me), local_sems.at[i]) for i in range(n)]
        first = []
        for i in range(n):
            mine[i].start()
            first.append(copy(0, i, me, sibling, src=x_refs[i]))
            first += [copy(1 + j, i, me, (*chip, mc), src=x_refs[i]) for j, chip in enumerate(chips)]
        for cp in first:
            cp.start()
        passed = []
        for i in range(n):
            for j, chip in enumerate(chips):
                copy(1 + j, i, (*chip, mc), me).wait_recv()
                cp = copy(4 + j, i, (*chip, mc), sibling)
                cp.start()
                passed.append(cp)
        for i in range(n):
            copy(0, i, sibling, me).wait_recv()
            for j, chip in enumerate(chips):
                copy(4 + j, i, (*chip, 1 - mc), me).wait_recv()
        for cp in first + passed:
            cp.wait_send()
        for cp in mine:
            cp.wait()

    res = pl.pallas_call(
        body, name=name, out_shape=[jax.ShapeDtypeStruct((N_DEV,) + x.shape, x.dtype) for x in xs],
        in_specs=[_ANY] * n, out_specs=[_ANY] * n,
        scratch_shapes=[pltpu.SemaphoreType.DMA((7, n)), pltpu.SemaphoreType.DMA((7, n)), pltpu.SemaphoreType.DMA((n,))],
    )(*xs)
    return list(res)


def _exchange_copies(g_refs, out_refs, send_sems, recv_sems, local_sems, with_incoming=True):
    mx, my, mc = (lax.axis_index(a) for a in MESH_AXES)
    me = 4 * mx + 2 * my + mc
    n = len(g_refs)
    local = [pltpu.make_async_copy(g_refs[i].at[me], out_refs[i].at[me], local_sems.at[i]) for i in range(n)]
    outgoing, incoming = [], []
    for k in range(1, N_DEV):
        peer, pidx = _peer(k)
        for i in range(n):
            sems = dict(send_sem=send_sems.at[k - 1, i], recv_sem=recv_sems.at[k - 1, i], device_id=peer,
                        device_id_type=pl.DeviceIdType.MESH)
            outgoing.append(pltpu.make_async_remote_copy(src_ref=g_refs[i].at[pidx], dst_ref=out_refs[i].at[me], **sems))
            if with_incoming:
                incoming.append(pltpu.make_async_remote_copy(src_ref=g_refs[i].at[pidx], dst_ref=out_refs[i].at[pidx], **sems))
    return local, outgoing, incoming


def _exchange_start(*refs):
    local, outgoing, _ = _exchange_copies(*refs, with_incoming=False)
    for cp in local + outgoing:
        cp.start()


def _exchange_wait(*refs):
    local, outgoing, incoming = _exchange_copies(*refs)
    for cp in incoming:
        cp.wait_recv()
    for cp in outgoing:
        cp.wait_send()
    for cp in local:
        cp.wait()


def _exchange_sems(n):
    return [pltpu.SemaphoreType.DMA((7, n)), pltpu.SemaphoreType.DMA((7, n)), pltpu.SemaphoreType.DMA((n,))]


def exchange(name, gs):
    n = len(gs)

    def body(*refs):
        args = (refs[:n], refs[n:2 * n]) + tuple(refs[2 * n:])
        _exchange_start(*args)
        _exchange_wait(*args)

    res = pl.pallas_call(
        body, name=name, out_shape=[jax.ShapeDtypeStruct(g.shape, g.dtype) for g in gs], in_specs=[_ANY] * n, out_specs=[_ANY] * n,
        scratch_shapes=_exchange_sems(n),
    )(*gs)
    return list(res)


def sum_slots(name, buf):
    return addn(name, _parts(buf), PACK_ROWS)


def _adamw_math(ww, gg, mm_, vv):
    m2 = ADAM_B1 * mm_ + (1.0 - ADAM_B1) * gg
    v2 = ADAM_B2 * vv + (1.0 - ADAM_B2) * (gg * gg)
    m_hat = m2 / (1.0 - ADAM_B1 ** ADAM_STEP)
    v_hat = v2 / (1.0 - ADAM_B2 ** ADAM_STEP)
    delta = -ADAM_LR * (m_hat / (jnp.sqrt(v_hat) + ADAM_EPS) + ADAM_WD * ww)
    return delta, m2, v2


def adamw(name, w, g, m, v):
    spec = pl.BlockSpec((PACK_ROWS, PACK_W), lambda i: (i, 0))
    return run_fwd(name, _adamw_math, (w.shape[0] // PACK_ROWS,), [w, g, m, v], [spec] * 4, [_out(w.shape, spec)] * 3)[0]


def reduce_adamw(name, slots, w, m, v):
    shape = w.shape
    r, c = shape[-2:]
    a = math.prod(shape[:-2])
    per = a // DEPTH
    tr = r
    while tr * c * 4 > (1 << 20) and tr % 16 == 0:
        tr //= 2
    s3 = [s.reshape((N_DEV, per, r, c)) for s in slots]
    w3, m3, v3 = (t.reshape((a, r, c)) for t in (w, m, v))

    def fn(*t):
        layer = pl.program_id(0) // per
        g = None
        for l in range(DEPTH):
            gl = t[l * N_DEV].astype(f32)
            for d in range(1, N_DEV):
                gl = gl + t[l * N_DEV + d].astype(f32)
            g = gl if g is None else jnp.where(layer == l, gl, g)
        k = DEPTH * N_DEV
        return (g,) + _adamw_math(t[k], g, t[k + 1], t[k + 2])

    specs, ins = [], []
    for l in range(DEPTH):
        for d in range(N_DEV):
            specs.append(pl.BlockSpec((None, None, tr, c), lambda i, j, _d=d, _l=l: (_d, jnp.clip(i - _l * per, 0, per - 1), j, 0)))
            ins.append(s3[l])
    spec = pl.BlockSpec((None, tr, c), lambda i, j: (i, j, 0))
    res = run_fwd(name, fn, (a, r // tr), ins + [w3, m3, v3], specs + [spec] * 3, [_out((a, r, c), spec)] * 4)[0]
    return [t.reshape(shape) for t in res]


def kernel(*args):
    nw = len(W_NAMES)
    x, p = args[0], args[1]
    w = dict(zip(W_NAMES, args[2:2 + nw]))
    target = args[2 + nw]
    m = dict(zip(W_NAMES, args[3 + nw:3 + 2 * nw]))
    v = dict(zip(W_NAMES, args[3 + 2 * nw:3 + 3 * nw]))

    wire = lambda k, i: w[k][i].astype(bf16) if k in BIG_NAMES else w[k][i]

    def layer_params(i, gathered):
        full = {k: _gathered_to_full(k, g[:, None])[0] for k, g in zip(SHARDED, gathered)}
        small = {k: w[k][i] for k in _SMALL_KEYS if k not in CONV_NAMES}
        small.update({k: full[k] for k in CONV_NAMES})
        return big_layout({k: full[k] for k in BIG_NAMES}), small

    big0, small0 = layer_params(0, all_gather("ag_layer0", [wire(k, 0) for k in SHARDED]))
    to_all = lambda k: jnp.broadcast_to(wire(k, DEPTH - 1)[None], (N_DEV,) + w[k].shape[1:])
    early = ("ffn1_w_in", "ffn1_w_out")
    late = tuple(k for k in SHARDED if k not in early)

    def last_layer(got):
        by_name = dict(zip(early + late, list(got[0]) + list(got[1])))
        return layer_params(DEPTH - 1, [by_name[k] for k in SHARDED])

    bigs, smalls = [big0, None], [small0, None]
    next_layer = (([to_all(k) for k in early], [to_all(k) for k in late]), last_layer)
    loss, gx, gbs, gss, gfn, rode = _local_step(x[0], p[:, 0], target[0], bigs, smalls, w["final_norm"],
                                                lambda gb, gs: _scattered(SHARDED, gb, gs), next_layer)
    loss = lax.psum(loss[0, 0], MESH_AXES)

    rode[0].update(zip(RIDE_LAST, exchange("rs_last", _scattered(RIDE_LAST, gbs[0], gss[0]))))
    rode[0] = [rode[0][k] for k in SHARDED]
    gfull = [dict(big_unlayout(gbs[i]), **gss[i]) for i in range(DEPTH)]
    stack = lambda k: jnp.stack([gfull[i][k] for i in range(DEPTH)])
    outs = {}
    kinds = ("grad", "delta", "new_m", "new_v")
    for j, k in enumerate(SHARDED):
        for kind, a in zip(kinds, reduce_adamw("adamw_" + k, [rode[i][j] for i in range(DEPTH)], w[k], m[k], v[k])):
            outs[kind + "_" + k] = a

    shapes = [w[k].shape for k in REPLICATED]
    g_rep = sum_slots("sum_replicated", all_gather("ag_replicated", [_pack([gfn if k == "final_norm" else stack(k) for k in REPLICATED])])[0])
    res = adamw("adamw_replicated", _pack([w[k] for k in REPLICATED]), g_rep, _pack([m[k] for k in REPLICATED]), _pack([v[k] for k in REPLICATED]))
    for kind, buf in zip(kinds, [g_rep] + list(res)):
        for k, a in zip(REPLICATED, _unpack(buf, shapes)):
            outs[kind + "_" + k] = a
    return (loss, gx[None]) + tuple(outs[kind + "_" + k] for kind in kinds for k in W_NAMES)
```

```python
import functools
import math

import jax
import jax.numpy as jnp
from jax import lax
from jax.experimental import pallas as pl
from jax.experimental.pallas import tpu as pltpu

f32 = jnp.float32
bf16 = jnp.bfloat16

EPS = 1e-6
DEPTH = 2
D_MODEL = 1024
FFN_DIM = 2816
BW = 512
IN_WIDTH = 5136
PW = 5376
SMALL_OFF = 5120
CHUNK = 64
LS = 256
LRU_C = 8.0
N_DEV = 8
VMEM_LIMIT_BYTES = 56 * 1024 * 1024

ADAM_LR, ADAM_B1, ADAM_B2, ADAM_EPS, ADAM_WD, ADAM_STEP = 0.001, 0.9, 0.999, 1e-08, 0.01, 10


_NN, _NT, _TN = ((1,), (0,)), ((1,), (1,)), ((0,), (0,))


def _pieces(x, n):
    parts, r = [], x
    for i in range(n):
        p = r.astype(bf16)
        parts.append(p)
        if i + 1 < n:
            r = r - p.astype(f32)
    return parts


def _dg(a, b, dims, mode):
    sa, sb = mode
    pa, pb = _pieces(a, sa), _pieces(b, sb)
    out = None
    for i in reversed(range(sa)):
        for j in reversed(range(sb)):
            if i + j < max(sa, sb):
                d = lax.dot_general(pa[i], pb[j], (dims, ((), ())), preferred_element_type=f32)
                out = d if out is None else out + d
    return out


def _make_mm(mode):
    sa, sb = mode
    cot = lambda s_other: 1 if mode == (1, 1) else (3 if s_other == 1 else 2)
    m_g_b, m_a_g, m_g_a, m_b_g = (cot(sb), sb), (sa, cot(sa)), (cot(sa), sa), (sb, cot(sb))

    @jax.custom_vjp
    def nn(a, b):
        return _dg(a, b, _NN, mode)

    @jax.custom_vjp
    def nt(a, b):
        return _dg(a, b, _NT, mode)

    @jax.custom_vjp
    def tn(a, b):
        return _dg(a, b, _TN, mode)

    nn.defvjp(lambda a, b: (nn(a, b), (a, b)), lambda r, g: (_dg(g, r[1], _NT, m_g_b), _dg(r[0], g, _TN, m_a_g)))
    nt.defvjp(lambda a, b: (nt(a, b), (a, b)), lambda r, g: (_dg(g, r[1], _NN, m_g_b), _dg(g, r[0], _TN, m_g_a)))
    tn.defvjp(lambda a, b: (tn(a, b), (a, b)), lambda r, g: (_dg(r[1], g, _NT, m_b_g), _dg(r[0], g, _NN, m_a_g)))
    return nn, nt, tn


mm, mm_nt, mm_tn = _make_mm((1, 1))
xmm, xmm_nt, xmm_tn = _make_mm((2, 2))
lmm, lmm_nt, lmm_tn = _make_mm((1, 3))
rmm, rmm_nt, rmm_tn = _make_mm((3, 1))


@jax.custom_vjp
def _tri_inv(m):
    n = m.shape[0]
    eye = (_rows((n, n)) == _lanes((n, n))).astype(f32)
    blk = (_rows((n, n)) // 16) == (_lanes((n, n)) // 16)
    x = lambda a, b: _dg(a, b, _NN, (2, 2))
    nb = jnp.where(blk, m, 0.0)
    p = -nb
    t = eye + p
    for _ in range(3):
        p = x(p, p)
        t = t + x(t, p)
    q = x(t, m - nb)
    imq = eye - q
    return x(imq + x(imq, x(q, q)), t)


def _tri_inv_fwd(m):
    t = _tri_inv(m)
    return t, t


def _tri_inv_bwd(t, g):
    return (-_dg(_dg(t, g, _TN, (2, 2)), t, _NT, (2, 2)),)


_tri_inv.defvjp(_tri_inv_fwd, _tri_inv_bwd)


def _rows(shape):
    return lax.broadcasted_iota(jnp.int32, shape, 0)


def _lanes(shape):
    return lax.broadcasted_iota(jnp.int32, shape, 1)


def _rms(x, g):
    return x * lax.rsqrt(jnp.mean(x * x, axis=-1, keepdims=True) + EPS) * g


def _silu(x):
    return x * jax.nn.sigmoid(x)


def _gelu(x):
    return 0.5 * x * (1.0 + jnp.tanh(0.7978845608028654 * (x + 0.044715 * x * x * x)))


def _softplus(x):
    return jnp.maximum(x, 0.0) + jnp.log1p(jnp.exp(-jnp.abs(x)))


def _expm1(x):
    p = x * (1.0 + x * (0.5 + x * (1.0 / 6 + x * (1.0 / 24 + x * (1.0 / 120 + x * (1.0 / 720 + x * (1.0 / 5040)))))))
    return jnp.where(x > -0.3, p, jnp.exp(x) - 1.0)


def _pick_row(x, r):
    return jnp.sum(jnp.where(_rows(x.shape) == r, x, 0.0), axis=0, keepdims=True)


def _pick_lane(x, c):
    return jnp.sum(jnp.where(_lanes(x.shape) == c, x, 0.0), axis=1, keepdims=True)


def _shift_up(g, j):
    n = g.shape[0]
    return jnp.where(_rows(g.shape) < n - j, pltpu.roll(g, n - j, 0), 0.0)


@functools.partial(jax.custom_vjp, nondiff_argnums=(1, 2))
def _shift(x, j, fill):
    return jnp.where(_rows(x.shape) >= j, pltpu.roll(x, j, 0), fill)


_shift.defvjp(lambda x, j, fill: (_shift(x, j, fill), None), lambda j, fill, _, g: (_shift_up(g, j),))


@functools.partial(jax.custom_vjp, nondiff_argnums=(2,))
def _shift_halo(x, prev8, j):
    xr = pltpu.roll(x, j, 0)
    pr = pltpu.roll(prev8, j, 0)
    top = jnp.where(_rows(pr.shape) < j, pr, xr[:8])
    return jnp.concatenate([top, xr[8:]], axis=0)


def _shift_halo_bwd(j, _, g):
    g8 = g[:8]
    dprev = jnp.where(_rows(g8.shape) >= 8 - j, pltpu.roll(g8, 8 - j, 0), 0.0)
    return _shift_up(g, j), dprev


_shift_halo.defvjp(lambda x, p, j: (_shift_halo(x, p, j), None), _shift_halo_bwd)


def _conv4(x, prev8, w, b):
    y = _pick_row(w, 3) * x
    for k in range(3):
        y = y + _pick_row(w, k) * _shift_halo(x, prev8, 3 - k)
    return y if b is None else y + b


def _cmul(ar, ai, br, bi):
    return ar * br - ai * bi, ar * bi + ai * br


_ANY = pl.BlockSpec(memory_space=pl.ANY)


def _params(grid):
    return pltpu.CompilerParams(dimension_semantics=("arbitrary",) * len(grid), vmem_limit_bytes=VMEM_LIMIT_BYTES)


def _first(axes):
    ok = pl.program_id(axes[0]) == 0
    for a in axes[1:]:
        ok = jnp.logical_and(ok, pl.program_id(a) == 0)
    return ok


def _store(ref, val, acc):
    val = val.astype(ref.dtype)
    if acc is None:
        ref[...] = val
        return
    first = _first(acc)

    @pl.when(first)
    def _():
        ref[...] = val

    @pl.when(jnp.logical_not(first))
    def _():
        ref[...] += val


def _full(a):
    nd = a.ndim
    return pl.BlockSpec(a.shape, lambda *g: (0,) * nd)


def _out(shape, spec, acc=None, dtype=f32):
    return dict(shape=tuple(shape), spec=spec, acc=acc, dtype=dtype)


def _grid_step(grid):
    step = pl.program_id(0)
    for ax in range(1, len(grid)):
        step = step * grid[ax] + pl.program_id(ax)
    return step


def run_fwd(name, fn, grid, ins, in_specs, outs, carry=None, ride=None):
    n_in, n_out = len(ins), len(outs)
    cshapes = carry["shapes"] if carry else []
    nc = len(cshapes)
    ng = len(grid)
    n_r = len(ride) if ride else 0

    def body(*refs):
        in_refs = refs[:n_in]
        out_refs = refs[n_in + n_r:n_in + n_r + n_out]
        save_refs = refs[n_in + n_r + n_out:n_in + n_r + n_out + nc]
        c_refs = refs[n_in + 2 * n_r + n_out + nc:n_in + 2 * n_r + n_out + 2 * nc]
        if n_r:
            step = _grid_step(grid)
            r_args = (refs[n_in:n_in + n_r], refs[n_in + n_r + n_out + nc:n_in + 2 * n_r + n_out + nc]) \
                + tuple(refs[n_in + 2 * n_r + n_out + 2 * nc:])
            pl.when(step == 0)(lambda: _exchange_start(*r_args))
        vals = [r[...] for r in in_refs]
        if carry:
            @pl.when(pl.program_id(carry["axis"]) == 0)
            def _():
                for c in c_refs:
                    c[...] = jnp.zeros(c.shape, f32)
            cin = tuple(c[...] for c in c_refs)
            for s, v in zip(save_refs, cin):
                s[...] = v
            cout, res = fn(cin, *vals)
            for c, v in zip(c_refs, cout):
                c[...] = v
        else:
            res = fn(*vals)
        for o, r, d in zip(out_refs, res, outs):
            _store(o, r, d["acc"])
        if n_r:
            pl.when(step == math.prod(grid) - 1)(lambda: _exchange_wait(*r_args))

    out_shape = [jax.ShapeDtypeStruct(d["shape"], d["dtype"]) for d in outs]
    out_specs = [d["spec"] for d in outs]
    for cs in cshapes:
        out_shape.append(jax.ShapeDtypeStruct(tuple(grid) + tuple(cs), f32))
        out_specs.append(pl.BlockSpec((None,) * ng + tuple(cs), lambda *g, _n=len(cs): tuple(g) + (0,) * _n))
    if n_r:
        out_shape += [jax.ShapeDtypeStruct(g.shape, g.dtype) for g in ride]
        out_specs += [_ANY] * n_r
    res = pl.pallas_call(
        body, name=name, grid=grid, in_specs=list(in_specs) + [_ANY] * n_r, out_specs=out_specs, out_shape=out_shape,
        scratch_shapes=[pltpu.VMEM(tuple(cs), f32) for cs in cshapes] + (_exchange_sems(n_r) if n_r else []),
        compiler_params=_params(grid),
    )(*ins, *(ride or []))
    if n_r:
        return list(res[:n_out]), list(res[n_out:n_out + nc]), list(res[n_out + nc:])
    return list(res[:n_out]), list(res[n_out:])


def run_bwd(name, fn, grid, ins, in_specs, gouts, gout_specs, wants, carry=None, into=None, ride=None):
    n_in, n_g, n_w = len(ins), len(gouts), len(wants)
    n_a = 0 if into is None else 1
    ride = list(ride or [])
    n_r = len(ride)
    saved = carry["saved"] if carry else []
    nc = len(saved)
    ng = len(grid)

    def body(*refs):
        in_refs = refs[:n_in]
        g_refs = refs[n_in:n_in + n_g]
        s_refs = refs[n_in + n_g:n_in + n_g + nc]
        base = n_in + n_g + nc + n_a
        r_in = refs[base:base + n_r]
        w_refs = refs[base + n_r:base + n_r + n_w]
        r_out = refs[base + n_r + n_w:base + 2 * n_r + n_w]
        dc_refs = refs[base + 2 * n_r + n_w:base + 2 * n_r + n_w + nc]
        if n_r:
            steps = math.prod(grid)
            step = pl.program_id(0)
            for ax in range(1, ng):
                step = step * grid[ax] + pl.program_id(ax)
            r_args = (r_in, r_out) + tuple(refs[base + 2 * n_r + n_w + nc:])
            pl.when(step == 0)(lambda: _exchange_start(*r_args))
        vals = [r[...].astype(f32) for r in in_refs]
        gs = tuple(r[...].astype(f32) for r in g_refs)
        if carry:
            @pl.when(pl.program_id(carry["axis"]) == 0)
            def _():
                for c in dc_refs:
                    c[...] = jnp.zeros(c.shape, f32)
            cin = tuple(s[...] for s in s_refs)
            _, vjp = jax.vjp(fn, cin, *vals)
            grads = vjp((tuple(c[...] for c in dc_refs), gs))
            for c, v in zip(dc_refs, grads[0]):
                c[...] = v
            dvals = grads[1:]
        else:
            _, vjp = jax.vjp(fn, *vals)
            dvals = vjp(gs)
        for o, d in zip(w_refs, wants):
            idx = d["idx"]
            val = dvals[idx] if isinstance(idx, int) else jnp.concatenate([dvals[j] for j in idx], axis=1)
            _store(o, val, d["acc"])
        if n_r:
            pl.when(step == steps - 1)(lambda: _exchange_wait(*r_args))

    rev = carry["rev"] if carry else None
    s_specs = []
    for a in saved:
        n = a.ndim - ng
        s_specs.append(pl.BlockSpec((None,) * ng + tuple(a.shape[ng:]), lambda *g, _n=n: tuple(rev(g)) + (0,) * _n))
    res = pl.pallas_call(
        body, name=name, grid=grid, in_specs=list(in_specs) + list(gout_specs) + s_specs + [_ANY] * (n_a + n_r),
        out_specs=[d["spec"] for d in wants] + [_ANY] * n_r,
        out_shape=[jax.ShapeDtypeStruct(d["shape"], d["dtype"]) for d in wants] + [jax.ShapeDtypeStruct(g.shape, g.dtype) for g in ride],
        input_output_aliases={n_in + n_g + nc: 0} if n_a else {},
        scratch_shapes=[pltpu.VMEM(tuple(a.shape[ng:]), f32) for a in saved] + (_exchange_sems(n_r) if n_r else []),
        compiler_params=_params(grid),
    )(*ins, *gouts, *saved, *([into] if n_a else []), *ride)
    return list(res)


def _want(idx, shape, spec, acc=None, dtype=f32):
    d = _out(shape, spec, acc, dtype)
    d["idx"] = idx
    return d


def addn(name, items, t):
    s, w = items[0][0].shape[-2:]
    specs = []
    for a, j in items:
        if j is None:
            specs.append(pl.BlockSpec((t, w), lambda i: (i, 0)))
        else:
            specs.append(pl.BlockSpec((None, t, w), lambda i, _j=j: (_j, i, 0)))

    def fn(*xs):
        y = xs[0]
        for x in xs[1:]:
            y = y + x
        return (y,)

    return run_fwd(name, fn, (s // t,), [a for a, _ in items], specs,
                   [_out((s, w), pl.BlockSpec((t, w), lambda i: (i, 0)))])[0][0]


def _parts(a):
    return [(a, j) for j in range(a.shape[0])]


def _ffn_core(n, wg, wu, wo):
    return (0.5 * mm(_silu(mm(n, wg)) * mm(n, wu), wo),)


def ffn_fwd(name, h, g, wg, wu, wo, t=512, tf=1408):
    s, d = h.shape
    f = wg.shape[1]

    def fn(hh, gg, a, b, c):
        n = _rms(hh, gg)
        return _ffn_core(n, a, b, c)[0] + (pl.program_id(1) == 0).astype(f32) * hh, n

    specs = [pl.BlockSpec((t, d), lambda i, j: (i, 0)), _full(g), pl.BlockSpec((d, tf), lambda i, j: (0, j)),
             pl.BlockSpec((d, tf), lambda i, j: (0, j)), pl.BlockSpec((tf, d), lambda i, j: (j, 0))]
    outs = [_out((s, d), pl.BlockSpec((t, d), lambda i, j: (i, 0)), acc=(1,)),
            _out((s, d), pl.BlockSpec((t, d), lambda i, j: (i, 0)), dtype=bf16)]
    return run_fwd(name, fn, (s // t, f // tf), [h, g, wg, wu, wo], specs, outs)[0]


def ffn_bwd(name, n, wg, wu, wo, gout, t=1024, tf=256, ride=None):
    s, d = n.shape
    t = min(t, s)
    f = wg.shape[1]
    nj = f // tf
    specs = [pl.BlockSpec((t, d), lambda j, i: (i, 0)), pl.BlockSpec((d, tf), lambda j, i: (0, j)),
             pl.BlockSpec((d, tf), lambda j, i: (0, j)), pl.BlockSpec((tf, d), lambda j, i: (j, 0))]
    wants = [_want(0, (nj, s, d), pl.BlockSpec((None, t, d), lambda j, i: (j, i, 0)), dtype=bf16),
             _want(1, wg.shape, pl.BlockSpec((d, tf), lambda j, i: (0, j)), acc=(1,)),
             _want(2, wu.shape, pl.BlockSpec((d, tf), lambda j, i: (0, j)), acc=(1,)),
             _want(3, wo.shape, pl.BlockSpec((tf, d), lambda j, i: (j, 0)), acc=(1,))]
    return run_bwd(name, _ffn_core, (nj, s // t), [n, wg, wu, wo], specs,
                   [gout], [pl.BlockSpec((t, d), lambda j, i: (i, 0))], wants, ride=ride)


def norm_bwd(name, h, g, base, parts, t=512):
    s, d = h.shape

    def fn(hh, gg, bb, *ps):
        dn = ps[0].astype(f32)
        for p in ps[1:]:
            dn = dn + p.astype(f32)
        _, vjp = jax.vjp(_rms, hh, gg)
        dh, dg = vjp(dn)
        return bb + dh, dg

    row = pl.BlockSpec((t, d), lambda i: (i, 0))
    specs = [row, _full(g), row] + [pl.BlockSpec((None, t, d), lambda i, _j=j: (_j, i, 0)) for _, j in parts]
    outs = [_out((s, d), row), _out(g.shape, _full(g), acc=(0,))]
    return run_fwd(name, fn, (s // t,), [h, g, base] + [a for a, _ in parts], specs, outs)[0]


def _lin_tile(u, w, b):
    return (mm(u, w) + b,)


def normlin_fwd(name, h, g, w, b, t=1024, tn=1024):
    s, d = h.shape
    t = min(t, s)
    n = w.shape[1]

    def fn(hh, gg, ww, bb):
        u = _rms(hh, gg)
        return mm(u, ww) + bb, u

    specs = [pl.BlockSpec((t, d), lambda i, j: (i, 0)), _full(g), pl.BlockSpec((d, tn), lambda i, j: (0, j)),
             pl.BlockSpec((1, tn), lambda i, j: (0, j))]
    outs = [_out((s, n), pl.BlockSpec((t, tn), lambda i, j: (i, j))),
            _out((s, d), pl.BlockSpec((t, d), lambda i, j: (i, 0)), dtype=bf16)]
    return run_fwd(name, fn, (s // t, n // tn), [h, g, w, b], specs, outs)[0]


def lin_fwd(name, u, w, b, t=1024, tn=1024):
    s, d = u.shape
    t = min(t, s)
    n = w.shape[1]
    specs = [pl.BlockSpec((t, d), lambda i, j: (i, 0)), pl.BlockSpec((d, tn), lambda i, j: (0, j)),
             pl.BlockSpec((1, tn), lambda i, j: (0, j))]
    outs = [_out((s, n), pl.BlockSpec((t, tn), lambda i, j: (i, j)))]
    return run_fwd(name, _lin_tile, (s // t, n // tn), [u, w, b], specs, outs)[0][0]


def lin_bwd(name, u, w, b, gout, t=1024, tn=1024):
    s, d = u.shape
    t = min(t, s)
    n = w.shape[1]
    nj = n // tn
    specs = [pl.BlockSpec((t, d), lambda j, i: (i, 0)), pl.BlockSpec((d, tn), lambda j, i: (0, j)),
             pl.BlockSpec((1, tn), lambda j, i: (0, j))]
    wants = [_want(0, (nj, s, d), pl.BlockSpec((None, t, d), lambda j, i: (j, i, 0)), dtype=bf16),
             _want(1, w.shape, pl.BlockSpec((d, tn), lambda j, i: (0, j)), acc=(1,)),
             _want(2, b.shape, pl.BlockSpec((1, tn), lambda j, i: (0, j)), acc=(1,))]
    return run_bwd(name, _lin_tile, (nj, s // t), [u, w, b], specs,
                   [gout], [pl.BlockSpec((t, tn), lambda j, i: (i, j))], wants)


def _merge_tile(n_axis, residual):
    def fn(h, y, gp, wb, wo):
        part = mm(jax.nn.sigmoid(gp) * mm(y, wb), wo)
        if residual:
            part = part + (pl.program_id(n_axis) == 0).astype(f32) * h
        return (part,)
    return fn


def merge_fwd(name, h, ys, gpre, wb, wo, t=512):
    s, d = h.shape
    specs = [pl.BlockSpec((t, d), lambda i, n: (i, 0)), pl.BlockSpec((t, BW), lambda i, n: (i, n)),
             pl.BlockSpec((t, d), lambda i, n: (i, n)), pl.BlockSpec((None, BW, d), lambda i, n: (n, 0, 0)), _full(wo)]
    outs = [_out((s, d), pl.BlockSpec((t, d), lambda i, n: (i, 0)), acc=(1,))]
    return run_fwd(name, _merge_tile(1, True), (s // t, 4), [h, ys, gpre, wb, wo], specs, outs)[0][0]


def merge_bwd(name, h, ys, gpre, wb, wo, gout, t=512):
    s, d = h.shape
    specs = [pl.BlockSpec((t, d), lambda n, i: (i, 0)), pl.BlockSpec((t, BW), lambda n, i: (i, n)),
             pl.BlockSpec((t, d), lambda n, i: (i, n)), pl.BlockSpec((None, BW, d), lambda n, i: (n, 0, 0)), _full(wo)]
    wants = [_want(1, ys.shape, pl.BlockSpec((t, BW), lambda n, i: (i, n))),
             _want(2, gpre.shape, pl.BlockSpec((t, d), lambda n, i: (i, n))),
             _want(3, wb.shape, pl.BlockSpec((None, BW, d), lambda n, i: (n, 0, 0)), acc=(1,)),
             _want(4, wo.shape, _full(wo), acc=(0, 1))]
    return run_bwd(name, _merge_tile(0, False), (4, s // t), [h, ys, gpre, wb, wo], specs,
                   [gout], [pl.BlockSpec((t, d), lambda n, i: (i, 0))], wants)


def _ple_tile(residual):
    def fn(h, pe, g, wgate, wproj):
        y = jax.nn.sigmoid(mm(_rms(h, g), wgate)) * mm(pe, wproj)
        return (y + h,) if residual else (y,)
    return fn


def ple_fwd(name, h, pe, g, wgate, wproj, t=512):
    s, d = h.shape
    specs = [pl.BlockSpec((t, d), lambda i: (i, 0)), pl.BlockSpec((t, pe.shape[1]), lambda i: (i, 0)),
             _full(g), _full(wgate), _full(wproj)]
    return run_fwd(name, _ple_tile(True), (s // t,), [h, pe, g, wgate, wproj], specs,
                   [_out((s, d), pl.BlockSpec((t, d), lambda i: (i, 0)))])[0][0]


def ple_bwd(name, h, pe, g, wgate, wproj, gout, t=512):
    s, d = h.shape
    specs = [pl.BlockSpec((t, d), lambda i: (i, 0)), pl.BlockSpec((t, pe.shape[1]), lambda i: (i, 0)),
             _full(g), _full(wgate), _full(wproj)]
    wants = [_want(0, h.shape, pl.BlockSpec((t, d), lambda i: (i, 0))), _want(2, g.shape, _full(g), acc=(0,)),
             _want(3, wgate.shape, _full(wgate), acc=(0,)), _want(4, wproj.shape, _full(wproj), acc=(0,))]
    return run_bwd(name, _ple_tile(True), (s // t,), [h, pe, g, wgate, wproj], specs,
                   [gout], [pl.BlockSpec((t, d), lambda i: (i, 0))], wants)


def final_loss(name, h, g, target, t=512):
    s, d = h.shape

    def fn(hh, gg, tt):
        def loss_fn(a, b):
            err = _rms(a, b) - tt
            return 0.5 * jnp.sum(jnp.mean(err * err, axis=-1, keepdims=True), axis=0, keepdims=True)
        loss, vjp = jax.vjp(loss_fn, hh, gg)
        dh, dgain = vjp(jnp.ones((1, 1), f32))
        return loss, dh, dgain

    specs = [pl.BlockSpec((t, d), lambda i: (i, 0)), _full(g), pl.BlockSpec((t, d), lambda i: (i, 0))]
    outs = [_out((1, 1), pl.BlockSpec((1, 1), lambda i: (0, 0)), acc=(0,)),
            _out((s, d), pl.BlockSpec((t, d), lambda i: (i, 0))), _out(g.shape, _full(g), acc=(0,))]
    return run_fwd(name, fn, (s // t,), [h, g, target], specs, outs)[0]


def _rev(nc, rev):
    return (lambda c: nc - 1 - c) if rev else (lambda c: c)


def _s5_ops_tile(are, aim, lstep, bre, bim):
    step = jnp.exp(lstep)
    mag = jnp.exp(are * step)
    ab_re, ab_im = mag * jnp.cos(aim * step), mag * jnp.sin(aim * step)
    den = are * are + aim * aim
    num_re = ab_re - 1.0
    f_re = (num_re * are + ab_im * aim) / den
    f_im = (ab_im * are - num_re * aim) / den
    bb_re = f_re * bre - f_im * bim
    bb_im = f_re * bim + f_im * bre
    pr = jnp.broadcast_to(ab_re, (LS, ab_re.shape[1]))
    pi = jnp.broadcast_to(ab_im, (LS, ab_im.shape[1]))
    k = 1
    while k < LS:
        pr, pi = _cmul(pr, pi, _shift(pr, k, 1.0), _shift(pi, k, 0.0))
        k *= 2
    return ab_re, ab_im, bb_re, bb_im, pr, pi


def _s5_ops_specs(arrs):
    return [pl.BlockSpec((None,) + a.shape[1:], lambda gb: (gb, 0, 0)) for a in arrs]


def s5_ops_fwd(name, raw):
    shapes = [(4, 1, 512), (4, 1, 512), (4, 128, 512), (4, 128, 512), (4, LS, 512), (4, LS, 512)]
    outs = [_out(sh, pl.BlockSpec((None,) + sh[1:], lambda gb: (gb, 0, 0))) for sh in shapes]
    return run_fwd(name, _s5_ops_tile, (4,), raw, _s5_ops_specs(raw), outs)[0]


def s5_ops_bwd(name, raw, gops):
    wants = [_want(i, a.shape, pl.BlockSpec((None,) + a.shape[1:], lambda gb: (gb, 0, 0))) for i, a in enumerate(raw)]
    return run_bwd(name, _s5_ops_tile, (4,), raw, _s5_ops_specs(raw), gops, _s5_ops_specs(gops), wants)


def _lti_scan_raw(xr, xi, pr, pi, up):
    n = xr.shape[0]
    move = (lambda a, k: _shift_up(a, k)) if up else (lambda a, k: jnp.where(_rows(a.shape) >= k, pltpu.roll(a, k, 0), 0.0))
    k = 1
    while k < n:
        sr, si = _cmul(pr, pi, move(xr, k), move(xi, k))
        xr, xi = xr + sr, xi + si
        pr, pi = _cmul(pr, pi, pr, pi)
        k *= 2
    return xr, xi


@jax.custom_vjp
def _lti_scan(xr, xi, pr, pi):
    return _lti_scan_raw(xr, xi, pr, pi, False)


def _lti_scan_fwd(xr, xi, pr, pi):
    sr, si = _lti_scan_raw(xr, xi, pr, pi, False)
    return (sr, si), (sr, si, pr, pi)


def _lti_scan_bwd(res, g):
    sr, si, pr, pi = res
    mr, mi = _lti_scan_raw(g[0], g[1], pr, -pi, True)
    qr = jnp.where(_rows(sr.shape) >= 1, pltpu.roll(sr, 1, 0), 0.0)
    qi = jnp.where(_rows(si.shape) >= 1, pltpu.roll(si, 1, 0), 0.0)
    dpr = jnp.sum(qr * mr + qi * mi, axis=0, keepdims=True)
    dpi = jnp.sum(qr * mi - qi * mr, axis=0, keepdims=True)
    return mr, mi, dpr, dpi


_lti_scan.defvjp(_lti_scan_fwd, _lti_scan_bwd)


def _s5_tile(carry, u, ab_re, ab_im, bb_re, bb_im, pw_re, pw_im, c_re, c_im, dskip):
    h_re, h_im = carry
    xr, xi = _lti_scan(mm(u, bb_re), mm(u, bb_im), ab_re, ab_im)
    cr, ci = _cmul(pw_re, pw_im, h_re, h_im)
    xr, xi = xr + cr, xi + ci
    y = mm(xr, c_re) - mm(xi, c_im) + dskip * u
    return (_pick_row(xr, LS - 1), _pick_row(xi, LS - 1)), (y,)


def _s5_io(u, ops, c_re, c_im, dskip, nc, rev):
    cm = _rev(nc, rev)
    ins = [u] + list(ops) + [c_re, c_im, dskip]
    specs = [pl.BlockSpec((LS, 128), lambda gb, c: (cm(c), 36 + gb))]
    specs += [pl.BlockSpec((None,) + a.shape[1:], lambda gb, c: (gb, 0, 0)) for a in list(ops) + [c_re, c_im]]
    specs += [pl.BlockSpec((1, 128), lambda gb, c: (0, gb))]
    return ins, specs, cm


def s5_fwd(name, proj, ops, c_re, c_im, dskip, ride=None):
    s = proj.shape[0]
    nc = s // LS
    ins, specs, cm = _s5_io(proj, ops, c_re, c_im, dskip, nc, False)
    outs = [_out((s, BW), pl.BlockSpec((LS, 128), lambda gb, c: (c, gb)))]
    res = run_fwd(name, _s5_tile, (4, nc), ins, specs, outs, carry=dict(shapes=[(1, 512), (1, 512)], axis=1), ride=ride)
    return (res[0][0],) + tuple(res[1:])


def s5_bwd(name, proj, ops, c_re, c_im, dskip, saved, gy, gproj, ride=None):
    s = proj.shape[0]
    nc = s // LS
    ins, specs, cm = _s5_io(proj, ops, c_re, c_im, dskip, nc, True)
    wants = [_want(0, (s, PW), pl.BlockSpec((LS, 128), lambda gb, c: (cm(c), 36 + gb)))]
    for i, a in enumerate(list(ops) + [c_re, c_im]):
        wants.append(_want(1 + i, a.shape, pl.BlockSpec((None,) + a.shape[1:], lambda gb, c: (gb, 0, 0)), acc=(1,)))
    wants.append(_want(9, dskip.shape, pl.BlockSpec((1, 128), lambda gb, c: (0, gb)), acc=(1,)))
    return run_bwd(name, _s5_tile, (4, nc), ins, specs, [gy], [pl.BlockSpec((LS, 128), lambda gb, c: (cm(c), gb))],
                   wants, carry=dict(axis=1, saved=saved, rev=lambda g: (g[0], cm(g[1]))), into=gproj, ride=ride)


def _s5_glu_tile(y, w, b):
    z = _gelu(y)
    return (z * jax.nn.sigmoid(mm(z, w) + b),)


def s5_glu_fwd(name, y, w, b, t=512):
    s = y.shape[0]
    spec = pl.BlockSpec((t, BW), lambda i: (i, 0))
    return run_fwd(name, _s5_glu_tile, (s // t,), [y, w, b], [spec, _full(w), _full(b)], [_out((s, BW), spec)])[0][0]


def s5_glu_bwd(name, y, w, b, gout, gout_col, t=512):
    s = y.shape[0]
    spec = pl.BlockSpec((t, BW), lambda i: (i, 0))
    wants = [_want(0, y.shape, spec), _want(1, w.shape, _full(w), acc=(0,)), _want(2, b.shape, _full(b), acc=(0,))]
    return run_bwd(name, _s5_glu_tile, (s // t,), [y, w, b], [spec, _full(w), _full(b)], [gout],
                   [pl.BlockSpec((t, BW), lambda i: (i, gout_col))], wants)


def _lru_tile(carry, xb, gate, cw, cb, wr, br, wi, bi, lam):
    h_in, prev8 = carry
    xc = _conv4(xb, prev8, cw, cb)
    r = jax.nn.sigmoid(mm(xc, wr) + br)
    ig = jax.nn.sigmoid(mm(xc, wi) + bi)
    log_a = -LRU_C * r * _softplus(-lam)
    a = jnp.exp(log_a)
    b = jnp.sqrt(-_expm1(2.0 * log_a)) * (ig * xc)
    k = 1
    while k < LS:
        b = b + a * _shift(b, k, 0.0)
        a = a * _shift(a, k, 1.0)
        k *= 2
    h = b + a * h_in
    return (_pick_row(h, LS - 1), xb[LS - 8:, :]), (h * _gelu(gate),)


def _lru_io(proj, ws, nc, rev):
    cm = _rev(nc, rev)
    ins = [proj, proj] + list(ws)
    specs = [pl.BlockSpec((LS, BW), lambda c: (cm(c), 4)), pl.BlockSpec((LS, BW), lambda c: (cm(c), 5))]
    specs += [_full(a) for a in ws]
    return ins, specs, cm


def lru_fwd(name, proj, ws):
    s = proj.shape[0]
    nc = s // LS
    ins, specs, cm = _lru_io(proj, ws, nc, False)
    outs = [_out((s, BW), pl.BlockSpec((LS, BW), lambda c: (c, 0)))]
    (y,), saved = run_fwd(name, _lru_tile, (nc,), ins, specs, outs, carry=dict(shapes=[(1, BW), (8, BW)], axis=0))
    return y, saved


def lru_bwd(name, proj, ws, saved, gy, gy_col, gproj):
    s = proj.shape[0]
    nc = s // LS
    ins, specs, cm = _lru_io(proj, ws, nc, True)
    wants = [_want((0, 1), (s, PW), pl.BlockSpec((LS, 2 * BW), lambda c: (cm(c), 2)))]
    wants += [_want(2 + i, a.shape, _full(a), acc=(0,)) for i, a in enumerate(ws)]
    return run_bwd(name, _lru_tile, (nc,), ins, specs, [gy], [pl.BlockSpec((LS, BW), lambda c: (cm(c), gy_col))], wants,
                   carry=dict(axis=0, saved=saved, rev=lambda g: (cm(g[0]),)), into=gproj)


def _causal(n):
    return _rows((n, n)) >= _lanes((n, n))


def _decay(col, rowv):
    causal = _causal(col.shape[0])
    return jnp.where(causal, jnp.exp(jnp.where(causal, col - rowv, 0.0)), 0.0)


def _m2_tile(carry, z, xs_raw, b_raw, c_raw, small, cwx, cwb, cwc, cbx, cbb, cbc, dtb, alog, dsk, ng):
    state, px, pb, pc = carry
    n = CHUNK
    xs = _silu(_conv4(xs_raw, px, cwx, cbx))
    bm = _silu(_conv4(b_raw, pb, cwb, cbb))
    cmx = _silu(_conv4(c_raw, pc, cwc, cbc))
    expand = (_lanes((16, BW)) // 64 == _rows((16, BW))).astype(f32)
    tri = _causal(n).astype(f32)
    triu = (_rows((n, n)) <= _lanes((n, n))).astype(f32)
    dt = _softplus(small + dtb)
    da = dt * (-jnp.exp(alog))
    cs = lmm(tri, da)
    cs_t = rmm_tn(da, triu)
    cs_w = rmm(cs, expand)
    last_w = _pick_row(cs_w, n - 1)
    xdt = xs * rmm(dt, expand)
    g0 = _lanes((n, BW)) < 256
    bm0, bm1, cm0, cm1 = bm[:, :128], bm[:, 128:], cmx[:, :128], cmx[:, 128:]
    cb0, cb1 = mm_nt(cm0, bm0), mm_nt(cm1, bm1)
    y = jnp.where(g0, mm(cm0, state), mm(cm1, state)) * jnp.exp(cs_w)
    for h in range(8):
        sc = (cb0 if h < 4 else cb1) * _decay(_pick_lane(cs, h), _pick_row(cs_t, h))
        y = y + jnp.where(_lanes((n, BW)) // 64 == h, mm(sc, xdt), 0.0)
    xd = xdt * jnp.exp(last_w - cs_w)
    g0s = _lanes((128, BW)) < 256
    state_out = state * jnp.exp(last_w) + jnp.where(g0s, mm_tn(bm0, xd), mm_tn(bm1, xd))
    y = (y + dsk * xs) * _silu(z)
    return (state_out, xs_raw[n - 8:, :], b_raw[n - 8:, :], c_raw[n - 8:, :]), (_rms(y, ng),)


def _m2_io(proj, small, ws, nc, rev):
    cm = _rev(nc, rev)
    n = CHUNK
    ins = [proj, proj, proj, proj, small] + list(ws)
    specs = [pl.BlockSpec((n, BW), lambda c: (cm(c), 6)), pl.BlockSpec((n, BW), lambda c: (cm(c), 7)),
             pl.BlockSpec((n, 256), lambda c: (cm(c), 16)), pl.BlockSpec((n, 256), lambda c: (cm(c), 17)),
             pl.BlockSpec((n, 16), lambda c: (cm(c), 0))]
    specs += [_full(a) for a in ws]
    return ins, specs, cm


_M2_CARRY = [(128, BW), (8, BW), (8, 256), (8, 256)]


def m2_fwd(name, proj, small, ws):
    s = proj.shape[0]
    nc = s // CHUNK
    ins, specs, cm = _m2_io(proj, small, ws, nc, False)
    outs = [_out((s, BW), pl.BlockSpec((CHUNK, BW), lambda c: (c, 0)))]
    (y,), saved = run_fwd(name, _m2_tile, (nc,), ins, specs, outs, carry=dict(shapes=_M2_CARRY, axis=0))
    return y, saved


def m2_bwd(name, proj, small, ws, saved, gy, gy_col, gproj):
    s = proj.shape[0]
    nc = s // CHUNK
    ins, specs, cm = _m2_io(proj, small, ws, nc, True)
    n = CHUNK
    wants = [_want((0, 1, 2, 3), (s, PW), pl.BlockSpec((n, 3 * BW), lambda c: (cm(c), 2))),
             _want(4, (s, 16), pl.BlockSpec((n, 16), lambda c: (cm(c), 0)))]
    wants += [_want(5 + i, a.shape, _full(a), acc=(0,)) for i, a in enumerate(ws)]
    return run_bwd(name, _m2_tile, (nc,), ins, specs, [gy], [pl.BlockSpec((n, BW), lambda c: (cm(c), gy_col))], wants,
                   carry=dict(axis=0, saved=saved, rev=lambda g: (cm(g[0]),)), into=gproj)


def _l2n(x):
    return x * lax.rsqrt(jnp.sum(x * x, axis=-1, keepdims=True) + EPS)


def _gdn_tile(carry, q_raw, k_raw, v_raw, gate, small, cwq, cwk, cwv, dtb, alog, ng):
    state, pq, pk, pv = carry
    n, nh = CHUNK, 4
    nn_ = n * nh
    qc = _silu(_conv4(q_raw, pq, cwq, None))
    kc = _silu(_conv4(k_raw, pk, cwk, None))
    vc = _silu(_conv4(v_raw, pv, cwv, None))
    beta16 = jax.nn.sigmoid(small)
    g16 = -jnp.exp(alog) * _softplus(small + dtb)
    tri = _causal(n).astype(f32)
    triu = (_rows((n, n)) <= _lanes((n, n))).astype(f32)
    cs16 = lmm(tri, g16)
    cs16_t = rmm_tn(g16, triu)
    lanes_of = lambda h: slice(128 * h, 128 * (h + 1))
    rows_of = lambda h: slice(n * h, n * (h + 1))
    stack = lambda f: jnp.concatenate([f(h) for h in range(nh)], axis=0)
    q = stack(lambda h: _l2n(qc[:, lanes_of(h)]) * (128 ** -0.5))
    k = stack(lambda h: _l2n(kc[:, lanes_of(h)]))
    v = stack(lambda h: vc[:, lanes_of(h)])
    beta = stack(lambda h: _pick_lane(beta16, 8 + h))
    col = stack(lambda h: _pick_lane(cs16, 12 + h))
    last_h = [_pick_row(_pick_lane(cs16, 12 + h), n - 1) for h in range(nh)]
    last = stack(lambda h: jnp.broadcast_to(last_h[h], (n, 1)))
    last_w = jnp.concatenate([jnp.broadcast_to(last_h[h], (1, 128)) for h in range(nh)], axis=1)
    spread = (_rows((n, nn_)) == _lanes((n, nn_)) % n).astype(f32)
    cs_w = rmm(cs16_t, spread)
    rowv = jnp.sum(jnp.where(_rows((16, nn_)) == 12 + _lanes((16, nn_)) // n, cs_w, 0.0), axis=0, keepdims=True)
    same = (_rows((nn_, nn_)) // n) == (_lanes((nn_, nn_)) // n)
    causal = jnp.logical_and(same, _rows((nn_, nn_)) >= _lanes((nn_, nn_)))
    strict = jnp.logical_and(same, _rows((nn_, nn_)) > _lanes((nn_, nn_)))
    decay = jnp.where(causal, jnp.exp(jnp.where(causal, col - rowv, 0.0)), 0.0)
    kb = k * beta
    t = _tri_inv(jnp.where(strict, mm_nt(kb, k) * decay, 0.0))
    e_col = jnp.exp(col)
    uw = xmm(t, jnp.concatenate([v * beta, kb * e_col], axis=1))
    u, w = uw[:, :128], uw[:, 128:]
    qk = mm_nt(q, k) * decay
    own = lambda r: stack(lambda h: r[rows_of(h), lanes_of(h)])
    v_new = u - own(mm(w, state))
    o = own(mm(q * e_col, state)) + mm(qk, v_new)
    zero = jnp.zeros((n, 128), f32)
    v_blocks = stack(lambda h: jnp.concatenate([v_new[rows_of(h), :] if j == h else zero for j in range(nh)], axis=1))
    state_out = state * jnp.exp(last_w) + mm_tn(k * jnp.exp(last - col), v_blocks)
    gt = stack(lambda h: gate[:, lanes_of(h)])
    out = _rms(o, ng) * _silu(gt)
    out = jnp.concatenate([out[rows_of(h), :] for h in range(nh)], axis=1)
    return (state_out, q_raw[n - 8:, :], k_raw[n - 8:, :], v_raw[n - 8:, :]), (out,)


def _gdn_io(proj, small, cw, dtb, alog, ng, nc, rev):
    cm = _rev(nc, rev)
    n = CHUNK
    ins = [proj, proj, proj, proj, small, cw, cw, cw, dtb, alog, ng]
    specs = [pl.BlockSpec((n, BW), lambda c, _j=j: (cm(c), _j)) for j in (0, 1, 2, 3)]
    specs += [pl.BlockSpec((n, 16), lambda c: (cm(c), 0))]
    specs += [pl.BlockSpec((4, BW), lambda c, _j=j: (0, _j)) for j in (0, 1, 2)]
    specs += [_full(dtb), _full(alog), _full(ng)]
    return ins, specs, cm


_GDN_CARRY = [(128, BW)] + [(8, BW)] * 3


def gdn_fwd(name, proj, small, cw, dtb, alog, ng, ride=None):
    s = proj.shape[0]
    nc = s // CHUNK
    ins, specs, cm = _gdn_io(proj, small, cw, dtb, alog, ng, nc, False)
    outs = [_out((s, BW), pl.BlockSpec((CHUNK, BW), lambda c: (c, 0)))]
    res = run_fwd(name, _gdn_tile, (nc,), ins, specs, outs, carry=dict(shapes=_GDN_CARRY, axis=0), ride=ride)
    return (res[0][0],) + tuple(res[1:])


def gdn_bwd(name, proj, small, cw, dtb, alog, ng, saved, gy, gy_col, ride=None):
    s = proj.shape[0]
    nc = s // CHUNK
    n = CHUNK
    ins, specs, cm = _gdn_io(proj, small, cw, dtb, alog, ng, nc, True)
    wants = [_want((0, 1, 2, 3), (s, PW), pl.BlockSpec((n, 4 * BW), lambda c: (cm(c), 0))),
             _want(4, (s, 16), pl.BlockSpec((n, 16), lambda c: (cm(c), 0)))]
    wants += [_want(5 + i, (4, BW), pl.BlockSpec((4, BW), lambda c: (0, 0)), acc=(0,)) for i in range(3)]
    wants += [_want(8, dtb.shape, _full(dtb), acc=(0,)), _want(9, alog.shape, _full(alog), acc=(0,)),
              _want(10, ng.shape, _full(ng), acc=(0,))]
    return run_bwd(name, _gdn_tile, (nc,), ins, specs, [gy], [pl.BlockSpec((n, BW), lambda c: (cm(c), gy_col))], wants,
                   carry=dict(axis=0, saved=saved, rev=lambda g: (cm(g[0]),)), ride=ride)


def _perm_cols(w):
    pad = jnp.zeros(w.shape[:-1] + (PW - IN_WIDTH,), w.dtype)
    return jnp.concatenate([w[..., 3080:5128], w[..., 512:3072], w[..., :512], w[..., 3072:3080], w[..., 5128:5136], pad], axis=-1)


def _unperm_cols(g):
    return jnp.concatenate([g[..., 4608:5120], g[..., 2048:4608], g[..., 5120:5128], g[..., :2048], g[..., 5128:5136]], axis=-1)


def _bd(blocks):
    n, a, b = blocks.shape
    eye = jnp.eye(n, dtype=blocks.dtype)
    return jnp.einsum("nab,nm->namb", blocks, eye).reshape(n * a, n * b)


def _layer_layout(lw):
    o = {}
    row = lambda a: a.reshape(1, -1)
    o["b_gate"] = row(lw["b_gate"])
    o["s5_are"] = lw["s5_a_re"].reshape(4, 1, 512)
    o["s5_aim"] = lw["s5_a_im"].reshape(4, 1, 512)
    o["s5_lstep"] = jnp.repeat(lw["s5_log_step"], 64).reshape(4, 1, 512)
    bt = lambda b: jax.vmap(_bd)(jnp.swapaxes(b, 1, 2).reshape(4, 8, 16, 64))
    o["s5_bre"], o["s5_bim"] = bt(lw["s5_b_re"]), bt(lw["s5_b_im"])
    ct = lambda c: jax.vmap(_bd)(jnp.swapaxes(c, 1, 2).reshape(4, 8, 64, 16))
    o["s5_cre"], o["s5_cim"] = ct(lw["s5_c_re"]), ct(lw["s5_c_im"])
    o["s5_d"] = row(lw["s5_d"])
    o["s5_b_glu"] = row(lw["s5_b_glu"])
    o["lru_conv_b"], o["lru_b_r"], o["lru_b_i"], o["lru_lambda"] = (row(lw[k]) for k in ("lru_conv_b", "lru_b_r", "lru_b_i", "lru_lambda"))
    o["lru_wr"], o["lru_wi"] = _bd(lw["lru_w_r"]), _bd(lw["lru_w_i"])
    cw, cb = lw["m2_conv_w"], lw["m2_conv_b"]
    o["m2_cwx"], o["m2_cwb"], o["m2_cwc"] = cw[:, :512], cw[:, 512:768], cw[:, 768:]
    o["m2_cbx"], o["m2_cbb"], o["m2_cbc"] = row(cb[:512]), row(cb[512:768]), row(cb[768:])
    o["m2_dtb"] = jnp.pad(lw["m2_dt_bias"], (0, 8)).reshape(1, 16)
    o["m2_alog"] = jnp.pad(lw["m2_a_log"], (0, 8)).reshape(1, 16)
    o["m2_dsk"] = jnp.repeat(lw["m2_d"], 64).reshape(1, 512)
    o["m2_norm"] = row(lw["m2_norm"])
    o["gdn_dtb"] = jnp.pad(lw["gdn_dt_bias"], (12, 0)).reshape(1, 16)
    o["gdn_alog"] = jnp.pad(lw["gdn_a_log"], (12, 0)).reshape(1, 16)
    o["gdn_norm"] = row(lw["gdn_norm"])
    for k in ("ffn1_norm", "mix_norm", "ffn2_norm", "ple_norm"):
        o[k] = row(lw[k])
    return o


_BIG_PLAIN = ("w_gate", "s5_w_glu", "w_branch", "w_out", "ple_w_gate", "ple_w_proj")


def big_layout(w):
    o = {k: w[k] for k in _BIG_PLAIN}
    for f in ("ffn1", "ffn2"):
        o[f + "_wg"], o[f + "_wu"] = w[f + "_w_in"][:, :FFN_DIM], w[f + "_w_in"][:, FFN_DIM:]
        o[f + "_wo"] = w[f + "_w_out"]
    o["w_in"] = _perm_cols(w["w_in"])
    return {k: v.astype(bf16) for k, v in o.items()}


def big_unlayout(g):
    o = {k: g[k] for k in _BIG_PLAIN}
    for f in ("ffn1", "ffn2"):
        o[f + "_w_in"] = jnp.concatenate([g[f + "_wg"], g[f + "_wu"]], axis=1)
        o[f + "_w_out"] = g[f + "_wo"]
    o["w_in"] = _unperm_cols(g["w_in"])
    return o


_SMALL_KEYS = ("b_gate", "s5_log_step", "s5_a_re", "s5_a_im", "s5_b_re", "s5_b_im", "s5_c_re", "s5_c_im", "s5_d",
               "s5_b_glu", "lru_conv_b", "lru_w_r", "lru_b_r", "lru_w_i", "lru_b_i", "lru_lambda", "m2_conv_w",
               "m2_conv_b", "m2_dt_bias", "m2_a_log", "m2_d", "m2_norm", "gdn_dt_bias", "gdn_a_log", "gdn_norm",
               "ffn1_norm", "mix_norm", "ffn2_norm", "ple_norm")


def _layer_fwd(i, h0, pe, lw, big, ride=None):
    ride = ride or (None, None)
    n = f"l{i}_"
    lay = _layer_layout(lw)
    a = {"h0": h0, "lay": lay}
    h1, n1 = ffn_fwd(n + "ffn1_fwd", h0, lay["ffn1_norm"], big["ffn1_wg"], big["ffn1_wu"], big["ffn1_wo"])
    zero_b = jnp.zeros((1, PW), f32)
    proj, u = normlin_fwd(n + "inproj_fwd", h1, lay["mix_norm"], big["w_in"], zero_b, tn=896)
    gpre = lin_fwd(n + "gate_fwd", u, big["w_gate"], lay["b_gate"], tn=1024)
    small = proj[:, SMALL_OFF:SMALL_OFF + 16]
    raw = [lay["s5_are"], lay["s5_aim"], lay["s5_lstep"], lay["s5_bre"], lay["s5_bim"]]
    ops = s5_ops_fwd(n + "s5ops_fwd", raw)
    y5, sv5, *rode5 = s5_fwd(n + "s5_fwd", proj, ops, lay["s5_cre"], lay["s5_cim"], lay["s5_d"], ride[0])
    ya = s5_glu_fwd(n + "s5glu_fwd", y5, big["s5_w_glu"], lay["s5_b_glu"])
    lru_ws = [lw["lru_conv_w"], lay["lru_conv_b"], lay["lru_wr"], lay["lru_b_r"], lay["lru_wi"], lay["lru_b_i"], lay["lru_lambda"]]
    yb, svb = lru_fwd(n + "lru_fwd", proj, lru_ws)
    m2_ws = [lay[k] for k in ("m2_cwx", "m2_cwb", "m2_cwc", "m2_cbx", "m2_cbb", "m2_cbc", "m2_dtb", "m2_alog", "m2_dsk", "m2_norm")]
    yc, svc = m2_fwd(n + "m2_fwd", proj, small, m2_ws)
    yd, svd, *roded = gdn_fwd(n + "gdn_fwd", proj, small, lw["gdn_conv_w"], lay["gdn_dtb"], lay["gdn_alog"], lay["gdn_norm"], ride[1])
    ys = jnp.concatenate([ya, yb, yc, yd], axis=1)
    h2 = merge_fwd(n + "merge_fwd", h1, ys, gpre, big["w_branch"], big["w_out"])
    h3, n2 = ffn_fwd(n + "ffn2_fwd", h2, lay["ffn2_norm"], big["ffn2_wg"], big["ffn2_wu"], big["ffn2_wo"])
    h4 = ple_fwd(n + "ple_fwd", h3, pe, lay["ple_norm"], big["ple_w_gate"], big["ple_w_proj"])
    a.update(h1=h1, n1=n1, u=u, n2=n2, proj=proj, gpre=gpre, small=small, raw=raw, ops=ops, y5=y5, sv5=sv5, lru_ws=lru_ws, svb=svb,
             m2_ws=m2_ws, svc=svc, svd=svd, ys=ys, h2=h2, h3=h3, pe=pe)
    return h4, a, (rode5[0] if rode5 else None, roded[0] if roded else None)


def small_into(name, a, b, gproj, t=512):
    s, w = a.shape
    pad_w = PW - SMALL_OFF

    def body(a_ref, b_ref, _, o_ref):
        place = (_rows((w, pad_w)) == _lanes((w, pad_w))).astype(f32)
        o_ref[...] = _dg(a_ref[...] + b_ref[...], place, _NN, (3, 1))

    row = pl.BlockSpec((t, w), lambda i: (i, 0))
    return pl.pallas_call(
        body, name=name, grid=(s // t,), in_specs=[row, row, _ANY],
        out_specs=pl.BlockSpec((t, pad_w), lambda i: (i, SMALL_OFF // pad_w)),
        out_shape=jax.ShapeDtypeStruct(gproj.shape, gproj.dtype), input_output_aliases={2: 0}, compiler_params=_params((1,)),
    )(a, b, gproj)


RIDE_S5 = ("ffn2_w_in", "ffn2_w_out", "w_branch", "w_out", "ple_w_gate", "ple_w_proj", "s5_w_glu")
RIDE_FFN1 = ("w_in", "w_gate", "lru_conv_w", "m2_conv_w", "gdn_conv_w")
RIDE_LAST = ("ffn1_w_in", "ffn1_w_out")


def _grad_of(k, gb, small):
    if k in ("ffn1_w_in", "ffn2_w_in"):
        return jnp.concatenate([gb[k[:4] + "_wg"], gb[k[:4] + "_wu"]], axis=1)
    if k in ("ffn1_w_out", "ffn2_w_out"):
        return gb[k[:4] + "_wo"]
    if k == "w_in":
        return _unperm_cols(gb[k])
    if k == "m2_conv_w" and k not in small:
        return jnp.concatenate([small["m2_cwx"], small["m2_cwb"], small["m2_cwc"]], axis=1)
    return gb[k] if k in gb else small[k]


def _scattered(names, gb, small):
    return [_full_to_scattered(k, _grad_of(k, gb, small)[None]).astype(bf16) for k in names]


def _layer_bwd(i, a, lw, big, gh4, ride=None):
    n = f"l{i}_"
    lay = a["lay"]
    gb, gl = {}, {}
    s = gh4.shape[0]
    t = 512
    gh3, gl["ple_norm"], gb["ple_w_gate"], gb["ple_w_proj"] = ple_bwd(
        n + "ple_bwd", a["h3"], a["pe"], lay["ple_norm"], big["ple_w_gate"], big["ple_w_proj"], gh4)
    dn2, gb["ffn2_wg"], gb["ffn2_wu"], gb["ffn2_wo"] = ffn_bwd(
        n + "ffn2_bwd", a["n2"], big["ffn2_wg"], big["ffn2_wu"], big["ffn2_wo"], gh3)
    gh2, gl["ffn2_norm"] = norm_bwd(n + "gh2", a["h2"], lay["ffn2_norm"], gh3, _parts(dn2))
    gys, ggpre, gb["w_branch"], gb["w_out"] = merge_bwd(
        n + "merge_bwd", a["h1"], a["ys"], a["gpre"], big["w_branch"], big["w_out"], gh2)
    rd = gdn_bwd(n + "gdn_bwd", a["proj"], a["small"], lw["gdn_conv_w"], lay["gdn_dtb"], lay["gdn_alog"], lay["gdn_norm"], a["svd"], gys, 3, ride)
    gproj, dsm_d = rd[:2]
    gl["gdn_conv_w"] = jnp.concatenate(rd[2:5], axis=1)
    gl["gdn_dtb"], gl["gdn_alog"], gl["gdn_norm"] = rd[5:8]
    rode = rd[8:]
    rb = lru_bwd(n + "lru_bwd", a["proj"], a["lru_ws"], a["svb"], gys, 1, gproj)
    gproj = rb[0]
    gl["lru_conv_w"], gl["lru_conv_b"], gl["lru_wr"], gl["lru_b_r"], gl["lru_wi"], gl["lru_b_i"], gl["lru_lambda"] = rb[1:]
    rc = m2_bwd(n + "m2_bwd", a["proj"], a["small"], a["m2_ws"], a["svc"], gys, 2, gproj)
    gproj, dsm_c = rc[:2]
    for k, v in zip(("m2_cwx", "m2_cwb", "m2_cwc", "m2_cbx", "m2_cbb", "m2_cbc", "m2_dtb", "m2_alog", "m2_dsk", "m2_norm"), rc[2:]):
        gl[k] = v
    gy5, gb["s5_w_glu"], gl["s5_b_glu"] = s5_glu_bwd(n + "s5glu_bwd", a["y5"], big["s5_w_glu"], lay["s5_b_glu"], gys, 0)
    own = {}
    r5 = s5_bwd(n + "s5_bwd", a["proj"], a["ops"], lay["s5_cre"], lay["s5_cim"], lay["s5_d"], a["sv5"], gy5, gproj,
                _scattered(RIDE_S5, gb, gl) if ride is not None else None)
    gproj, gops, gl["s5_cre"], gl["s5_cim"], gl["s5_d"] = r5[0], r5[1:7], r5[7], r5[8], r5[9]
    own.update(zip(RIDE_S5, r5[10:]))
    gl["s5_are"], gl["s5_aim"], gl["s5_lstep"], gl["s5_bre"], gl["s5_bim"] = s5_ops_bwd(n + "s5ops_bwd", a["raw"], gops)
    gproj = small_into(n + "gsmall", dsm_c, dsm_d, gproj)
    zero_b = jnp.zeros((1, PW), f32)
    du_p, gb["w_in"], _ = lin_bwd(n + "inproj_bwd", a["u"], big["w_in"], zero_b, gproj, tn=896)
    du_g, gb["w_gate"], gl["b_gate"] = lin_bwd(n + "gate_bwd", a["u"], big["w_gate"], lay["b_gate"], ggpre, tn=1024)
    gh1, gl["mix_norm"] = norm_bwd(n + "gh1", a["h1"], lay["mix_norm"], gh2, _parts(du_p) + _parts(du_g))
    rf = ffn_bwd(n + "ffn1_bwd", a["n1"], big["ffn1_wg"], big["ffn1_wu"], big["ffn1_wo"], gh1,
                 ride=_scattered(RIDE_FFN1, gb, gl) if ride is not None else None)
    dn1, gb["ffn1_wg"], gb["ffn1_wu"], gb["ffn1_wo"] = rf[:4]
    own.update(zip(RIDE_FFN1, rf[4:]))
    gh0, gl["ffn1_norm"] = norm_bwd(n + "gh0", a["h0"], lay["ffn1_norm"], gh1, _parts(dn1))
    return gh0, gb, gl, (rode, own)


def _local_step(x, p, target, bigs, smalls, final_norm, ride_of=None, next_layer=None):
    h = x
    acts = []
    bigs, smalls = list(bigs), list(smalls)
    for i in range(DEPTH):
        h, a, got = _layer_fwd(i, h, p[i], smalls[i], bigs[i], next_layer[0] if next_layer and i == DEPTH - 2 else None)
        if next_layer and i == DEPTH - 2:
            bigs[i + 1], smalls[i + 1] = next_layer[1](got)
        acts.append(a)
    fg = final_norm.reshape(1, -1)
    loss, gh, gfn = final_loss("final_loss", h, fg, target)
    gbs, gss, rode = [None] * DEPTH, [None] * DEPTH, [None] * DEPTH
    for i in reversed(range(DEPTH)):
        ride = ride_of(gbs[i + 1], gss[i + 1]) if ride_of is not None and i + 1 < DEPTH else None
        gh, gb, gl, (got, own) = _layer_bwd(i, acts[i], smalls[i], bigs[i], gh, ride)
        if ride is not None:
            rode[i + 1], rode[i] = got, own
        _, pull = jax.vjp(_layer_layout, smalls[i])
        lay_g = {k: gl[k] for k in acts[i]["lay"]}
        gs = pull(lay_g)[0]
        gs = dict(gs)
        gs["lru_conv_w"] = gl["lru_conv_w"]
        gs["gdn_conv_w"] = gl["gdn_conv_w"]
        gbs[i], gss[i] = gb, gs
    return loss, gh, gbs, gss, gfn.reshape(-1), rode


MESH_AXES = ("x", "y", "c")
PACK_W = 1024
PACK_ROWS = 256

W_NAMES = ("ffn1_norm", "ffn1_w_in", "ffn1_w_out", "mix_norm", "w_in", "w_gate", "b_gate", "s5_log_step", "s5_a_re",
           "s5_a_im", "s5_b_re", "s5_b_im", "s5_c_re", "s5_c_im", "s5_d", "s5_w_glu", "s5_b_glu", "lru_conv_w",
           "lru_conv_b", "lru_w_r", "lru_b_r", "lru_w_i", "lru_b_i", "lru_lambda", "m2_conv_w", "m2_conv_b",
           "m2_dt_bias", "m2_a_log", "m2_d", "m2_norm", "gdn_conv_w", "gdn_dt_bias", "gdn_a_log", "gdn_norm",
           "w_branch", "w_out", "ffn2_norm", "ffn2_w_in", "ffn2_w_out", "ple_norm", "ple_w_gate", "ple_w_proj",
           "final_norm")
COL_SHARDED = ("ffn1_w_in", "w_in", "w_gate", "lru_conv_w", "m2_conv_w", "gdn_conv_w", "w_branch", "ffn2_w_in", "ple_w_proj")
ROW_SHARDED = ("ffn1_w_out", "s5_w_glu", "w_out", "ffn2_w_out", "ple_w_gate")
BIG_NAMES = ("ffn1_w_in", "ffn1_w_out", "w_in", "w_gate", "s5_w_glu", "w_branch", "w_out", "ffn2_w_in", "ffn2_w_out",
             "ple_w_gate", "ple_w_proj")
CONV_NAMES = ("lru_conv_w", "m2_conv_w", "gdn_conv_w")
SHARDED = BIG_NAMES + CONV_NAMES
REPLICATED = tuple(k for k in W_NAMES if k not in SHARDED)


def _gathered_to_full(name, g):
    if name in COL_SHARDED:
        g = jnp.moveaxis(g, 0, -2)
        return g.reshape(g.shape[:-2] + (g.shape[-2] * g.shape[-1],))
    g = jnp.moveaxis(g, 0, 1)
    return g.reshape((g.shape[0], g.shape[1] * g.shape[2]) + g.shape[3:])


def _full_to_scattered(name, w):
    if name in COL_SHARDED:
        w = w.reshape(w.shape[:-1] + (N_DEV, w.shape[-1] // N_DEV))
        return jnp.moveaxis(w, -2, 0)
    w = w.reshape((w.shape[0], N_DEV, w.shape[1] // N_DEV) + w.shape[2:])
    return jnp.moveaxis(w, 1, 0)


def _pack(arrs):
    pieces = []
    for a in arrs:
        k = -(-a.size // PACK_W)
        pieces.append(jnp.pad(a.reshape(-1), (0, k * PACK_W - a.size)).reshape(k, PACK_W))
    buf = jnp.concatenate(pieces, axis=0)
    return jnp.pad(buf, ((0, -buf.shape[0] % PACK_ROWS), (0, 0)))


def _unpack(buf, shapes):
    out, r = [], 0
    for sh in shapes:
        n = math.prod(sh)
        k = -(-n // PACK_W)
        out.append(buf[r:r + k].reshape(-1)[:n].reshape(sh))
        r += k
    return out


def _peer(k):
    mx, my, mc = (lax.axis_index(a) for a in MESH_AXES)
    px = 1 - mx if k & 4 else mx
    py = 1 - my if k & 2 else my
    pc = 1 - mc if k & 1 else mc
    return (px, py, pc), 4 * px + 2 * py + pc


def all_gather(name, xs):
    n = len(xs)

    def body(*refs):
        x_refs, out_refs = refs[:n], refs[n:2 * n]
        send_sems, recv_sems, local_sems = refs[2 * n:]
        mx, my, mc = (lax.axis_index(a) for a in MESH_AXES)
        me, sibling = (mx, my, mc), (mx, my, 1 - mc)
        chips = [(1 - mx, my), (mx, 1 - my), (1 - mx, 1 - my)]

        def slot(i, px, py, pc):
            return out_refs[i].at[4 * px + 2 * py + pc]

        def copy(k, i, block, to, src=None):
            return pltpu.make_async_remote_copy(
                src_ref=slot(i, *block) if src is None else src, dst_ref=slot(i, *block),
                send_sem=send_sems.at[k, i], recv_sem=recv_sems.at[k, i], device_id=to, device_id_type=pl.DeviceIdType.MESH)

        mine = [pltpu.make_async_copy(x_refs[i], slot(i, *me), local_sems.at[i]) for i in range(n)]
        first = []
        for i in range(n):
            mine[i].start()
            first.append(copy(0, i, me, sibling, src=x_refs[i]))
            first += [copy(1 + j, i, me, (*chip, mc), src=x_refs[i]) for j, chip in enumerate(chips)]
        for cp in first:
            cp.start()
        passed = []
        for i in range(n):
            for j, chip in enumerate(chips):
                copy(1 + j, i, (*chip, mc), me).wait_recv()
                cp = copy(4 + j, i, (*chip, mc), sibling)
                cp.start()
                passed.append(cp)
        for i in range(n):
            copy(0, i, sibling, me).wait_recv()
            for j, chip in enumerate(chips):
                copy(4 + j, i, (*chip, 1 - mc), me).wait_recv()
        for cp in first + passed:
            cp.wait_send()
        for cp in mine:
            cp.wait()

    res = pl.pallas_call(
        body, name=name, out_shape=[jax.ShapeDtypeStruct((N_DEV,) + x.shape, x.dtype) for x in xs],
        in_specs=[_ANY] * n, out_specs=[_ANY] * n,
        scratch_shapes=[pltpu.SemaphoreType.DMA((7, n)), pltpu.SemaphoreType.DMA((7, n)), pltpu.SemaphoreType.DMA((n,))],
    )(*xs)
    return list(res)


def _exchange_copies(g_refs, out_refs, send_sems, recv_sems, local_sems, with_incoming=True):
    mx, my, mc = (lax.axis_index(a) for a in MESH_AXES)
    me = 4 * mx + 2 * my + mc
    n = len(g_refs)
    local = [pltpu.make_async_copy(g_refs[i].at[me], out_refs[i].at[me], local_sems.at[i]) for i in range(n)]
    outgoing, incoming = [], []
    for k in range(1, N_DEV):
        peer, pidx = _peer(k)
        for i in range(n):
            sems = dict(send_sem=send_sems.at[k - 1, i], recv_sem=recv_sems.at[k - 1, i], device_id=peer,
                        device_id_type=pl.DeviceIdType.MESH)
            outgoing.append(pltpu.make_async_remote_copy(src_ref=g_refs[i].at[pidx], dst_ref=out_refs[i].at[me], **sems))
            if with_incoming:
                incoming.append(pltpu.make_async_remote_copy(src_ref=g_refs[i].at[pidx], dst_ref=out_refs[i].at[pidx], **sems))
    return local, outgoing, incoming


def _exchange_start(*refs):
    local, outgoing, _ = _exchange_copies(*refs, with_incoming=False)
    for cp in local + outgoing:
        cp.start()


def _exchange_wait(*refs):
    local, outgoing, incoming = _exchange_copies(*refs)
    for cp in incoming:
        cp.wait_recv()
    for cp in outgoing:
        cp.wait_send()
    for cp in local:
        cp.wait()


def _exchange_sems(n):
    return [pltpu.SemaphoreType.DMA((7, n)), pltpu.SemaphoreType.DMA((7, n)), pltpu.SemaphoreType.DMA((n,))]


def exchange(name, gs):
    n = len(gs)

    def body(*refs):
        args = (refs[:n], refs[n:2 * n]) + tuple(refs[2 * n:])
        _exchange_start(*args)
        _exchange_wait(*args)

    res = pl.pallas_call(
        body, name=name, out_shape=[jax.ShapeDtypeStruct(g.shape, g.dtype) for g in gs], in_specs=[_ANY] * n, out_specs=[_ANY] * n,
        scratch_shapes=_exchange_sems(n),
    )(*gs)
    return list(res)


def sum_slots(name, buf):
    return addn(name, _parts(buf), PACK_ROWS)


def _adamw_math(ww, gg, mm_, vv):
    m2 = ADAM_B1 * mm_ + (1.0 - ADAM_B1) * gg
    v2 = ADAM_B2 * vv + (1.0 - ADAM_B2) * (gg * gg)
    m_hat = m2 / (1.0 - ADAM_B1 ** ADAM_STEP)
    v_hat = v2 / (1.0 - ADAM_B2 ** ADAM_STEP)
    delta = -ADAM_LR * (m_hat / (jnp.sqrt(v_hat) + ADAM_EPS) + ADAM_WD * ww)
    return delta, m2, v2


def adamw(name, w, g, m, v):
    spec = pl.BlockSpec((PACK_ROWS, PACK_W), lambda i: (i, 0))
    return run_fwd(name, _adamw_math, (w.shape[0] // PACK_ROWS,), [w, g, m, v], [spec] * 4, [_out(w.shape, spec)] * 3)[0]


def reduce_adamw(name, slots, w, m, v):
    shape = w.shape
    r, c = shape[-2:]
    a = math.prod(shape[:-2])
    per = a // DEPTH
    tr = r
    while tr * c * 4 > (1 << 20) and tr % 16 == 0:
        tr //= 2
    s3 = [s.reshape((N_DEV, per, r, c)) for s in slots]
    w3, m3, v3 = (t.reshape((a, r, c)) for t in (w, m, v))

    def fn(*t):
        layer = pl.program_id(0) // per
        g = None
        for l in range(DEPTH):
            gl = t[l * N_DEV].astype(f32)
            for d in range(1, N_DEV):
                gl = gl + t[l * N_DEV + d].astype(f32)
            g = gl if g is None else jnp.where(layer == l, gl, g)
        k = DEPTH * N_DEV
        return (g,) + _adamw_math(t[k], g, t[k + 1], t[k + 2])

    specs, ins = [], []
    for l in range(DEPTH):
        for d in range(N_DEV):
            specs.append(pl.BlockSpec((None, None, tr, c), lambda i, j, _d=d, _l=l: (_d, jnp.clip(i - _l * per, 0, per - 1), j, 0)))
            ins.append(s3[l])
    spec = pl.BlockSpec((None, tr, c), lambda i, j: (i, j, 0))
    res = run_fwd(name, fn, (a, r // tr), ins + [w3, m3, v3], specs + [spec] * 3, [_out((a, r, c), spec)] * 4)[0]
    return [t.reshape(shape) for t in res]


def kernel(*args):
    nw = len(W_NAMES)
    x, p = args[0], args[1]
    w = dict(zip(W_NAMES, args[2:2 + nw]))
    target = args[2 + nw]
    m = dict(zip(W_NAMES, args[3 + nw:3 + 2 * nw]))
    v = dict(zip(W_NAMES, args[3 + 2 * nw:3 + 3 * nw]))

    wire = lambda k, i: w[k][i].astype(bf16) if k in BIG_NAMES else w[k][i]

    def layer_params(i, gathered):
        full = {k: _gathered_to_full(k, g[:, None])[0] for k, g in zip(SHARDED, gathered)}
        small = {k: w[k][i] for k in _SMALL_KEYS if k not in CONV_NAMES}
        small.update({k: full[k] for k in CONV_NAMES})
        return big_layout({k: full[k] for k in BIG_NAMES}), small

    big0, small0 = layer_params(0, all_gather("ag_layer0", [wire(k, 0) for k in SHARDED]))
    to_all = lambda k: jnp.broadcast_to(wire(k, DEPTH - 1)[None], (N_DEV,) + w[k].shape[1:])
    early = ("ffn1_w_in", "ffn1_w_out")
    late = tuple(k for k in SHARDED if k not in early)

    def last_layer(got):
        by_name = dict(zip(early + late, list(got[0]) + list(got[1])))
        return layer_params(DEPTH - 1, [by_name[k] for k in SHARDED])

    bigs, smalls = [big0, None], [small0, None]
    next_layer = (([to_all(k) for k in early], [to_all(k) for k in late]), last_layer)
    loss, gx, gbs, gss, gfn, rode = _local_step(x[0], p[:, 0], target[0], bigs, smalls, w["final_norm"],
                                                lambda gb, gs: _scattered(SHARDED, gb, gs), next_layer)
    loss = lax.psum(loss[0, 0], MESH_AXES)

    rode[0].update(zip(RIDE_LAST, exchange("rs_last", _scattered(RIDE_LAST, gbs[0], gss[0]))))
    rode[0] = [rode[0][k] for k in SHARDED]
    gfull = [dict(big_unlayout(gbs[i]), **gss[i]) for i in range(DEPTH)]
    stack = lambda k: jnp.stack([gfull[i][k] for i in range(DEPTH)])
    outs = {}
    kinds = ("grad", "delta", "new_m", "new_v")
    for j, k in enumerate(SHARDED):
        for kind, a in zip(kinds, reduce_adamw("adamw_" + k, [rode[i][j] for i in range(DEPTH)], w[k], m[k], v[k])):
            outs[kind + "_" + k] = a

    shapes = [w[k].shape for k in REPLICATED]
    g_rep = sum_slots("sum_replicated", all_gather("ag_replicated", [_pack([gfn if k == "final_norm" else stack(k) for k in REPLICATED])])[0])
    res = adamw("adamw_replicated", _pack([w[k] for k in REPLICATED]), g_rep, _pack([m[k] for k in REPLICATED]), _pack([v[k] for k in REPLICATED]))
    for kind, buf in zip(kinds, [g_rep] + list(res)):
        for k, a in zip(REPLICATED, _unpack(buf, shapes)):
            outs[kind + "_" + k] = a
    return (loss, gx[None]) + tuple(outs[kind + "_" + k] for kind in kinds for k in W_NAMES)
```

```python
import functools
import math

import jax
import jax.numpy as jnp
from jax import lax
from jax.experimental import pallas as pl
from jax.experimental.pallas import tpu as pltpu

f32 = jnp.float32
bf16 = jnp.bfloat16

EPS = 1e-6
DEPTH = 2
D_MODEL = 1024
FFN_DIM = 2816
BW = 512
IN_WIDTH = 5136
PW = 5376
SMALL_OFF = 5120
CHUNK = 64
LS = 128
LRU_C = 8.0
N_DEV = 8
VMEM_LIMIT_BYTES = 56 * 1024 * 1024

ADAM_LR, ADAM_B1, ADAM_B2, ADAM_EPS, ADAM_WD, ADAM_STEP = 0.001, 0.9, 0.999, 1e-08, 0.01, 10


_NN, _NT, _TN = ((1,), (0,)), ((1,), (1,)), ((0,), (0,))


def _pieces(x, n):
    parts, r = [], x
    for i in range(n):
        p = r.astype(bf16)
        parts.append(p)
        if i + 1 < n:
            r = r - p.astype(f32)
    return parts


def _dg(a, b, dims, mode):
    sa, sb = mode
    pa, pb = _pieces(a, sa), _pieces(b, sb)
    out = None
    for i in reversed(range(sa)):
        for j in reversed(range(sb)):
            if i + j < max(sa, sb):
                d = lax.dot_general(pa[i], pb[j], (dims, ((), ())), preferred_element_type=f32)
                out = d if out is None else out + d
    return out


def _make_mm(mode):
    sa, sb = mode
    cot = lambda s_other: 1 if mode == (1, 1) else (3 if s_other == 1 else 2)
    m_g_b, m_a_g, m_g_a, m_b_g = (cot(sb), sb), (sa, cot(sa)), (cot(sa), sa), (sb, cot(sb))

    @jax.custom_vjp
    def nn(a, b):
        return _dg(a, b, _NN, mode)

    @jax.custom_vjp
    def nt(a, b):
        return _dg(a, b, _NT, mode)

    @jax.custom_vjp
    def tn(a, b):
        return _dg(a, b, _TN, mode)

    nn.defvjp(lambda a, b: (nn(a, b), (a, b)), lambda r, g: (_dg(g, r[1], _NT, m_g_b), _dg(r[0], g, _TN, m_a_g)))
    nt.defvjp(lambda a, b: (nt(a, b), (a, b)), lambda r, g: (_dg(g, r[1], _NN, m_g_b), _dg(g, r[0], _TN, m_g_a)))
    tn.defvjp(lambda a, b: (tn(a, b), (a, b)), lambda r, g: (_dg(r[1], g, _NT, m_b_g), _dg(r[0], g, _NN, m_a_g)))
    return nn, nt, tn


mm, mm_nt, mm_tn = _make_mm((1, 1))
xmm, xmm_nt, xmm_tn = _make_mm((2, 2))
lmm, lmm_nt, lmm_tn = _make_mm((1, 3))
rmm, rmm_nt, rmm_tn = _make_mm((3, 1))


@jax.custom_vjp
def _tri_inv(m):
    n = m.shape[0]
    eye = (_rows((n, n)) == _lanes((n, n))).astype(f32)
    blk = (_rows((n, n)) // 16) == (_lanes((n, n)) // 16)
    x = lambda a, b: _dg(a, b, _NN, (2, 2))
    nb = jnp.where(blk, m, 0.0)
    p = -nb
    t = eye + p
    for _ in range(3):
        p = x(p, p)
        t = t + x(t, p)
    q = x(t, m - nb)
    imq = eye - q
    return x(imq + x(imq, x(q, q)), t)


def _tri_inv_fwd(m):
    t = _tri_inv(m)
    return t, t


def _tri_inv_bwd(t, g):
    return (-_dg(_dg(t, g, _TN, (2, 2)), t, _NT, (2, 2)),)


_tri_inv.defvjp(_tri_inv_fwd, _tri_inv_bwd)


def _rows(shape):
    return lax.broadcasted_iota(jnp.int32, shape, 0)


def _lanes(shape):
    return lax.broadcasted_iota(jnp.int32, shape, 1)


def _rms(x, g):
    return x * lax.rsqrt(jnp.mean(x * x, axis=-1, keepdims=True) + EPS) * g


def _silu(x):
    return x * jax.nn.sigmoid(x)


def _gelu(x):
    return 0.5 * x * (1.0 + jnp.tanh(0.7978845608028654 * (x + 0.044715 * x * x * x)))


def _softplus(x):
    return jnp.maximum(x, 0.0) + jnp.log1p(jnp.exp(-jnp.abs(x)))


def _expm1(x):
    p = x * (1.0 + x * (0.5 + x * (1.0 / 6 + x * (1.0 / 24 + x * (1.0 / 120 + x * (1.0 / 720 + x * (1.0 / 5040)))))))
    return jnp.where(x > -0.3, p, jnp.exp(x) - 1.0)


def _pick_row(x, r):
    return jnp.sum(jnp.where(_rows(x.shape) == r, x, 0.0), axis=0, keepdims=True)


def _pick_lane(x, c):
    return jnp.sum(jnp.where(_lanes(x.shape) == c, x, 0.0), axis=1, keepdims=True)


def _shift_up(g, j):
    n = g.shape[0]
    return jnp.where(_rows(g.shape) < n - j, pltpu.roll(g, n - j, 0), 0.0)


@functools.partial(jax.custom_vjp, nondiff_argnums=(1, 2))
def _shift(x, j, fill):
    return jnp.where(_rows(x.shape) >= j, pltpu.roll(x, j, 0), fill)


_shift.defvjp(lambda x, j, fill: (_shift(x, j, fill), None), lambda j, fill, _, g: (_shift_up(g, j),))


@functools.partial(jax.custom_vjp, nondiff_argnums=(2,))
def _shift_halo(x, prev8, j):
    xr = pltpu.roll(x, j, 0)
    pr = pltpu.roll(prev8, j, 0)
    top = jnp.where(_rows(pr.shape) < j, pr, xr[:8])
    return jnp.concatenate([top, xr[8:]], axis=0)


def _shift_halo_bwd(j, _, g):
    g8 = g[:8]
    dprev = jnp.where(_rows(g8.shape) >= 8 - j, pltpu.roll(g8, 8 - j, 0), 0.0)
    return _shift_up(g, j), dprev


_shift_halo.defvjp(lambda x, p, j: (_shift_halo(x, p, j), None), _shift_halo_bwd)


def _conv4(x, prev8, w, b):
    y = _pick_row(w, 3) * x
    for k in range(3):
        y = y + _pick_row(w, k) * _shift_halo(x, prev8, 3 - k)
    return y if b is None else y + b


def _cmul(ar, ai, br, bi):
    return ar * br - ai * bi, ar * bi + ai * br


_ANY = pl.BlockSpec(memory_space=pl.ANY)


def _params(grid):
    return pltpu.CompilerParams(dimension_semantics=("arbitrary",) * len(grid), vmem_limit_bytes=VMEM_LIMIT_BYTES)


def _first(axes):
    ok = pl.program_id(axes[0]) == 0
    for a in axes[1:]:
        ok = jnp.logical_and(ok, pl.program_id(a) == 0)
    return ok


def _store(ref, val, acc):
    val = val.astype(ref.dtype)
    if acc is None:
        ref[...] = val
        return
    first = _first(acc)

    @pl.when(first)
    def _():
        ref[...] = val

    @pl.when(jnp.logical_not(first))
    def _():
        ref[...] += val


def _full(a):
    nd = a.ndim
    return pl.BlockSpec(a.shape, lambda *g: (0,) * nd)


def _out(shape, spec, acc=None, dtype=f32):
    return dict(shape=tuple(shape), spec=spec, acc=acc, dtype=dtype)


def _grid_step(grid):
    step = pl.program_id(0)
    for ax in range(1, len(grid)):
        step = step * grid[ax] + pl.program_id(ax)
    return step


def run_fwd(name, fn, grid, ins, in_specs, outs, carry=None, ride=None):
    n_in, n_out = len(ins), len(outs)
    cshapes = carry["shapes"] if carry else []
    nc = len(cshapes)
    ng = len(grid)
    n_r = len(ride) if ride else 0

    def body(*refs):
        in_refs = refs[:n_in]
        out_refs = refs[n_in + n_r:n_in + n_r + n_out]
        save_refs = refs[n_in + n_r + n_out:n_in + n_r + n_out + nc]
        c_refs = refs[n_in + 2 * n_r + n_out + nc:n_in + 2 * n_r + n_out + 2 * nc]
        if n_r:
            step = _grid_step(grid)
            r_args = (refs[n_in:n_in + n_r], refs[n_in + n_r + n_out + nc:n_in + 2 * n_r + n_out + nc]) \
                + tuple(refs[n_in + 2 * n_r + n_out + 2 * nc:])
            pl.when(step == 0)(lambda: _exchange_start(*r_args))
        vals = [r[...] for r in in_refs]
        if carry:
            @pl.when(pl.program_id(carry["axis"]) == 0)
            def _():
                for c in c_refs:
                    c[...] = jnp.zeros(c.shape, f32)
            cin = tuple(c[...] for c in c_refs)
            for s, v in zip(save_refs, cin):
                s[...] = v
            cout, res = fn(cin, *vals)
            for c, v in zip(c_refs, cout):
                c[...] = v
        else:
            res = fn(*vals)
        for o, r, d in zip(out_refs, res, outs):
            _store(o, r, d["acc"])
        if n_r:
            pl.when(step == math.prod(grid) - 1)(lambda: _exchange_wait(*r_args))

    out_shape = [jax.ShapeDtypeStruct(d["shape"], d["dtype"]) for d in outs]
    out_specs = [d["spec"] for d in outs]
    for cs in cshapes:
        out_shape.append(jax.ShapeDtypeStruct(tuple(grid) + tuple(cs), f32))
        out_specs.append(pl.BlockSpec((None,) * ng + tuple(cs), lambda *g, _n=len(cs): tuple(g) + (0,) * _n))
    if n_r:
        out_shape += [jax.ShapeDtypeStruct(g.shape, g.dtype) for g in ride]
        out_specs += [_ANY] * n_r
    res = pl.pallas_call(
        body, name=name, grid=grid, in_specs=list(in_specs) + [_ANY] * n_r, out_specs=out_specs, out_shape=out_shape,
        scratch_shapes=[pltpu.VMEM(tuple(cs), f32) for cs in cshapes] + (_exchange_sems(n_r) if n_r else []),
        compiler_params=_params(grid),
    )(*ins, *(ride or []))
    if n_r:
        return list(res[:n_out]), list(res[n_out:n_out + nc]), list(res[n_out + nc:])
    return list(res[:n_out]), list(res[n_out:])


def run_bwd(name, fn, grid, ins, in_specs, gouts, gout_specs, wants, carry=None, into=None, ride=None):
    n_in, n_g, n_w = len(ins), len(gouts), len(wants)
    n_a = 0 if into is None else 1
    ride = list(ride or [])
    n_r = len(ride)
    saved = carry["saved"] if carry else []
    nc = len(saved)
    ng = len(grid)

    def body(*refs):
        in_refs = refs[:n_in]
        g_refs = refs[n_in:n_in + n_g]
        s_refs = refs[n_in + n_g:n_in + n_g + nc]
        base = n_in + n_g + nc + n_a
        r_in = refs[base:base + n_r]
        w_refs = refs[base + n_r:base + n_r + n_w]
        r_out = refs[base + n_r + n_w:base + 2 * n_r + n_w]
        dc_refs = refs[base + 2 * n_r + n_w:base + 2 * n_r + n_w + nc]
        if n_r:
            steps = math.prod(grid)
            step = pl.program_id(0)
            for ax in range(1, ng):
                step = step * grid[ax] + pl.program_id(ax)
            r_args = (r_in, r_out) + tuple(refs[base + 2 * n_r + n_w + nc:])
            pl.when(step == 0)(lambda: _exchange_start(*r_args))
        vals = [r[...].astype(f32) for r in in_refs]
        gs = tuple(r[...].astype(f32) for r in g_refs)
        if carry:
            @pl.when(pl.program_id(carry["axis"]) == 0)
            def _():
                for c in dc_refs:
                    c[...] = jnp.zeros(c.shape, f32)
            cin = tuple(s[...] for s in s_refs)
            _, vjp = jax.vjp(fn, cin, *vals)
            grads = vjp((tuple(c[...] for c in dc_refs), gs))
            for c, v in zip(dc_refs, grads[0]):
                c[...] = v
            dvals = grads[1:]
        else:
            _, vjp = jax.vjp(fn, *vals)
            dvals = vjp(gs)
        for o, d in zip(w_refs, wants):
            idx = d["idx"]
            val = dvals[idx] if isinstance(idx, int) else jnp.concatenate([dvals[j] for j in idx], axis=1)
            _store(o, val, d["acc"])
        if n_r:
            pl.when(step == steps - 1)(lambda: _exchange_wait(*r_args))

    rev = carry["rev"] if carry else None
    s_specs = []
    for a in saved:
        n = a.ndim - ng
        s_specs.append(pl.BlockSpec((None,) * ng + tuple(a.shape[ng:]), lambda *g, _n=n: tuple(rev(g)) + (0,) * _n))
    res = pl.pallas_call(
        body, name=name, grid=grid, in_specs=list(in_specs) + list(gout_specs) + s_specs + [_ANY] * (n_a + n_r),
        out_specs=[d["spec"] for d in wants] + [_ANY] * n_r,
        out_shape=[jax.ShapeDtypeStruct(d["shape"], d["dtype"]) for d in wants] + [jax.ShapeDtypeStruct(g.shape, g.dtype) for g in ride],
        input_output_aliases={n_in + n_g + nc: 0} if n_a else {},
        scratch_shapes=[pltpu.VMEM(tuple(a.shape[ng:]), f32) for a in saved] + (_exchange_sems(n_r) if n_r else []),
        compiler_params=_params(grid),
    )(*ins, *gouts, *saved, *([into] if n_a else []), *ride)
    return list(res)


def _want(idx, shape, spec, acc=None, dtype=f32):
    d = _out(shape, spec, acc, dtype)
    d["idx"] = idx
    return d


def addn(name, items, t):
    s, w = items[0][0].shape[-2:]
    specs = []
    for a, j in items:
        if j is None:
            specs.append(pl.BlockSpec((t, w), lambda i: (i, 0)))
        else:
            specs.append(pl.BlockSpec((None, t, w), lambda i, _j=j: (_j, i, 0)))

    def fn(*xs):
        y = xs[0]
        for x in xs[1:]:
            y = y + x
        return (y,)

    return run_fwd(name, fn, (s // t,), [a for a, _ in items], specs,
                   [_out((s, w), pl.BlockSpec((t, w), lambda i: (i, 0)))])[0][0]


def _parts(a):
    return [(a, j) for j in range(a.shape[0])]


def _ffn_core(n, wg, wu, wo):
    return (0.5 * mm(_silu(mm(n, wg)) * mm(n, wu), wo),)


def ffn_fwd(name, h, g, wg, wu, wo, t=512, tf=1408):
    s, d = h.shape
    f = wg.shape[1]

    def fn(hh, gg, a, b, c):
        n = _rms(hh, gg)
        return _ffn_core(n, a, b, c)[0] + (pl.program_id(1) == 0).astype(f32) * hh, n

    specs = [pl.BlockSpec((t, d), lambda i, j: (i, 0)), _full(g), pl.BlockSpec((d, tf), lambda i, j: (0, j)),
             pl.BlockSpec((d, tf), lambda i, j: (0, j)), pl.BlockSpec((tf, d), lambda i, j: (j, 0))]
    outs = [_out((s, d), pl.BlockSpec((t, d), lambda i, j: (i, 0)), acc=(1,)),
            _out((s, d), pl.BlockSpec((t, d), lambda i, j: (i, 0)), dtype=bf16)]
    return run_fwd(name, fn, (s // t, f // tf), [h, g, wg, wu, wo], specs, outs)[0]


def ffn_bwd(name, n, wg, wu, wo, gout, t=1024, tf=256, ride=None):
    s, d = n.shape
    t = min(t, s)
    f = wg.shape[1]
    nj = f // tf
    specs = [pl.BlockSpec((t, d), lambda j, i: (i, 0)), pl.BlockSpec((d, tf), lambda j, i: (0, j)),
             pl.BlockSpec((d, tf), lambda j, i: (0, j)), pl.BlockSpec((tf, d), lambda j, i: (j, 0))]
    wants = [_want(0, (nj, s, d), pl.BlockSpec((None, t, d), lambda j, i: (j, i, 0)), dtype=bf16),
             _want(1, wg.shape, pl.BlockSpec((d, tf), lambda j, i: (0, j)), acc=(1,)),
             _want(2, wu.shape, pl.BlockSpec((d, tf), lambda j, i: (0, j)), acc=(1,)),
             _want(3, wo.shape, pl.BlockSpec((tf, d), lambda j, i: (j, 0)), acc=(1,))]
    return run_bwd(name, _ffn_core, (nj, s // t), [n, wg, wu, wo], specs,
                   [gout], [pl.BlockSpec((t, d), lambda j, i: (i, 0))], wants, ride=ride)


def norm_bwd(name, h, g, base, parts, t=512):
    s, d = h.shape

    def fn(hh, gg, bb, *ps):
        dn = ps[0].astype(f32)
        for p in ps[1:]:
            dn = dn + p.astype(f32)
        _, vjp = jax.vjp(_rms, hh, gg)
        dh, dg = vjp(dn)
        return bb + dh, dg

    row = pl.BlockSpec((t, d), lambda i: (i, 0))
    specs = [row, _full(g), row] + [pl.BlockSpec((None, t, d), lambda i, _j=j: (_j, i, 0)) for _, j in parts]
    outs = [_out((s, d), row), _out(g.shape, _full(g), acc=(0,))]
    return run_fwd(name, fn, (s // t,), [h, g, base] + [a for a, _ in parts], specs, outs)[0]


def _lin_tile(u, w, b):
    return (mm(u, w) + b,)


def normlin_fwd(name, h, g, w, b, t=1024, tn=1024):
    s, d = h.shape
    t = min(t, s)
    n = w.shape[1]

    def fn(hh, gg, ww, bb):
        u = _rms(hh, gg)
        return mm(u, ww) + bb, u

    specs = [pl.BlockSpec((t, d), lambda i, j: (i, 0)), _full(g), pl.BlockSpec((d, tn), lambda i, j: (0, j)),
             pl.BlockSpec((1, tn), lambda i, j: (0, j))]
    outs = [_out((s, n), pl.BlockSpec((t, tn), lambda i, j: (i, j))),
            _out((s, d), pl.BlockSpec((t, d), lambda i, j: (i, 0)), dtype=bf16)]
    return run_fwd(name, fn, (s // t, n // tn), [h, g, w, b], specs, outs)[0]


def lin_fwd(name, u, w, b, t=1024, tn=1024):
    s, d = u.shape
    t = min(t, s)
    n = w.shape[1]
    specs = [pl.BlockSpec((t, d), lambda i, j: (i, 0)), pl.BlockSpec((d, tn), lambda i, j: (0, j)),
             pl.BlockSpec((1, tn), lambda i, j: (0, j))]
    outs = [_out((s, n), pl.BlockSpec((t, tn), lambda i, j: (i, j)))]
    return run_fwd(name, _lin_tile, (s // t, n // tn), [u, w, b], specs, outs)[0][0]


def lin_bwd(name, u, w, b, gout, t=1024, tn=1024):
    s, d = u.shape
    t = min(t, s)
    n = w.shape[1]
    nj = n // tn
    specs = [pl.BlockSpec((t, d), lambda j, i: (i, 0)), pl.BlockSpec((d, tn), lambda j, i: (0, j)),
             pl.BlockSpec((1, tn), lambda j, i: (0, j))]
    wants = [_want(0, (nj, s, d), pl.BlockSpec((None, t, d), lambda j, i: (j, i, 0)), dtype=bf16),
             _want(1, w.shape, pl.BlockSpec((d, tn), lambda j, i: (0, j)), acc=(1,)),
             _want(2, b.shape, pl.BlockSpec((1, tn), lambda j, i: (0, j)), acc=(1,))]
    return run_bwd(name, _lin_tile, (nj, s // t), [u, w, b], specs,
                   [gout], [pl.BlockSpec((t, tn), lambda j, i: (i, j))], wants)


def _merge_tile(n_axis, residual):
    def fn(h, y, gp, wb, wo):
        part = mm(jax.nn.sigmoid(gp) * mm(y, wb), wo)
        if residual:
            part = part + (pl.program_id(n_axis) == 0).astype(f32) * h
        return (part,)
    return fn


def merge_fwd(name, h, ys, gpre, wb, wo, t=512):
    s, d = h.shape
    specs = [pl.BlockSpec((t, d), lambda i, n: (i, 0)), pl.BlockSpec((t, BW), lambda i, n: (i, n)),
             pl.BlockSpec((t, d), lambda i, n: (i, n)), pl.BlockSpec((None, BW, d), lambda i, n: (n, 0, 0)), _full(wo)]
    outs = [_out((s, d), pl.BlockSpec((t, d), lambda i, n: (i, 0)), acc=(1,))]
    return run_fwd(name, _merge_tile(1, True), (s // t, 4), [h, ys, gpre, wb, wo], specs, outs)[0][0]


def merge_bwd(name, h, ys, gpre, wb, wo, gout, t=512):
    s, d = h.shape
    specs = [pl.BlockSpec((t, d), lambda n, i: (i, 0)), pl.BlockSpec((t, BW), lambda n, i: (i, n)),
             pl.BlockSpec((t, d), lambda n, i: (i, n)), pl.BlockSpec((None, BW, d), lambda n, i: (n, 0, 0)), _full(wo)]
    wants = [_want(1, ys.shape, pl.BlockSpec((t, BW), lambda n, i: (i, n))),
             _want(2, gpre.shape, pl.BlockSpec((t, d), lambda n, i: (i, n))),
             _want(3, wb.shape, pl.BlockSpec((None, BW, d), lambda n, i: (n, 0, 0)), acc=(1,)),
             _want(4, wo.shape, _full(wo), acc=(0, 1))]
    return run_bwd(name, _merge_tile(0, False), (4, s // t), [h, ys, gpre, wb, wo], specs,
                   [gout], [pl.BlockSpec((t, d), lambda n, i: (i, 0))], wants)


def _ple_tile(residual):
    def fn(h, pe, g, wgate, wproj):
        y = jax.nn.sigmoid(mm(_rms(h, g), wgate)) * mm(pe, wproj)
        return (y + h,) if residual else (y,)
    return fn


def ple_fwd(name, h, pe, g, wgate, wproj, t=512):
    s, d = h.shape
    specs = [pl.BlockSpec((t, d), lambda i: (i, 0)), pl.BlockSpec((t, pe.shape[1]), lambda i: (i, 0)),
             _full(g), _full(wgate), _full(wproj)]
    return run_fwd(name, _ple_tile(True), (s // t,), [h, pe, g, wgate, wproj], specs,
                   [_out((s, d), pl.BlockSpec((t, d), lambda i: (i, 0)))])[0][0]


def ple_bwd(name, h, pe, g, wgate, wproj, gout, t=512):
    s, d = h.shape
    specs = [pl.BlockSpec((t, d), lambda i: (i, 0)), pl.BlockSpec((t, pe.shape[1]), lambda i: (i, 0)),
             _full(g), _full(wgate), _full(wproj)]
    wants = [_want(0, h.shape, pl.BlockSpec((t, d), lambda i: (i, 0))), _want(2, g.shape, _full(g), acc=(0,)),
             _want(3, wgate.shape, _full(wgate), acc=(0,)), _want(4, wproj.shape, _full(wproj), acc=(0,))]
    return run_bwd(name, _ple_tile(True), (s // t,), [h, pe, g, wgate, wproj], specs,
                   [gout], [pl.BlockSpec((t, d), lambda i: (i, 0))], wants)


def final_loss(name, h, g, target, t=512):
    s, d = h.shape

    def fn(hh, gg, tt):
        def loss_fn(a, b):
            err = _rms(a, b) - tt
            return 0.5 * jnp.sum(jnp.mean(err * err, axis=-1, keepdims=True), axis=0, keepdims=True)
        loss, vjp = jax.vjp(loss_fn, hh, gg)
        dh, dgain = vjp(jnp.ones((1, 1), f32))
        return loss, dh, dgain

    specs = [pl.BlockSpec((t, d), lambda i: (i, 0)), _full(g), pl.BlockSpec((t, d), lambda i: (i, 0))]
    outs = [_out((1, 1), pl.BlockSpec((1, 1), lambda i: (0, 0)), acc=(0,)),
            _out((s, d), pl.BlockSpec((t, d), lambda i: (i, 0))), _out(g.shape, _full(g), acc=(0,))]
    return run_fwd(name, fn, (s // t,), [h, g, target], specs, outs)[0]


def _rev(nc, rev):
    return (lambda c: nc - 1 - c) if rev else (lambda c: c)


def _s5_ops_tile(are, aim, lstep, bre, bim):
    step = jnp.exp(lstep)
    mag = jnp.exp(are * step)
    ab_re, ab_im = mag * jnp.cos(aim * step), mag * jnp.sin(aim * step)
    den = are * are + aim * aim
    num_re = ab_re - 1.0
    f_re = (num_re * are + ab_im * aim) / den
    f_im = (ab_im * are - num_re * aim) / den
    bb_re = f_re * bre - f_im * bim
    bb_im = f_re * bim + f_im * bre
    pr = jnp.broadcast_to(ab_re, (LS, ab_re.shape[1]))
    pi = jnp.broadcast_to(ab_im, (LS, ab_im.shape[1]))
    k = 1
    while k < LS:
        pr, pi = _cmul(pr, pi, _shift(pr, k, 1.0), _shift(pi, k, 0.0))
        k *= 2
    return ab_re, ab_im, bb_re, bb_im, pr, pi


def _s5_ops_specs(arrs):
    return [pl.BlockSpec((None,) + a.shape[1:], lambda gb: (gb, 0, 0)) for a in arrs]


def s5_ops_fwd(name, raw):
    shapes = [(4, 1, 512), (4, 1, 512), (4, 128, 512), (4, 128, 512), (4, LS, 512), (4, LS, 512)]
    outs = [_out(sh, pl.BlockSpec((None,) + sh[1:], lambda gb: (gb, 0, 0))) for sh in shapes]
    return run_fwd(name, _s5_ops_tile, (4,), raw, _s5_ops_specs(raw), outs)[0]


def s5_ops_bwd(name, raw, gops):
    wants = [_want(i, a.shape, pl.BlockSpec((None,) + a.shape[1:], lambda gb: (gb, 0, 0))) for i, a in enumerate(raw)]
    return run_bwd(name, _s5_ops_tile, (4,), raw, _s5_ops_specs(raw), gops, _s5_ops_specs(gops), wants)


def _lti_scan_raw(xr, xi, pr, pi, up):
    n = xr.shape[0]
    move = (lambda a, k: _shift_up(a, k)) if up else (lambda a, k: jnp.where(_rows(a.shape) >= k, pltpu.roll(a, k, 0), 0.0))
    k = 1
    while k < n:
        sr, si = _cmul(pr, pi, move(xr, k), move(xi, k))
        xr, xi = xr + sr, xi + si
        pr, pi = _cmul(pr, pi, pr, pi)
        k *= 2
    return xr, xi


@jax.custom_vjp
def _lti_scan(xr, xi, pr, pi):
    return _lti_scan_raw(xr, xi, pr, pi, False)


def _lti_scan_fwd(xr, xi, pr, pi):
    sr, si = _lti_scan_raw(xr, xi, pr, pi, False)
    return (sr, si), (sr, si, pr, pi)


def _lti_scan_bwd(res, g):
    sr, si, pr, pi = res
    mr, mi = _lti_scan_raw(g[0], g[1], pr, -pi, True)
    qr = jnp.where(_rows(sr.shape) >= 1, pltpu.roll(sr, 1, 0), 0.0)
    qi = jnp.where(_rows(si.shape) >= 1, pltpu.roll(si, 1, 0), 0.0)
    dpr = jnp.sum(qr * mr + qi * mi, axis=0, keepdims=True)
    dpi = jnp.sum(qr * mi - qi * mr, axis=0, keepdims=True)
    return mr, mi, dpr, dpi


_lti_scan.defvjp(_lti_scan_fwd, _lti_scan_bwd)


def _s5_tile(carry, u, ab_re, ab_im, bb_re, bb_im, pw_re, pw_im, c_re, c_im, dskip):
    h_re, h_im = carry
    xr, xi = _lti_scan(mm(u, bb_re), mm(u, bb_im), ab_re, ab_im)
    cr, ci = _cmul(pw_re, pw_im, h_re, h_im)
    xr, xi = xr + cr, xi + ci
    y = mm(xr, c_re) - mm(xi, c_im) + dskip * u
    return (_pick_row(xr, LS - 1), _pick_row(xi, LS - 1)), (y,)


def _s5_io(u, ops, c_re, c_im, dskip, nc, rev):
    cm = _rev(nc, rev)
    ins = [u] + list(ops) + [c_re, c_im, dskip]
    specs = [pl.BlockSpec((LS, 128), lambda gb, c: (cm(c), 36 + gb))]
    specs += [pl.BlockSpec((None,) + a.shape[1:], lambda gb, c: (gb, 0, 0)) for a in list(ops) + [c_re, c_im]]
    specs += [pl.BlockSpec((1, 128), lambda gb, c: (0, gb))]
    return ins, specs, cm


def s5_fwd(name, proj, ops, c_re, c_im, dskip, ride=None):
    s = proj.shape[0]
    nc = s // LS
    ins, specs, cm = _s5_io(proj, ops, c_re, c_im, dskip, nc, False)
    outs = [_out((s, BW), pl.BlockSpec((LS, 128), lambda gb, c: (c, gb)))]
    res = run_fwd(name, _s5_tile, (4, nc), ins, specs, outs, carry=dict(shapes=[(1, 512), (1, 512)], axis=1), ride=ride)
    return (res[0][0],) + tuple(res[1:])


def s5_bwd(name, proj, ops, c_re, c_im, dskip, saved, gy, gproj, ride=None):
    s = proj.shape[0]
    nc = s // LS
    ins, specs, cm = _s5_io(proj, ops, c_re, c_im, dskip, nc, True)
    wants = [_want(0, (s, PW), pl.BlockSpec((LS, 128), lambda gb, c: (cm(c), 36 + gb)))]
    for i, a in enumerate(list(ops) + [c_re, c_im]):
        wants.append(_want(1 + i, a.shape, pl.BlockSpec((None,) + a.shape[1:], lambda gb, c: (gb, 0, 0)), acc=(1,)))
    wants.append(_want(9, dskip.shape, pl.BlockSpec((1, 128), lambda gb, c: (0, gb)), acc=(1,)))
    return run_bwd(name, _s5_tile, (4, nc), ins, specs, [gy], [pl.BlockSpec((LS, 128), lambda gb, c: (cm(c), gb))],
                   wants, carry=dict(axis=1, saved=saved, rev=lambda g: (g[0], cm(g[1]))), into=gproj, ride=ride)


def _s5_glu_tile(y, w, b):
    z = _gelu(y)
    return (z * jax.nn.sigmoid(mm(z, w) + b),)


def s5_glu_fwd(name, y, w, b, t=512):
    s = y.shape[0]
    spec = pl.BlockSpec((t, BW), lambda i: (i, 0))
    return run_fwd(name, _s5_glu_tile, (s // t,), [y, w, b], [spec, _full(w), _full(b)], [_out((s, BW), spec)])[0][0]


def s5_glu_bwd(name, y, w, b, gout, gout_col, t=512):
    s = y.shape[0]
    spec = pl.BlockSpec((t, BW), lambda i: (i, 0))
    wants = [_want(0, y.shape, spec), _want(1, w.shape, _full(w), acc=(0,)), _want(2, b.shape, _full(b), acc=(0,))]
    return run_bwd(name, _s5_glu_tile, (s // t,), [y, w, b], [spec, _full(w), _full(b)], [gout],
                   [pl.BlockSpec((t, BW), lambda i: (i, gout_col))], wants)


def _lru_tile(carry, xb, gate, cw, cb, wr, br, wi, bi, lam):
    h_in, prev8 = carry
    xc = _conv4(xb, prev8, cw, cb)
    r = jax.nn.sigmoid(mm(xc, wr) + br)
    ig = jax.nn.sigmoid(mm(xc, wi) + bi)
    log_a = -LRU_C * r * _softplus(-lam)
    a = jnp.exp(log_a)
    b = jnp.sqrt(-_expm1(2.0 * log_a)) * (ig * xc)
    k = 1
    while k < LS:
        b = b + a * _shift(b, k, 0.0)
        a = a * _shift(a, k, 1.0)
        k *= 2
    h = b + a * h_in
    return (_pick_row(h, LS - 1), xb[LS - 8:, :]), (h * _gelu(gate),)


def _lru_io(proj, ws, nc, rev):
    cm = _rev(nc, rev)
    ins = [proj, proj] + list(ws)
    specs = [pl.BlockSpec((LS, BW), lambda c: (cm(c), 4)), pl.BlockSpec((LS, BW), lambda c: (cm(c), 5))]
    specs += [_full(a) for a in ws]
    return ins, specs, cm


def lru_fwd(name, proj, ws):
    s = proj.shape[0]
    nc = s // LS
    ins, specs, cm = _lru_io(proj, ws, nc, False)
    outs = [_out((s, BW), pl.BlockSpec((LS, BW), lambda c: (c, 0)))]
    (y,), saved = run_fwd(name, _lru_tile, (nc,), ins, specs, outs, carry=dict(shapes=[(1, BW), (8, BW)], axis=0))
    return y, saved


def lru_bwd(name, proj, ws, saved, gy, gy_col, gproj):
    s = proj.shape[0]
    nc = s // LS
    ins, specs, cm = _lru_io(proj, ws, nc, True)
    wants = [_want((0, 1), (s, PW), pl.BlockSpec((LS, 2 * BW), lambda c: (cm(c), 2)))]
    wants += [_want(2 + i, a.shape, _full(a), acc=(0,)) for i, a in enumerate(ws)]
    return run_bwd(name, _lru_tile, (nc,), ins, specs, [gy], [pl.BlockSpec((LS, BW), lambda c: (cm(c), gy_col))], wants,
                   carry=dict(axis=0, saved=saved, rev=lambda g: (cm(g[0]),)), into=gproj)


def _causal(n):
    return _rows((n, n)) >= _lanes((n, n))


def _decay(col, rowv):
    causal = _causal(col.shape[0])
    return jnp.where(causal, jnp.exp(jnp.where(causal, col - rowv, 0.0)), 0.0)


def _m2_tile(carry, z, xs_raw, b_raw, c_raw, small, cwx, cwb, cwc, cbx, cbb, cbc, dtb, alog, dsk, ng):
    state, px, pb, pc = carry
    n = CHUNK
    xs = _silu(_conv4(xs_raw, px, cwx, cbx))
    bm = _silu(_conv4(b_raw, pb, cwb, cbb))
    cmx = _silu(_conv4(c_raw, pc, cwc, cbc))
    expand = (_lanes((16, BW)) // 64 == _rows((16, BW))).astype(f32)
    tri = _causal(n).astype(f32)
    triu = (_rows((n, n)) <= _lanes((n, n))).astype(f32)
    dt = _softplus(small + dtb)
    da = dt * (-jnp.exp(alog))
    cs = lmm(tri, da)
    cs_t = rmm_tn(da, triu)
    cs_w = rmm(cs, expand)
    last_w = _pick_row(cs_w, n - 1)
    xdt = xs * rmm(dt, expand)
    g0 = _lanes((n, BW)) < 256
    bm0, bm1, cm0, cm1 = bm[:, :128], bm[:, 128:], cmx[:, :128], cmx[:, 128:]
    cb0, cb1 = mm_nt(cm0, bm0), mm_nt(cm1, bm1)
    y = jnp.where(g0, mm(cm0, state), mm(cm1, state)) * jnp.exp(cs_w)
    for h in range(8):
        sc = (cb0 if h < 4 else cb1) * _decay(_pick_lane(cs, h), _pick_row(cs_t, h))
        y = y + jnp.where(_lanes((n, BW)) // 64 == h, mm(sc, xdt), 0.0)
    xd = xdt * jnp.exp(last_w - cs_w)
    g0s = _lanes((128, BW)) < 256
    state_out = state * jnp.exp(last_w) + jnp.where(g0s, mm_tn(bm0, xd), mm_tn(bm1, xd))
    y = (y + dsk * xs) * _silu(z)
    return (state_out, xs_raw[n - 8:, :], b_raw[n - 8:, :], c_raw[n - 8:, :]), (_rms(y, ng),)


def _m2_io(proj, small, ws, nc, rev):
    cm = _rev(nc, rev)
    n = CHUNK
    ins = [proj, proj, proj, proj, small] + list(ws)
    specs = [pl.BlockSpec((n, BW), lambda c: (cm(c), 6)), pl.BlockSpec((n, BW), lambda c: (cm(c), 7)),
             pl.BlockSpec((n, 256), lambda c: (cm(c), 16)), pl.BlockSpec((n, 256), lambda c: (cm(c), 17)),
             pl.BlockSpec((n, 16), lambda c: (cm(c), 0))]
    specs += [_full(a) for a in ws]
    return ins, specs, cm


_M2_CARRY = [(128, BW), (8, BW), (8, 256), (8, 256)]


def m2_fwd(name, proj, small, ws):
    s = proj.shape[0]
    nc = s // CHUNK
    ins, specs, cm = _m2_io(proj, small, ws, nc, False)
    outs = [_out((s, BW), pl.BlockSpec((CHUNK, BW), lambda c: (c, 0)))]
    (y,), saved = run_fwd(name, _m2_tile, (nc,), ins, specs, outs, carry=dict(shapes=_M2_CARRY, axis=0))
    return y, saved


def m2_bwd(name, proj, small, ws, saved, gy, gy_col, gproj):
    s = proj.shape[0]
    nc = s // CHUNK
    ins, specs, cm = _m2_io(proj, small, ws, nc, True)
    n = CHUNK
    wants = [_want((0, 1, 2, 3), (s, PW), pl.BlockSpec((n, 3 * BW), lambda c: (cm(c), 2))),
             _want(4, (s, 16), pl.BlockSpec((n, 16), lambda c: (cm(c), 0)))]
    wants += [_want(5 + i, a.shape, _full(a), acc=(0,)) for i, a in enumerate(ws)]
    return run_bwd(name, _m2_tile, (nc,), ins, specs, [gy], [pl.BlockSpec((n, BW), lambda c: (cm(c), gy_col))], wants,
                   carry=dict(axis=0, saved=saved, rev=lambda g: (cm(g[0]),)), into=gproj)


def _l2n(x):
    return x * lax.rsqrt(jnp.sum(x * x, axis=-1, keepdims=True) + EPS)


def _gdn_tile(carry, q_raw, k_raw, v_raw, gate, small, cwq, cwk, cwv, dtb, alog, ng):
    state, pq, pk, pv = carry
    n, nh = CHUNK, 4
    nn_ = n * nh
    qc = _silu(_conv4(q_raw, pq, cwq, None))
    kc = _silu(_conv4(k_raw, pk, cwk, None))
    vc = _silu(_conv4(v_raw, pv, cwv, None))
    beta16 = jax.nn.sigmoid(small)
    g16 = -jnp.exp(alog) * _softplus(small + dtb)
    tri = _causal(n).astype(f32)
    triu = (_rows((n, n)) <= _lanes((n, n))).astype(f32)
    cs16 = lmm(tri, g16)
    cs16_t = rmm_tn(g16, triu)
    lanes_of = lambda h: slice(128 * h, 128 * (h + 1))
    rows_of = lambda h: slice(n * h, n * (h + 1))
    stack = lambda f: jnp.concatenate([f(h) for h in range(nh)], axis=0)
    q = stack(lambda h: _l2n(qc[:, lanes_of(h)]) * (128 ** -0.5))
    k = stack(lambda h: _l2n(kc[:, lanes_of(h)]))
    v = stack(lambda h: vc[:, lanes_of(h)])
    beta = stack(lambda h: _pick_lane(beta16, 8 + h))
    col = stack(lambda h: _pick_lane(cs16, 12 + h))
    last_h = [_pick_row(_pick_lane(cs16, 12 + h), n - 1) for h in range(nh)]
    last = stack(lambda h: jnp.broadcast_to(last_h[h], (n, 1)))
    last_w = jnp.concatenate([jnp.broadcast_to(last_h[h], (1, 128)) for h in range(nh)], axis=1)
    spread = (_rows((n, nn_)) == _lanes((n, nn_)) % n).astype(f32)
    cs_w = rmm(cs16_t, spread)
    rowv = jnp.sum(jnp.where(_rows((16, nn_)) == 12 + _lanes((16, nn_)) // n, cs_w, 0.0), axis=0, keepdims=True)
    same = (_rows((nn_, nn_)) // n) == (_lanes((nn_, nn_)) // n)
    causal = jnp.logical_and(same, _rows((nn_, nn_)) >= _lanes((nn_, nn_)))
    strict = jnp.logical_and(same, _rows((nn_, nn_)) > _lanes((nn_, nn_)))
    decay = jnp.where(causal, jnp.exp(jnp.where(causal, col - rowv, 0.0)), 0.0)
    kb = k * beta
    t = _tri_inv(jnp.where(strict, mm_nt(kb, k) * decay, 0.0))
    e_col = jnp.exp(col)
    uw = xmm(t, jnp.concatenate([v * beta, kb * e_col], axis=1))
    u, w = uw[:, :128], uw[:, 128:]
    qk = mm_nt(q, k) * decay
    own = lambda r: stack(lambda h: r[rows_of(h), lanes_of(h)])
    v_new = u - own(mm(w, state))
    o = own(mm(q * e_col, state)) + mm(qk, v_new)
    zero = jnp.zeros((n, 128), f32)
    v_blocks = stack(lambda h: jnp.concatenate([v_new[rows_of(h), :] if j == h else zero for j in range(nh)], axis=1))
    state_out = state * jnp.exp(last_w) + mm_tn(k * jnp.exp(last - col), v_blocks)
    gt = stack(lambda h: gate[:, lanes_of(h)])
    out = _rms(o, ng) * _silu(gt)
    out = jnp.concatenate([out[rows_of(h), :] for h in range(nh)], axis=1)
    return (state_out, q_raw[n - 8:, :], k_raw[n - 8:, :], v_raw[n - 8:, :]), (out,)


def _gdn_io(proj, small, cw, dtb, alog, ng, nc, rev):
    cm = _rev(nc, rev)
    n = CHUNK
    ins = [proj, proj, proj, proj, small, cw, cw, cw, dtb, alog, ng]
    specs = [pl.BlockSpec((n, BW), lambda c, _j=j: (cm(c), _j)) for j in (0, 1, 2, 3)]
    specs += [pl.BlockSpec((n, 16), lambda c: (cm(c), 0))]
    specs += [pl.BlockSpec((4, BW), lambda c, _j=j: (0, _j)) for j in (0, 1, 2)]
    specs += [_full(dtb), _full(alog), _full(ng)]
    return ins, specs, cm


_GDN_CARRY = [(128, BW)] + [(8, BW)] * 3


def gdn_fwd(name, proj, small, cw, dtb, alog, ng, ride=None):
    s = proj.shape[0]
    nc = s // CHUNK
    ins, specs, cm = _gdn_io(proj, small, cw, dtb, alog, ng, nc, False)
    outs = [_out((s, BW), pl.BlockSpec((CHUNK, BW), lambda c: (c, 0)))]
    res = run_fwd(name, _gdn_tile, (nc,), ins, specs, outs, carry=dict(shapes=_GDN_CARRY, axis=0), ride=ride)
    return (res[0][0],) + tuple(res[1:])


def gdn_bwd(name, proj, small, cw, dtb, alog, ng, saved, gy, gy_col, ride=None):
    s = proj.shape[0]
    nc = s // CHUNK
    n = CHUNK
    ins, specs, cm = _gdn_io(proj, small, cw, dtb, alog, ng, nc, True)
    wants = [_want((0, 1, 2, 3), (s, PW), pl.BlockSpec((n, 4 * BW), lambda c: (cm(c), 0))),
             _want(4, (s, 16), pl.BlockSpec((n, 16), lambda c: (cm(c), 0)))]
    wants += [_want(5 + i, (4, BW), pl.BlockSpec((4, BW), lambda c: (0, 0)), acc=(0,)) for i in range(3)]
    wants += [_want(8, dtb.shape, _full(dtb), acc=(0,)), _want(9, alog.shape, _full(alog), acc=(0,)),
              _want(10, ng.shape, _full(ng), acc=(0,))]
    return run_bwd(name, _gdn_tile, (nc,), ins, specs, [gy], [pl.BlockSpec((n, BW), lambda c: (cm(c), gy_col))], wants,
                   carry=dict(axis=0, saved=saved, rev=lambda g: (cm(g[0]),)), ride=ride)


def _perm_cols(w):
    pad = jnp.zeros(w.shape[:-1] + (PW - IN_WIDTH,), w.dtype)
    return jnp.concatenate([w[..., 3080:5128], w[..., 512:3072], w[..., :512], w[..., 3072:3080], w[..., 5128:5136], pad], axis=-1)


def _unperm_cols(g):
    return jnp.concatenate([g[..., 4608:5120], g[..., 2048:4608], g[..., 5120:5128], g[..., :2048], g[..., 5128:5136]], axis=-1)


def _bd(blocks):
    n, a, b = blocks.shape
    eye = jnp.eye(n, dtype=blocks.dtype)
    return jnp.einsum("nab,nm->namb", blocks, eye).reshape(n * a, n * b)


def _layer_layout(lw):
    o = {}
    row = lambda a: a.reshape(1, -1)
    o["b_gate"] = row(lw["b_gate"])
    o["s5_are"] = lw["s5_a_re"].reshape(4, 1, 512)
    o["s5_aim"] = lw["s5_a_im"].reshape(4, 1, 512)
    o["s5_lstep"] = jnp.repeat(lw["s5_log_step"], 64).reshape(4, 1, 512)
    bt = lambda b: jax.vmap(_bd)(jnp.swapaxes(b, 1, 2).reshape(4, 8, 16, 64))
    o["s5_bre"], o["s5_bim"] = bt(lw["s5_b_re"]), bt(lw["s5_b_im"])
    ct = lambda c: jax.vmap(_bd)(jnp.swapaxes(c, 1, 2).reshape(4, 8, 64, 16))
    o["s5_cre"], o["s5_cim"] = ct(lw["s5_c_re"]), ct(lw["s5_c_im"])
    o["s5_d"] = row(lw["s5_d"])
    o["s5_b_glu"] = row(lw["s5_b_glu"])
    o["lru_conv_b"], o["lru_b_r"], o["lru_b_i"], o["lru_lambda"] = (row(lw[k]) for k in ("lru_conv_b", "lru_b_r", "lru_b_i", "lru_lambda"))
    o["lru_wr"], o["lru_wi"] = _bd(lw["lru_w_r"]), _bd(lw["lru_w_i"])
    cw, cb = lw["m2_conv_w"], lw["m2_conv_b"]
    o["m2_cwx"], o["m2_cwb"], o["m2_cwc"] = cw[:, :512], cw[:, 512:768], cw[:, 768:]
    o["m2_cbx"], o["m2_cbb"], o["m2_cbc"] = row(cb[:512]), row(cb[512:768]), row(cb[768:])
    o["m2_dtb"] = jnp.pad(lw["m2_dt_bias"], (0, 8)).reshape(1, 16)
    o["m2_alog"] = jnp.pad(lw["m2_a_log"], (0, 8)).reshape(1, 16)
    o["m2_dsk"] = jnp.repeat(lw["m2_d"], 64).reshape(1, 512)
    o["m2_norm"] = row(lw["m2_norm"])
    o["gdn_dtb"] = jnp.pad(lw["gdn_dt_bias"], (12, 0)).reshape(1, 16)
    o["gdn_alog"] = jnp.pad(lw["gdn_a_log"], (12, 0)).reshape(1, 16)
    o["gdn_norm"] = row(lw["gdn_norm"])
    for k in ("ffn1_norm", "mix_norm", "ffn2_norm", "ple_norm"):
        o[k] = row(lw[k])
    return o


_BIG_PLAIN = ("w_gate", "s5_w_glu", "w_branch", "w_out", "ple_w_gate", "ple_w_proj")


def big_layout(w):
    o = {k: w[k] for k in _BIG_PLAIN}
    for f in ("ffn1", "ffn2"):
        o[f + "_wg"], o[f + "_wu"] = w[f + "_w_in"][:, :FFN_DIM], w[f + "_w_in"][:, FFN_DIM:]
        o[f + "_wo"] = w[f + "_w_out"]
    o["w_in"] = _perm_cols(w["w_in"])
    return {k: v.astype(bf16) for k, v in o.items()}


def big_unlayout(g):
    o = {k: g[k] for k in _BIG_PLAIN}
    for f in ("ffn1", "ffn2"):
        o[f + "_w_in"] = jnp.concatenate([g[f + "_wg"], g[f + "_wu"]], axis=1)
        o[f + "_w_out"] = g[f + "_wo"]
    o["w_in"] = _unperm_cols(g["w_in"])
    return o


_SMALL_KEYS = ("b_gate", "s5_log_step", "s5_a_re", "s5_a_im", "s5_b_re", "s5_b_im", "s5_c_re", "s5_c_im", "s5_d",
               "s5_b_glu", "lru_conv_b", "lru_w_r", "lru_b_r", "lru_w_i", "lru_b_i", "lru_lambda", "m2_conv_w",
               "m2_conv_b", "m2_dt_bias", "m2_a_log", "m2_d", "m2_norm", "gdn_dt_bias", "gdn_a_log", "gdn_norm",
               "ffn1_norm", "mix_norm", "ffn2_norm", "ple_norm")


def _layer_fwd(i, h0, pe, lw, big, ride=None):
    ride = ride or (None, None)
    n = f"l{i}_"
    lay = _layer_layout(lw)
    a = {"h0": h0, "lay": lay}
    h1, n1 = ffn_fwd(n + "ffn1_fwd", h0, lay["ffn1_norm"], big["ffn1_wg"], big["ffn1_wu"], big["ffn1_wo"])
    zero_b = jnp.zeros((1, PW), f32)
    proj, u = normlin_fwd(n + "inproj_fwd", h1, lay["mix_norm"], big["w_in"], zero_b, tn=896)
    gpre = lin_fwd(n + "gate_fwd", u, big["w_gate"], lay["b_gate"], tn=1024)
    small = proj[:, SMALL_OFF:SMALL_OFF + 16]
    raw = [lay["s5_are"], lay["s5_aim"], lay["s5_lstep"], lay["s5_bre"], lay["s5_bim"]]
    ops = s5_ops_fwd(n + "s5ops_fwd", raw)
    y5, sv5, *rode5 = s5_fwd(n + "s5_fwd", proj, ops, lay["s5_cre"], lay["s5_cim"], lay["s5_d"], ride[0])
    ya = s5_glu_fwd(n + "s5glu_fwd", y5, big["s5_w_glu"], lay["s5_b_glu"])
    lru_ws = [lw["lru_conv_w"], lay["lru_conv_b"], lay["lru_wr"], lay["lru_b_r"], lay["lru_wi"], lay["lru_b_i"], lay["lru_lambda"]]
    yb, svb = lru_fwd(n + "lru_fwd", proj, lru_ws)
    m2_ws = [lay[k] for k in ("m2_cwx", "m2_cwb", "m2_cwc", "m2_cbx", "m2_cbb", "m2_cbc", "m2_dtb", "m2_alog", "m2_dsk", "m2_norm")]
    yc, svc = m2_fwd(n + "m2_fwd", proj, small, m2_ws)
    yd, svd, *roded = gdn_fwd(n + "gdn_fwd", proj, small, lw["gdn_conv_w"], lay["gdn_dtb"], lay["gdn_alog"], lay["gdn_norm"], ride[1])
    ys = jnp.concatenate([ya, yb, yc, yd], axis=1)
    h2 = merge_fwd(n + "merge_fwd", h1, ys, gpre, big["w_branch"], big["w_out"])
    h3, n2 = ffn_fwd(n + "ffn2_fwd", h2, lay["ffn2_norm"], big["ffn2_wg"], big["ffn2_wu"], big["ffn2_wo"])
    h4 = ple_fwd(n + "ple_fwd", h3, pe, lay["ple_norm"], big["ple_w_gate"], big["ple_w_proj"])
    a.update(h1=h1, n1=n1, u=u, n2=n2, proj=proj, gpre=gpre, small=small, raw=raw, ops=ops, y5=y5, sv5=sv5, lru_ws=lru_ws, svb=svb,
             m2_ws=m2_ws, svc=svc, svd=svd, ys=ys, h2=h2, h3=h3, pe=pe)
    return h4, a, (rode5[0] if rode5 else None, roded[0] if roded else None)


def small_into(name, a, b, gproj, t=512):
    s, w = a.shape
    pad_w = PW - SMALL_OFF

    def body(a_ref, b_ref, _, o_ref):
        place = (_rows((w, pad_w)) == _lanes((w, pad_w))).astype(f32)
        o_ref[...] = _dg(a_ref[...] + b_ref[...], place, _NN, (3, 1))

    row = pl.BlockSpec((t, w), lambda i: (i, 0))
    return pl.pallas_call(
        body, name=name, grid=(s // t,), in_specs=[row, row, _ANY],
        out_specs=pl.BlockSpec((t, pad_w), lambda i: (i, SMALL_OFF // pad_w)),
        out_shape=jax.ShapeDtypeStruct(gproj.shape, gproj.dtype), input_output_aliases={2: 0}, compiler_params=_params((1,)),
    )(a, b, gproj)


RIDE_S5 = ("ffn2_w_in", "ffn2_w_out", "w_branch", "w_out", "ple_w_gate", "ple_w_proj", "s5_w_glu")
RIDE_FFN1 = ("w_in", "w_gate", "lru_conv_w", "m2_conv_w", "gdn_conv_w")
RIDE_LAST = ("ffn1_w_in", "ffn1_w_out")


def _grad_of(k, gb, small):
    if k in ("ffn1_w_in", "ffn2_w_in"):
        return jnp.concatenate([gb[k[:4] + "_wg"], gb[k[:4] + "_wu"]], axis=1)
    if k in ("ffn1_w_out", "ffn2_w_out"):
        return gb[k[:4] + "_wo"]
    if k == "w_in":
        return _unperm_cols(gb[k])
    if k == "m2_conv_w" and k not in small:
        return jnp.concatenate([small["m2_cwx"], small["m2_cwb"], small["m2_cwc"]], axis=1)
    return gb[k] if k in gb else small[k]


def _scattered(names, gb, small):
    return [_full_to_scattered(k, _grad_of(k, gb, small)[None]).astype(bf16) for k in names]


def _layer_bwd(i, a, lw, big, gh4, ride=None):
    n = f"l{i}_"
    lay = a["lay"]
    gb, gl = {}, {}
    s = gh4.shape[0]
    t = 512
    gh3, gl["ple_norm"], gb["ple_w_gate"], gb["ple_w_proj"] = ple_bwd(
        n + "ple_bwd", a["h3"], a["pe"], lay["ple_norm"], big["ple_w_gate"], big["ple_w_proj"], gh4)
    dn2, gb["ffn2_wg"], gb["ffn2_wu"], gb["ffn2_wo"] = ffn_bwd(
        n + "ffn2_bwd", a["n2"], big["ffn2_wg"], big["ffn2_wu"], big["ffn2_wo"], gh3)
    gh2, gl["ffn2_norm"] = norm_bwd(n + "gh2", a["h2"], lay["ffn2_norm"], gh3, _parts(dn2))
    gys, ggpre, gb["w_branch"], gb["w_out"] = merge_bwd(
        n + "merge_bwd", a["h1"], a["ys"], a["gpre"], big["w_branch"], big["w_out"], gh2)
    rd = gdn_bwd(n + "gdn_bwd", a["proj"], a["small"], lw["gdn_conv_w"], lay["gdn_dtb"], lay["gdn_alog"], lay["gdn_norm"], a["svd"], gys, 3, ride)
    gproj, dsm_d = rd[:2]
    gl["gdn_conv_w"] = jnp.concatenate(rd[2:5], axis=1)
    gl["gdn_dtb"], gl["gdn_alog"], gl["gdn_norm"] = rd[5:8]
    rode = rd[8:]
    rb = lru_bwd(n + "lru_bwd", a["proj"], a["lru_ws"], a["svb"], gys, 1, gproj)
    gproj = rb[0]
    gl["lru_conv_w"], gl["lru_conv_b"], gl["lru_wr"], gl["lru_b_r"], gl["lru_wi"], gl["lru_b_i"], gl["lru_lambda"] = rb[1:]
    rc = m2_bwd(n + "m2_bwd", a["proj"], a["small"], a["m2_ws"], a["svc"], gys, 2, gproj)
    gproj, dsm_c = rc[:2]
    for k, v in zip(("m2_cwx", "m2_cwb", "m2_cwc", "m2_cbx", "m2_cbb", "m2_cbc", "m2_dtb", "m2_alog", "m2_dsk", "m2_norm"), rc[2:]):
        gl[k] = v
    gy5, gb["s5_w_glu"], gl["s5_b_glu"] = s5_glu_bwd(n + "s5glu_bwd", a["y5"], big["s5_w_glu"], lay["s5_b_glu"], gys, 0)
    own = {}
    r5 = s5_bwd(n + "s5_bwd", a["proj"], a["ops"], lay["s5_cre"], lay["s5_cim"], lay["s5_d"], a["sv5"], gy5, gproj,
                _scattered(RIDE_S5, gb, gl) if ride is not None else None)
    gproj, gops, gl["s5_cre"], gl["s5_cim"], gl["s5_d"] = r5[0], r5[1:7], r5[7], r5[8], r5[9]
    own.update(zip(RIDE_S5, r5[10:]))
    gl["s5_are"], gl["s5_aim"], gl["s5_lstep"], gl["s5_bre"], gl["s5_bim"] = s5_ops_bwd(n + "s5ops_bwd", a["raw"], gops)
    gproj = small_into(n + "gsmall", dsm_c, dsm_d, gproj)
    zero_b = jnp.zeros((1, PW), f32)
    du_p, gb["w_in"], _ = lin_bwd(n + "inproj_bwd", a["u"], big["w_in"], zero_b, gproj, tn=896)
    du_g, gb["w_gate"], gl["b_gate"] = lin_bwd(n + "gate_bwd", a["u"], big["w_gate"], lay["b_gate"], ggpre, tn=1024)
    gh1, gl["mix_norm"] = norm_bwd(n + "gh1", a["h1"], lay["mix_norm"], gh2, _parts(du_p) + _parts(du_g))
    rf = ffn_bwd(n + "ffn1_bwd", a["n1"], big["ffn1_wg"], big["ffn1_wu"], big["ffn1_wo"], gh1,
                 ride=_scattered(RIDE_FFN1, gb, gl) if ride is not None else None)
    dn1, gb["ffn1_wg"], gb["ffn1_wu"], gb["ffn1_wo"] = rf[:4]
    own.update(zip(RIDE_FFN1, rf[4:]))
    gh0, gl["ffn1_norm"] = norm_bwd(n + "gh0", a["h0"], lay["ffn1_norm"], gh1, _parts(dn1))
    return gh0, gb, gl, (rode, own)


def _local_step(x, p, target, bigs, smalls, final_norm, ride_of=None, next_layer=None):
    h = x
    acts = []
    bigs, smalls = list(bigs), list(smalls)
    for i in range(DEPTH):
        h, a, got = _layer_fwd(i, h, p[i], smalls[i], bigs[i], next_layer[0] if next_layer and i == DEPTH - 2 else None)
        if next_layer and i == DEPTH - 2:
            bigs[i + 1], smalls[i + 1] = next_layer[1](got)
        acts.append(a)
    fg = final_norm.reshape(1, -1)
    loss, gh, gfn = final_loss("final_loss", h, fg, target)
    gbs, gss, rode = [None] * DEPTH, [None] * DEPTH, [None] * DEPTH
    for i in reversed(range(DEPTH)):
        ride = ride_of(gbs[i + 1], gss[i + 1]) if ride_of is not None and i + 1 < DEPTH else None
        gh, gb, gl, (got, own) = _layer_bwd(i, acts[i], smalls[i], bigs[i], gh, ride)
        if ride is not None:
            rode[i + 1], rode[i] = got, own
        _, pull = jax.vjp(_layer_layout, smalls[i])
        lay_g = {k: gl[k] for k in acts[i]["lay"]}
        gs = pull(lay_g)[0]
        gs = dict(gs)
        gs["lru_conv_w"] = gl["lru_conv_w"]
        gs["gdn_conv_w"] = gl["gdn_conv_w"]
        gbs[i], gss[i] = gb, gs
    return loss, gh, gbs, gss, gfn.reshape(-1), rode


MESH_AXES = ("x", "y", "c")
PACK_W = 1024
PACK_ROWS = 256

W_NAMES = ("ffn1_norm", "ffn1_w_in", "ffn1_w_out", "mix_norm", "w_in", "w_gate", "b_gate", "s5_log_step", "s5_a_re",
           "s5_a_im", "s5_b_re", "s5_b_im", "s5_c_re", "s5_c_im", "s5_d", "s5_w_glu", "s5_b_glu", "lru_conv_w",
           "lru_conv_b", "lru_w_r", "lru_b_r", "lru_w_i", "lru_b_i", "lru_lambda", "m2_conv_w", "m2_conv_b",
           "m2_dt_bias", "m2_a_log", "m2_d", "m2_norm", "gdn_conv_w", "gdn_dt_bias", "gdn_a_log", "gdn_norm",
           "w_branch", "w_out", "ffn2_norm", "ffn2_w_in", "ffn2_w_out", "ple_norm", "ple_w_gate", "ple_w_proj",
           "final_norm")
COL_SHARDED = ("ffn1_w_in", "w_in", "w_gate", "lru_conv_w", "m2_conv_w", "gdn_conv_w", "w_branch", "ffn2_w_in", "ple_w_proj")
ROW_SHARDED = ("ffn1_w_out", "s5_w_glu", "w_out", "ffn2_w_out", "ple_w_gate")
BIG_NAMES = ("ffn1_w_in", "ffn1_w_out", "w_in", "w_gate", "s5_w_glu", "w_branch", "w_out", "ffn2_w_in", "ffn2_w_out",
             "ple_w_gate", "ple_w_proj")
CONV_NAMES = ("lru_conv_w", "m2_conv_w", "gdn_conv_w")
SHARDED = BIG_NAMES + CONV_NAMES
REPLICATED = tuple(k for k in W_NAMES if k not in SHARDED)


def _gathered_to_full(name, g):
    if name in COL_SHARDED:
        g = jnp.moveaxis(g, 0, -2)
        return g.reshape(g.shape[:-2] + (g.shape[-2] * g.shape[-1],))
    g = jnp.moveaxis(g, 0, 1)
    return g.reshape((g.shape[0], g.shape[1] * g.shape[2]) + g.shape[3:])


def _full_to_scattered(name, w):
    if name in COL_SHARDED:
        w = w.reshape(w.shape[:-1] + (N_DEV, w.shape[-1] // N_DEV))
        return jnp.moveaxis(w, -2, 0)
    w = w.reshape((w.shape[0], N_DEV, w.shape[1] // N_DEV) + w.shape[2:])
    return jnp.moveaxis(w, 1, 0)


def _pack(arrs):
    pieces = []
    for a in arrs:
        k = -(-a.size // PACK_W)
        pieces.append(jnp.pad(a.reshape(-1), (0, k * PACK_W - a.size)).reshape(k, PACK_W))
    buf = jnp.concatenate(pieces, axis=0)
    return jnp.pad(buf, ((0, -buf.shape[0] % PACK_ROWS), (0, 0)))


def _unpack(buf, shapes):
    out, r = [], 0
    for sh in shapes:
        n = math.prod(sh)
        k = -(-n // PACK_W)
        out.append(buf[r:r + k].reshape(-1)[:n].reshape(sh))
        r += k
    return out


def _peer(k):
    mx, my, mc = (lax.axis_index(a) for a in MESH_AXES)
    px = 1 - mx if k & 4 else mx
    py = 1 - my if k & 2 else my
    pc = 1 - mc if k & 1 else mc
    return (px, py, pc), 4 * px + 2 * py + pc


def all_gather(name, xs):
    n = len(xs)

    def body(*refs):
        x_refs, out_refs = refs[:n], refs[n:2 * n]
        send_sems, recv_sems, local_sems = refs[2 * n:]
        mx, my, mc = (lax.axis_index(a) for a in MESH_AXES)
        me, sibling = (mx, my, mc), (mx, my, 1 - mc)
        chips = [(1 - mx, my), (mx, 1 - my), (1 - mx, 1 - my)]

        def slot(i, px, py, pc):
            return out_refs[i].at[4 * px + 2 * py + pc]

        def copy(k, i, block, to, src=None):
            return pltpu.make_async_remote_copy(
                src_ref=slot(i, *block) if src is None else src, dst_ref=slot(i, *block),
                send_sem=send_sems.at[k, i], recv_sem=recv_sems.at[k, i], device_id=to, device_id_type=pl.DeviceIdType.MESH)

        mine = [pltpu.make_async_copy(x_refs[i], slot(i, *me), local_sems.at[i]) for i in range(n)]
        first = []
        for i in range(n):
            mine[i].start()
            first.append(copy(0, i, me, sibling, src=x_refs[i]))
            first += [copy(1 + j, i, me, (*chip, mc), src=x_refs[i]) for j, chip in enumerate(chips)]
        for cp in first:
            cp.start()
        passed = []
        for i in range(n):
            for j, chip in enumerate(chips):
                copy(1 + j, i, (*chip, mc), me).wait_recv()
                cp = copy(4 + j, i, (*chip, mc), sibling)
                cp.start()
                passed.append(cp)
        for i in range(n):
            copy(0, i, sibling, me).wait_recv()
            for j, chip in enumerate(chips):
                copy(4 + j, i, (*chip, 1 - mc), me).wait_recv()
        for cp in first + passed:
            cp.wait_send()
        for cp in mine:
            cp.wait()

    res = pl.pallas_call(
        body, name=name, out_shape=[jax.ShapeDtypeStruct((N_DEV,) + x.shape, x.dtype) for x in xs],
        in_specs=[_ANY] * n, out_specs=[_ANY] * n,
        scratch_shapes=[pltpu.SemaphoreType.DMA((7, n)), pltpu.SemaphoreType.DMA((7, n)), pltpu.SemaphoreType.DMA((n,))],
    )(*xs)
    return list(res)


def _exchange_copies(g_refs, out_refs, send_sems, recv_sems, local_sems, with_incoming=True):
    mx, my, mc = (lax.axis_index(a) for a in MESH_AXES)
    me = 4 * mx + 2 * my + mc
    n = len(g_refs)
    local = [pltpu.make_async_copy(g_refs[i].at[me], out_refs[i].at[me], local_sems.at[i]) for i in range(n)]
    outgoing, incoming = [], []
    for k in range(1, N_DEV):
        peer, pidx = _peer(k)
        for i in range(n):
            sems = dict(send_sem=send_sems.at[k - 1, i], recv_sem=recv_sems.at[k - 1, i], device_id=peer,
                        device_id_type=pl.DeviceIdType.MESH)
            outgoing.append(pltpu.make_async_remote_copy(src_ref=g_refs[i].at[pidx], dst_ref=out_refs[i].at[me], **sems))
            if with_incoming:
                incoming.append(pltpu.make_async_remote_copy(src_ref=g_refs[i].at[pidx], dst_ref=out_refs[i].at[pidx], **sems))
    return local, outgoing, incoming


def _exchange_start(*refs):
    local, outgoing, _ = _exchange_copies(*refs, with_incoming=False)
    for cp in local + outgoing:
        cp.start()


def _exchange_wait(*refs):
    local, outgoing, incoming = _exchange_copies(*refs)
    for cp in incoming:
        cp.wait_recv()
    for cp in outgoing:
        cp.wait_send()
    for cp in local:
        cp.wait()


def _exchange_sems(n):
    return [pltpu.SemaphoreType.DMA((7, n)), pltpu.SemaphoreType.DMA((7, n)), pltpu.SemaphoreType.DMA((n,))]


def exchange(name, gs):
    n = len(gs)

    def body(*refs):
        args = (refs[:n], refs[n:2 * n]) + tuple(refs[2 * n:])
        _exchange_start(*args)
        _exchange_wait(*args)

    res = pl.pallas_call(
        body, name=name, out_shape=[jax.ShapeDtypeStruct(g.shape, g.dtype) for g in gs], in_specs=[_ANY] * n, out_specs=[_ANY] * n,
        scratch_shapes=_exchange_sems(n),
    )(*gs)
    return list(res)


def sum_slots(name, buf):
    return addn(name, _parts(buf), PACK_ROWS)


def _adamw_math(ww, gg, mm_, vv):
    m2 = ADAM_B1 * mm_ + (1.0 - ADAM_B1) * gg
    v2 = ADAM_B2 * vv + (1.0 - ADAM_B2) * (gg * gg)
    m_hat = m2 / (1.0 - ADAM_B1 ** ADAM_STEP)
    v_hat = v2 / (1.0 - ADAM_B2 ** ADAM_STEP)
    delta = -ADAM_LR * (m_hat / (jnp.sqrt(v_hat) + ADAM_EPS) + ADAM_WD * ww)
    return delta, m2, v2


def adamw(name, w, g, m, v):
    spec = pl.BlockSpec((PACK_ROWS, PACK_W), lambda i: (i, 0))
    return run_fwd(name, _adamw_math, (w.shape[0] // PACK_ROWS,), [w, g, m, v], [spec] * 4, [_out(w.shape, spec)] * 3)[0]


def reduce_adamw(name, slots, w, m, v):
    shape = w.shape
    r, c = shape[-2:]
    a = math.prod(shape[:-2])
    per = a // DEPTH
    tr = r
    while tr * c * 4 > (1 << 20) and tr % 16 == 0:
        tr //= 2
    s3 = [s.reshape((N_DEV, per, r, c)) for s in slots]
    w3, m3, v3 = (t.reshape((a, r, c)) for t in (w, m, v))

    def fn(*t):
        layer = pl.program_id(0) // per
        g = None
        for l in range(DEPTH):
            gl = t[l * N_DEV].astype(f32)
            for d in range(1, N_DEV):
                gl = gl + t[l * N_DEV + d].astype(f32)
            g = gl if g is None else jnp.where(layer == l, gl, g)
        k = DEPTH * N_DEV
        return (g,) + _adamw_math(t[k], g, t[k + 1], t[k + 2])

    specs, ins = [], []
    for l in range(DEPTH):
        for d in range(N_DEV):
            specs.append(pl.BlockSpec((None, None, tr, c), lambda i, j, _d=d, _l=l: (_d, jnp.clip(i - _l * per, 0, per - 1), j, 0)))
            ins.append(s3[l])
    spec = pl.BlockSpec((None, tr, c), lambda i, j: (i, j, 0))
    res = run_fwd(name, fn, (a, r // tr), ins + [w3, m3, v3], specs + [spec] * 3, [_out((a, r, c), spec)] * 4)[0]
    return [t.reshape(shape) for t in res]


def kernel(*args):
    nw = len(W_NAMES)
    x, p = args[0], args[1]
    w = dict(zip(W_NAMES, args[2:2 + nw]))
    target = args[2 + nw]
    m = dict(zip(W_NAMES, args[3 + nw:3 + 2 * nw]))
    v = dict(zip(W_NAMES, args[3 + 2 * nw:3 + 3 * nw]))

    wire = lambda k, i: w[k][i].astype(bf16) if k in BIG_NAMES else w[k][i]

    def layer_params(i, gathered):
        full = {k: _gathered_to_full(k, g[:, None])[0] for k, g in zip(SHARDED, gathered)}
        small = {k: w[k][i] for k in _SMALL_KEYS if k not in CONV_NAMES}
        small.update({k: full[k] for k in CONV_NAMES})
        return big_layout({k: full[k] for k in BIG_NAMES}), small

    big0, small0 = layer_params(0, all_gather("ag_layer0", [wire(k, 0) for k in SHARDED]))
    to_all = lambda k: jnp.broadcast_to(wire(k, DEPTH - 1)[None], (N_DEV,) + w[k].shape[1:])
    early = ("ffn1_w_in", "ffn1_w_out")
    late = tuple(k for k in SHARDED if k not in early)

    def last_layer(got):
        by_name = dict(zip(early + late, list(got[0]) + list(got[1])))
        return layer_params(DEPTH - 1, [by_name[k] for k in SHARDED])

    bigs, smalls = [big0, None], [small0, None]
    next_layer = (([to_all(k) for k in early], [to_all(k) for k in late]), last_layer)
    loss, gx, gbs, gss, gfn, rode = _local_step(x[0], p[:, 0], target[0], bigs, smalls, w["final_norm"],
                                                lambda gb, gs: _scattered(SHARDED, gb, gs), next_layer)
    loss = lax.psum(loss[0, 0], MESH_AXES)

    rode[0].update(zip(RIDE_LAST, exchange("rs_last", _scattered(RIDE_LAST, gbs[0], gss[0]))))
    rode[0] = [rode[0][k] for k in SHARDED]
    gfull = [dict(big_unlayout(gbs[i]), **gss[i]) for i in range(DEPTH)]
    stack = lambda k: jnp.stack([gfull[i][k] for i in range(DEPTH)])
    outs = {}
    kinds = ("grad", "delta", "new_m", "new_v")
    for j, k in enumerate(SHARDED):
        for kind, a in zip(kinds, reduce_adamw("adamw_" + k, [rode[i][j] for i in range(DEPTH)], w[k], m[k], v[k])):
            outs[kind + "_" + k] = a

    shapes = [w[k].shape for k in REPLICATED]
    g_rep = sum_slots("sum_replicated", all_gather("ag_replicated", [_pack([gfn if k == "final_norm" else stack(k) for k in REPLICATED])])[0])
    res = adamw("adamw_replicated", _pack([w[k] for k in REPLICATED]), g_rep, _pack([m[k] for k in REPLICATED]), _pack([v[k] for k in REPLICATED]))
    for kind, buf in zip(kinds, [g_rep] + list(res)):
        for k, a in zip(REPLICATED, _unpack(buf, shapes)):
            outs[kind + "_" + k] = a
    return (loss, gx[None]) + tuple(outs[kind + "_" + k] for kind in kinds for k in W_NAMES)
```
